```python
import math
import jax, jax.numpy as jnp
from jax import lax
import numpy as np

D_MODEL = 1024
BATCH = 8
SEQ = 2048
DEPTH = 4

CHUNK = 64
Q_BLOCK = 128
N_MEM = 256
EPS = 1e-6
F32 = jnp.float32

GLA_HEADS = 4
GLA_DK = 32
GLA_DV = 64
GLA_GATE_RANK = 16
GLA_GATE_TAU = 16.0
GLA_QK = GLA_HEADS * GLA_DK
GLA_V = GLA_HEADS * GLA_DV
S5_GROUPS = 16
S5_GROUP_CH = 16
S5_STATE = 64
S5_WIDTH = S5_GROUPS * S5_GROUP_CH
S5_DT_MIN = 1e-3
S5_DT_MAX = 1e-1
FOX_HEADS = 8
FOX_DH = 64
FOX_W = FOX_HEADS * FOX_DH
XA_HEADS = 4
XA_DH = D_MODEL // XA_HEADS
D_FF = 2816
N_BRANCH = 3
IN_SIZES = (GLA_QK, GLA_QK, GLA_V, GLA_V, GLA_GATE_RANK, S5_WIDTH, FOX_W, FOX_W, FOX_W, FOX_HEADS, N_BRANCH * D_MODEL)
D_IN = GLA_QK + GLA_QK + GLA_V + GLA_V + GLA_GATE_RANK + S5_WIDTH + 3 * FOX_W + FOX_HEADS + N_BRANCH * D_MODEL

kernel_name = 'hybrid_gla_s5_fox_macaron_sandwich'


def rms_norm(x, g):
    xf = x.astype(F32)
    y = xf * lax.rsqrt(jnp.mean(xf * xf, axis=-1, keepdims=True) + EPS)
    return (y * g.astype(F32)).astype(x.dtype)


def swiglu(h, w_gu, w_down):
    gate, up = jnp.split(h @ w_gu, 2, axis=-1)
    return (jax.nn.silu(gate) * up) @ w_down


def split_cols(p, sizes):
    outs, off = [], 0
    for s in sizes:
        outs.append(p[..., off:off + s])
        off += s
    return outs


def gla_chunked(q, k, v, log_a):
    b, l, h, dk = q.shape
    dv = v.shape[-1]
    n = l // CHUNK

    def to_chunks(t):
        return t.astype(F32).reshape(b, n, CHUNK, h, t.shape[-1]).transpose(0, 3, 1, 2, 4)

    qc = to_chunks(q) * (dk ** -0.5)
    kc = to_chunks(k)
    vc = to_chunks(v)
    g = jnp.cumsum(to_chunks(log_a), axis=3)
    g_last = g[:, :, :, -1:, :]
    eg, ieg = jnp.exp(g), jnp.exp(-g)
    q_fwd = qc * eg
    a_fwd = jnp.einsum('bhnid,bhnjd->bhnij', q_fwd, kc * ieg)
    a_bwd = jnp.einsum('bhnid,bhnjd->bhnij', qc * ieg, kc * eg)
    lower = jnp.tril(jnp.ones((CHUNK, CHUNK), dtype=bool))
    attn = jnp.where(lower, a_fwd, a_bwd)
    o_intra = jnp.einsum('bhnij,bhnje->bhnie', attn, vc)
    ds = jnp.einsum('bhncd,bhnce->nbhde', kc * jnp.exp(g_last - g), vc)
    decay = jnp.exp(g_last[:, :, :, 0, :]).transpose(2, 0, 1, 3)

    def step(s, inp):
        d, dsn = inp
        return d[..., None] * s + dsn, s

    _, s_prev = lax.scan(step, jnp.zeros((b, h, dk, dv), F32), (decay, ds))
    o_inter = jnp.einsum('bhncd,nbhde->bhnce', q_fwd, s_prev)
    o = (o_intra + o_inter).transpose(0, 2, 3, 1, 4).reshape(b, l, h, dv)
    return o.astype(v.dtype)


def s5_ssm(u, a_re, a_im, log_dt, b_re, b_im, c_re, c_im, d_skip):
    bsz, l, _ = u.shape
    uf = u.astype(F32).reshape(bsz, l, S5_GROUPS, S5_GROUP_CH)
    lam_re = jnp.minimum(a_re.astype(F32), -1e-4)
    lam_im = a_im.astype(F32)
    dt = jnp.exp(log_dt.astype(F32))[:, None]
    mag = jnp.exp(lam_re * dt)
    ab_re = mag * jnp.cos(lam_im * dt)
    ab_im = mag * jnp.sin(lam_im * dt)
    den = lam_re * lam_re + lam_im * lam_im
    z_re = ((ab_re - 1.0) * lam_re + ab_im * lam_im) / den
    z_im = (ab_im * lam_re - (ab_re - 1.0) * lam_im) / den
    br, bi = b_re.astype(F32), b_im.astype(F32)
    bb_re = z_re[..., None] * br - z_im[..., None] * bi
    bb_im = z_re[..., None] * bi + z_im[..., None] * br
    bu_re = jnp.einsum('gph,blgh->blgp', bb_re, uf)
    bu_im = jnp.einsum('gph,blgh->blgp', bb_im, uf)

    def combine(e1, e2):
        a1r, a1i, b1r, b1i = e1
        a2r, a2i, b2r, b2i = e2
        return (a2r * a1r - a2i * a1i, a2r * a1i + a2i * a1r,
                a2r * b1r - a2i * b1i + b2r, a2r * b1i + a2i * b1r + b2i)

    shp = bu_re.shape
    _, _, x_re, x_im = lax.associative_scan(
        combine, (jnp.broadcast_to(ab_re, shp), jnp.broadcast_to(ab_im, shp), bu_re, bu_im), axis=1)
    y = (jnp.einsum('ghp,blgp->blgh', c_re.astype(F32), x_re)
         - jnp.einsum('ghp,blgp->blgh', c_im.astype(F32), x_im))
    y = y + d_skip.astype(F32).reshape(S5_GROUPS, S5_GROUP_CH) * uf
    return y.reshape(bsz, l, S5_WIDTH).astype(u.dtype)


def forgetting_attention(q, k, v, log_f):
    b, l, h, dh = q.shape
    fcum = jnp.cumsum(log_f, axis=1).transpose(0, 2, 1)
    qh, kh, vh = (t.transpose(0, 2, 1, 3) for t in (q, k, v))
    scale = dh ** -0.5
    neg = jnp.finfo(F32).min
    outs = []
    for i in range(l // Q_BLOCK):
        s0, s1 = i * Q_BLOCK, (i + 1) * Q_BLOCK
        logits = (jnp.einsum('bhqd,bhkd->bhqk', qh[:, :, s0:s1], kh[:, :, :s1]).astype(F32) * scale
                  + fcum[:, :, s0:s1, None] - fcum[:, :, None, :s1])
        mask = (s0 + jnp.arange(Q_BLOCK))[:, None] >= jnp.arange(s1)[None, :]
        p = jax.nn.softmax(jnp.where(mask, logits, neg), axis=-1)
        outs.append(jnp.einsum('bhqk,bhkd->bhqd', p.astype(vh.dtype), vh[:, :, :s1]))
    o = jnp.concatenate(outs, axis=2)
    return o.transpose(0, 2, 1, 3).reshape(b, l, h * dh)


def hybrid_mixer(h, w_in, gla_gate_w, gla_gate_b, gla_norm_g, w_gla_up,
                 s5_a_re, s5_a_im, s5_log_dt, s5_b_re, s5_b_im, s5_c_re, s5_c_im, s5_d,
                 s5_glu_w, s5_glu_b, w_s5_up, fox_f_b, w_fox_up, w_mix_out):
    b, l, _ = h.shape
    (gq, gk, gv, gr, gdown, su, fq, fk, fv, ff, gates) = split_cols(h @ w_in, IN_SIZES)
    log_a = jax.nn.log_sigmoid((gdown @ gla_gate_w + gla_gate_b).astype(F32)) / GLA_GATE_TAU
    o = gla_chunked(gq.reshape(b, l, GLA_HEADS, GLA_DK), gk.reshape(b, l, GLA_HEADS, GLA_DK),
                    gv.reshape(b, l, GLA_HEADS, GLA_DV), log_a.reshape(b, l, GLA_HEADS, GLA_DK))
    o = rms_norm(o, gla_norm_g.reshape(GLA_HEADS, GLA_DV)).reshape(b, l, GLA_V)
    gla_out = (o * jax.nn.silu(gr)) @ w_gla_up
    y = jax.nn.gelu(s5_ssm(su, s5_a_re, s5_a_im, s5_log_dt, s5_b_re, s5_b_im, s5_c_re, s5_c_im, s5_d))
    s5_out = (y * jax.nn.sigmoid(y @ s5_glu_w + s5_glu_b)) @ w_s5_up
    log_f = jax.nn.log_sigmoid(ff.astype(F32) + fox_f_b.astype(F32))
    fo = forgetting_attention(fq.reshape(b, l, FOX_HEADS, FOX_DH), fk.reshape(b, l, FOX_HEADS, FOX_DH),
                              fv.reshape(b, l, FOX_HEADS, FOX_DH), log_f)
    fox_out = fo @ w_fox_up
    g = jax.nn.sigmoid(gates.reshape(b, l, N_BRANCH, D_MODEL))
    mix = g[:, :, 0] * gla_out + g[:, :, 1] * s5_out + g[:, :, 2] * fox_out
    return mix @ w_mix_out


def memory_cross_attention(h, mem_n, w_q, w_kv, w_o):
    b, l, _ = h.shape
    m = mem_n.shape[1]
    q = (h @ w_q).reshape(b, l, XA_HEADS, XA_DH)
    k, v = jnp.split(mem_n @ w_kv, 2, axis=-1)
    k = k.reshape(b, m, XA_HEADS, XA_DH)
    v = v.reshape(b, m, XA_HEADS, XA_DH)
    logits = jnp.einsum('blhd,bmhd->bhlm', q, k).astype(F32) * (XA_DH ** -0.5)
    p = jax.nn.softmax(logits, axis=-1)
    o = jnp.einsum('bhlm,bmhd->blhd', p.astype(v.dtype), v).reshape(b, l, D_MODEL)
    return o @ w_o


def _normal(key, shape, scale):
    return jax.random.normal(key, shape, F32) * scale


def _gain(key, shape):
    return 1.0 + 0.02 * jax.random.normal(key, shape, F32)


def setup_inputs(seed: int = 0) -> dict:
    key = jax.random.key(seed)
    ks = list(jax.random.split(key, 48))
    L, D, P, G, HC = DEPTH, D_MODEL, S5_STATE, S5_GROUPS, S5_GROUP_CH
    inp = {}
    inp['x'] = _normal(ks[0], (BATCH, SEQ, D), 1.0)
    inp['mem'] = _normal(ks[1], (BATCH, N_MEM, D), 1.0)
    inp['ffn1_pre_g'] = _gain(ks[2], (L, D))
    inp['ffn1_w_gu'] = _normal(ks[3], (L, D, 2 * D_FF), D ** -0.5)
    inp['ffn1_w_down'] = _normal(ks[4], (L, D_FF, D), D_FF ** -0.5)
    inp['ffn1_post_g'] = _gain(ks[5], (L, D))
    inp['mix_pre_g'] = _gain(ks[6], (L, D))
    inp['w_in'] = _normal(ks[7], (L, D, D_IN), D ** -0.5)
    inp['gla_gate_w'] = _normal(ks[8], (L, GLA_GATE_RANK, GLA_QK), GLA_GATE_RANK ** -0.5)
    inp['gla_gate_b'] = _normal(ks[9], (L, GLA_QK), 0.1)
    inp['gla_norm_g'] = _gain(ks[10], (L, GLA_V))
    inp['w_gla_up'] = _normal(ks[11], (L, GLA_V, D), GLA_V ** -0.5)
    inp['s5_a_re'] = -0.5 + _normal(ks[12], (L, G, P), 0.01)
    inp['s5_a_im'] = jnp.pi * jnp.arange(P, dtype=F32) + _normal(ks[13], (L, G, P), 0.01)
    inp['s5_log_dt'] = jax.random.uniform(ks[14], (L, G), F32, math.log(S5_DT_MIN), math.log(S5_DT_MAX))
    inp['s5_b_re'] = _normal(ks[15], (L, G, P, HC), (HC ** -0.5) * math.sqrt(0.5))
    inp['s5_b_im'] = _normal(ks[16], (L, G, P, HC), (HC ** -0.5) * math.sqrt(0.5))
    inp['s5_c_re'] = _normal(ks[17], (L, G, HC, P), (P ** -0.5) * math.sqrt(0.5))
    inp['s5_c_im'] = _normal(ks[18], (L, G, HC, P), (P ** -0.5) * math.sqrt(0.5))
    inp['s5_d'] = _normal(ks[19], (L, S5_WIDTH), 1.0)
    inp['s5_glu_w'] = _normal(ks[20], (L, S5_WIDTH, S5_WIDTH), S5_WIDTH ** -0.5)
    inp['s5_glu_b'] = _normal(ks[21], (L, S5_WIDTH), 0.01)
    inp['w_s5_up'] = _normal(ks[22], (L, S5_WIDTH, D), S5_WIDTH ** -0.5)
    inp['fox_f_b'] = 2.0 + _normal(ks[23], (L, FOX_HEADS), 0.1)
    inp['w_fox_up'] = _normal(ks[24], (L, FOX_W, D), FOX_W ** -0.5)
    inp['w_mix_out'] = _normal(ks[25], (L, D, D), D ** -0.5)
    inp['mix_post_g'] = _gain(ks[26], (L, D))
    inp['xa_pre_g'] = _gain(ks[27], (L, D))
    inp['xa_mem_g'] = _gain(ks[28], (L, D))
    inp['xa_w_q'] = _normal(ks[29], (L, D, D), D ** -0.5)
    inp['xa_w_kv'] = _normal(ks[30], (L, D, 2 * D), D ** -0.5)
    inp['xa_w_o'] = _normal(ks[31], (L, D, D), D ** -0.5)
    inp['xa_post_g'] = _gain(ks[32], (L, D))
    inp['ffn2_pre_g'] = _gain(ks[33], (L, D))
    inp['ffn2_w_gu'] = _normal(ks[34], (L, D, 2 * D_FF), D ** -0.5)
    inp['ffn2_w_down'] = _normal(ks[35], (L, D_FF, D), D_FF ** -0.5)
    inp['ffn2_post_g'] = _gain(ks[36], (L, D))
    return inp


def reference(x, mem, ffn1_pre_g, ffn1_w_gu, ffn1_w_down, ffn1_post_g,
              mix_pre_g, w_in, gla_gate_w, gla_gate_b, gla_norm_g, w_gla_up,
              s5_a_re, s5_a_im, s5_log_dt, s5_b_re, s5_b_im, s5_c_re, s5_c_im, s5_d,
              s5_glu_w, s5_glu_b, w_s5_up, fox_f_b, w_fox_up, w_mix_out, mix_post_g,
              xa_pre_g, xa_mem_g, xa_w_q, xa_w_kv, xa_w_o, xa_post_g,
              ffn2_pre_g, ffn2_w_gu, ffn2_w_down, ffn2_post_g):
    for l in range(DEPTH):
        h = rms_norm(x, ffn1_pre_g[l])
        x = x + 0.5 * rms_norm(swiglu(h, ffn1_w_gu[l], ffn1_w_down[l]), ffn1_post_g[l])
        h = rms_norm(x, mix_pre_g[l])
        y = hybrid_mixer(h, w_in[l], gla_gate_w[l], gla_gate_b[l], gla_norm_g[l], w_gla_up[l],
                         s5_a_re[l], s5_a_im[l], s5_log_dt[l], s5_b_re[l], s5_b_im[l],
                         s5_c_re[l], s5_c_im[l], s5_d[l], s5_glu_w[l], s5_glu_b[l], w_s5_up[l],
                         fox_f_b[l], w_fox_up[l], w_mix_out[l])
        x = x + rms_norm(y, mix_post_g[l])
        h = rms_norm(x, xa_pre_g[l])
        mem_n = rms_norm(mem, xa_mem_g[l])
        x = x + rms_norm(memory_cross_attention(h, mem_n, xa_w_q[l], xa_w_kv[l], xa_w_o[l]), xa_post_g[l])
        h = rms_norm(x, ffn2_pre_g[l])
        x = x + 0.5 * rms_norm(swiglu(h, ffn2_w_gu[l], ffn2_w_down[l]), ffn2_post_g[l])
    return x
```

```python
import functools

import jax
import jax.numpy as jnp
from jax import lax
from jax.experimental import pallas as pl
from jax.experimental.pallas import tpu as pltpu

F32 = jnp.float32
BF16 = jnp.bfloat16
EPS = 1e-6

LANES = 128
VMEM_LIMIT_BYTES = 56 * 1024 * 1024

CHUNK = 64
GLA_HEADS, GLA_DK, GLA_DV, GLA_RANK, GLA_TAU = 4, 32, 64, 16, 16.0
GLA_QK = GLA_HEADS * GLA_DK
GLA_V = GLA_HEADS * GLA_DV
DK_SHIFT = GLA_DK.bit_length() - 1
DV_SHIFT = GLA_DV.bit_length() - 1
S5_GROUPS, S5_CH, S5_STATE = 16, 16, 64
S5_WIDTH = S5_GROUPS * S5_CH
S5_NSTATE = S5_GROUPS * S5_STATE
FOX_HEADS, FOX_DH = 8, 64
FOX_W = FOX_HEADS * FOX_DH
FOX_PAIRS = FOX_W // LANES
XA_HEADS = 4
N_BRANCH = 3

NT_DIMS = (((1,), (1,)), ((), ()))


def _params(*semantics):
    return pltpu.CompilerParams(dimension_semantics=semantics, vmem_limit_bytes=VMEM_LIMIT_BYTES)


def _rms(x, g):
    return x * lax.rsqrt(jnp.mean(x * x, axis=-1, keepdims=True) + EPS) * g


def _dot(a, b):
    return jnp.dot(a, b, preferred_element_type=F32)


def _dot_nt(a, b):
    return lax.dot_general(a, b, NT_DIMS, preferred_element_type=F32)


def _log_sigmoid(x):
    return jnp.minimum(x, 0.0) - jnp.log1p(jnp.exp(-jnp.abs(x)))


def _split3(x):
    hi = x.astype(BF16)
    r1 = x - hi.astype(F32)
    mid = r1.astype(BF16)
    lo = (r1 - mid.astype(F32)).astype(BF16)
    return hi, mid, lo


def _ffn_kernel(x_ref, pre_g_ref, wg_ref, wu_ref, wd_ref, post_g_ref, o_ref, h_sc, acc_sc):
    j = pl.program_id(1)

    @pl.when(j == 0)
    def _():
        h_sc[...] = _rms(x_ref[...], pre_g_ref[...]).astype(BF16)
        acc_sc[...] = jnp.zeros_like(acc_sc)

    h = h_sc[...]
    gate = _dot(h, wg_ref[...])
    up = _dot(h, wu_ref[...])
    a = (gate * jax.nn.sigmoid(gate) * up).astype(BF16)
    acc_sc[...] += _dot(a, wd_ref[...])

    @pl.when(j == pl.num_programs(1) - 1)
    def _():
        o_ref[...] = x_ref[...] + 0.5 * _rms(acc_sc[...], post_g_ref[...])


def _ffn(x, pre_g, w_gu, w_down, post_g, *, tm, tf):
    m, d = x.shape
    d_ff = w_down.shape[0]
    nf = d_ff // tf
    return pl.pallas_call(
        _ffn_kernel,
        grid=(m // tm, nf),
        in_specs=[
            pl.BlockSpec((tm, d), lambda i, j: (i, 0)),
            pl.BlockSpec((1, d), lambda i, j: (0, 0)),
            pl.BlockSpec((d, tf), lambda i, j: (0, j)),
            pl.BlockSpec((d, tf), lambda i, j: (0, j + nf)),
            pl.BlockSpec((tf, d), lambda i, j: (j, 0)),
            pl.BlockSpec((1, d), lambda i, j: (0, 0)),
        ],
        out_specs=pl.BlockSpec((tm, d), lambda i, j: (i, 0)),
        out_shape=jax.ShapeDtypeStruct((m, d), F32),
        scratch_shapes=[pltpu.VMEM((tm, d), BF16), pltpu.VMEM((tm, d), F32)],
        compiler_params=_params("parallel", "arbitrary"),
        name="ffn",
    )(x, pre_g, w_gu, w_gu, w_down, post_g)


IN_QK = 2 * GLA_QK
IN_VR = 2 * GLA_V
IN_SU = S5_WIDTH
IN_FOX = 3 * FOX_W
IN_SMALL = LANES
OFF_VR = IN_QK
OFF_SU = OFF_VR + IN_VR
OFF_FOX = OFF_SU + IN_SU
OFF_SMALL = OFF_FOX + IN_FOX
IN_TOTAL = OFF_SMALL + IN_SMALL


def _inproj_kernel(x_ref, g_ref, w_ref, qk_ref, vr_ref, su_ref, fox_ref, small_ref):
    h = _rms(x_ref[...], g_ref[...]).astype(BF16)
    qk_ref[...] = _dot(h, w_ref[:, 0:OFF_VR])
    vr_ref[...] = _dot(h, w_ref[:, OFF_VR:OFF_SU]).astype(BF16)
    su_ref[...] = _dot(h, w_ref[:, OFF_SU:OFF_FOX])
    fox_ref[:, 0:FOX_W] = (_dot(h, w_ref[:, OFF_FOX:OFF_FOX + FOX_W]) * (FOX_DH ** -0.5)).astype(BF16)
    fox_ref[:, FOX_W:IN_FOX] = _dot(h, w_ref[:, OFF_FOX + FOX_W:OFF_SMALL]).astype(BF16)
    small_ref[...] = _dot(h, w_ref[:, OFF_SMALL:IN_TOTAL])


def _inproj(x, g, w, *, bsz, seq, tm):
    m, d = x.shape
    nl = seq // tm
    row = lambda b, l: (b * nl + l, 0)
    return pl.pallas_call(
        _inproj_kernel,
        grid=(bsz, nl),
        in_specs=[
            pl.BlockSpec((tm, d), row),
            pl.BlockSpec((1, d), lambda b, l: (0, 0)),
            pl.BlockSpec((d, IN_TOTAL), lambda b, l: (0, 0)),
        ],
        out_specs=[
            pl.BlockSpec((tm, IN_QK), row),
            pl.BlockSpec((tm, IN_VR), row),
            pl.BlockSpec((tm, IN_SU), lambda b, l: (l, b)),
            pl.BlockSpec((tm, IN_FOX), row),
            pl.BlockSpec((tm, IN_SMALL), row),
        ],
        out_shape=[
            jax.ShapeDtypeStruct((m, IN_QK), F32),
            jax.ShapeDtypeStruct((m, IN_VR), BF16),
            jax.ShapeDtypeStruct((seq, bsz * IN_SU), F32),
            jax.ShapeDtypeStruct((m, IN_FOX), BF16),
            jax.ShapeDtypeStruct((m, IN_SMALL), F32),
        ],
        compiler_params=_params("parallel", "parallel"),
        name="inproj",
    )(x, g, w)


def _gla_kernel(qk_ref, vr_ref, small_ref, wg_ref, bg_ref, ng_ref, o_ref, la_sc, o_sc):
    seq = qk_ref.shape[0]
    n_chunks = seq // CHUNK
    c, hq, hv = CHUNK, GLA_QK, GLA_V

    z = _dot(small_ref[...].astype(BF16), wg_ref[...]) + bg_ref[...]
    la_sc[...] = _log_sigmoid(z) * (1.0 / GLA_TAU)

    row = lax.broadcasted_iota(jnp.int32, (c, c), 0)
    col = lax.broadcasted_iota(jnp.int32, (c, c), 1)
    tri = (col <= row).astype(BF16)
    row_x = lax.broadcasted_iota(jnp.int32, (GLA_HEADS * c, c), 0)
    col_x = lax.broadcasted_iota(jnp.int32, (GLA_HEADS * c, c), 1)
    lower = col_x <= (row_x & (c - 1))
    lane_q = lax.broadcasted_iota(jnp.int32, (c, hq), 1)
    lane_v = lax.broadcasted_iota(jnp.int32, (c, hv), 1)
    q_head = [(lane_q >> DK_SHIFT) == h for h in range(GLA_HEADS)]
    v_head = [(lane_v >> DV_SHIFT) == h for h in range(GLA_HEADS)]
    st_row = lax.broadcasted_iota(jnp.int32, (hv, hq), 0)
    st_col = lax.broadcasted_iota(jnp.int32, (hv, hq), 1)
    same_head = (st_row >> DV_SHIFT) == (st_col >> DK_SHIFT)

    def expand(t):
        return jnp.concatenate([jnp.where(q_head[h], t, 0.0) for h in range(GLA_HEADS)], axis=0).astype(BF16)

    def chunk(i, s_t):
        rows = pl.ds(pl.multiple_of(i * c, c), c)
        la = la_sc[rows, :]
        hi, mid, lo = _split3(la)
        g = _dot(tri, hi) + _dot(tri, mid) + _dot(tri, lo)
        g_last = g[c - 1:c, :]
        eg, ieg = jnp.exp(g), jnp.exp(-g)
        qc = qk_ref[rows, 0:hq] * (GLA_DK ** -0.5)
        kc = qk_ref[rows, hq:2 * hq]
        v = vr_ref[rows, 0:hv]
        q_fwd = qc * eg
        a_fwd = _dot_nt(expand(q_fwd), (kc * ieg).astype(BF16))
        a_bwd = _dot_nt(expand(qc * ieg), (kc * eg).astype(BF16))
        attn = jnp.where(lower, a_fwd, a_bwd).astype(BF16)
        p = _dot(attn, v)
        o = _dot_nt(q_fwd.astype(BF16), s_t.astype(BF16))
        for h in range(GLA_HEADS):
            o = o + jnp.where(v_head[h], p[h * c:(h + 1) * c, :], 0.0)
        o_sc[rows, :] = o
        kw = (kc * jnp.exp(g_last - g)).astype(BF16)
        ds_t = _dot(v.astype(F32).T.astype(BF16), kw)
        return s_t * jnp.exp(g_last) + jnp.where(same_head, ds_t, 0.0)

    lax.fori_loop(0, n_chunks, chunk, jnp.zeros((hv, hq), F32))

    o = o_sc[...]
    gr = lax.broadcasted_iota(jnp.int32, (hv, hv), 0) >> DV_SHIFT
    gc = lax.broadcasted_iota(jnp.int32, (hv, hv), 1) >> DV_SHIFT
    avg = jnp.where(gr == gc, 1.0 / GLA_DV, 0.0).astype(BF16)
    hi, mid, lo = _split3(o * o)
    ms = _dot(hi, avg) + _dot(mid, avg) + _dot(lo, avg)
    r = vr_ref[:, hv:2 * hv].astype(F32)
    o_ref[...] = (o * lax.rsqrt(ms + EPS) * ng_ref[...] * (r * jax.nn.sigmoid(r))).astype(BF16)


def _gla(qk, vr, small, wg, bg, ng, *, bsz, seq):
    m = qk.shape[0]
    row = lambda b: (b, 0)
    const = lambda b: (0, 0)
    return pl.pallas_call(
        _gla_kernel,
        grid=(bsz,),
        in_specs=[
            pl.BlockSpec((seq, IN_QK), row),
            pl.BlockSpec((seq, IN_VR), row),
            pl.BlockSpec((seq, IN_SMALL), row),
            pl.BlockSpec((IN_SMALL, GLA_QK), const),
            pl.BlockSpec((1, GLA_QK), const),
            pl.BlockSpec((1, GLA_V), const),
        ],
        out_specs=pl.BlockSpec((seq, GLA_V), row),
        out_shape=jax.ShapeDtypeStruct((m, GLA_V), BF16),
        scratch_shapes=[pltpu.VMEM((seq, GLA_QK), F32), pltpu.VMEM((seq, GLA_V), F32)],
        compiler_params=_params("parallel"),
        name="gla",
    )(qk, vr, small, wg, bg, ng)


def _foxprep_kernel(small_ref, b_ref, o_ref):
    seq = small_ref.shape[0]
    ff = small_ref[...].T[GLA_RANK:GLA_RANK + FOX_HEADS, :]
    x = _log_sigmoid(ff + b_ref[...])
    lane = lax.broadcasted_iota(jnp.int32, x.shape, 1)
    shift = 1
    while shift < seq:
        x = x + jnp.where(lane >= shift, pltpu.roll(x, shift, axis=1), 0.0)
        shift *= 2
    o_ref[...] = x


def _foxprep(small, bias, *, bsz, seq):
    return pl.pallas_call(
        _foxprep_kernel,
        grid=(bsz,),
        in_specs=[
            pl.BlockSpec((seq, IN_SMALL), lambda b: (b, 0)),
            pl.BlockSpec((FOX_HEADS, 1), lambda b: (0, 0)),
        ],
        out_specs=pl.BlockSpec((None, FOX_HEADS, seq), lambda b: (b, 0, 0)),
        out_shape=jax.ShapeDtypeStruct((bsz, FOX_HEADS, seq), F32),
        compiler_params=_params("parallel"),
        name="foxprep",
    )(small, bias)


def _fox_kernel(q_ref, k_ref, v_ref, fq_ref, fk_ref, o_ref, *, tq):
    qi = pl.program_id(2)
    q = q_ref[...]
    lane = lax.broadcasted_iota(jnp.int32, q.shape, 1)
    first = lane < FOX_DH
    qh = (jnp.where(first, q, jnp.zeros_like(q)), jnp.where(first, jnp.zeros_like(q), q))
    fq = fq_ref[...]
    row = lax.broadcasted_iota(jnp.int32, (tq, tq), 0)
    col = lax.broadcasted_iota(jnp.int32, (tq, tq), 1)
    causal = col <= row

    def tile(j, carry, masked):
        rows = pl.ds(pl.multiple_of(j * tq, tq), tq)
        k = k_ref[rows, :]
        v = v_ref[rows, :]
        fk = fk_ref[j]
        out = []
        for h in range(2):
            m_old, l_old, acc = carry[h]
            s = _dot_nt(qh[h], k) + (fq[:, h:h + 1] - fk[h:h + 1, :])
            if masked:
                s = jnp.where(causal, s, -jnp.inf)
            m_new = jnp.maximum(m_old, jnp.max(s, axis=1, keepdims=True))
            alpha = jnp.exp(m_old - m_new)
            p = jnp.exp(s - m_new)
            l_new = alpha * l_old + jnp.sum(p, axis=1, keepdims=True)
            out.append((m_new, l_new, alpha * acc + _dot(p.astype(BF16), v)))
        return tuple(out)

    init = tuple((jnp.full((tq, 1), -jnp.inf, F32), jnp.zeros((tq, 1), F32), jnp.zeros((tq, LANES), F32))
                 for _ in range(2))
    carry = tile(qi, init, True)
    carry = lax.fori_loop(0, qi, lambda j, cr: tile(j, cr, False), carry)
    (_, l0, a0), (_, l1, a1) = carry
    o_ref[...] = jnp.where(first, a0 / l0, a1 / l1).astype(BF16)


def _fox(fox, fq, fk, *, bsz, seq, tq):
    m = fox.shape[0]
    nq = seq // tq
    return pl.pallas_call(
        functools.partial(_fox_kernel, tq=tq),
        grid=(bsz, FOX_PAIRS, nq),
        in_specs=[
            pl.BlockSpec((tq, LANES), lambda b, p, i: (b * nq + i, p)),
            pl.BlockSpec((seq, LANES), lambda b, p, i: (b, FOX_PAIRS + p)),
            pl.BlockSpec((seq, LANES), lambda b, p, i: (b, 2 * FOX_PAIRS + p)),
            pl.BlockSpec((None, None, tq, 2), lambda b, p, i: (b, p, i, 0)),
            pl.BlockSpec((None, None, nq, 2, tq), lambda b, p, i: (b, p, 0, 0, 0)),
        ],
        out_specs=pl.BlockSpec((tq, LANES), lambda b, p, i: (b * nq + i, p)),
        out_shape=jax.ShapeDtypeStruct((m, FOX_W), BF16),
        compiler_params=_params("parallel", "parallel", "arbitrary"),
        name="fox",
    )(fox, fox, fox, fq, fk)


def _s5prep_kernel(are_ref, aim_ref, ldt_ref, bre_ref, bim_ref, abre_ref, abim_ref, bbre_ref, bbim_ref):
    lam_re = jnp.minimum(are_ref[...], -1e-4)
    lam_im = aim_ref[...]
    dt = jnp.exp(ldt_ref[...])
    mag = jnp.exp(lam_re * dt)
    ab_re = mag * jnp.cos(lam_im * dt)
    ab_im = mag * jnp.sin(lam_im * dt)
    den = lam_re * lam_re + lam_im * lam_im
    z_re = ((ab_re - 1.0) * lam_re + ab_im * lam_im) / den
    z_im = (ab_im * lam_re - (ab_re - 1.0) * lam_im) / den
    br, bi = bre_ref[...], bim_ref[...]
    abre_ref[...] = ab_re
    abim_ref[...] = ab_im
    bbre_ref[...] = z_re * br - z_im * bi
    bbim_ref[...] = z_re * bi + z_im * br


def _s5prep(a_re, a_im, log_dt, b_re, b_im):
    depth = a_re.shape[0]
    col = lambda a: a.reshape(depth, S5_NSTATE, 1)
    ldt = jnp.broadcast_to(log_dt[:, :, None], (depth, S5_GROUPS, S5_STATE))
    vec = pl.BlockSpec((None, S5_NSTATE, 1), lambda l: (l, 0, 0))
    mat = pl.BlockSpec((None, S5_NSTATE, S5_CH), lambda l: (l, 0, 0))
    return pl.pallas_call(
        _s5prep_kernel,
        grid=(depth,),
        in_specs=[vec, vec, vec, mat, mat],
        out_specs=[vec, vec, mat, mat],
        out_shape=[jax.ShapeDtypeStruct((depth, S5_NSTATE, 1), F32)] * 2
        + [jax.ShapeDtypeStruct((depth, S5_NSTATE, S5_CH), F32)] * 2,
        compiler_params=_params("parallel"),
        name="s5prep",
    )(col(a_re), col(a_im), col(ldt), b_re.reshape(depth, S5_NSTATE, S5_CH), b_im.reshape(depth, S5_NSTATE, S5_CH))


def _s5_kernel(u_ref, abre_ref, abim_ref, bre_ref, bim_ref, cre_ref, cim_ref, d_ref, gw_ref, gb_ref, o_ref,
               xre_sc, xim_sc, sre_sc, sim_sc, *, bsz, tb):
    @pl.when(pl.program_id(0) == 0)
    def _():
        sre_sc[...] = jnp.zeros_like(sre_sc)
        sim_sc[...] = jnp.zeros_like(sim_sc)

    u = u_ref[...]
    ub = u.astype(BF16)
    xre_sc[...] = _dot(ub, bre_ref[...])
    xim_sc[...] = _dot(ub, bim_ref[...])
    a_re = jnp.broadcast_to(abre_ref[...], (bsz, S5_NSTATE))
    a_im = jnp.broadcast_to(abim_ref[...], (bsz, S5_NSTATE))

    def step(t, carry):
        s_re, s_im = carry
        rows = pl.ds(pl.multiple_of(t * bsz, bsz), bsz)
        n_re = a_re * s_re - a_im * s_im + xre_sc[rows, :]
        n_im = a_re * s_im + a_im * s_re + xim_sc[rows, :]
        xre_sc[rows, :] = n_re
        xim_sc[rows, :] = n_im
        return n_re, n_im

    s_re, s_im = lax.fori_loop(0, tb, step, (sre_sc[...], sim_sc[...]), unroll=8)
    sre_sc[...] = s_re
    sim_sc[...] = s_im

    y = _dot(xre_sc[...].astype(BF16), cre_ref[...]) - _dot(xim_sc[...].astype(BF16), cim_ref[...])
    y = jax.nn.gelu(y + d_ref[...] * u, approximate=True)
    o_ref[...] = (y * jax.nn.sigmoid(_dot(y.astype(BF16), gw_ref[...]) + gb_ref[...])).astype(BF16)


def _s5(u, ab_re, ab_im, b_re, b_im, c_re, c_im, d, glu_w, glu_b, *, bsz, seq, tb):
    rows = tb * bsz
    const = lambda i: (0, 0)
    return pl.pallas_call(
        functools.partial(_s5_kernel, bsz=bsz, tb=tb),
        grid=(seq // tb,),
        in_specs=[
            pl.BlockSpec((rows, S5_WIDTH), lambda i: (i, 0)),
            pl.BlockSpec((1, S5_NSTATE), const),
            pl.BlockSpec((1, S5_NSTATE), const),
            pl.BlockSpec((S5_WIDTH, S5_NSTATE), const),
            pl.BlockSpec((S5_WIDTH, S5_NSTATE), const),
            pl.BlockSpec((S5_NSTATE, S5_WIDTH), const),
            pl.BlockSpec((S5_NSTATE, S5_WIDTH), const),
            pl.BlockSpec((1, S5_WIDTH), const),
            pl.BlockSpec((S5_WIDTH, S5_WIDTH), const),
            pl.BlockSpec((1, S5_WIDTH), const),
        ],
        out_specs=pl.BlockSpec((rows, S5_WIDTH), lambda i: (i, 0)),
        out_shape=jax.ShapeDtypeStruct((seq * bsz, S5_WIDTH), BF16),
        scratch_shapes=[pltpu.VMEM((rows, S5_NSTATE), F32), pltpu.VMEM((rows, S5_NSTATE), F32),
                        pltpu.VMEM((bsz, S5_NSTATE), F32), pltpu.VMEM((bsz, S5_NSTATE), F32)],
        compiler_params=_params("arbitrary"),
        name="s5",
    )(u, ab_re, ab_im, b_re, b_im, c_re, c_im, d, glu_w, glu_b)


def _merge_kernel(x_ref, gla_ref, s5_ref, fox_ref, pre_g_ref, wgate_ref, wgla_ref, ws5_ref, wfox_ref, wo_ref,
                  post_g_ref, o_ref):
    x = x_ref[...]
    d = x.shape[1]
    h = _rms(x, pre_g_ref[...]).astype(BF16)
    branches = ((gla_ref, wgla_ref), (s5_ref, ws5_ref), (fox_ref, wfox_ref))
    mix = None
    for n, (b_ref, w_ref) in enumerate(branches):
        gate = jax.nn.sigmoid(_dot(h, wgate_ref[:, n * d:(n + 1) * d]))
        term = gate * _dot(b_ref[...], w_ref[...])
        mix = term if mix is None else mix + term
    y = _dot(mix.astype(BF16), wo_ref[...])
    o_ref[...] = x + _rms(y, post_g_ref[...])


def _merge(x, gla_o, s5_o, fox_o, pre_g, w_gate, w_gla, w_s5, w_fox, w_o, post_g, *, bsz, seq, tm):
    m, d = x.shape
    nl = seq // tm
    row = lambda b, l: (b * nl + l, 0)
    const = lambda b, l: (0, 0)
    return pl.pallas_call(
        _merge_kernel,
        grid=(bsz, nl),
        in_specs=[
            pl.BlockSpec((tm, d), row),
            pl.BlockSpec((tm, GLA_V), row),
            pl.BlockSpec((tm, S5_WIDTH), lambda b, l: (l, b)),
            pl.BlockSpec((tm, FOX_W), row),
            pl.BlockSpec((1, d), const),
            pl.BlockSpec((d, N_BRANCH * d), const),
            pl.BlockSpec((GLA_V, d), const),
            pl.BlockSpec((S5_WIDTH, d), const),
            pl.BlockSpec((FOX_W, d), const),
            pl.BlockSpec((d, d), const),
            pl.BlockSpec((1, d), const),
        ],
        out_specs=pl.BlockSpec((tm, d), row),
        out_shape=jax.ShapeDtypeStruct((m, d), F32),
        compiler_params=_params("parallel", "parallel"),
        name="merge",
    )(x, gla_o, s5_o, fox_o, pre_g, w_gate, w_gla, w_s5, w_fox, w_o, post_g)


def _memkv_kernel(mem_ref, g_ref, w_ref, k_ref, v_ref):
    d = mem_ref.shape[1]
    h = _rms(mem_ref[...], g_ref[...]).astype(BF16)
    k_ref[...] = _dot(h, w_ref[:, 0:d]).astype(BF16)
    v_ref[...] = _dot(h, w_ref[:, d:2 * d]).astype(BF16)


def _memkv(mem, g, w_kv, *, tm):
    m, d = mem.shape
    row = lambda i: (i, 0)
    const = lambda i: (0, 0)
    return pl.pallas_call(
        _memkv_kernel,
        grid=(m // tm,),
        in_specs=[pl.BlockSpec((tm, d), row), pl.BlockSpec((1, d), const), pl.BlockSpec((d, 2 * d), const)],
        out_specs=[pl.BlockSpec((tm, d), row)] * 2,
        out_shape=[jax.ShapeDtypeStruct((m, d), BF16)] * 2,
        compiler_params=_params("parallel"),
        name="memkv",
    )(mem, g, w_kv)


def _xattn_kernel(x_ref, k_ref, v_ref, pre_g_ref, wq_ref, wo_ref, post_g_ref, o_ref):
    x = x_ref[...]
    d = x.shape[1]
    dh = d // XA_HEADS
    h = _rms(x, pre_g_ref[...]).astype(BF16)
    q = (_dot(h, wq_ref[...]) * (dh ** -0.5)).astype(BF16)
    heads = []
    for n in range(XA_HEADS):
        cols = slice(n * dh, (n + 1) * dh)
        s = _dot_nt(q[:, cols], k_ref[:, cols])
        e = jnp.exp(s - jnp.max(s, axis=1, keepdims=True))
        p = e / jnp.sum(e, axis=1, keepdims=True)
        heads.append(_dot(p.astype(BF16), v_ref[:, cols]).astype(BF16))
    y = _dot(jnp.concatenate(heads, axis=1), wo_ref[...])
    o_ref[...] = x + _rms(y, post_g_ref[...])


def _xattn(x, k, v, pre_g, w_q, w_o, post_g, *, bsz, seq, n_mem, tm):
    m, d = x.shape
    nl = seq // tm
    row = lambda b, l: (b * nl + l, 0)
    const = lambda b, l: (0, 0)
    return pl.pallas_call(
        _xattn_kernel,
        grid=(bsz, nl),
        in_specs=[
            pl.BlockSpec((tm, d), row),
            pl.BlockSpec((n_mem, d), lambda b, l: (b, 0)),
            pl.BlockSpec((n_mem, d), lambda b, l: (b, 0)),
            pl.BlockSpec((1, d), const),
            pl.BlockSpec((d, d), const),
            pl.BlockSpec((d, d), const),
            pl.BlockSpec((1, d), const),
        ],
        out_specs=pl.BlockSpec((tm, d), row),
        out_shape=jax.ShapeDtypeStruct((m, d), F32),
        compiler_params=_params("parallel", "parallel"),
        name="xattn",
    )(x, k, v, pre_g, w_q, w_o, post_g)


def _tile(n, want):
    t = min(n, want)
    while n % t:
        t -= 1
    return t


def _block_diag(t):
    depth, g, r, c = t.shape
    eye = jnp.eye(g, dtype=t.dtype)
    return jnp.einsum("lgrc,gk->lgrkc", t, eye).reshape(depth, g * r, g * c)


def kernel(x, mem, ffn1_pre_g, ffn1_w_gu, ffn1_w_down, ffn1_post_g, mix_pre_g, w_in, gla_gate_w, gla_gate_b, gla_norm_g, w_gla_up, s5_a_re, s5_a_im, s5_log_dt, s5_b_re, s5_b_im, s5_c_re, s5_c_im, s5_d, s5_glu_w, s5_glu_b, w_s5_up, fox_f_b, w_fox_up, w_mix_out, mix_post_g, xa_pre_g, xa_mem_g, xa_w_q, xa_w_kv, xa_w_o, xa_post_g, ffn2_pre_g, ffn2_w_gu, ffn2_w_down, ffn2_post_g):
    bsz, seq, d = x.shape
    n_mem = mem.shape[1]
    depth = w_in.shape[0]
    d_ff = ffn1_w_down.shape[1]
    m = bsz * seq
    assert seq % CHUNK == 0 and d % LANES == 0

    tm_ffn = _tile(m, 1024)
    tf = _tile(d_ff // LANES, 2) * LANES
    tm = _tile(seq, 512)
    tq = _tile(seq, 256)
    tb = _tile(seq, 128)

    bf = lambda a: a.astype(BF16)
    vec = lambda a: a.reshape(depth, 1, a.shape[-1])

    sizes = (GLA_QK, GLA_QK, GLA_V, GLA_V, GLA_RANK, S5_WIDTH, FOX_W, FOX_W, FOX_W, FOX_HEADS, N_BRANCH * d)
    offs = [0]
    for s in sizes:
        offs.append(offs[-1] + s)
    seg = lambda i: w_in[:, :, offs[i]:offs[i + 1]]
    pad = jnp.zeros((depth, d, IN_SMALL - GLA_RANK - FOX_HEADS), w_in.dtype)
    w_proj = bf(jnp.concatenate([seg(0), seg(1), seg(2), seg(3), seg(5), seg(6), seg(7), seg(8), seg(4), seg(9), pad],
                                axis=2))
    w_gate = bf(seg(10))
    gla_wg = bf(jnp.concatenate([gla_gate_w, jnp.zeros((depth, IN_SMALL - GLA_RANK, GLA_QK), gla_gate_w.dtype)],
                                axis=1))

    ab_re, ab_im, bb_re, bb_im = _s5prep(s5_a_re, s5_a_im, s5_log_dt, s5_b_re, s5_b_im)
    grp = lambda t: t.reshape(depth, S5_GROUPS, S5_STATE, S5_CH)
    s5_bre = bf(_block_diag(grp(bb_re).transpose(0, 1, 3, 2)))
    s5_bim = bf(_block_diag(grp(bb_im).transpose(0, 1, 3, 2)))
    s5_cre = bf(_block_diag(s5_c_re.transpose(0, 1, 3, 2)))
    s5_cim = bf(_block_diag(s5_c_im.transpose(0, 1, 3, 2)))
    ab_re = ab_re.reshape(depth, 1, S5_NSTATE)
    ab_im = ab_im.reshape(depth, 1, S5_NSTATE)

    w = dict(
        ffn1_gu=bf(ffn1_w_gu), ffn1_down=bf(ffn1_w_down), ffn2_gu=bf(ffn2_w_gu), ffn2_down=bf(ffn2_w_down),
        gla_up=bf(w_gla_up), s5_up=bf(w_s5_up), fox_up=bf(w_fox_up), mix_out=bf(w_mix_out),
        glu=bf(s5_glu_w), xa_q=bf(xa_w_q), xa_kv=bf(xa_w_kv), xa_o=bf(xa_w_o),
    )
    g = dict(
        ffn1_pre=vec(ffn1_pre_g), ffn1_post=vec(ffn1_post_g), mix_pre=vec(mix_pre_g), mix_post=vec(mix_post_g),
        xa_pre=vec(xa_pre_g), xa_mem=vec(xa_mem_g), xa_post=vec(xa_post_g),
        ffn2_pre=vec(ffn2_pre_g), ffn2_post=vec(ffn2_post_g),
        gla_b=vec(gla_gate_b), gla_n=vec(gla_norm_g), s5_d=vec(s5_d), glu_b=vec(s5_glu_b),
    )
    fox_b = fox_f_b.reshape(depth, FOX_HEADS, 1)

    xs = x.reshape(m, d)
    mem2 = mem.reshape(bsz * n_mem, d)
    for l in range(depth):
        xs = _ffn(xs, g["ffn1_pre"][l], w["ffn1_gu"][l], w["ffn1_down"][l], g["ffn1_post"][l], tm=tm_ffn, tf=tf)

        qk, vr, su, fox, small = _inproj(xs, g["mix_pre"][l], w_proj[l], bsz=bsz, seq=seq, tm=tm)
        gla_o = _gla(qk, vr, small, gla_wg[l], g["gla_b"][l], g["gla_n"][l], bsz=bsz, seq=seq)
        fcum = _foxprep(small, fox_b[l], bsz=bsz, seq=seq)
        fq = fcum.reshape(bsz, FOX_PAIRS, 2, seq).transpose(0, 1, 3, 2)
        fk = fcum.reshape(bsz, FOX_PAIRS, 2, seq // tq, tq).transpose(0, 1, 3, 2, 4)
        fox_o = _fox(fox, fq, fk, bsz=bsz, seq=seq, tq=tq)
        s5_o = _s5(su.reshape(seq * bsz, S5_WIDTH), ab_re[l], ab_im[l], s5_bre[l], s5_bim[l], s5_cre[l], s5_cim[l],
                   g["s5_d"][l], w["glu"][l], g["glu_b"][l], bsz=bsz, seq=seq, tb=tb)
        xs = _merge(xs, gla_o, s5_o.reshape(seq, bsz * S5_WIDTH), fox_o, g["mix_pre"][l], w_gate[l], w["gla_up"][l],
                    w["s5_up"][l], w["fox_up"][l], w["mix_out"][l], g["mix_post"][l], bsz=bsz, seq=seq, tm=tm)

        mk, mv = _memkv(mem2, g["xa_mem"][l], w["xa_kv"][l], tm=_tile(bsz * n_mem, 512))
        xs = _xattn(xs, mk, mv, g["xa_pre"][l], w["xa_q"][l], w["xa_o"][l], g["xa_post"][l],
                    bsz=bsz, seq=seq, n_mem=n_mem, tm=tm)

        xs = _ffn(xs, g["ffn2_pre"][l], w["ffn2_gu"][l], w["ffn2_down"][l], g["ffn2_post"][l], tm=tm_ffn, tf=tf)
    return xs.reshape(bsz, seq, d)
```

```python
import functools
import math

import jax
import jax.numpy as jnp
from jax import lax
from jax.experimental import pallas as pl
from jax.experimental.pallas import tpu as pltpu

F32 = jnp.float32
BF16 = jnp.bfloat16
EPS = 1e-6
LOG2E = math.log2(math.e)

LANES = 128
VMEM_LIMIT_BYTES = 56 * 1024 * 1024

CHUNK = 64
GLA_HEADS, GLA_DK, GLA_DV, GLA_RANK, GLA_TAU = 4, 32, 64, 16, 16.0
GLA_QK = GLA_HEADS * GLA_DK
GLA_V = GLA_HEADS * GLA_DV
DK_SHIFT = GLA_DK.bit_length() - 1
DV_SHIFT = GLA_DV.bit_length() - 1
S5_GROUPS, S5_CH, S5_STATE = 16, 16, 64
S5_WIDTH = S5_GROUPS * S5_CH
S5_NSTATE = S5_GROUPS * S5_STATE
FOX_HEADS, FOX_DH = 8, 64
FOX_W = FOX_HEADS * FOX_DH
FOX_PAIRS = FOX_W // LANES
XA_HEADS = 4
N_BRANCH = 3

NT_DIMS = (((1,), (1,)), ((), ()))


def _params(*semantics):
    return pltpu.CompilerParams(dimension_semantics=semantics, vmem_limit_bytes=VMEM_LIMIT_BYTES)


def _layer_spec(layer, block, index_map=None):
    block = tuple(block)
    if index_map is None:
        index_map = lambda *_: (0,) * len(block)
    return pl.BlockSpec((None,) + block, lambda *ids: (layer,) + tuple(index_map(*ids)))


def _rms(x, g):
    return x * lax.rsqrt(jnp.mean(x * x, axis=-1, keepdims=True) + EPS) * g


def _dot(a, b):
    return jnp.dot(a, b, preferred_element_type=F32)


def _dot_nt(a, b):
    return lax.dot_general(a, b, NT_DIMS, preferred_element_type=F32)


def _log_sigmoid(x):
    return jnp.minimum(x, 0.0) - jnp.log1p(jnp.exp(-jnp.abs(x)))


def _split3(x):
    hi = x.astype(BF16)
    r1 = x - hi.astype(F32)
    mid = r1.astype(BF16)
    lo = (r1 - mid.astype(F32)).astype(BF16)
    return hi, mid, lo


def _ffn_kernel(x_ref, pre_g_ref, wg_ref, wu_ref, wd_ref, post_g_ref, o_ref, h_sc, acc_sc):
    j = pl.program_id(1)

    @pl.when(j == 0)
    def _():
        h_sc[...] = _rms(x_ref[...], pre_g_ref[...]).astype(BF16)
        acc_sc[...] = jnp.zeros_like(acc_sc)

    h = h_sc[...]
    gate = _dot(h, wg_ref[...])
    up = _dot(h, wu_ref[...])
    a = (gate * jax.nn.sigmoid(gate) * up).astype(BF16)
    acc_sc[...] += _dot(a, wd_ref[...])

    @pl.when(j == pl.num_programs(1) - 1)
    def _():
        o_ref[...] = x_ref[...] + 0.5 * _rms(acc_sc[...], post_g_ref[...])


def _ffn(layer, x, pre_g, w_gu, w_down, post_g, *, tm, tf):
    m, d = x.shape
    d_ff = w_down.shape[1]
    nf = d_ff // tf
    return pl.pallas_call(
        _ffn_kernel,
        grid=(m // tm, nf),
        in_specs=[
            pl.BlockSpec((tm, d), lambda i, j: (i, 0)),
            _layer_spec(layer, (1, d)),
            _layer_spec(layer, (d, tf), lambda i, j: (0, j)),
            _layer_spec(layer, (d, tf), lambda i, j: (0, j + nf)),
            _layer_spec(layer, (tf, d), lambda i, j: (j, 0)),
            _layer_spec(layer, (1, d)),
        ],
        out_specs=pl.BlockSpec((tm, d), lambda i, j: (i, 0)),
        out_shape=jax.ShapeDtypeStruct((m, d), F32),
        scratch_shapes=[pltpu.VMEM((tm, d), BF16), pltpu.VMEM((tm, d), F32)],
        compiler_params=_params("parallel", "arbitrary"),
        name="ffn",
    )(x, pre_g, w_gu, w_gu, w_down, post_g)


IN_QK = 2 * GLA_QK
IN_VR = 2 * GLA_V
IN_SU = S5_WIDTH
IN_FOX = 3 * FOX_W
IN_SMALL = LANES
OFF_VR = IN_QK
OFF_SU = OFF_VR + IN_VR
OFF_FOX = OFF_SU + IN_SU
OFF_SMALL = OFF_FOX + IN_FOX
IN_TOTAL = OFF_SMALL + IN_SMALL


def _inproj_kernel(x_ref, g_ref, w_ref, qk_ref, vr_ref, su_ref, fox_ref, small_ref):
    h = _rms(x_ref[...], g_ref[...]).astype(BF16)
    qk_ref[...] = _dot(h, w_ref[:, 0:OFF_VR])
    vr_ref[...] = _dot(h, w_ref[:, OFF_VR:OFF_SU]).astype(BF16)
    su_ref[...] = _dot(h, w_ref[:, OFF_SU:OFF_FOX])
    fox_ref[:, 0:FOX_W] = (_dot(h, w_ref[:, OFF_FOX:OFF_FOX + FOX_W]) * (FOX_DH ** -0.5 * LOG2E)).astype(BF16)
    fox_ref[:, FOX_W:IN_FOX] = _dot(h, w_ref[:, OFF_FOX + FOX_W:OFF_SMALL]).astype(BF16)
    small_ref[...] = _dot(h, w_ref[:, OFF_SMALL:IN_TOTAL])


def _inproj(layer, x, g, w, *, bsz, seq, tm):
    m, d = x.shape
    nl = seq // tm
    row = lambda b, l: (b * nl + l, 0)
    return pl.pallas_call(
        _inproj_kernel,
        grid=(bsz, nl),
        in_specs=[
            pl.BlockSpec((tm, d), row),
            _layer_spec(layer, (1, d)),
            _layer_spec(layer, (d, IN_TOTAL)),
        ],
        out_specs=[
            pl.BlockSpec((tm, IN_QK), row),
            pl.BlockSpec((tm, IN_VR), row),
            pl.BlockSpec((tm, IN_SU), lambda b, l: (l, b)),
            pl.BlockSpec((tm, IN_FOX), row),
            pl.BlockSpec((tm, IN_SMALL), row),
        ],
        out_shape=[
            jax.ShapeDtypeStruct((m, IN_QK), F32),
            jax.ShapeDtypeStruct((m, IN_VR), BF16),
            jax.ShapeDtypeStruct((seq, bsz * IN_SU), F32),
            jax.ShapeDtypeStruct((m, IN_FOX), BF16),
            jax.ShapeDtypeStruct((m, IN_SMALL), F32),
        ],
        compiler_params=_params("parallel", "parallel"),
        name="inproj",
    )(x, g, w)


def _gla_kernel(qk_ref, vr_ref, small_ref, wg_ref, bg_ref, ng_ref, o_ref, la_sc, o_sc):
    seq = qk_ref.shape[0]
    n_chunks = seq // CHUNK
    c, hq, hv = CHUNK, GLA_QK, GLA_V

    z = _dot(small_ref[...].astype(BF16), wg_ref[...]) + bg_ref[...]
    la_sc[...] = _log_sigmoid(z) * (1.0 / GLA_TAU)

    row = lax.broadcasted_iota(jnp.int32, (c, c), 0)
    col = lax.broadcasted_iota(jnp.int32, (c, c), 1)
    tri = (col <= row).astype(BF16)
    row_x = lax.broadcasted_iota(jnp.int32, (GLA_HEADS * c, c), 0)
    col_x = lax.broadcasted_iota(jnp.int32, (GLA_HEADS * c, c), 1)
    lower = col_x <= (row_x & (c - 1))
    lane_q = lax.broadcasted_iota(jnp.int32, (c, hq), 1)
    lane_v = lax.broadcasted_iota(jnp.int32, (c, hv), 1)
    q_head = [(lane_q >> DK_SHIFT) == h for h in range(GLA_HEADS)]
    v_head = [(lane_v >> DV_SHIFT) == h for h in range(GLA_HEADS)]
    st_row = lax.broadcasted_iota(jnp.int32, (hv, hq), 0)
    st_col = lax.broadcasted_iota(jnp.int32, (hv, hq), 1)
    same_head = (st_row >> DV_SHIFT) == (st_col >> DK_SHIFT)

    def expand(t):
        return jnp.concatenate([jnp.where(q_head[h], t, 0.0) for h in range(GLA_HEADS)], axis=0).astype(BF16)

    def chunk(i, s_t):
        rows = pl.ds(pl.multiple_of(i * c, c), c)
        la = la_sc[rows, :]
        hi, mid, lo = _split3(la)
        g = _dot(tri, hi) + _dot(tri, mid) + _dot(tri, lo)
        g_last = g[c - 1:c, :]
        eg, ieg = jnp.exp(g), jnp.exp(-g)
        qc = qk_ref[rows, 0:hq] * (GLA_DK ** -0.5)
        kc = qk_ref[rows, hq:2 * hq]
        v = vr_ref[rows, 0:hv]
        q_fwd = qc * eg
        a_fwd = _dot_nt(expand(q_fwd), (kc * ieg).astype(BF16))
        a_bwd = _dot_nt(expand(qc * ieg), (kc * eg).astype(BF16))
        attn = jnp.where(lower, a_fwd, a_bwd).astype(BF16)
        p = _dot(attn, v)
        o = _dot_nt(q_fwd.astype(BF16), s_t.astype(BF16))
        for h in range(GLA_HEADS):
            o = o + jnp.where(v_head[h], p[h * c:(h + 1) * c, :], 0.0)
        o_sc[rows, :] = o
        kw = (kc * jnp.exp(g_last - g)).astype(BF16)
        ds_t = _dot(v.astype(F32).T.astype(BF16), kw)
        return s_t * jnp.exp(g_last) + jnp.where(same_head, ds_t, 0.0)

    lax.fori_loop(0, n_chunks, chunk, jnp.zeros((hv, hq), F32))

    o = o_sc[...]
    gr = lax.broadcasted_iota(jnp.int32, (hv, hv), 0) >> DV_SHIFT
    gc = lax.broadcasted_iota(jnp.int32, (hv, hv), 1) >> DV_SHIFT
    avg = jnp.where(gr == gc, 1.0 / GLA_DV, 0.0).astype(BF16)
    hi, mid, lo = _split3(o * o)
    ms = _dot(hi, avg) + _dot(mid, avg) + _dot(lo, avg)
    r = vr_ref[:, hv:2 * hv].astype(F32)
    o_ref[...] = (o * lax.rsqrt(ms + EPS) * ng_ref[...] * (r * jax.nn.sigmoid(r))).astype(BF16)


def _gla(layer, qk, vr, small, wg, bg, ng, *, bsz, seq):
    m = qk.shape[0]
    row = lambda b: (b, 0)
    return pl.pallas_call(
        _gla_kernel,
        grid=(bsz,),
        in_specs=[
            pl.BlockSpec((seq, IN_QK), row),
            pl.BlockSpec((seq, IN_VR), row),
            pl.BlockSpec((seq, IN_SMALL), row),
            _layer_spec(layer, (IN_SMALL, GLA_QK)),
            _layer_spec(layer, (1, GLA_QK)),
            _layer_spec(layer, (1, GLA_V)),
        ],
        out_specs=pl.BlockSpec((seq, GLA_V), row),
        out_shape=jax.ShapeDtypeStruct((m, GLA_V), BF16),
        scratch_shapes=[pltpu.VMEM((seq, GLA_QK), F32), pltpu.VMEM((seq, GLA_V), F32)],
        compiler_params=_params("parallel"),
        name="gla",
    )(qk, vr, small, wg, bg, ng)


def _foxprep_kernel(small_ref, b_ref, o_ref):
    seq = small_ref.shape[0]
    ff = small_ref[...].T[GLA_RANK:GLA_RANK + FOX_HEADS, :]
    x = _log_sigmoid(ff + b_ref[...])
    lane = lax.broadcasted_iota(jnp.int32, x.shape, 1)
    shift = 1
    while shift < seq:
        x = x + jnp.where(lane >= shift, pltpu.roll(x, shift, axis=1), 0.0)
        shift *= 2
    o_ref[...] = x


def _foxprep(layer, small, bias, *, bsz, seq):
    return pl.pallas_call(
        _foxprep_kernel,
        grid=(bsz,),
        in_specs=[
            pl.BlockSpec((seq, IN_SMALL), lambda b: (b, 0)),
            _layer_spec(layer, (FOX_HEADS, 1)),
        ],
        out_specs=pl.BlockSpec((None, FOX_HEADS, seq), lambda b: (b, 0, 0)),
        out_shape=jax.ShapeDtypeStruct((bsz, FOX_HEADS, seq), F32),
        compiler_params=_params("parallel"),
        name="foxprep",
    )(small, bias)


FOX_BIAS_TERMS = 3


def _fox_side_lanes(f_col, lane, base, *, key):
    hi, mid, lo = (t.astype(F32) for t in _split3(f_col * LOG2E))
    parts = (-hi, -mid, -lo) if key else (hi, mid, lo)
    ones_at = base if key else base + FOX_BIAS_TERMS
    parts_at = base + FOX_BIAS_TERMS if key else base
    out = jnp.where((lane >= ones_at) & (lane < ones_at + FOX_BIAS_TERMS), 1.0, 0.0)
    for n, part in enumerate(parts):
        out = jnp.where(lane == parts_at + n, part, out)
    return out


def _fox_kernel(q_ref, k_ref, v_ref, f_ref, o_ref, kx_sc, vx_sc, *, tq, tk):
    qi = pl.program_id(2)
    seq = k_ref.shape[0]
    spare = (FOX_DH, 0)

    @pl.when(qi == 0)
    def _():
        lane = lax.broadcasted_iota(jnp.int32, (seq, LANES), 1)
        k, v = k_ref[...], v_ref[...]
        for h in range(2):
            own = (lane < FOX_DH) if h == 0 else (lane >= FOX_DH)
            side = _fox_side_lanes(f_ref[:, h:h + 1], lane, spare[h], key=True).astype(BF16)
            kx_sc[h] = jnp.where(own, k, side)
            vx_sc[h] = jnp.where(own, v, jnp.ones_like(v))

    q = q_ref[...]
    lane = lax.broadcasted_iota(jnp.int32, (tq, LANES), 1)
    f_rows = f_ref[pl.ds(pl.multiple_of(qi * tq, tq), tq), :]
    qx = []
    for h in range(2):
        own = (lane < FOX_DH) if h == 0 else (lane >= FOX_DH)
        side = _fox_side_lanes(f_rows[:, h:h + 1], lane, spare[h], key=False).astype(BF16)
        qx.append(jnp.where(own, q, side))

    def step(carry, col0, width, masked):
        cols = pl.ds(pl.multiple_of(col0, width), width)
        out = []
        for h in range(2):
            m_old, acc = carry[h]
            s = _dot_nt(qx[h], kx_sc[h, cols, :])
            if masked:
                row = lax.broadcasted_iota(jnp.int32, (tq, width), 0)
                col = lax.broadcasted_iota(jnp.int32, (tq, width), 1)
                s = jnp.where(col <= row, s, -jnp.inf)
            m_new = jnp.maximum(m_old, jnp.max(s, axis=1, keepdims=True))
            p = jnp.exp2(s - m_new).astype(BF16)
            out.append((m_new, jnp.exp2(m_old - m_new) * acc + _dot(p, vx_sc[h, cols, :])))
        return tuple(out)

    init = tuple((jnp.full((tq, 1), -jnp.inf, F32), jnp.zeros((tq, LANES), F32)) for _ in range(2))
    carry = step(init, qi * tq, tq, True)
    per = tk // tq
    carry = lax.fori_loop(0, qi // per, lambda j, cr: step(cr, j * tk, tk, False), carry)
    for r in range(1, per):
        carry = lax.cond(qi % per >= r, lambda cr, r=r: step(cr, (qi - r) * tq, tq, False), lambda cr: cr, carry)
    (_, a0), (_, a1) = carry
    o0 = a0 / pltpu.roll(a0, FOX_DH, axis=1)
    o1 = a1 / pltpu.roll(a1, FOX_DH, axis=1)
    o_ref[...] = jnp.where(lane < FOX_DH, o0, o1).astype(BF16)


def _fox(fox, f, *, bsz, seq, tq, tk):
    m = fox.shape[0]
    nq = seq // tq
    return pl.pallas_call(
        functools.partial(_fox_kernel, tq=tq, tk=tk),
        grid=(bsz, FOX_PAIRS, nq),
        in_specs=[
            pl.BlockSpec((tq, LANES), lambda b, p, i: (b * nq + i, p)),
            pl.BlockSpec((seq, LANES), lambda b, p, i: (b, FOX_PAIRS + p)),
            pl.BlockSpec((seq, LANES), lambda b, p, i: (b, 2 * FOX_PAIRS + p)),
            pl.BlockSpec((None, None, seq, 2), lambda b, p, i: (b, p, 0, 0)),
        ],
        out_specs=pl.BlockSpec((tq, LANES), lambda b, p, i: (b * nq + i, p)),
        out_shape=jax.ShapeDtypeStruct((m, FOX_W), BF16),
        scratch_shapes=[pltpu.VMEM((2, seq, LANES), BF16), pltpu.VMEM((2, seq, LANES), BF16)],
        compiler_params=_params("parallel", "parallel", "arbitrary"),
        name="fox",
    )(fox, fox, fox, f)


def _s5prep_kernel(are_ref, aim_ref, ldt_ref, bre_ref, bim_ref, abre_ref, abim_ref, bbre_ref, bbim_ref):
    lam_re = jnp.minimum(are_ref[...], -1e-4)
    lam_im = aim_ref[...]
    dt = jnp.exp(ldt_ref[...])
    mag = jnp.exp(lam_re * dt)
    ab_re = mag * jnp.cos(lam_im * dt)
    ab_im = mag * jnp.sin(lam_im * dt)
    den = lam_re * lam_re + lam_im * lam_im
    z_re = ((ab_re - 1.0) * lam_re + ab_im * lam_im) / den
    z_im = (ab_im * lam_re - (ab_re - 1.0) * lam_im) / den
    br, bi = bre_ref[...], bim_ref[...]
    abre_ref[...] = ab_re
    abim_ref[...] = ab_im
    bbre_ref[...] = z_re * br - z_im * bi
    bbim_ref[...] = z_re * bi + z_im * br


def _s5prep(a_re, a_im, log_dt, b_re, b_im):
    depth = a_re.shape[0]
    col = lambda a: a.reshape(depth, S5_NSTATE, 1)
    ldt = jnp.broadcast_to(log_dt[:, :, None], (depth, S5_GROUPS, S5_STATE))
    vec = pl.BlockSpec((None, S5_NSTATE, 1), lambda l: (l, 0, 0))
    mat = pl.BlockSpec((None, S5_NSTATE, S5_CH), lambda l: (l, 0, 0))
    return pl.pallas_call(
        _s5prep_kernel,
        grid=(depth,),
        in_specs=[vec, vec, vec, mat, mat],
        out_specs=[vec, vec, mat, mat],
        out_shape=[jax.ShapeDtypeStruct((depth, S5_NSTATE, 1), F32)] * 2
        + [jax.ShapeDtypeStruct((depth, S5_NSTATE, S5_CH), F32)] * 2,
        compiler_params=_params("parallel"),
        name="s5prep",
    )(col(a_re), col(a_im), col(ldt), b_re.reshape(depth, S5_NSTATE, S5_CH), b_im.reshape(depth, S5_NSTATE, S5_CH))


def _s5_kernel(u_ref, abre_ref, abim_ref, bre_ref, bim_ref, cre_ref, cim_ref, d_ref, gw_ref, gb_ref, o_ref,
               xre_sc, xim_sc, sre_sc, sim_sc, *, bsz, tb):
    @pl.when(pl.program_id(0) == 0)
    def _():
        sre_sc[...] = jnp.zeros_like(sre_sc)
        sim_sc[...] = jnp.zeros_like(sim_sc)

    u = u_ref[...]
    ub = u.astype(BF16)
    xre_sc[...] = _dot(ub, bre_ref[...])
    xim_sc[...] = _dot(ub, bim_ref[...])
    a_re = jnp.broadcast_to(abre_ref[...], (bsz, S5_NSTATE))
    a_im = jnp.broadcast_to(abim_ref[...], (bsz, S5_NSTATE))

    def step(t, carry):
        s_re, s_im = carry
        rows = pl.ds(pl.multiple_of(t * bsz, bsz), bsz)
        n_re = a_re * s_re - a_im * s_im + xre_sc[rows, :]
        n_im = a_re * s_im + a_im * s_re + xim_sc[rows, :]
        xre_sc[rows, :] = n_re
        xim_sc[rows, :] = n_im
        return n_re, n_im

    s_re, s_im = lax.fori_loop(0, tb, step, (sre_sc[...], sim_sc[...]), unroll=8)
    sre_sc[...] = s_re
    sim_sc[...] = s_im

    y = _dot(xre_sc[...].astype(BF16), cre_ref[...]) - _dot(xim_sc[...].astype(BF16), cim_ref[...])
    y = jax.nn.gelu(y + d_ref[...] * u, approximate=True)
    o_ref[...] = (y * jax.nn.sigmoid(_dot(y.astype(BF16), gw_ref[...]) + gb_ref[...])).astype(BF16)


def _s5(layer, u, ab_re, ab_im, b_re, b_im, c_re, c_im, d, glu_w, glu_b, *, bsz, seq, tb):
    rows = tb * bsz
    return pl.pallas_call(
        functools.partial(_s5_kernel, bsz=bsz, tb=tb),
        grid=(seq // tb,),
        in_specs=[
            pl.BlockSpec((rows, S5_WIDTH), lambda i: (i, 0)),
            _layer_spec(layer, (1, S5_NSTATE)),
            _layer_spec(layer, (1, S5_NSTATE)),
            _layer_spec(layer, (S5_WIDTH, S5_NSTATE)),
            _layer_spec(layer, (S5_WIDTH, S5_NSTATE)),
            _layer_spec(layer, (S5_NSTATE, S5_WIDTH)),
            _layer_spec(layer, (S5_NSTATE, S5_WIDTH)),
            _layer_spec(layer, (1, S5_WIDTH)),
            _layer_spec(layer, (S5_WIDTH, S5_WIDTH)),
            _layer_spec(layer, (1, S5_WIDTH)),
        ],
        out_specs=pl.BlockSpec((rows, S5_WIDTH), lambda i: (i, 0)),
        out_shape=jax.ShapeDtypeStruct((seq * bsz, S5_WIDTH), BF16),
        scratch_shapes=[pltpu.VMEM((rows, S5_NSTATE), F32), pltpu.VMEM((rows, S5_NSTATE), F32),
                        pltpu.VMEM((bsz, S5_NSTATE), F32), pltpu.VMEM((bsz, S5_NSTATE), F32)],
        compiler_params=_params("arbitrary"),
        name="s5",
    )(u, ab_re, ab_im, b_re, b_im, c_re, c_im, d, glu_w, glu_b)


def _merge_kernel(x_ref, gla_ref, s5_ref, fox_ref, pre_g_ref, wgate_ref, wgla_ref, ws5_ref, wfox_ref, wo_ref,
                  post_g_ref, o_ref):
    x = x_ref[...]
    d = x.shape[1]
    h = _rms(x, pre_g_ref[...]).astype(BF16)
    branches = ((gla_ref, wgla_ref), (s5_ref, ws5_ref), (fox_ref, wfox_ref))
    mix = None
    for n, (b_ref, w_ref) in enumerate(branches):
        gate = jax.nn.sigmoid(_dot(h, wgate_ref[:, n * d:(n + 1) * d]))
        term = gate * _dot(b_ref[...], w_ref[...])
        mix = term if mix is None else mix + term
    y = _dot(mix.astype(BF16), wo_ref[...])
    o_ref[...] = x + _rms(y, post_g_ref[...])


def _merge(layer, x, gla_o, s5_o, fox_o, pre_g, w_gate, w_gla, w_s5, w_fox, w_o, post_g, *, bsz, seq, tm):
    m, d = x.shape
    nl = seq // tm
    row = lambda b, l: (b * nl + l, 0)
    return pl.pallas_call(
        _merge_kernel,
        grid=(bsz, nl),
        in_specs=[
            pl.BlockSpec((tm, d), row),
            pl.BlockSpec((tm, GLA_V), row),
            pl.BlockSpec((tm, S5_WIDTH), lambda b, l: (l, b)),
            pl.BlockSpec((tm, FOX_W), row),
            _layer_spec(layer, (1, d)),
            _layer_spec(layer, (d, N_BRANCH * d)),
            _layer_spec(layer, (GLA_V, d)),
            _layer_spec(layer, (S5_WIDTH, d)),
            _layer_spec(layer, (FOX_W, d)),
            _layer_spec(layer, (d, d)),
            _layer_spec(layer, (1, d)),
        ],
        out_specs=pl.BlockSpec((tm, d), row),
        out_shape=jax.ShapeDtypeStruct((m, d), F32),
        compiler_params=_params("parallel", "parallel"),
        name="merge",
    )(x, gla_o, s5_o, fox_o, pre_g, w_gate, w_gla, w_s5, w_fox, w_o, post_g)


def _memkv_kernel(mem_ref, g_ref, w_ref, k_ref, v_ref):
    d = mem_ref.shape[1]
    h = _rms(mem_ref[...], g_ref[...]).astype(BF16)
    k_ref[...] = _dot(h, w_ref[:, 0:d]).astype(BF16)
    v_ref[...] = _dot(h, w_ref[:, d:2 * d]).astype(BF16)


def _memkv(layer, mem, g, w_kv, *, tm):
    m, d = mem.shape
    row = lambda i: (i, 0)
    return pl.pallas_call(
        _memkv_kernel,
        grid=(m // tm,),
        in_specs=[pl.BlockSpec((tm, d), row), _layer_spec(layer, (1, d)), _layer_spec(layer, (d, 2 * d))],
        out_specs=[pl.BlockSpec((tm, d), row)] * 2,
        out_shape=[jax.ShapeDtypeStruct((m, d), BF16)] * 2,
        compiler_params=_params("parallel"),
        name="memkv",
    )(mem, g, w_kv)


def _xattn_kernel(x_ref, k_ref, v_ref, pre_g_ref, wq_ref, wo_ref, post_g_ref, o_ref):
    x = x_ref[...]
    d = x.shape[1]
    dh = d // XA_HEADS
    h = _rms(x, pre_g_ref[...]).astype(BF16)
    q = (_dot(h, wq_ref[...]) * (dh ** -0.5)).astype(BF16)
    heads = []
    for n in range(XA_HEADS):
        cols = slice(n * dh, (n + 1) * dh)
        s = _dot_nt(q[:, cols], k_ref[:, cols])
        e = jnp.exp(s - jnp.max(s, axis=1, keepdims=True))
        p = e / jnp.sum(e, axis=1, keepdims=True)
        heads.append(_dot(p.astype(BF16), v_ref[:, cols]).astype(BF16))
    y = _dot(jnp.concatenate(heads, axis=1), wo_ref[...])
    o_ref[...] = x + _rms(y, post_g_ref[...])


def _xattn(layer, x, k, v, pre_g, w_q, w_o, post_g, *, bsz, seq, n_mem, tm):
    m, d = x.shape
    nl = seq // tm
    row = lambda b, l: (b * nl + l, 0)
    return pl.pallas_call(
        _xattn_kernel,
        grid=(bsz, nl),
        in_specs=[
            pl.BlockSpec((tm, d), row),
            pl.BlockSpec((n_mem, d), lambda b, l: (b, 0)),
            pl.BlockSpec((n_mem, d), lambda b, l: (b, 0)),
            _layer_spec(layer, (1, d)),
            _layer_spec(layer, (d, d)),
            _layer_spec(layer, (d, d)),
            _layer_spec(layer, (1, d)),
        ],
        out_specs=pl.BlockSpec((tm, d), row),
        out_shape=jax.ShapeDtypeStruct((m, d), F32),
        compiler_params=_params("parallel", "parallel"),
        name="xattn",
    )(x, k, v, pre_g, w_q, w_o, post_g)


def _tile(n, want):
    t = min(n, want)
    while n % t:
        t -= 1
    return t


def _block_diag(t):
    depth, g, r, c = t.shape
    eye = jnp.eye(g, dtype=t.dtype)
    return jnp.einsum("lgrc,gk->lgrkc", t, eye).reshape(depth, g * r, g * c)


def kernel(x, mem, ffn1_pre_g, ffn1_w_gu, ffn1_w_down, ffn1_post_g, mix_pre_g, w_in, gla_gate_w, gla_gate_b, gla_norm_g, w_gla_up, s5_a_re, s5_a_im, s5_log_dt, s5_b_re, s5_b_im, s5_c_re, s5_c_im, s5_d, s5_glu_w, s5_glu_b, w_s5_up, fox_f_b, w_fox_up, w_mix_out, mix_post_g, xa_pre_g, xa_mem_g, xa_w_q, xa_w_kv, xa_w_o, xa_post_g, ffn2_pre_g, ffn2_w_gu, ffn2_w_down, ffn2_post_g):
    bsz, seq, d = x.shape
    n_mem = mem.shape[1]
    depth = w_in.shape[0]
    d_ff = ffn1_w_down.shape[1]
    m = bsz * seq
    assert seq % CHUNK == 0 and d % LANES == 0

    tm_ffn = _tile(m, 1024)
    tf = _tile(d_ff // LANES, 2) * LANES
    tm = _tile(seq, 512)
    tq = _tile(seq, 256)
    tk = 2 * tq if seq % (2 * tq) == 0 else tq
    tb = _tile(seq, 128)

    bf = lambda a: a.astype(BF16)
    vec = lambda a: a.reshape(depth, 1, a.shape[-1])

    sizes = (GLA_QK, GLA_QK, GLA_V, GLA_V, GLA_RANK, S5_WIDTH, FOX_W, FOX_W, FOX_W, FOX_HEADS, N_BRANCH * d)
    offs = [0]
    for s in sizes:
        offs.append(offs[-1] + s)
    seg = lambda i: bf(w_in[:, :, offs[i]:offs[i + 1]])
    pad = jnp.zeros((depth, d, IN_SMALL - GLA_RANK - FOX_HEADS), BF16)
    w_proj = jnp.concatenate([seg(0), seg(1), seg(2), seg(3), seg(5), seg(6), seg(7), seg(8), seg(4), seg(9), pad],
                             axis=2)
    w_gate = seg(10)
    gla_wg = bf(jnp.concatenate([gla_gate_w, jnp.zeros((depth, IN_SMALL - GLA_RANK, GLA_QK), gla_gate_w.dtype)],
                                axis=1))

    ab_re, ab_im, bb_re, bb_im = _s5prep(s5_a_re, s5_a_im, s5_log_dt, s5_b_re, s5_b_im)
    grp = lambda t: t.reshape(depth, S5_GROUPS, S5_STATE, S5_CH)
    s5_bre = bf(_block_diag(grp(bb_re).transpose(0, 1, 3, 2)))
    s5_bim = bf(_block_diag(grp(bb_im).transpose(0, 1, 3, 2)))
    s5_cre = bf(_block_diag(s5_c_re.transpose(0, 1, 3, 2)))
    s5_cim = bf(_block_diag(s5_c_im.transpose(0, 1, 3, 2)))
    ab_re = ab_re.reshape(depth, 1, S5_NSTATE)
    ab_im = ab_im.reshape(depth, 1, S5_NSTATE)

    ffn1_gu, ffn1_down, ffn2_gu, ffn2_down = bf(ffn1_w_gu), bf(ffn1_w_down), bf(ffn2_w_gu), bf(ffn2_w_down)
    gla_up, s5_up, fox_up, mix_out = bf(w_gla_up), bf(w_s5_up), bf(w_fox_up), bf(w_mix_out)
    glu_w, xa_q, xa_kv, xa_o = bf(s5_glu_w), bf(xa_w_q), bf(xa_w_kv), bf(xa_w_o)
    fox_b = fox_f_b.reshape(depth, FOX_HEADS, 1)

    xs = x.reshape(m, d)
    mem2 = mem.reshape(bsz * n_mem, d)
    for l in range(depth):
        xs = _ffn(l, xs, vec(ffn1_pre_g), ffn1_gu, ffn1_down, vec(ffn1_post_g), tm=tm_ffn, tf=tf)

        qk, vr, su, fox, small = _inproj(l, xs, vec(mix_pre_g), w_proj, bsz=bsz, seq=seq, tm=tm)
        gla_o = _gla(l, qk, vr, small, gla_wg, vec(gla_gate_b), vec(gla_norm_g), bsz=bsz, seq=seq)
        fcum = _foxprep(l, small, fox_b, bsz=bsz, seq=seq)
        f_cols = fcum.reshape(bsz, FOX_PAIRS, 2, seq).transpose(0, 1, 3, 2)
        fox_o = _fox(fox, f_cols, bsz=bsz, seq=seq, tq=tq, tk=tk)
        s5_o = _s5(l, su.reshape(seq * bsz, S5_WIDTH), ab_re, ab_im, s5_bre, s5_bim, s5_cre, s5_cim,
                   vec(s5_d), glu_w, vec(s5_glu_b), bsz=bsz, seq=seq, tb=tb)
        xs = _merge(l, xs, gla_o, s5_o.reshape(seq, bsz * S5_WIDTH), fox_o, vec(mix_pre_g), w_gate, gla_up,
                    s5_up, fox_up, mix_out, vec(mix_post_g), bsz=bsz, seq=seq, tm=tm)

        mk, mv = _memkv(l, mem2, vec(xa_mem_g), xa_kv, tm=_tile(bsz * n_mem, 512))
        xs = _xattn(l, xs, mk, mv, vec(xa_pre_g), xa_q, xa_o, vec(xa_post_g), bsz=bsz, seq=seq, n_mem=n_mem, tm=tm)

        xs = _ffn(l, xs, vec(ffn2_pre_g), ffn2_gu, ffn2_down, vec(ffn2_post_g), tm=tm_ffn, tf=tf)
    return xs.reshape(bsz, seq, d)
```

```python
import functools
import math

import jax
import jax.numpy as jnp
from jax import lax
from jax.experimental import pallas as pl
from jax.experimental.pallas import tpu as pltpu

F32 = jnp.float32
BF16 = jnp.bfloat16
EPS = 1e-6
LOG2E = math.log2(math.e)

LANES = 128
VMEM_LIMIT_BYTES = 56 * 1024 * 1024

CHUNK = 64
GLA_HEADS, GLA_DK, GLA_DV, GLA_RANK, GLA_TAU = 4, 32, 64, 16, 16.0
GLA_QK = GLA_HEADS * GLA_DK
GLA_V = GLA_HEADS * GLA_DV
DK_SHIFT = GLA_DK.bit_length() - 1
DV_SHIFT = GLA_DV.bit_length() - 1
S5_GROUPS, S5_CH, S5_STATE = 16, 16, 64
S5_WIDTH = S5_GROUPS * S5_CH
S5_NSTATE = S5_GROUPS * S5_STATE
FOX_HEADS, FOX_DH = 8, 64
FOX_W = FOX_HEADS * FOX_DH
FOX_PAIRS = FOX_W // LANES
XA_HEADS = 4
N_BRANCH = 3

NT_DIMS = (((1,), (1,)), ((), ()))


def _params(*semantics):
    return pltpu.CompilerParams(dimension_semantics=semantics, vmem_limit_bytes=VMEM_LIMIT_BYTES)


def _layer_spec(layer, block, index_map=None, pipeline_mode=None):
    block = tuple(block)
    if index_map is None:
        index_map = lambda *_: (0,) * len(block)
    return pl.BlockSpec((None,) + block, lambda *ids: (layer,) + tuple(index_map(*ids)),
                        pipeline_mode=pipeline_mode)


def _rms(x, g):
    return x * lax.rsqrt(jnp.mean(x * x, axis=-1, keepdims=True) + EPS) * g


def _dot(a, b):
    return jnp.dot(a, b, preferred_element_type=F32)


def _dot_nt(a, b):
    return lax.dot_general(a, b, NT_DIMS, preferred_element_type=F32)


def _log_sigmoid(x):
    return jnp.minimum(x, 0.0) - jnp.log1p(jnp.exp(-jnp.abs(x)))


def _split3(x):
    hi = x.astype(BF16)
    r1 = x - hi.astype(F32)
    mid = r1.astype(BF16)
    lo = (r1 - mid.astype(F32)).astype(BF16)
    return hi, mid, lo


def _ffn_kernel(x_ref, pre_g_ref, wgu_ref, wd_ref, post_g_ref, o_ref, *, tf):
    d_ff = wd_ref.shape[0]
    x = x_ref[...]
    h = _rms(x, pre_g_ref[...]).astype(BF16)
    acc = None
    for c0 in range(0, d_ff, tf):
        gate = _dot(h, wgu_ref[:, c0:c0 + tf])
        up = _dot(h, wgu_ref[:, d_ff + c0:d_ff + c0 + tf])
        a = (gate * jax.nn.sigmoid(gate) * up).astype(BF16)
        part = _dot(a, wd_ref[c0:c0 + tf, :])
        acc = part if acc is None else acc + part
    o_ref[...] = x + 0.5 * _rms(acc, post_g_ref[...])


def _ffn(layer, x, pre_g, w_gu, w_down, post_g, *, tm, tf):
    m, d = x.shape
    d_ff = w_down.shape[1]
    resident = pl.Buffered(1)
    return pl.pallas_call(
        functools.partial(_ffn_kernel, tf=tf),
        grid=(m // tm,),
        in_specs=[
            pl.BlockSpec((tm, d), lambda i: (i, 0)),
            _layer_spec(layer, (1, d)),
            _layer_spec(layer, (d, 2 * d_ff), pipeline_mode=resident),
            _layer_spec(layer, (d_ff, d), pipeline_mode=resident),
            _layer_spec(layer, (1, d)),
        ],
        out_specs=pl.BlockSpec((tm, d), lambda i: (i, 0)),
        out_shape=jax.ShapeDtypeStruct((m, d), F32),
        compiler_params=_params("parallel"),
        name="ffn",
    )(x, pre_g, w_gu, w_down, post_g)


IN_QK = 2 * GLA_QK
IN_VR = 2 * GLA_V
IN_SU = S5_WIDTH
IN_FOX = 3 * FOX_W
IN_SMALL = LANES
OFF_VR = IN_QK
OFF_SU = OFF_VR + IN_VR
OFF_FOX = OFF_SU + IN_SU
OFF_SMALL = OFF_FOX + IN_FOX
IN_TOTAL = OFF_SMALL + IN_SMALL


def _inproj_kernel(x_ref, g_ref, w_ref, qk_ref, vr_ref, su_ref, fox_ref, small_ref):
    h = _rms(x_ref[...], g_ref[...]).astype(BF16)
    qk_ref[...] = _dot(h, w_ref[:, 0:OFF_VR])
    vr_ref[...] = _dot(h, w_ref[:, OFF_VR:OFF_SU]).astype(BF16)
    su_ref[...] = _dot(h, w_ref[:, OFF_SU:OFF_FOX])
    fox_ref[:, 0:FOX_W] = (_dot(h, w_ref[:, OFF_FOX:OFF_FOX + FOX_W]) * (FOX_DH ** -0.5 * LOG2E)).astype(BF16)
    fox_ref[:, FOX_W:IN_FOX] = _dot(h, w_ref[:, OFF_FOX + FOX_W:OFF_SMALL]).astype(BF16)
    small_ref[...] = _dot(h, w_ref[:, OFF_SMALL:IN_TOTAL])


def _inproj(layer, x, g, w, *, bsz, seq, tm):
    m, d = x.shape
    nl = seq // tm
    row = lambda b, l: (b * nl + l, 0)
    return pl.pallas_call(
        _inproj_kernel,
        grid=(bsz, nl),
        in_specs=[
            pl.BlockSpec((tm, d), row),
            _layer_spec(layer, (1, d)),
            _layer_spec(layer, (d, IN_TOTAL)),
        ],
        out_specs=[
            pl.BlockSpec((tm, IN_QK), row),
            pl.BlockSpec((tm, IN_VR), row),
            pl.BlockSpec((tm, IN_SU), lambda b, l: (l, b)),
            pl.BlockSpec((tm, IN_FOX), row),
            pl.BlockSpec((tm, IN_SMALL), row),
        ],
        out_shape=[
            jax.ShapeDtypeStruct((m, IN_QK), F32),
            jax.ShapeDtypeStruct((m, IN_VR), BF16),
            jax.ShapeDtypeStruct((seq, bsz * IN_SU), F32),
            jax.ShapeDtypeStruct((m, IN_FOX), BF16),
            jax.ShapeDtypeStruct((m, IN_SMALL), F32),
        ],
        compiler_params=_params("parallel", "parallel"),
        name="inproj",
    )(x, g, w)


def _gla_kernel(qk_ref, vr_ref, small_ref, wg_ref, bg_ref, ng_ref, o_ref, la_sc, o_sc):
    seq = qk_ref.shape[0]
    n_chunks = seq // CHUNK
    c, hq, hv = CHUNK, GLA_QK, GLA_V

    z = _dot(small_ref[...].astype(BF16), wg_ref[...]) + bg_ref[...]
    la_sc[...] = _log_sigmoid(z) * (1.0 / GLA_TAU)

    row = lax.broadcasted_iota(jnp.int32, (c, c), 0)
    col = lax.broadcasted_iota(jnp.int32, (c, c), 1)
    tri = (col <= row).astype(BF16)
    row_x = lax.broadcasted_iota(jnp.int32, (GLA_HEADS * c, c), 0)
    col_x = lax.broadcasted_iota(jnp.int32, (GLA_HEADS * c, c), 1)
    lower = col_x <= (row_x & (c - 1))
    lane_q = lax.broadcasted_iota(jnp.int32, (c, hq), 1)
    lane_v = lax.broadcasted_iota(jnp.int32, (c, hv), 1)
    q_head = [(lane_q >> DK_SHIFT) == h for h in range(GLA_HEADS)]
    v_head = [(lane_v >> DV_SHIFT) == h for h in range(GLA_HEADS)]
    st_row = lax.broadcasted_iota(jnp.int32, (hv, hq), 0)
    st_col = lax.broadcasted_iota(jnp.int32, (hv, hq), 1)
    same_head = (st_row >> DV_SHIFT) == (st_col >> DK_SHIFT)

    def expand(t):
        return jnp.concatenate([jnp.where(q_head[h], t, 0.0) for h in range(GLA_HEADS)], axis=0).astype(BF16)

    def chunk(i, s_t):
        rows = pl.ds(pl.multiple_of(i * c, c), c)
        la = la_sc[rows, :]
        hi, mid, lo = _split3(la)
        g = _dot(tri, hi) + _dot(tri, mid) + _dot(tri, lo)
        g_last = g[c - 1:c, :]
        eg, ieg = jnp.exp(g), jnp.exp(-g)
        qc = qk_ref[rows, 0:hq] * (GLA_DK ** -0.5)
        kc = qk_ref[rows, hq:2 * hq]
        v = vr_ref[rows, 0:hv]
        q_fwd = qc * eg
        a_fwd = _dot_nt(expand(q_fwd), (kc * ieg).astype(BF16))
        a_bwd = _dot_nt(expand(qc * ieg), (kc * eg).astype(BF16))
        attn = jnp.where(lower, a_fwd, a_bwd).astype(BF16)
        p = _dot(attn, v)
        o = _dot_nt(q_fwd.astype(BF16), s_t.astype(BF16))
        for h in range(GLA_HEADS):
            o = o + jnp.where(v_head[h], p[h * c:(h + 1) * c, :], 0.0)
        o_sc[rows, :] = o
        kw = (kc * jnp.exp(g_last - g)).astype(BF16)
        ds_t = _dot(v.astype(F32).T.astype(BF16), kw)
        return s_t * jnp.exp(g_last) + jnp.where(same_head, ds_t, 0.0)

    lax.fori_loop(0, n_chunks, chunk, jnp.zeros((hv, hq), F32))

    o = o_sc[...]
    gr = lax.broadcasted_iota(jnp.int32, (hv, hv), 0) >> DV_SHIFT
    gc = lax.broadcasted_iota(jnp.int32, (hv, hv), 1) >> DV_SHIFT
    avg = jnp.where(gr == gc, 1.0 / GLA_DV, 0.0).astype(BF16)
    hi, mid, lo = _split3(o * o)
    ms = _dot(hi, avg) + _dot(mid, avg) + _dot(lo, avg)
    r = vr_ref[:, hv:2 * hv].astype(F32)
    o_ref[...] = (o * lax.rsqrt(ms + EPS) * ng_ref[...] * (r * jax.nn.sigmoid(r))).astype(BF16)


def _gla(layer, qk, vr, small, wg, bg, ng, *, bsz, seq):
    m = qk.shape[0]
    row = lambda b: (b, 0)
    return pl.pallas_call(
        _gla_kernel,
        grid=(bsz,),
        in_specs=[
            pl.BlockSpec((seq, IN_QK), row),
            pl.BlockSpec((seq, IN_VR), row),
            pl.BlockSpec((seq, IN_SMALL), row),
            _layer_spec(layer, (IN_SMALL, GLA_QK)),
            _layer_spec(layer, (1, GLA_QK)),
            _layer_spec(layer, (1, GLA_V)),
        ],
        out_specs=pl.BlockSpec((seq, GLA_V), row),
        out_shape=jax.ShapeDtypeStruct((m, GLA_V), BF16),
        scratch_shapes=[pltpu.VMEM((seq, GLA_QK), F32), pltpu.VMEM((seq, GLA_V), F32)],
        compiler_params=_params("parallel"),
        name="gla",
    )(qk, vr, small, wg, bg, ng)


def _foxprep_kernel(small_ref, b_ref, o_ref):
    seq = small_ref.shape[0]
    ff = small_ref[...].T[GLA_RANK:GLA_RANK + FOX_HEADS, :]
    x = _log_sigmoid(ff + b_ref[...])
    lane = lax.broadcasted_iota(jnp.int32, x.shape, 1)
    shift = 1
    while shift < seq:
        x = x + jnp.where(lane >= shift, pltpu.roll(x, shift, axis=1), 0.0)
        shift *= 2
    hi, mid, lo = (t.astype(F32) for t in _split3(x * LOG2E))
    o_ref[...] = jnp.concatenate([hi, mid, lo, jnp.ones_like(x)], axis=0)


def _foxprep(layer, small, bias, *, bsz, seq):
    return pl.pallas_call(
        _foxprep_kernel,
        grid=(bsz,),
        in_specs=[
            pl.BlockSpec((seq, IN_SMALL), lambda b: (b, 0)),
            _layer_spec(layer, (FOX_HEADS, 1)),
        ],
        out_specs=pl.BlockSpec((None, FOX_TERMS * FOX_HEADS, seq), lambda b: (b, 0, 0)),
        out_shape=jax.ShapeDtypeStruct((bsz, FOX_TERMS * FOX_HEADS, seq), F32),
        compiler_params=_params("parallel"),
        name="foxprep",
    )(small, bias)


FOX_TERMS = 4
FOX_SIDE = 16


def _fox_placement(h, base, *, key):
    r = lax.broadcasted_iota(jnp.int32, (FOX_SIDE, LANES), 0)
    rel = lax.broadcasted_iota(jnp.int32, (FOX_SIDE, LANES), 1) - base
    term = r - FOX_TERMS * h
    is_part = (term >= 0) & (term < FOX_TERMS - 1)
    is_one = term == FOX_TERMS - 1
    n = FOX_TERMS - 1
    if key:
        m = jnp.where(is_part & (rel == term + n), -1.0, jnp.where(is_one & (rel >= 0) & (rel < n), 1.0, 0.0))
    else:
        m = jnp.where(is_part & (rel == term), 1.0, jnp.where(is_one & (rel >= n) & (rel < 2 * n), 1.0, 0.0))
    return m.astype(BF16)


def _fox_kernel(q_ref, k_ref, v_ref, f_ref, o_ref, qx_sc, kx_sc, vx_sc, *, tq, tk):
    seq = k_ref.shape[0]
    spare = (FOX_DH, 0)
    lane = lax.broadcasted_iota(jnp.int32, (seq, LANES), 1)
    q, k, v, f = q_ref[...], k_ref[...], v_ref[...], f_ref[...]
    for h in range(2):
        own = (lane < FOX_DH) if h == 0 else (lane >= FOX_DH)
        qx_sc[h] = jnp.where(own, q, _dot(f, _fox_placement(h, spare[h], key=False)).astype(BF16))
        kx_sc[h] = jnp.where(own, k, _dot(f, _fox_placement(h, spare[h], key=True)).astype(BF16))
        vx_sc[h] = jnp.where(own, v, jnp.ones_like(v))

    row = lax.broadcasted_iota(jnp.int32, (tq, tq), 0)
    col = lax.broadcasted_iota(jnp.int32, (tq, tq), 1)
    causal = col <= row
    first = lax.broadcasted_iota(jnp.int32, (tq, LANES), 1) < FOX_DH

    def step(carry, h, rows, col0, width, masked):
        m_old, acc = carry
        cols = slice(col0, col0 + width)
        s = _dot_nt(qx_sc[h, rows, :], kx_sc[h, cols, :])
        if masked:
            s = jnp.where(causal, s, -jnp.inf)
        m_new = jnp.maximum(m_old, jnp.max(s, axis=1, keepdims=True))
        p = jnp.exp2(s - m_new).astype(BF16)
        return m_new, jnp.exp2(m_old - m_new) * acc + _dot(p, vx_sc[h, cols, :])

    per = tk // tq
    for qi in range(seq // tq):
        rows = slice(qi * tq, (qi + 1) * tq)
        tiles = [(qi * tq, tq, True)] + [(j * tk, tk, False) for j in range(qi // per)]
        tiles += [((qi - r) * tq, tq, False) for r in range(1, qi % per + 1)]
        outs = []
        for h in range(2):
            carry = (jnp.full((tq, 1), -jnp.inf, F32), jnp.zeros((tq, LANES), F32))
            for col0, width, masked in tiles:
                carry = step(carry, h, rows, col0, width, masked)
            acc = carry[1]
            outs.append(acc / pltpu.roll(acc, FOX_DH, axis=1))
        o_ref[rows, :] = jnp.where(first, outs[0], outs[1]).astype(BF16)


def _fox(fox, f, *, bsz, seq, tq, tk):
    m = fox.shape[0]
    return pl.pallas_call(
        functools.partial(_fox_kernel, tq=tq, tk=tk),
        grid=(bsz, FOX_PAIRS),
        in_specs=[
            pl.BlockSpec((seq, LANES), lambda b, p: (b, p)),
            pl.BlockSpec((seq, LANES), lambda b, p: (b, FOX_PAIRS + p)),
            pl.BlockSpec((seq, LANES), lambda b, p: (b, 2 * FOX_PAIRS + p)),
            pl.BlockSpec((None, None, seq, FOX_SIDE), lambda b, p: (b, p, 0, 0)),
        ],
        out_specs=pl.BlockSpec((seq, LANES), lambda b, p: (b, p)),
        out_shape=jax.ShapeDtypeStruct((m, FOX_W), BF16),
        scratch_shapes=[pltpu.VMEM((2, seq, LANES), BF16)] * 3,
        compiler_params=_params("parallel", "parallel"),
        name="fox",
    )(fox, fox, fox, f)


def _s5prep_kernel(are_ref, aim_ref, ldt_ref, bre_ref, bim_ref, abre_ref, abim_ref, bbre_ref, bbim_ref):
    lam_re = jnp.minimum(are_ref[...], -1e-4)
    lam_im = aim_ref[...]
    dt = jnp.exp(ldt_ref[...])
    mag = jnp.exp(lam_re * dt)
    ab_re = mag * jnp.cos(lam_im * dt)
    ab_im = mag * jnp.sin(lam_im * dt)
    den = lam_re * lam_re + lam_im * lam_im
    z_re = ((ab_re - 1.0) * lam_re + ab_im * lam_im) / den
    z_im = (ab_im * lam_re - (ab_re - 1.0) * lam_im) / den
    br, bi = bre_ref[...], bim_ref[...]
    abre_ref[...] = ab_re
    abim_ref[...] = ab_im
    bbre_ref[...] = z_re * br - z_im * bi
    bbim_ref[...] = z_re * bi + z_im * br


def _s5prep(a_re, a_im, log_dt, b_re, b_im):
    depth = a_re.shape[0]
    col = lambda a: a.reshape(depth, S5_NSTATE, 1)
    ldt = jnp.broadcast_to(log_dt[:, :, None], (depth, S5_GROUPS, S5_STATE))
    vec = pl.BlockSpec((None, S5_NSTATE, 1), lambda l: (l, 0, 0))
    mat = pl.BlockSpec((None, S5_NSTATE, S5_CH), lambda l: (l, 0, 0))
    return pl.pallas_call(
        _s5prep_kernel,
        grid=(depth,),
        in_specs=[vec, vec, vec, mat, mat],
        out_specs=[vec, vec, mat, mat],
        out_shape=[jax.ShapeDtypeStruct((depth, S5_NSTATE, 1), F32)] * 2
        + [jax.ShapeDtypeStruct((depth, S5_NSTATE, S5_CH), F32)] * 2,
        compiler_params=_params("parallel"),
        name="s5prep",
    )(col(a_re), col(a_im), col(ldt), b_re.reshape(depth, S5_NSTATE, S5_CH), b_im.reshape(depth, S5_NSTATE, S5_CH))


def _s5_kernel(u_ref, abre_ref, abim_ref, bre_ref, bim_ref, cre_ref, cim_ref, d_ref, gw_ref, gb_ref, o_ref,
               xre_sc, xim_sc, sre_sc, sim_sc, *, bsz, tb):
    @pl.when(pl.program_id(0) == 0)
    def _():
        sre_sc[...] = jnp.zeros_like(sre_sc)
        sim_sc[...] = jnp.zeros_like(sim_sc)

    u = u_ref[...]
    ub = u.astype(BF16)
    xre_sc[...] = _dot(ub, bre_ref[...])
    xim_sc[...] = _dot(ub, bim_ref[...])
    a_re = jnp.broadcast_to(abre_ref[...], (bsz, S5_NSTATE))
    a_im = jnp.broadcast_to(abim_ref[...], (bsz, S5_NSTATE))

    def step(t, carry):
        s_re, s_im = carry
        rows = pl.ds(pl.multiple_of(t * bsz, bsz), bsz)
        n_re = a_re * s_re - a_im * s_im + xre_sc[rows, :]
        n_im = a_re * s_im + a_im * s_re + xim_sc[rows, :]
        xre_sc[rows, :] = n_re
        xim_sc[rows, :] = n_im
        return n_re, n_im

    s_re, s_im = lax.fori_loop(0, tb, step, (sre_sc[...], sim_sc[...]), unroll=8)
    sre_sc[...] = s_re
    sim_sc[...] = s_im

    y = _dot(xre_sc[...].astype(BF16), cre_ref[...]) - _dot(xim_sc[...].astype(BF16), cim_ref[...])
    y = jax.nn.gelu(y + d_ref[...] * u, approximate=True)
    o_ref[...] = (y * jax.nn.sigmoid(_dot(y.astype(BF16), gw_ref[...]) + gb_ref[...])).astype(BF16)


def _s5(layer, u, ab_re, ab_im, b_re, b_im, c_re, c_im, d, glu_w, glu_b, *, bsz, seq, tb):
    rows = tb * bsz
    return pl.pallas_call(
        functools.partial(_s5_kernel, bsz=bsz, tb=tb),
        grid=(seq // tb,),
        in_specs=[
            pl.BlockSpec((rows, S5_WIDTH), lambda i: (i, 0)),
            _layer_spec(layer, (1, S5_NSTATE)),
            _layer_spec(layer, (1, S5_NSTATE)),
            _layer_spec(layer, (S5_WIDTH, S5_NSTATE)),
            _layer_spec(layer, (S5_WIDTH, S5_NSTATE)),
            _layer_spec(layer, (S5_NSTATE, S5_WIDTH)),
            _layer_spec(layer, (S5_NSTATE, S5_WIDTH)),
            _layer_spec(layer, (1, S5_WIDTH)),
            _layer_spec(layer, (S5_WIDTH, S5_WIDTH)),
            _layer_spec(layer, (1, S5_WIDTH)),
        ],
        out_specs=pl.BlockSpec((rows, S5_WIDTH), lambda i: (i, 0)),
        out_shape=jax.ShapeDtypeStruct((seq * bsz, S5_WIDTH), BF16),
        scratch_shapes=[pltpu.VMEM((rows, S5_NSTATE), F32), pltpu.VMEM((rows, S5_NSTATE), F32),
                        pltpu.VMEM((bsz, S5_NSTATE), F32), pltpu.VMEM((bsz, S5_NSTATE), F32)],
        compiler_params=_params("arbitrary"),
        name="s5",
    )(u, ab_re, ab_im, b_re, b_im, c_re, c_im, d, glu_w, glu_b)


def _merge_kernel(x_ref, gla_ref, s5_ref, fox_ref, pre_g_ref, wgate_ref, wgla_ref, ws5_ref, wfox_ref, wo_ref,
                  post_g_ref, o_ref):
    x = x_ref[...]
    d = x.shape[1]
    h = _rms(x, pre_g_ref[...]).astype(BF16)
    branches = ((gla_ref, wgla_ref), (s5_ref, ws5_ref), (fox_ref, wfox_ref))
    mix = None
    for n, (b_ref, w_ref) in enumerate(branches):
        gate = jax.nn.sigmoid(_dot(h, wgate_ref[:, n * d:(n + 1) * d]))
        term = gate * _dot(b_ref[...], w_ref[...])
        mix = term if mix is None else mix + term
    y = _dot(mix.astype(BF16), wo_ref[...])
    o_ref[...] = x + _rms(y, post_g_ref[...])


def _merge(layer, x, gla_o, s5_o, fox_o, pre_g, w_gate, w_gla, w_s5, w_fox, w_o, post_g, *, bsz, seq, tm):
    m, d = x.shape
    nl = seq // tm
    row = lambda b, l: (b * nl + l, 0)
    return pl.pallas_call(
        _merge_kernel,
        grid=(bsz, nl),
        in_specs=[
            pl.BlockSpec((tm, d), row),
            pl.BlockSpec((tm, GLA_V), row),
            pl.BlockSpec((tm, S5_WIDTH), lambda b, l: (l, b)),
            pl.BlockSpec((tm, FOX_W), row),
            _layer_spec(layer, (1, d)),
            _layer_spec(layer, (d, N_BRANCH * d)),
            _layer_spec(layer, (GLA_V, d)),
            _layer_spec(layer, (S5_WIDTH, d)),
            _layer_spec(layer, (FOX_W, d)),
            _layer_spec(layer, (d, d)),
            _layer_spec(layer, (1, d)),
        ],
        out_specs=pl.BlockSpec((tm, d), row),
        out_shape=jax.ShapeDtypeStruct((m, d), F32),
        compiler_params=_params("parallel", "parallel"),
        name="merge",
    )(x, gla_o, s5_o, fox_o, pre_g, w_gate, w_gla, w_s5, w_fox, w_o, post_g)


def _memkv_kernel(mem_ref, g_ref, w_ref, k_ref, v_ref):
    d = mem_ref.shape[1]
    h = _rms(mem_ref[...], g_ref[...]).astype(BF16)
    k_ref[...] = _dot(h, w_ref[:, 0:d]).astype(BF16)
    v_ref[...] = _dot(h, w_ref[:, d:2 * d]).astype(BF16)


def _memkv(layer, mem, g, w_kv, *, tm):
    m, d = mem.shape
    row = lambda i: (i, 0)
    return pl.pallas_call(
        _memkv_kernel,
        grid=(m // tm,),
        in_specs=[pl.BlockSpec((tm, d), row), _layer_spec(layer, (1, d)), _layer_spec(layer, (d, 2 * d))],
        out_specs=[pl.BlockSpec((tm, d), row)] * 2,
        out_shape=[jax.ShapeDtypeStruct((m, d), BF16)] * 2,
        compiler_params=_params("parallel"),
        name="memkv",
    )(mem, g, w_kv)


def _xattn_kernel(x_ref, k_ref, v_ref, pre_g_ref, wq_ref, wo_ref, post_g_ref, o_ref):
    x = x_ref[...]
    d = x.shape[1]
    dh = d // XA_HEADS
    h = _rms(x, pre_g_ref[...]).astype(BF16)
    q = (_dot(h, wq_ref[...]) * (dh ** -0.5)).astype(BF16)
    heads = []
    for n in range(XA_HEADS):
        cols = slice(n * dh, (n + 1) * dh)
        s = _dot_nt(q[:, cols], k_ref[:, cols])
        e = jnp.exp(s - jnp.max(s, axis=1, keepdims=True))
        p = e / jnp.sum(e, axis=1, keepdims=True)
        heads.append(_dot(p.astype(BF16), v_ref[:, cols]).astype(BF16))
    y = _dot(jnp.concatenate(heads, axis=1), wo_ref[...])
    o_ref[...] = x + _rms(y, post_g_ref[...])


def _xattn(layer, x, k, v, pre_g, w_q, w_o, post_g, *, bsz, seq, n_mem, tm):
    m, d = x.shape
    nl = seq // tm
    row = lambda b, l: (b * nl + l, 0)
    return pl.pallas_call(
        _xattn_kernel,
        grid=(bsz, nl),
        in_specs=[
            pl.BlockSpec((tm, d), row),
            pl.BlockSpec((n_mem, d), lambda b, l: (b, 0)),
            pl.BlockSpec((n_mem, d), lambda b, l: (b, 0)),
            _layer_spec(layer, (1, d)),
            _layer_spec(layer, (d, d)),
            _layer_spec(layer, (d, d)),
            _layer_spec(layer, (1, d)),
        ],
        out_specs=pl.BlockSpec((tm, d), row),
        out_shape=jax.ShapeDtypeStruct((m, d), F32),
        compiler_params=_params("parallel", "parallel"),
        name="xattn",
    )(x, k, v, pre_g, w_q, w_o, post_g)


def _tile(n, want):
    t = min(n, want)
    while n % t:
        t -= 1
    return t


def _block_diag(t):
    depth, g, r, c = t.shape
    eye = jnp.eye(g, dtype=t.dtype)
    return jnp.einsum("lgrc,gk->lgrkc", t, eye).reshape(depth, g * r, g * c)


def kernel(x, mem, ffn1_pre_g, ffn1_w_gu, ffn1_w_down, ffn1_post_g, mix_pre_g, w_in, gla_gate_w, gla_gate_b, gla_norm_g, w_gla_up, s5_a_re, s5_a_im, s5_log_dt, s5_b_re, s5_b_im, s5_c_re, s5_c_im, s5_d, s5_glu_w, s5_glu_b, w_s5_up, fox_f_b, w_fox_up, w_mix_out, mix_post_g, xa_pre_g, xa_mem_g, xa_w_q, xa_w_kv, xa_w_o, xa_post_g, ffn2_pre_g, ffn2_w_gu, ffn2_w_down, ffn2_post_g):
    bsz, seq, d = x.shape
    n_mem = mem.shape[1]
    depth = w_in.shape[0]
    d_ff = ffn1_w_down.shape[1]
    m = bsz * seq
    assert seq % CHUNK == 0 and d % LANES == 0

    tm_ffn = _tile(m, 1024)
    tf = _tile(d_ff // LANES, 2) * LANES
    tm = _tile(seq, 512)
    tq = _tile(seq, 256)
    tk = 2 * tq if seq % (2 * tq) == 0 else tq
    tb = _tile(seq, 128)

    bf = lambda a: a.astype(BF16)
    vec = lambda a: a.reshape(depth, 1, a.shape[-1])

    sizes = (GLA_QK, GLA_QK, GLA_V, GLA_V, GLA_RANK, S5_WIDTH, FOX_W, FOX_W, FOX_W, FOX_HEADS, N_BRANCH * d)
    offs = [0]
    for s in sizes:
        offs.append(offs[-1] + s)
    seg = lambda i: bf(w_in[:, :, offs[i]:offs[i + 1]])
    pad = jnp.zeros((depth, d, IN_SMALL - GLA_RANK - FOX_HEADS), BF16)
    w_proj = jnp.concatenate([seg(0), seg(1), seg(2), seg(3), seg(5), seg(6), seg(7), seg(8), seg(4), seg(9), pad],
                             axis=2)
    w_gate = seg(10)
    gla_wg = bf(jnp.concatenate([gla_gate_w, jnp.zeros((depth, IN_SMALL - GLA_RANK, GLA_QK), gla_gate_w.dtype)],
                                axis=1))

    ab_re, ab_im, bb_re, bb_im = _s5prep(s5_a_re, s5_a_im, s5_log_dt, s5_b_re, s5_b_im)
    grp = lambda t: t.reshape(depth, S5_GROUPS, S5_STATE, S5_CH)
    s5_bre = bf(_block_diag(grp(bb_re).transpose(0, 1, 3, 2)))
    s5_bim = bf(_block_diag(grp(bb_im).transpose(0, 1, 3, 2)))
    s5_cre = bf(_block_diag(s5_c_re.transpose(0, 1, 3, 2)))
    s5_cim = bf(_block_diag(s5_c_im.transpose(0, 1, 3, 2)))
    ab_re = ab_re.reshape(depth, 1, S5_NSTATE)
    ab_im = ab_im.reshape(depth, 1, S5_NSTATE)

    ffn1_gu, ffn1_down, ffn2_gu, ffn2_down = bf(ffn1_w_gu), bf(ffn1_w_down), bf(ffn2_w_gu), bf(ffn2_w_down)
    gla_up, s5_up, fox_up, mix_out = bf(w_gla_up), bf(w_s5_up), bf(w_fox_up), bf(w_mix_out)
    glu_w, xa_q, xa_kv, xa_o = bf(s5_glu_w), bf(xa_w_q), bf(xa_w_kv), bf(xa_w_o)
    fox_b = fox_f_b.reshape(depth, FOX_HEADS, 1)

    xs = x.reshape(m, d)
    mem2 = mem.reshape(bsz * n_mem, d)
    for l in range(depth):
        xs = _ffn(l, xs, vec(ffn1_pre_g), ffn1_gu, ffn1_down, vec(ffn1_post_g), tm=tm_ffn, tf=tf)

        qk, vr, su, fox, small = _inproj(l, xs, vec(mix_pre_g), w_proj, bsz=bsz, seq=seq, tm=tm)
        gla_o = _gla(l, qk, vr, small, gla_wg, vec(gla_gate_b), vec(gla_norm_g), bsz=bsz, seq=seq)
        f_terms = _foxprep(l, small, fox_b, bsz=bsz, seq=seq)
        f_cols = f_terms.reshape(bsz, FOX_TERMS, FOX_PAIRS, 2, seq).transpose(0, 2, 4, 3, 1)
        f_cols = bf(f_cols.reshape(bsz, FOX_PAIRS, seq, 2 * FOX_TERMS))
        f_cols = jnp.pad(f_cols, ((0, 0), (0, 0), (0, 0), (0, FOX_SIDE - 2 * FOX_TERMS)))
        fox_o = _fox(fox, f_cols, bsz=bsz, seq=seq, tq=tq, tk=tk)
        s5_o = _s5(l, su.reshape(seq * bsz, S5_WIDTH), ab_re, ab_im, s5_bre, s5_bim, s5_cre, s5_cim,
                   vec(s5_d), glu_w, vec(s5_glu_b), bsz=bsz, seq=seq, tb=tb)
        xs = _merge(l, xs, gla_o, s5_o.reshape(seq, bsz * S5_WIDTH), fox_o, vec(mix_pre_g), w_gate, gla_up,
                    s5_up, fox_up, mix_out, vec(mix_post_g), bsz=bsz, seq=seq, tm=tm)

        mk, mv = _memkv(l, mem2, vec(xa_mem_g), xa_kv, tm=_tile(bsz * n_mem, 512))
        xs = _xattn(l, xs, mk, mv, vec(xa_pre_g), xa_q, xa_o, vec(xa_post_g), bsz=bsz, seq=seq, n_mem=n_mem, tm=tm)

        xs = _ffn(l, xs, vec(ffn2_pre_g), ffn2_gu, ffn2_down, vec(ffn2_post_g), tm=tm_ffn, tf=tf)
    return xs.reshape(bsz, seq, d)
```

```python
import functools
import math

import jax
import jax.numpy as jnp
from jax import lax
from jax.experimental import pallas as pl
from jax.experimental.pallas import tpu as pltpu

F32 = jnp.float32
BF16 = jnp.bfloat16
EPS = 1e-6
LOG2E = math.log2(math.e)

LANES = 128
SUBLANES = 8
VMEM_LIMIT_BYTES = 56 * 1024 * 1024

CHUNK = 64
GLA_HEADS, GLA_DK, GLA_DV, GLA_RANK, GLA_TAU = 4, 32, 64, 16, 16.0
GLA_QK = GLA_HEADS * GLA_DK
GLA_V = GLA_HEADS * GLA_DV
DK_SHIFT = GLA_DK.bit_length() - 1
DV_SHIFT = GLA_DV.bit_length() - 1
S5_GROUPS, S5_CH, S5_STATE = 16, 16, 64
S5_WIDTH = S5_GROUPS * S5_CH
S5_NSTATE = S5_GROUPS * S5_STATE
FOX_HEADS, FOX_DH = 8, 64
FOX_W = FOX_HEADS * FOX_DH
FOX_PAIRS = FOX_W // LANES
FOX_TERMS = 4
FOX_SIDE = 16
TERMS_SHIFT = FOX_TERMS.bit_length() - 1
SIDE_SHIFT = FOX_SIDE.bit_length() - 1
XA_HEADS = 4
N_BRANCH = 3

NT_DIMS = (((1,), (1,)), ((), ()))


def _params(*semantics):
    return pltpu.CompilerParams(dimension_semantics=semantics, vmem_limit_bytes=VMEM_LIMIT_BYTES)


def _layer_spec(layer, block, index_map=None, pipeline_mode=None):
    block = tuple(block)
    if index_map is None:
        index_map = lambda *_: (0,) * len(block)
    return pl.BlockSpec((None,) + block, lambda *ids: (layer,) + tuple(index_map(*ids)),
                        pipeline_mode=pipeline_mode)


def _rms(x, g):
    return x * lax.rsqrt(jnp.mean(x * x, axis=-1, keepdims=True) + EPS) * g


def _dot(a, b):
    return jnp.dot(a, b, preferred_element_type=F32)


def _dot_nt(a, b):
    return lax.dot_general(a, b, NT_DIMS, preferred_element_type=F32)


def _log_sigmoid(x):
    return jnp.minimum(x, 0.0) - jnp.log1p(jnp.exp(-jnp.abs(x)))


def _split3(x):
    hi = x.astype(BF16)
    r1 = x - hi.astype(F32)
    mid = r1.astype(BF16)
    lo = (r1 - mid.astype(F32)).astype(BF16)
    return hi, mid, lo


def _ffn_kernel(x_ref, pre_g_ref, wgu_ref, wd_ref, post_g_ref, o_ref, *, tf):
    d_ff = wd_ref.shape[0]
    x = x_ref[...]
    h = _rms(x, pre_g_ref[...]).astype(BF16)
    acc = None
    for c0 in range(0, d_ff, tf):
        gate = _dot(h, wgu_ref[:, c0:c0 + tf])
        up = _dot(h, wgu_ref[:, d_ff + c0:d_ff + c0 + tf])
        a = (gate * jax.nn.sigmoid(gate) * up).astype(BF16)
        part = _dot(a, wd_ref[c0:c0 + tf, :])
        acc = part if acc is None else acc + part
    o_ref[...] = x + 0.5 * _rms(acc, post_g_ref[...])


def _ffn(layer, x, pre_g, w_gu, w_down, post_g, *, tm, tf):
    m, d = x.shape
    d_ff = w_down.shape[1]
    resident = pl.Buffered(1)
    return pl.pallas_call(
        functools.partial(_ffn_kernel, tf=tf),
        grid=(m // tm,),
        in_specs=[
            pl.BlockSpec((tm, d), lambda i: (i, 0)),
            _layer_spec(layer, (1, d)),
            _layer_spec(layer, (d, 2 * d_ff), pipeline_mode=resident),
            _layer_spec(layer, (d_ff, d), pipeline_mode=resident),
            _layer_spec(layer, (1, d)),
        ],
        out_specs=pl.BlockSpec((tm, d), lambda i: (i, 0)),
        out_shape=jax.ShapeDtypeStruct((m, d), F32),
        compiler_params=_params("parallel"),
        name="ffn",
    )(x, pre_g, w_gu, w_down, post_g)


IN_QK = 2 * GLA_QK
IN_VR = 2 * GLA_V
IN_SU = S5_WIDTH
IN_FOX = 3 * FOX_W
IN_SMALL = LANES
OFF_VR = IN_QK
OFF_SU = OFF_VR + IN_VR
OFF_FOX = OFF_SU + IN_SU
OFF_SMALL = OFF_FOX + IN_FOX
IN_TOTAL = OFF_SMALL + IN_SMALL


def _inproj_kernel(x_ref, g_ref, w_ref, qk_ref, vr_ref, su_ref, fox_ref, small_ref):
    h = _rms(x_ref[...], g_ref[...]).astype(BF16)
    qk_ref[...] = _dot(h, w_ref[:, 0:OFF_VR])
    vr_ref[...] = _dot(h, w_ref[:, OFF_VR:OFF_SU]).astype(BF16)
    su_ref[...] = _dot(h, w_ref[:, OFF_SU:OFF_FOX])
    fox_ref[:, 0:FOX_W] = (_dot(h, w_ref[:, OFF_FOX:OFF_FOX + FOX_W]) * (FOX_DH ** -0.5 * LOG2E)).astype(BF16)
    fox_ref[:, FOX_W:IN_FOX] = _dot(h, w_ref[:, OFF_FOX + FOX_W:OFF_SMALL]).astype(BF16)
    small_ref[...] = _dot(h, w_ref[:, OFF_SMALL:IN_TOTAL])


def _inproj(layer, x, g, w, *, bsz, seq, tm):
    m, d = x.shape
    nl = seq // tm
    row = lambda b, l: (b * nl + l, 0)
    return pl.pallas_call(
        _inproj_kernel,
        grid=(bsz, nl),
        in_specs=[
            pl.BlockSpec((tm, d), row),
            _layer_spec(layer, (1, d)),
            _layer_spec(layer, (d, IN_TOTAL)),
        ],
        out_specs=[
            pl.BlockSpec((tm, IN_QK), row),
            pl.BlockSpec((tm, IN_VR), row),
            pl.BlockSpec((tm, IN_SU), lambda b, l: (l, b)),
            pl.BlockSpec((tm, IN_FOX), row),
            pl.BlockSpec((tm, IN_SMALL), row),
        ],
        out_shape=[
            jax.ShapeDtypeStruct((m, IN_QK), F32),
            jax.ShapeDtypeStruct((m, IN_VR), BF16),
            jax.ShapeDtypeStruct((seq, bsz * IN_SU), F32),
            jax.ShapeDtypeStruct((m, IN_FOX), BF16),
            jax.ShapeDtypeStruct((m, IN_SMALL), F32),
        ],
        compiler_params=_params("parallel", "parallel"),
        name="inproj",
    )(x, g, w)


def _gla_kernel(qk_ref, vr_ref, small_ref, wg_ref, bg_ref, ng_ref, o_ref, la_sc, o_sc, qf_sc, st_sc, dec_sc):
    seq = qk_ref.shape[0]
    n_chunks = seq // CHUNK
    c, hq, hv = CHUNK, GLA_QK, GLA_V

    z = _dot(small_ref[...].astype(BF16), wg_ref[...]) + bg_ref[...]
    la_sc[...] = _log_sigmoid(z) * (1.0 / GLA_TAU)

    row = lax.broadcasted_iota(jnp.int32, (c, c), 0)
    col = lax.broadcasted_iota(jnp.int32, (c, c), 1)
    tri = (col <= row).astype(BF16)
    row_x = lax.broadcasted_iota(jnp.int32, (GLA_HEADS * c, c), 0)
    col_x = lax.broadcasted_iota(jnp.int32, (GLA_HEADS * c, c), 1)
    lower = col_x <= (row_x & (c - 1))
    lane_q = lax.broadcasted_iota(jnp.int32, (c, hq), 1)
    lane_v = lax.broadcasted_iota(jnp.int32, (c, hv), 1)
    q_head = [(lane_q >> DK_SHIFT) == h for h in range(GLA_HEADS)]
    v_head = [(lane_v >> DV_SHIFT) == h for h in range(GLA_HEADS)]
    st_row = lax.broadcasted_iota(jnp.int32, (hv, hq), 0)
    st_col = lax.broadcasted_iota(jnp.int32, (hv, hq), 1)
    same_head = (st_row >> DV_SHIFT) == (st_col >> DK_SHIFT)

    def expand(t):
        return jnp.concatenate([jnp.where(q_head[h], t, 0.0) for h in range(GLA_HEADS)], axis=0).astype(BF16)

    for i in range(n_chunks):
        rows = slice(i * c, (i + 1) * c)
        la = la_sc[rows, :]
        hi, mid, lo = _split3(la)
        g = _dot(tri, hi) + _dot(tri, mid) + _dot(tri, lo)
        g_last = g[c - 1:c, :]
        eg, ieg = jnp.exp(g), jnp.exp(-g)
        qc = qk_ref[rows, 0:hq] * (GLA_DK ** -0.5)
        kc = qk_ref[rows, hq:2 * hq]
        v = vr_ref[rows, 0:hv]
        q_fwd = qc * eg
        a_fwd = _dot_nt(expand(q_fwd), (kc * ieg).astype(BF16))
        a_bwd = _dot_nt(expand(qc * ieg), (kc * eg).astype(BF16))
        attn = jnp.where(lower, a_fwd, a_bwd).astype(BF16)
        p = _dot(attn, v)
        o = jnp.where(v_head[0], p[0:c, :], 0.0)
        for h in range(1, GLA_HEADS):
            o = o + jnp.where(v_head[h], p[h * c:(h + 1) * c, :], 0.0)
        o_sc[rows, :] = o
        qf_sc[rows, :] = q_fwd.astype(BF16)
        kw = (kc * jnp.exp(g_last - g)).astype(BF16)
        ds_t = _dot(v.astype(F32).T.astype(BF16), kw)
        st_sc[i] = jnp.where(same_head, ds_t, 0.0)
        dec_sc[i] = jnp.broadcast_to(jnp.exp(g_last), (SUBLANES, hq))

    s_t = jnp.zeros((hv, hq), F32)
    for i in range(n_chunks):
        ds_t = st_sc[i]
        st_sc[i] = s_t
        s_t = s_t * dec_sc[i][0:1, :] + ds_t

    for i in range(n_chunks):
        rows = slice(i * c, (i + 1) * c)
        o_sc[rows, :] += _dot_nt(qf_sc[rows, :], st_sc[i].astype(BF16))

    o = o_sc[...]
    gr = lax.broadcasted_iota(jnp.int32, (hv, hv), 0) >> DV_SHIFT
    gc = lax.broadcasted_iota(jnp.int32, (hv, hv), 1) >> DV_SHIFT
    avg = jnp.where(gr == gc, 1.0 / GLA_DV, 0.0).astype(BF16)
    hi, mid, lo = _split3(o * o)
    ms = _dot(hi, avg) + _dot(mid, avg) + _dot(lo, avg)
    r = vr_ref[:, hv:2 * hv].astype(F32)
    o_ref[...] = (o * lax.rsqrt(ms + EPS) * ng_ref[...] * (r * jax.nn.sigmoid(r))).astype(BF16)


def _gla(layer, qk, vr, small, wg, bg, ng, *, bsz, seq):
    m = qk.shape[0]
    row = lambda b: (b, 0)
    return pl.pallas_call(
        _gla_kernel,
        grid=(bsz,),
        in_specs=[
            pl.BlockSpec((seq, IN_QK), row),
            pl.BlockSpec((seq, IN_VR), row),
            pl.BlockSpec((seq, IN_SMALL), row),
            _layer_spec(layer, (IN_SMALL, GLA_QK)),
            _layer_spec(layer, (1, GLA_QK)),
            _layer_spec(layer, (1, GLA_V)),
        ],
        out_specs=pl.BlockSpec((seq, GLA_V), row),
        out_shape=jax.ShapeDtypeStruct((m, GLA_V), BF16),
        scratch_shapes=[pltpu.VMEM((seq, GLA_QK), F32), pltpu.VMEM((seq, GLA_V), F32),
                        pltpu.VMEM((seq, GLA_QK), BF16), pltpu.VMEM((seq // CHUNK, GLA_V, GLA_QK), F32),
                        pltpu.VMEM((seq // CHUNK, SUBLANES, GLA_QK), F32)],
        compiler_params=_params("parallel"),
        name="gla",
    )(qk, vr, small, wg, bg, ng)


def _foxprep_kernel(small_ref, b_ref, o_ref):
    seq = small_ref.shape[0]
    ff = small_ref[...].T[GLA_RANK:GLA_RANK + FOX_HEADS, :]
    x = _log_sigmoid(ff + b_ref[...])
    lane = lax.broadcasted_iota(jnp.int32, x.shape, 1)
    shift = 1
    while shift < seq:
        x = x + jnp.where(lane >= shift, pltpu.roll(x, shift, axis=1), 0.0)
        shift *= 2
    hi, mid, lo = _split3(x * LOG2E)
    terms = jnp.concatenate([hi, mid, lo, jnp.ones_like(hi)], axis=0)
    r = lax.broadcasted_iota(jnp.int32, (FOX_PAIRS * FOX_SIDE, FOX_TERMS * FOX_HEADS), 0)
    src = lax.broadcasted_iota(jnp.int32, (FOX_PAIRS * FOX_SIDE, FOX_TERMS * FOX_HEADS), 1)
    pair, slot = r >> SIDE_SHIFT, r & (FOX_SIDE - 1)
    head, term = 2 * pair + (slot >> TERMS_SHIFT), slot & (FOX_TERMS - 1)
    perm = ((slot < 2 * FOX_TERMS) & (src == term * FOX_HEADS + head)).astype(BF16)
    o_ref[...] = _dot(perm, terms).astype(BF16)


def _foxprep(layer, small, bias, *, bsz, seq):
    return pl.pallas_call(
        _foxprep_kernel,
        grid=(bsz,),
        in_specs=[
            pl.BlockSpec((seq, IN_SMALL), lambda b: (b, 0)),
            _layer_spec(layer, (FOX_HEADS, 1)),
        ],
        out_specs=pl.BlockSpec((None, FOX_PAIRS * FOX_SIDE, seq), lambda b: (b, 0, 0)),
        out_shape=jax.ShapeDtypeStruct((bsz, FOX_PAIRS * FOX_SIDE, seq), BF16),
        compiler_params=_params("parallel"),
        name="foxprep",
    )(small, bias)


def _fox_placement(h, base, *, key):
    r = lax.broadcasted_iota(jnp.int32, (FOX_SIDE, LANES), 0)
    rel = lax.broadcasted_iota(jnp.int32, (FOX_SIDE, LANES), 1) - base
    term = r - FOX_TERMS * h
    is_part = (term >= 0) & (term < FOX_TERMS - 1)
    is_one = term == FOX_TERMS - 1
    n = FOX_TERMS - 1
    if key:
        m = jnp.where(is_part & (rel == term + n), -1.0, jnp.where(is_one & (rel >= 0) & (rel < n), 1.0, 0.0))
    else:
        m = jnp.where(is_part & (rel == term), 1.0, jnp.where(is_one & (rel >= n) & (rel < 2 * n), 1.0, 0.0))
    return m.astype(BF16)


def _fox_kernel(q_ref, k_ref, v_ref, f_ref, o_ref, qx_sc, kx_sc, vx_sc, *, tq, tk):
    seq = k_ref.shape[0]
    spare = (FOX_DH, 0)
    lane = lax.broadcasted_iota(jnp.int32, (seq, LANES), 1)
    q, k, v, f = q_ref[...], k_ref[...], v_ref[...], f_ref[...]
    for h in range(2):
        own = (lane < FOX_DH) if h == 0 else (lane >= FOX_DH)
        qx_sc[h] = jnp.where(own, q, _dot(f, _fox_placement(h, spare[h], key=False)).astype(BF16))
        kx_sc[h] = jnp.where(own, k, _dot(f, _fox_placement(h, spare[h], key=True)).astype(BF16))
        vx_sc[h] = jnp.where(own, v, jnp.ones_like(v))

    row = lax.broadcasted_iota(jnp.int32, (tq, tq), 0)
    col = lax.broadcasted_iota(jnp.int32, (tq, tq), 1)
    causal = col <= row
    first = lax.broadcasted_iota(jnp.int32, (tq, LANES), 1) < FOX_DH

    def step(carry, h, rows, col0, width, masked):
        m_old, acc = carry
        cols = slice(col0, col0 + width)
        s = _dot_nt(qx_sc[h, rows, :], kx_sc[h, cols, :])
        if masked:
            s = jnp.where(causal, s, -jnp.inf)
        m_new = jnp.maximum(m_old, jnp.max(s, axis=1, keepdims=True))
        p = jnp.exp2(s - m_new).astype(BF16)
        return m_new, jnp.exp2(m_old - m_new) * acc + _dot(p, vx_sc[h, cols, :])

    per = tk // tq
    for qi in range(seq // tq):
        rows = slice(qi * tq, (qi + 1) * tq)
        tiles = [(qi * tq, tq, True)] + [(j * tk, tk, False) for j in range(qi // per)]
        tiles += [((qi - r) * tq, tq, False) for r in range(1, qi % per + 1)]
        outs = []
        for h in range(2):
            carry = (jnp.full((tq, 1), -jnp.inf, F32), jnp.zeros((tq, LANES), F32))
            for col0, width, masked in tiles:
                carry = step(carry, h, rows, col0, width, masked)
            acc = carry[1]
            outs.append(acc / pltpu.roll(acc, FOX_DH, axis=1))
        o_ref[rows, :] = jnp.where(first, outs[0], outs[1]).astype(BF16)


def _fox(fox, f, *, bsz, seq, tq, tk):
    m = fox.shape[0]
    return pl.pallas_call(
        functools.partial(_fox_kernel, tq=tq, tk=tk),
        grid=(bsz, FOX_PAIRS),
        in_specs=[
            pl.BlockSpec((seq, LANES), lambda b, p: (b, p)),
            pl.BlockSpec((seq, LANES), lambda b, p: (b, FOX_PAIRS + p)),
            pl.BlockSpec((seq, LANES), lambda b, p: (b, 2 * FOX_PAIRS + p)),
            pl.BlockSpec((None, None, seq, FOX_SIDE), lambda b, p: (b, p, 0, 0)),
        ],
        out_specs=pl.BlockSpec((seq, LANES), lambda b, p: (b, p)),
        out_shape=jax.ShapeDtypeStruct((m, FOX_W), BF16),
        scratch_shapes=[pltpu.VMEM((2, seq, LANES), BF16)] * 3,
        compiler_params=_params("parallel", "parallel"),
        name="fox",
    )(fox, fox, fox, f)


def _s5prep_kernel(are_ref, aim_ref, ldt_ref, bre_ref, bim_ref, abre_ref, abim_ref, bbre_ref, bbim_ref):
    lam_re = jnp.minimum(are_ref[...], -1e-4)
    lam_im = aim_ref[...]
    dt = jnp.exp(ldt_ref[...])
    mag = jnp.exp(lam_re * dt)
    ab_re = mag * jnp.cos(lam_im * dt)
    ab_im = mag * jnp.sin(lam_im * dt)
    den = lam_re * lam_re + lam_im * lam_im
    z_re = ((ab_re - 1.0) * lam_re + ab_im * lam_im) / den
    z_im = (ab_im * lam_re - (ab_re - 1.0) * lam_im) / den
    br, bi = bre_ref[...], bim_ref[...]
    abre_ref[...] = ab_re
    abim_ref[...] = ab_im
    bbre_ref[...] = z_re * br - z_im * bi
    bbim_ref[...] = z_re * bi + z_im * br


def _s5prep(a_re, a_im, log_dt, b_re, b_im):
    depth = a_re.shape[0]
    col = lambda a: a.reshape(depth, S5_NSTATE, 1)
    ldt = jnp.broadcast_to(log_dt[:, :, None], (depth, S5_GROUPS, S5_STATE))
    vec = pl.BlockSpec((None, S5_NSTATE, 1), lambda l: (l, 0, 0))
    mat = pl.BlockSpec((None, S5_NSTATE, S5_CH), lambda l: (l, 0, 0))
    return pl.pallas_call(
        _s5prep_kernel,
        grid=(depth,),
        in_specs=[vec, vec, vec, mat, mat],
        out_specs=[vec, vec, mat, mat],
        out_shape=[jax.ShapeDtypeStruct((depth, S5_NSTATE, 1), F32)] * 2
        + [jax.ShapeDtypeStruct((depth, S5_NSTATE, S5_CH), F32)] * 2,
        compiler_params=_params("parallel"),
        name="s5prep",
    )(col(a_re), col(a_im), col(ldt), b_re.reshape(depth, S5_NSTATE, S5_CH), b_im.reshape(depth, S5_NSTATE, S5_CH))


def _s5_kernel(u_ref, abre_ref, abim_ref, bre_ref, bim_ref, cre_ref, cim_ref, d_ref, gw_ref, gb_ref, o_ref,
               xre_sc, xim_sc, sre_sc, sim_sc, *, bsz, tb):
    @pl.when(pl.program_id(0) == 0)
    def _():
        sre_sc[...] = jnp.zeros_like(sre_sc)
        sim_sc[...] = jnp.zeros_like(sim_sc)

    u = u_ref[...]
    ub = u.astype(BF16)
    xre_sc[...] = _dot(ub, bre_ref[...])
    xim_sc[...] = _dot(ub, bim_ref[...])
    a_re = jnp.broadcast_to(abre_ref[...], (bsz, S5_NSTATE))
    a_im = jnp.broadcast_to(abim_ref[...], (bsz, S5_NSTATE))

    def step(t, carry):
        s_re, s_im = carry
        rows = pl.ds(pl.multiple_of(t * bsz, bsz), bsz)
        n_re = a_re * s_re - a_im * s_im + xre_sc[rows, :]
        n_im = a_re * s_im + a_im * s_re + xim_sc[rows, :]
        xre_sc[rows, :] = n_re
        xim_sc[rows, :] = n_im
        return n_re, n_im

    s_re, s_im = lax.fori_loop(0, tb, step, (sre_sc[...], sim_sc[...]), unroll=8)
    sre_sc[...] = s_re
    sim_sc[...] = s_im

    y = _dot(xre_sc[...].astype(BF16), cre_ref[...]) - _dot(xim_sc[...].astype(BF16), cim_ref[...])
    y = jax.nn.gelu(y + d_ref[...] * u, approximate=True)
    o_ref[...] = (y * jax.nn.sigmoid(_dot(y.astype(BF16), gw_ref[...]) + gb_ref[...])).astype(BF16)


def _s5(layer, u, ab_re, ab_im, b_re, b_im, c_re, c_im, d, glu_w, glu_b, *, bsz, seq, tb):
    rows = tb * bsz
    return pl.pallas_call(
        functools.partial(_s5_kernel, bsz=bsz, tb=tb),
        grid=(seq // tb,),
        in_specs=[
            pl.BlockSpec((rows, S5_WIDTH), lambda i: (i, 0)),
            _layer_spec(layer, (1, S5_NSTATE)),
            _layer_spec(layer, (1, S5_NSTATE)),
            _layer_spec(layer, (S5_WIDTH, S5_NSTATE)),
            _layer_spec(layer, (S5_WIDTH, S5_NSTATE)),
            _layer_spec(layer, (S5_NSTATE, S5_WIDTH)),
            _layer_spec(layer, (S5_NSTATE, S5_WIDTH)),
            _layer_spec(layer, (1, S5_WIDTH)),
            _layer_spec(layer, (S5_WIDTH, S5_WIDTH)),
            _layer_spec(layer, (1, S5_WIDTH)),
        ],
        out_specs=pl.BlockSpec((rows, S5_WIDTH), lambda i: (i, 0)),
        out_shape=jax.ShapeDtypeStruct((seq * bsz, S5_WIDTH), BF16),
        scratch_shapes=[pltpu.VMEM((rows, S5_NSTATE), F32), pltpu.VMEM((rows, S5_NSTATE), F32),
                        pltpu.VMEM((bsz, S5_NSTATE), F32), pltpu.VMEM((bsz, S5_NSTATE), F32)],
        compiler_params=_params("arbitrary"),
        name="s5",
    )(u, ab_re, ab_im, b_re, b_im, c_re, c_im, d, glu_w, glu_b)


def _merge_kernel(x_ref, gla_ref, s5_ref, fox_ref, pre_g_ref, wgate_ref, wgla_ref, ws5_ref, wfox_ref, wo_ref,
                  post_g_ref, o_ref):
    x = x_ref[...]
    d = x.shape[1]
    h = _rms(x, pre_g_ref[...]).astype(BF16)
    branches = ((gla_ref, wgla_ref), (s5_ref, ws5_ref), (fox_ref, wfox_ref))
    mix = None
    for n, (b_ref, w_ref) in enumerate(branches):
        gate = jax.nn.sigmoid(_dot(h, wgate_ref[:, n * d:(n + 1) * d]))
        term = gate * _dot(b_ref[...], w_ref[...])
        mix = term if mix is None else mix + term
    y = _dot(mix.astype(BF16), wo_ref[...])
    o_ref[...] = x + _rms(y, post_g_ref[...])


def _merge(layer, x, gla_o, s5_o, fox_o, pre_g, w_gate, w_gla, w_s5, w_fox, w_o, post_g, *, bsz, seq, tm):
    m, d = x.shape
    nl = seq // tm
    row = lambda b, l: (b * nl + l, 0)
    return pl.pallas_call(
        _merge_kernel,
        grid=(bsz, nl),
        in_specs=[
            pl.BlockSpec((tm, d), row),
            pl.BlockSpec((tm, GLA_V), row),
            pl.BlockSpec((tm, S5_WIDTH), lambda b, l: (l, b)),
            pl.BlockSpec((tm, FOX_W), row),
            _layer_spec(layer, (1, d)),
            _layer_spec(layer, (d, N_BRANCH * d)),
            _layer_spec(layer, (GLA_V, d)),
            _layer_spec(layer, (S5_WIDTH, d)),
            _layer_spec(layer, (FOX_W, d)),
            _layer_spec(layer, (d, d)),
            _layer_spec(layer, (1, d)),
        ],
        out_specs=pl.BlockSpec((tm, d), row),
        out_shape=jax.ShapeDtypeStruct((m, d), F32),
        compiler_params=_params("parallel", "parallel"),
        name="merge",
    )(x, gla_o, s5_o, fox_o, pre_g, w_gate, w_gla, w_s5, w_fox, w_o, post_g)


def _memkv_kernel(mem_ref, g_ref, w_ref, k_ref, v_ref):
    d = mem_ref.shape[1]
    h = _rms(mem_ref[...], g_ref[...]).astype(BF16)
    k_ref[...] = _dot(h, w_ref[:, 0:d]).astype(BF16)
    v_ref[...] = _dot(h, w_ref[:, d:2 * d]).astype(BF16)


def _memkv(layer, mem, g, w_kv, *, tm):
    m, d = mem.shape
    row = lambda i: (i, 0)
    return pl.pallas_call(
        _memkv_kernel,
        grid=(m // tm,),
        in_specs=[pl.BlockSpec((tm, d), row), _layer_spec(layer, (1, d)), _layer_spec(layer, (d, 2 * d))],
        out_specs=[pl.BlockSpec((tm, d), row)] * 2,
        out_shape=[jax.ShapeDtypeStruct((m, d), BF16)] * 2,
        compiler_params=_params("parallel"),
        name="memkv",
    )(mem, g, w_kv)


def _xattn_kernel(x_ref, k_ref, v_ref, pre_g_ref, wq_ref, wo_ref, post_g_ref, o_ref):
    x = x_ref[...]
    d = x.shape[1]
    dh = d // XA_HEADS
    h = _rms(x, pre_g_ref[...]).astype(BF16)
    q = (_dot(h, wq_ref[...]) * (dh ** -0.5)).astype(BF16)
    heads = []
    for n in range(XA_HEADS):
        cols = slice(n * dh, (n + 1) * dh)
        s = _dot_nt(q[:, cols], k_ref[:, cols])
        e = jnp.exp(s - jnp.max(s, axis=1, keepdims=True))
        p = e / jnp.sum(e, axis=1, keepdims=True)
        heads.append(_dot(p.astype(BF16), v_ref[:, cols]).astype(BF16))
    y = _dot(jnp.concatenate(heads, axis=1), wo_ref[...])
    o_ref[...] = x + _rms(y, post_g_ref[...])


def _xattn(layer, x, k, v, pre_g, w_q, w_o, post_g, *, bsz, seq, n_mem, tm):
    m, d = x.shape
    nl = seq // tm
    row = lambda b, l: (b * nl + l, 0)
    return pl.pallas_call(
        _xattn_kernel,
        grid=(bsz, nl),
        in_specs=[
            pl.BlockSpec((tm, d), row),
            pl.BlockSpec((n_mem, d), lambda b, l: (b, 0)),
            pl.BlockSpec((n_mem, d), lambda b, l: (b, 0)),
            _layer_spec(layer, (1, d)),
            _layer_spec(layer, (d, d)),
            _layer_spec(layer, (d, d)),
            _layer_spec(layer, (1, d)),
        ],
        out_specs=pl.BlockSpec((tm, d), row),
        out_shape=jax.ShapeDtypeStruct((m, d), F32),
        compiler_params=_params("parallel", "parallel"),
        name="xattn",
    )(x, k, v, pre_g, w_q, w_o, post_g)


def _tile(n, want):
    t = min(n, want)
    while n % t:
        t -= 1
    return t


def _block_diag(t):
    depth, g, r, c = t.shape
    eye = jnp.eye(g, dtype=t.dtype)
    return jnp.einsum("lgrc,gk->lgrkc", t, eye).reshape(depth, g * r, g * c)


def kernel(x, mem, ffn1_pre_g, ffn1_w_gu, ffn1_w_down, ffn1_post_g, mix_pre_g, w_in, gla_gate_w, gla_gate_b, gla_norm_g, w_gla_up, s5_a_re, s5_a_im, s5_log_dt, s5_b_re, s5_b_im, s5_c_re, s5_c_im, s5_d, s5_glu_w, s5_glu_b, w_s5_up, fox_f_b, w_fox_up, w_mix_out, mix_post_g, xa_pre_g, xa_mem_g, xa_w_q, xa_w_kv, xa_w_o, xa_post_g, ffn2_pre_g, ffn2_w_gu, ffn2_w_down, ffn2_post_g):
    bsz, seq, d = x.shape
    n_mem = mem.shape[1]
    depth = w_in.shape[0]
    d_ff = ffn1_w_down.shape[1]
    m = bsz * seq
    assert seq % CHUNK == 0 and d % LANES == 0

    tm_ffn = _tile(m, 1024)
    tf = _tile(d_ff // LANES, 2) * LANES
    tm = _tile(seq, 512)
    tq = _tile(seq, 256)
    tk = 2 * tq if seq % (2 * tq) == 0 else tq
    tb = _tile(seq, 128)

    bf = lambda a: a.astype(BF16)
    vec = lambda a: a.reshape(depth, 1, a.shape[-1])

    sizes = (GLA_QK, GLA_QK, GLA_V, GLA_V, GLA_RANK, S5_WIDTH, FOX_W, FOX_W, FOX_W, FOX_HEADS, N_BRANCH * d)
    offs = [0]
    for s in sizes:
        offs.append(offs[-1] + s)
    seg = lambda i: bf(w_in[:, :, offs[i]:offs[i + 1]])
    pad = jnp.zeros((depth, d, IN_SMALL - GLA_RANK - FOX_HEADS), BF16)
    w_proj = jnp.concatenate([seg(0), seg(1), seg(2), seg(3), seg(5), seg(6), seg(7), seg(8), seg(4), seg(9), pad],
                             axis=2)
    w_gate = seg(10)
    gla_wg = bf(jnp.concatenate([gla_gate_w, jnp.zeros((depth, IN_SMALL - GLA_RANK, GLA_QK), gla_gate_w.dtype)],
                                axis=1))

    ab_re, ab_im, bb_re, bb_im = _s5prep(s5_a_re, s5_a_im, s5_log_dt, s5_b_re, s5_b_im)
    grp = lambda t: t.reshape(depth, S5_GROUPS, S5_STATE, S5_CH)
    s5_bre = bf(_block_diag(grp(bb_re).transpose(0, 1, 3, 2)))
    s5_bim = bf(_block_diag(grp(bb_im).transpose(0, 1, 3, 2)))
    s5_cre = bf(_block_diag(s5_c_re.transpose(0, 1, 3, 2)))
    s5_cim = bf(_block_diag(s5_c_im.transpose(0, 1, 3, 2)))
    ab_re = ab_re.reshape(depth, 1, S5_NSTATE)
    ab_im = ab_im.reshape(depth, 1, S5_NSTATE)

    ffn1_gu, ffn1_down, ffn2_gu, ffn2_down = bf(ffn1_w_gu), bf(ffn1_w_down), bf(ffn2_w_gu), bf(ffn2_w_down)
    gla_up, s5_up, fox_up, mix_out = bf(w_gla_up), bf(w_s5_up), bf(w_fox_up), bf(w_mix_out)
    glu_w, xa_q, xa_kv, xa_o = bf(s5_glu_w), bf(xa_w_q), bf(xa_w_kv), bf(xa_w_o)
    fox_b = fox_f_b.reshape(depth, FOX_HEADS, 1)

    xs = x.reshape(m, d)
    mem2 = mem.reshape(bsz * n_mem, d)
    for l in range(depth):
        xs = _ffn(l, xs, vec(ffn1_pre_g), ffn1_gu, ffn1_down, vec(ffn1_post_g), tm=tm_ffn, tf=tf)

        qk, vr, su, fox, small = _inproj(l, xs, vec(mix_pre_g), w_proj, bsz=bsz, seq=seq, tm=tm)
        gla_o = _gla(l, qk, vr, small, gla_wg, vec(gla_gate_b), vec(gla_norm_g), bsz=bsz, seq=seq)
        f_rows = _foxprep(l, small, fox_b, bsz=bsz, seq=seq)
        f_cols = f_rows.reshape(bsz, FOX_PAIRS, FOX_SIDE, seq).transpose(0, 1, 3, 2)
        fox_o = _fox(fox, f_cols, bsz=bsz, seq=seq, tq=tq, tk=tk)
        s5_o = _s5(l, su.reshape(seq * bsz, S5_WIDTH), ab_re, ab_im, s5_bre, s5_bim, s5_cre, s5_cim,
                   vec(s5_d), glu_w, vec(s5_glu_b), bsz=bsz, seq=seq, tb=tb)
        xs = _merge(l, xs, gla_o, s5_o.reshape(seq, bsz * S5_WIDTH), fox_o, vec(mix_pre_g), w_gate, gla_up,
                    s5_up, fox_up, mix_out, vec(mix_post_g), bsz=bsz, seq=seq, tm=tm)

        mk, mv = _memkv(l, mem2, vec(xa_mem_g), xa_kv, tm=_tile(bsz * n_mem, 512))
        xs = _xattn(l, xs, mk, mv, vec(xa_pre_g), xa_q, xa_o, vec(xa_post_g), bsz=bsz, seq=seq, n_mem=n_mem, tm=tm)

        xs = _ffn(l, xs, vec(ffn2_pre_g), ffn2_gu, ffn2_down, vec(ffn2_post_g), tm=tm_ffn, tf=tf)
    return xs.reshape(bsz, seq, d)
```

```python
import functools
import math

import jax
import jax.numpy as jnp
from jax import lax
from jax.experimental import pallas as pl
from jax.experimental.pallas import tpu as pltpu

F32 = jnp.float32
BF16 = jnp.bfloat16
EPS = 1e-6
LOG2E = math.log2(math.e)

LANES = 128
SUBLANES = 8
VMEM_LIMIT_BYTES = 56 * 1024 * 1024

CHUNK = 64
GLA_GROUP = 8
GLA_HEADS, GLA_DK, GLA_DV, GLA_RANK, GLA_TAU = 4, 32, 64, 16, 16.0
GLA_QK = GLA_HEADS * GLA_DK
GLA_V = GLA_HEADS * GLA_DV
DK_SHIFT = GLA_DK.bit_length() - 1
DV_SHIFT = GLA_DV.bit_length() - 1
S5_GROUPS, S5_CH, S5_STATE = 16, 16, 64
S5_WIDTH = S5_GROUPS * S5_CH
S5_NSTATE = S5_GROUPS * S5_STATE
FOX_HEADS, FOX_DH = 8, 64
FOX_W = FOX_HEADS * FOX_DH
FOX_PAIRS = FOX_W // LANES
FOX_TERMS = 4
FOX_SIDE = 16
TERMS_SHIFT = FOX_TERMS.bit_length() - 1
SIDE_SHIFT = FOX_SIDE.bit_length() - 1
XA_HEADS = 4
N_BRANCH = 3

NT_DIMS = (((1,), (1,)), ((), ()))


def _params(*semantics):
    return pltpu.CompilerParams(dimension_semantics=semantics, vmem_limit_bytes=VMEM_LIMIT_BYTES)


def _layer_spec(layer, block, index_map=None, pipeline_mode=None):
    block = tuple(block)
    if index_map is None:
        index_map = lambda *_: (0,) * len(block)
    return pl.BlockSpec((None,) + block, lambda *ids: (layer,) + tuple(index_map(*ids)),
                        pipeline_mode=pipeline_mode)


def _rms(x, g):
    return x * lax.rsqrt(jnp.mean(x * x, axis=-1, keepdims=True) + EPS) * g


def _dot(a, b):
    return jnp.dot(a, b, preferred_element_type=F32)


def _dot_nt(a, b):
    return lax.dot_general(a, b, NT_DIMS, preferred_element_type=F32)


def _log_sigmoid(x):
    return jnp.minimum(x, 0.0) - jnp.log1p(jnp.exp(-jnp.abs(x)))


def _split3(x):
    hi = x.astype(BF16)
    r1 = x - hi.astype(F32)
    mid = r1.astype(BF16)
    lo = (r1 - mid.astype(F32)).astype(BF16)
    return hi, mid, lo


def _ffn_kernel(x_ref, pre_g_ref, wgu_ref, wd_ref, post_g_ref, o_ref, *, tf):
    d_ff = wd_ref.shape[0]
    x = x_ref[...]
    h = _rms(x, pre_g_ref[...]).astype(BF16)
    def gate_up(c0):
        return _dot(h, wgu_ref[:, c0:c0 + tf]), _dot(h, wgu_ref[:, d_ff + c0:d_ff + c0 + tf])

    acc = None
    nxt = gate_up(0)
    for c0 in range(0, d_ff, tf):
        gate, up = nxt
        if c0 + tf < d_ff:
            nxt = gate_up(c0 + tf)
        a = (gate * jax.nn.sigmoid(gate) * up).astype(BF16)
        part = _dot(a, wd_ref[c0:c0 + tf, :])
        acc = part if acc is None else acc + part
    o_ref[...] = x + 0.5 * _rms(acc, post_g_ref[...])


def _ffn(layer, x, pre_g, w_gu, w_down, post_g, *, tm, tf):
    m, d = x.shape
    d_ff = w_down.shape[1]
    resident = pl.Buffered(1)
    return pl.pallas_call(
        functools.partial(_ffn_kernel, tf=tf),
        grid=(m // tm,),
        in_specs=[
            pl.BlockSpec((tm, d), lambda i: (i, 0)),
            _layer_spec(layer, (1, d)),
            _layer_spec(layer, (d, 2 * d_ff), pipeline_mode=resident),
            _layer_spec(layer, (d_ff, d), pipeline_mode=resident),
            _layer_spec(layer, (1, d)),
        ],
        out_specs=pl.BlockSpec((tm, d), lambda i: (i, 0)),
        out_shape=jax.ShapeDtypeStruct((m, d), F32),
        compiler_params=_params("parallel"),
        name="ffn",
    )(x, pre_g, w_gu, w_down, post_g)


IN_QK = 2 * GLA_QK
IN_VR = 2 * GLA_V
IN_SU = S5_WIDTH
IN_FOX = 3 * FOX_W
IN_SMALL = LANES
OFF_VR = IN_QK
OFF_SU = OFF_VR + IN_VR
OFF_FOX = OFF_SU + IN_SU
OFF_SMALL = OFF_FOX + IN_FOX
IN_TOTAL = OFF_SMALL + IN_SMALL


def _inproj_kernel(x_ref, g_ref, w_ref, qk_ref, vr_ref, su_ref, fox_ref, small_ref):
    h = _rms(x_ref[...], g_ref[...]).astype(BF16)
    qk_ref[...] = _dot(h, w_ref[:, 0:OFF_VR])
    vr_ref[...] = _dot(h, w_ref[:, OFF_VR:OFF_SU]).astype(BF16)
    su_ref[...] = _dot(h, w_ref[:, OFF_SU:OFF_FOX])
    fox_ref[:, 0:FOX_W] = (_dot(h, w_ref[:, OFF_FOX:OFF_FOX + FOX_W]) * (FOX_DH ** -0.5 * LOG2E)).astype(BF16)
    fox_ref[:, FOX_W:IN_FOX] = _dot(h, w_ref[:, OFF_FOX + FOX_W:OFF_SMALL]).astype(BF16)
    small_ref[...] = _dot(h, w_ref[:, OFF_SMALL:IN_TOTAL])


def _inproj(layer, x, g, w, *, bsz, seq, tm):
    m, d = x.shape
    nl = seq // tm
    row = lambda b, l: (b * nl + l, 0)
    return pl.pallas_call(
        _inproj_kernel,
        grid=(bsz, nl),
        in_specs=[
            pl.BlockSpec((tm, d), row),
            _layer_spec(layer, (1, d)),
            _layer_spec(layer, (d, IN_TOTAL)),
        ],
        out_specs=[
            pl.BlockSpec((tm, IN_QK), row),
            pl.BlockSpec((tm, IN_VR), row),
            pl.BlockSpec((tm, IN_SU), lambda b, l: (l, b)),
            pl.BlockSpec((tm, IN_FOX), row),
            pl.BlockSpec((tm, IN_SMALL), row),
        ],
        out_shape=[
            jax.ShapeDtypeStruct((m, IN_QK), F32),
            jax.ShapeDtypeStruct((m, IN_VR), BF16),
            jax.ShapeDtypeStruct((seq, bsz * IN_SU), F32),
            jax.ShapeDtypeStruct((m, IN_FOX), BF16),
            jax.ShapeDtypeStruct((m, IN_SMALL), F32),
        ],
        compiler_params=_params("parallel", "parallel"),
        name="inproj",
    )(x, g, w)


def _gla_kernel(qk_ref, vr_ref, small_ref, wg_ref, bg_ref, ng_ref, o_ref, la_sc, o_sc):
    seq = qk_ref.shape[0]
    n_chunks = seq // CHUNK
    c, hq, hv = CHUNK, GLA_QK, GLA_V

    z = _dot(small_ref[...].astype(BF16), wg_ref[...]) + bg_ref[...]
    la_sc[...] = _log_sigmoid(z) * (1.0 / GLA_TAU)

    row = lax.broadcasted_iota(jnp.int32, (c, c), 0)
    col = lax.broadcasted_iota(jnp.int32, (c, c), 1)
    tri = (col <= row).astype(BF16)
    row_x = lax.broadcasted_iota(jnp.int32, (GLA_HEADS * c, c), 0)
    col_x = lax.broadcasted_iota(jnp.int32, (GLA_HEADS * c, c), 1)
    lower = col_x <= (row_x & (c - 1))
    lane_q = lax.broadcasted_iota(jnp.int32, (c, hq), 1)
    lane_v = lax.broadcasted_iota(jnp.int32, (c, hv), 1)
    q_head = [(lane_q >> DK_SHIFT) == h for h in range(GLA_HEADS)]
    v_head = [(lane_v >> DV_SHIFT) == h for h in range(GLA_HEADS)]
    st_row = lax.broadcasted_iota(jnp.int32, (hv, hq), 0)
    st_col = lax.broadcasted_iota(jnp.int32, (hv, hq), 1)
    same_head = (st_row >> DV_SHIFT) == (st_col >> DK_SHIFT)

    def expand(t):
        return jnp.concatenate([jnp.where(q_head[h], t, 0.0) for h in range(GLA_HEADS)], axis=0).astype(BF16)

    per_chunk = []
    for first in range(0, n_chunks, GLA_GROUP):
        group = range(first, min(first + GLA_GROUP, n_chunks))
        stage1 = []
        for i in group:
            hi, mid, lo = _split3(la_sc[i * c:(i + 1) * c, :])
            stage1.append(_dot(tri, hi) + _dot(tri, mid) + _dot(tri, lo))
        stage2 = []
        for i, g in zip(group, stage1):
            rows = slice(i * c, (i + 1) * c)
            g_last = g[c - 1:c, :]
            eg, ieg = jnp.exp(g), jnp.exp(-g)
            qc = qk_ref[rows, 0:hq] * (GLA_DK ** -0.5)
            kc = qk_ref[rows, hq:2 * hq]
            v = vr_ref[rows, 0:hv]
            q_fwd = qc * eg
            a_fwd = _dot_nt(expand(q_fwd), (kc * ieg).astype(BF16))
            a_bwd = _dot_nt(expand(qc * ieg), (kc * eg).astype(BF16))
            kw = (kc * jnp.exp(g_last - g)).astype(BF16)
            ds_t = _dot(v.astype(F32).T.astype(BF16), kw)
            stage2.append((a_fwd, a_bwd, v, q_fwd.astype(BF16), jnp.where(same_head, ds_t, 0.0), jnp.exp(g_last)))
        for a_fwd, a_bwd, v, q_fwd, ds_t, decay in stage2:
            attn = jnp.where(lower, a_fwd, a_bwd).astype(BF16)
            p = _dot(attn, v)
            o = jnp.where(v_head[0], p[0:c, :], 0.0)
            for h in range(1, GLA_HEADS):
                o = o + jnp.where(v_head[h], p[h * c:(h + 1) * c, :], 0.0)
            per_chunk.append((o, q_fwd, ds_t, decay))

    s_t = jnp.zeros((hv, hq), F32)
    states = []
    for _, _, ds_t, decay in per_chunk:
        states.append(s_t.astype(BF16))
        s_t = s_t * decay + ds_t

    for i, (o, q_fwd, _, _) in enumerate(per_chunk):
        o_sc[i * c:(i + 1) * c, :] = o + _dot_nt(q_fwd, states[i])

    o = o_sc[...]
    gr = lax.broadcasted_iota(jnp.int32, (hv, hv), 0) >> DV_SHIFT
    gc = lax.broadcasted_iota(jnp.int32, (hv, hv), 1) >> DV_SHIFT
    avg = jnp.where(gr == gc, 1.0 / GLA_DV, 0.0).astype(BF16)
    hi, mid, lo = _split3(o * o)
    ms = _dot(hi, avg) + _dot(mid, avg) + _dot(lo, avg)
    r = vr_ref[:, hv:2 * hv].astype(F32)
    o_ref[...] = (o * lax.rsqrt(ms + EPS) * ng_ref[...] * (r * jax.nn.sigmoid(r))).astype(BF16)


def _gla(layer, qk, vr, small, wg, bg, ng, *, bsz, seq):
    m = qk.shape[0]
    row = lambda b: (b, 0)
    return pl.pallas_call(
        _gla_kernel,
        grid=(bsz,),
        in_specs=[
            pl.BlockSpec((seq, IN_QK), row),
            pl.BlockSpec((seq, IN_VR), row),
            pl.BlockSpec((seq, IN_SMALL), row),
            _layer_spec(layer, (IN_SMALL, GLA_QK)),
            _layer_spec(layer, (1, GLA_QK)),
            _layer_spec(layer, (1, GLA_V)),
        ],
        out_specs=pl.BlockSpec((seq, GLA_V), row),
        out_shape=jax.ShapeDtypeStruct((m, GLA_V), BF16),
        scratch_shapes=[pltpu.VMEM((seq, GLA_QK), F32), pltpu.VMEM((seq, GLA_V), F32)],
        compiler_params=_params("parallel"),
        name="gla",
    )(qk, vr, small, wg, bg, ng)


def _foxprep_kernel(small_ref, b_ref, o_ref):
    seq = small_ref.shape[0]
    ff = small_ref[...].T[GLA_RANK:GLA_RANK + FOX_HEADS, :]
    x = _log_sigmoid(ff + b_ref[...])
    lane = lax.broadcasted_iota(jnp.int32, x.shape, 1)
    shift = 1
    while shift < seq:
        x = x + jnp.where(lane >= shift, pltpu.roll(x, shift, axis=1), 0.0)
        shift *= 2
    hi, mid, lo = _split3(x * LOG2E)
    terms = jnp.concatenate([hi, mid, lo, jnp.ones_like(hi)], axis=0)
    r = lax.broadcasted_iota(jnp.int32, (FOX_PAIRS * FOX_SIDE, FOX_TERMS * FOX_HEADS), 0)
    src = lax.broadcasted_iota(jnp.int32, (FOX_PAIRS * FOX_SIDE, FOX_TERMS * FOX_HEADS), 1)
    pair, slot = r >> SIDE_SHIFT, r & (FOX_SIDE - 1)
    head, term = 2 * pair + (slot >> TERMS_SHIFT), slot & (FOX_TERMS - 1)
    perm = ((slot < 2 * FOX_TERMS) & (src == term * FOX_HEADS + head)).astype(BF16)
    o_ref[...] = _dot(perm, terms).astype(BF16)


def _foxprep(layer, small, bias, *, bsz, seq):
    return pl.pallas_call(
        _foxprep_kernel,
        grid=(bsz,),
        in_specs=[
            pl.BlockSpec((seq, IN_SMALL), lambda b: (b, 0)),
            _layer_spec(layer, (FOX_HEADS, 1)),
        ],
        out_specs=pl.BlockSpec((None, FOX_PAIRS * FOX_SIDE, seq), lambda b: (b, 0, 0)),
        out_shape=jax.ShapeDtypeStruct((bsz, FOX_PAIRS * FOX_SIDE, seq), BF16),
        compiler_params=_params("parallel"),
        name="foxprep",
    )(small, bias)


def _fox_placement(h, base, *, key):
    r = lax.broadcasted_iota(jnp.int32, (FOX_SIDE, LANES), 0)
    rel = lax.broadcasted_iota(jnp.int32, (FOX_SIDE, LANES), 1) - base
    term = r - FOX_TERMS * h
    is_part = (term >= 0) & (term < FOX_TERMS - 1)
    is_one = term == FOX_TERMS - 1
    n = FOX_TERMS - 1
    if key:
        m = jnp.where(is_part & (rel == term + n), -1.0, jnp.where(is_one & (rel >= 0) & (rel < n), 1.0, 0.0))
    else:
        m = jnp.where(is_part & (rel == term), 1.0, jnp.where(is_one & (rel >= n) & (rel < 2 * n), 1.0, 0.0))
    return m.astype(BF16)


def _fox_kernel(q_ref, k_ref, v_ref, f_ref, o_ref, qx_sc, kx_sc, vx_sc, *, tk):
    seq = k_ref.shape[0]
    spare = (FOX_DH, 0)
    lane = lax.broadcasted_iota(jnp.int32, (seq, LANES), 1)
    q, k, v, f = q_ref[...], k_ref[...], v_ref[...], f_ref[...]
    for h in range(2):
        own = (lane < FOX_DH) if h == 0 else (lane >= FOX_DH)
        qx_sc[h] = jnp.where(own, q, _dot(f, _fox_placement(h, spare[h], key=False)).astype(BF16))
        kx_sc[h] = jnp.where(own, k, _dot(f, _fox_placement(h, spare[h], key=True)).astype(BF16))
        vx_sc[h] = jnp.where(own, v, jnp.ones_like(v))

    row = lax.broadcasted_iota(jnp.int32, (tk, tk), 0)
    col = lax.broadcasted_iota(jnp.int32, (tk, tk), 1)
    causal = col <= row

    items = [(h, r0, c0) for c0 in range(0, seq, tk) for h in range(2) for r0 in range(c0, seq, tk)]

    def logits(item):
        h, r0, c0 = item
        s = _dot_nt(qx_sc[h, r0:r0 + tk, :], kx_sc[h, c0:c0 + tk, :])
        return jnp.where(causal, s, -jnp.inf) if r0 == c0 else s

    state = {}
    s_next = logits(items[0])
    for n, (h, r0, c0) in enumerate(items):
        s = s_next
        if n + 1 < len(items):
            s_next = logits(items[n + 1])
        vx = vx_sc[h, c0:c0 + tk, :]
        m_new = jnp.max(s, axis=1, keepdims=True)
        if c0 == 0:
            acc = _dot(jnp.exp2(s - m_new).astype(BF16), vx)
        else:
            m_old, acc_old = state[h, r0]
            m_new = jnp.maximum(m_old, m_new)
            acc = jnp.exp2(m_old - m_new) * acc_old + _dot(jnp.exp2(s - m_new).astype(BF16), vx)
        state[h, r0] = (m_new, acc)

    first = lax.broadcasted_iota(jnp.int32, (tk, LANES), 1) < FOX_DH
    for r0 in range(0, seq, tk):
        a0, a1 = state[0, r0][1], state[1, r0][1]
        o_ref[r0:r0 + tk, :] = jnp.where(first, a0 / pltpu.roll(a0, FOX_DH, axis=1),
                                         a1 / pltpu.roll(a1, FOX_DH, axis=1)).astype(BF16)


def _fox(fox, f, *, bsz, seq, tk):
    m = fox.shape[0]
    return pl.pallas_call(
        functools.partial(_fox_kernel, tk=tk),
        grid=(bsz, FOX_PAIRS),
        in_specs=[
            pl.BlockSpec((seq, LANES), lambda b, p: (b, p)),
            pl.BlockSpec((seq, LANES), lambda b, p: (b, FOX_PAIRS + p)),
            pl.BlockSpec((seq, LANES), lambda b, p: (b, 2 * FOX_PAIRS + p)),
            pl.BlockSpec((None, None, seq, FOX_SIDE), lambda b, p: (b, p, 0, 0)),
        ],
        out_specs=pl.BlockSpec((seq, LANES), lambda b, p: (b, p)),
        out_shape=jax.ShapeDtypeStruct((m, FOX_W), BF16),
        scratch_shapes=[pltpu.VMEM((2, seq, LANES), BF16)] * 3,
        compiler_params=_params("parallel", "parallel"),
        name="fox",
    )(fox, fox, fox, f)


def _s5prep_kernel(are_ref, aim_ref, ldt_ref, bre_ref, bim_ref, abre_ref, abim_ref, bbre_ref, bbim_ref):
    lam_re = jnp.minimum(are_ref[...], -1e-4)
    lam_im = aim_ref[...]
    dt = jnp.exp(ldt_ref[...])
    mag = jnp.exp(lam_re * dt)
    ab_re = mag * jnp.cos(lam_im * dt)
    ab_im = mag * jnp.sin(lam_im * dt)
    den = lam_re * lam_re + lam_im * lam_im
    z_re = ((ab_re - 1.0) * lam_re + ab_im * lam_im) / den
    z_im = (ab_im * lam_re - (ab_re - 1.0) * lam_im) / den
    br, bi = bre_ref[...], bim_ref[...]
    abre_ref[...] = ab_re
    abim_ref[...] = ab_im
    bbre_ref[...] = z_re * br - z_im * bi
    bbim_ref[...] = z_re * bi + z_im * br


def _s5prep(a_re, a_im, log_dt, b_re, b_im):
    depth = a_re.shape[0]
    col = lambda a: a.reshape(depth, S5_NSTATE, 1)
    ldt = jnp.broadcast_to(log_dt[:, :, None], (depth, S5_GROUPS, S5_STATE))
    vec = pl.BlockSpec((None, S5_NSTATE, 1), lambda l: (l, 0, 0))
    mat = pl.BlockSpec((None, S5_NSTATE, S5_CH), lambda l: (l, 0, 0))
    return pl.pallas_call(
        _s5prep_kernel,
        grid=(depth,),
        in_specs=[vec, vec, vec, mat, mat],
        out_specs=[vec, vec, mat, mat],
        out_shape=[jax.ShapeDtypeStruct((depth, S5_NSTATE, 1), F32)] * 2
        + [jax.ShapeDtypeStruct((depth, S5_NSTATE, S5_CH), F32)] * 2,
        compiler_params=_params("parallel"),
        name="s5prep",
    )(col(a_re), col(a_im), col(ldt), b_re.reshape(depth, S5_NSTATE, S5_CH), b_im.reshape(depth, S5_NSTATE, S5_CH))


def _s5_kernel(u_ref, abre_ref, abim_ref, bre_ref, bim_ref, cre_ref, cim_ref, d_ref, gw_ref, gb_ref, o_ref,
               xre_sc, xim_sc, sre_sc, sim_sc, *, bsz, tb):
    @pl.when(pl.program_id(0) == 0)
    def _():
        sre_sc[...] = jnp.zeros_like(sre_sc)
        sim_sc[...] = jnp.zeros_like(sim_sc)

    u = u_ref[...]
    ub = u.astype(BF16)
    xre_sc[...] = _dot(ub, bre_ref[...])
    xim_sc[...] = _dot(ub, bim_ref[...])
    a_re = jnp.broadcast_to(abre_ref[...], (bsz, S5_NSTATE))
    a_im = jnp.broadcast_to(abim_ref[...], (bsz, S5_NSTATE))

    def step(t, carry):
        s_re, s_im = carry
        rows = pl.ds(pl.multiple_of(t * bsz, bsz), bsz)
        n_re = a_re * s_re - a_im * s_im + xre_sc[rows, :]
        n_im = a_re * s_im + a_im * s_re + xim_sc[rows, :]
        xre_sc[rows, :] = n_re
        xim_sc[rows, :] = n_im
        return n_re, n_im

    s_re, s_im = lax.fori_loop(0, tb, step, (sre_sc[...], sim_sc[...]), unroll=8)
    sre_sc[...] = s_re
    sim_sc[...] = s_im

    y = _dot(xre_sc[...].astype(BF16), cre_ref[...]) - _dot(xim_sc[...].astype(BF16), cim_ref[...])
    y = jax.nn.gelu(y + d_ref[...] * u, approximate=True)
    o_ref[...] = (y * jax.nn.sigmoid(_dot(y.astype(BF16), gw_ref[...]) + gb_ref[...])).astype(BF16)


def _s5(layer, u, ab_re, ab_im, b_re, b_im, c_re, c_im, d, glu_w, glu_b, *, bsz, seq, tb):
    rows = tb * bsz
    return pl.pallas_call(
        functools.partial(_s5_kernel, bsz=bsz, tb=tb),
        grid=(seq // tb,),
        in_specs=[
            pl.BlockSpec((rows, S5_WIDTH), lambda i: (i, 0)),
            _layer_spec(layer, (1, S5_NSTATE)),
            _layer_spec(layer, (1, S5_NSTATE)),
            _layer_spec(layer, (S5_WIDTH, S5_NSTATE)),
            _layer_spec(layer, (S5_WIDTH, S5_NSTATE)),
            _layer_spec(layer, (S5_NSTATE, S5_WIDTH)),
            _layer_spec(layer, (S5_NSTATE, S5_WIDTH)),
            _layer_spec(layer, (1, S5_WIDTH)),
            _layer_spec(layer, (S5_WIDTH, S5_WIDTH)),
            _layer_spec(layer, (1, S5_WIDTH)),
        ],
        out_specs=pl.BlockSpec((rows, S5_WIDTH), lambda i: (i, 0)),
        out_shape=jax.ShapeDtypeStruct((seq * bsz, S5_WIDTH), BF16),
        scratch_shapes=[pltpu.VMEM((rows, S5_NSTATE), F32), pltpu.VMEM((rows, S5_NSTATE), F32),
                        pltpu.VMEM((bsz, S5_NSTATE), F32), pltpu.VMEM((bsz, S5_NSTATE), F32)],
        compiler_params=_params("arbitrary"),
        name="s5",
    )(u, ab_re, ab_im, b_re, b_im, c_re, c_im, d, glu_w, glu_b)


def _merge_kernel(x_ref, gla_ref, s5_ref, fox_ref, pre_g_ref, wgate_ref, wgla_ref, ws5_ref, wfox_ref, wo_ref,
                  post_g_ref, o_ref):
    x = x_ref[...]
    d = x.shape[1]
    h = _rms(x, pre_g_ref[...]).astype(BF16)
    branches = ((gla_ref, wgla_ref), (s5_ref, ws5_ref), (fox_ref, wfox_ref))
    mix = None
    for n, (b_ref, w_ref) in enumerate(branches):
        gate = jax.nn.sigmoid(_dot(h, wgate_ref[:, n * d:(n + 1) * d]))
        term = gate * _dot(b_ref[...], w_ref[...])
        mix = term if mix is None else mix + term
    y = _dot(mix.astype(BF16), wo_ref[...])
    o_ref[...] = x + _rms(y, post_g_ref[...])


def _merge(layer, x, gla_o, s5_o, fox_o, pre_g, w_gate, w_gla, w_s5, w_fox, w_o, post_g, *, bsz, seq, tm):
    m, d = x.shape
    nl = seq // tm
    row = lambda b, l: (b * nl + l, 0)
    return pl.pallas_call(
        _merge_kernel,
        grid=(bsz, nl),
        in_specs=[
            pl.BlockSpec((tm, d), row),
            pl.BlockSpec((tm, GLA_V), row),
            pl.BlockSpec((tm, S5_WIDTH), lambda b, l: (l, b)),
            pl.BlockSpec((tm, FOX_W), row),
            _layer_spec(layer, (1, d)),
            _layer_spec(layer, (d, N_BRANCH * d)),
            _layer_spec(layer, (GLA_V, d)),
            _layer_spec(layer, (S5_WIDTH, d)),
            _layer_spec(layer, (FOX_W, d)),
            _layer_spec(layer, (d, d)),
            _layer_spec(layer, (1, d)),
        ],
        out_specs=pl.BlockSpec((tm, d), row),
        out_shape=jax.ShapeDtypeStruct((m, d), F32),
        compiler_params=_params("parallel", "parallel"),
        name="merge",
    )(x, gla_o, s5_o, fox_o, pre_g, w_gate, w_gla, w_s5, w_fox, w_o, post_g)


def _memkv_kernel(mem_ref, g_ref, w_ref, k_ref, v_ref):
    d = mem_ref.shape[1]
    h = _rms(mem_ref[...], g_ref[...]).astype(BF16)
    k_ref[...] = _dot(h, w_ref[:, 0:d]).astype(BF16)
    v_ref[...] = _dot(h, w_ref[:, d:2 * d]).astype(BF16)


def _memkv(layer, mem, g, w_kv, *, tm):
    m, d = mem.shape
    row = lambda i: (i, 0)
    return pl.pallas_call(
        _memkv_kernel,
        grid=(m // tm,),
        in_specs=[pl.BlockSpec((tm, d), row), _layer_spec(layer, (1, d)), _layer_spec(layer, (d, 2 * d))],
        out_specs=[pl.BlockSpec((tm, d), row)] * 2,
        out_shape=[jax.ShapeDtypeStruct((m, d), BF16)] * 2,
        compiler_params=_params("parallel"),
        name="memkv",
    )(mem, g, w_kv)


def _xattn_kernel(x_ref, k_ref, v_ref, pre_g_ref, wq_ref, wo_ref, post_g_ref, o_ref):
    x = x_ref[...]
    d = x.shape[1]
    dh = d // XA_HEADS
    h = _rms(x, pre_g_ref[...]).astype(BF16)
    q = (_dot(h, wq_ref[...]) * (dh ** -0.5)).astype(BF16)
    cols = [slice(n * dh, (n + 1) * dh) for n in range(XA_HEADS)]
    logits = [_dot_nt(q[:, c], k_ref[:, c]) for c in cols]
    probs = []
    for s in logits:
        e = jnp.exp(s - jnp.max(s, axis=1, keepdims=True))
        probs.append((e / jnp.sum(e, axis=1, keepdims=True)).astype(BF16))
    heads = [_dot(p, v_ref[:, c]).astype(BF16) for p, c in zip(probs, cols)]
    y = _dot(jnp.concatenate(heads, axis=1), wo_ref[...])
    o_ref[...] = x + _rms(y, post_g_ref[...])


def _xattn(layer, x, k, v, pre_g, w_q, w_o, post_g, *, bsz, seq, n_mem, tm):
    m, d = x.shape
    nl = seq // tm
    row = lambda b, l: (b * nl + l, 0)
    return pl.pallas_call(
        _xattn_kernel,
        grid=(bsz, nl),
        in_specs=[
            pl.BlockSpec((tm, d), row),
            pl.BlockSpec((n_mem, d), lambda b, l: (b, 0)),
            pl.BlockSpec((n_mem, d), lambda b, l: (b, 0)),
            _layer_spec(layer, (1, d)),
            _layer_spec(layer, (d, d)),
            _layer_spec(layer, (d, d)),
            _layer_spec(layer, (1, d)),
        ],
        out_specs=pl.BlockSpec((tm, d), row),
        out_shape=jax.ShapeDtypeStruct((m, d), F32),
        compiler_params=_params("parallel", "parallel"),
        name="xattn",
    )(x, k, v, pre_g, w_q, w_o, post_g)


def _tile(n, want):
    t = min(n, want)
    while n % t:
        t -= 1
    return t


def _block_diag(t):
    depth, g, r, c = t.shape
    eye = jnp.eye(g, dtype=t.dtype)
    return jnp.einsum("lgrc,gk->lgrkc", t, eye).reshape(depth, g * r, g * c)


def kernel(x, mem, ffn1_pre_g, ffn1_w_gu, ffn1_w_down, ffn1_post_g, mix_pre_g, w_in, gla_gate_w, gla_gate_b, gla_norm_g, w_gla_up, s5_a_re, s5_a_im, s5_log_dt, s5_b_re, s5_b_im, s5_c_re, s5_c_im, s5_d, s5_glu_w, s5_glu_b, w_s5_up, fox_f_b, w_fox_up, w_mix_out, mix_post_g, xa_pre_g, xa_mem_g, xa_w_q, xa_w_kv, xa_w_o, xa_post_g, ffn2_pre_g, ffn2_w_gu, ffn2_w_down, ffn2_post_g):
    bsz, seq, d = x.shape
    n_mem = mem.shape[1]
    depth = w_in.shape[0]
    d_ff = ffn1_w_down.shape[1]
    m = bsz * seq
    assert seq % CHUNK == 0 and d % LANES == 0

    tm_ffn = _tile(m, 1024)
    tf = _tile(d_ff // LANES, 2) * LANES
    tm = _tile(seq, 512)
    tk = _tile(seq, 512)
    tb = _tile(seq, 128)

    bf = lambda a: a.astype(BF16)
    vec = lambda a: a.reshape(depth, 1, a.shape[-1])

    sizes = (GLA_QK, GLA_QK, GLA_V, GLA_V, GLA_RANK, S5_WIDTH, FOX_W, FOX_W, FOX_W, FOX_HEADS, N_BRANCH * d)
    offs = [0]
    for s in sizes:
        offs.append(offs[-1] + s)
    seg = lambda i: bf(w_in[:, :, offs[i]:offs[i + 1]])
    pad = jnp.zeros((depth, d, IN_SMALL - GLA_RANK - FOX_HEADS), BF16)
    w_proj = jnp.concatenate([seg(0), seg(1), seg(2), seg(3), seg(5), seg(6), seg(7), seg(8), seg(4), seg(9), pad],
                             axis=2)
    w_gate = seg(10)
    gla_wg = bf(jnp.concatenate([gla_gate_w, jnp.zeros((depth, IN_SMALL - GLA_RANK, GLA_QK), gla_gate_w.dtype)],
                                axis=1))

    ab_re, ab_im, bb_re, bb_im = _s5prep(s5_a_re, s5_a_im, s5_log_dt, s5_b_re, s5_b_im)
    grp = lambda t: t.reshape(depth, S5_GROUPS, S5_STATE, S5_CH)
    s5_bre = bf(_block_diag(grp(bb_re).transpose(0, 1, 3, 2)))
    s5_bim = bf(_block_diag(grp(bb_im).transpose(0, 1, 3, 2)))
    s5_cre = bf(_block_diag(s5_c_re.transpose(0, 1, 3, 2)))
    s5_cim = bf(_block_diag(s5_c_im.transpose(0, 1, 3, 2)))
    ab_re = ab_re.reshape(depth, 1, S5_NSTATE)
    ab_im = ab_im.reshape(depth, 1, S5_NSTATE)

    ffn1_gu, ffn1_down, ffn2_gu, ffn2_down = bf(ffn1_w_gu), bf(ffn1_w_down), bf(ffn2_w_gu), bf(ffn2_w_down)
    gla_up, s5_up, fox_up, mix_out = bf(w_gla_up), bf(w_s5_up), bf(w_fox_up), bf(w_mix_out)
    glu_w, xa_q, xa_kv, xa_o = bf(s5_glu_w), bf(xa_w_q), bf(xa_w_kv), bf(xa_w_o)
    fox_b = fox_f_b.reshape(depth, FOX_HEADS, 1)

    xs = x.reshape(m, d)
    mem2 = mem.reshape(bsz * n_mem, d)
    for l in range(depth):
        xs = _ffn(l, xs, vec(ffn1_pre_g), ffn1_gu, ffn1_down, vec(ffn1_post_g), tm=tm_ffn, tf=tf)

        qk, vr, su, fox, small = _inproj(l, xs, vec(mix_pre_g), w_proj, bsz=bsz, seq=seq, tm=tm)
        gla_o = _gla(l, qk, vr, small, gla_wg, vec(gla_gate_b), vec(gla_norm_g), bsz=bsz, seq=seq)
        f_rows = _foxprep(l, small, fox_b, bsz=bsz, seq=seq)
        f_cols = f_rows.reshape(bsz, FOX_PAIRS, FOX_SIDE, seq).transpose(0, 1, 3, 2)
        fox_o = _fox(fox, f_cols, bsz=bsz, seq=seq, tk=tk)
        s5_o = _s5(l, su.reshape(seq * bsz, S5_WIDTH), ab_re, ab_im, s5_bre, s5_bim, s5_cre, s5_cim,
                   vec(s5_d), glu_w, vec(s5_glu_b), bsz=bsz, seq=seq, tb=tb)
        xs = _merge(l, xs, gla_o, s5_o.reshape(seq, bsz * S5_WIDTH), fox_o, vec(mix_pre_g), w_gate, gla_up,
                    s5_up, fox_up, mix_out, vec(mix_post_g), bsz=bsz, seq=seq, tm=tm)

        mk, mv = _memkv(l, mem2, vec(xa_mem_g), xa_kv, tm=_tile(bsz * n_mem, 512))
        xs = _xattn(l, xs, mk, mv, vec(xa_pre_g), xa_q, xa_o, vec(xa_post_g), bsz=bsz, seq=seq, n_mem=n_mem, tm=tm)

        xs = _ffn(l, xs, vec(ffn2_pre_g), ffn2_gu, ffn2_down, vec(ffn2_post_g), tm=tm_ffn, tf=tf)
    return xs.reshape(bsz, seq, d)
```

```python
import functools
import math

import jax
import jax.numpy as jnp
from jax import lax
from jax.experimental import pallas as pl
from jax.experimental.pallas import tpu as pltpu

F32 = jnp.float32
BF16 = jnp.bfloat16
EPS = 1e-6
LOG2E = math.log2(math.e)

LANES = 128
SUBLANES = 8
VMEM_LIMIT_BYTES = 56 * 1024 * 1024

CHUNK = 64
GLA_GROUP = 8
GLA_HEADS, GLA_DK, GLA_DV, GLA_RANK, GLA_TAU = 4, 32, 64, 16, 16.0
GLA_QK = GLA_HEADS * GLA_DK
GLA_V = GLA_HEADS * GLA_DV
DK_SHIFT = GLA_DK.bit_length() - 1
DV_SHIFT = GLA_DV.bit_length() - 1
S5_GROUPS, S5_CH, S5_STATE = 16, 16, 64
S5_WIDTH = S5_GROUPS * S5_CH
S5_NSTATE = S5_GROUPS * S5_STATE
FOX_HEADS, FOX_DH = 8, 64
FOX_W = FOX_HEADS * FOX_DH
FOX_PAIRS = FOX_W // LANES
FOX_TERMS = 4
HEADS_SHIFT = FOX_HEADS.bit_length() - 1
XA_HEADS = 4
N_BRANCH = 3

NT_DIMS = (((1,), (1,)), ((), ()))


def _params(*semantics):
    return pltpu.CompilerParams(dimension_semantics=semantics, vmem_limit_bytes=VMEM_LIMIT_BYTES)


def _layer_spec(layer, block, index_map=None, pipeline_mode=None):
    block = tuple(block)
    if index_map is None:
        index_map = lambda *_: (0,) * len(block)
    return pl.BlockSpec((None,) + block, lambda *ids: (layer,) + tuple(index_map(*ids)),
                        pipeline_mode=pipeline_mode)


def _rms(x, g):
    return x * lax.rsqrt(jnp.mean(x * x, axis=-1, keepdims=True) + EPS) * g


def _dot(a, b):
    return jnp.dot(a, b, preferred_element_type=F32)


def _dot_nt(a, b):
    return lax.dot_general(a, b, NT_DIMS, preferred_element_type=F32)


def _log_sigmoid(x):
    return jnp.minimum(x, 0.0) - jnp.log1p(jnp.exp(-jnp.abs(x)))


def _split3(x):
    hi = x.astype(BF16)
    r1 = x - hi.astype(F32)
    mid = r1.astype(BF16)
    lo = (r1 - mid.astype(F32)).astype(BF16)
    return hi, mid, lo


def _ffn_kernel(x_ref, pre_g_ref, wgu_ref, wd_ref, post_g_ref, o_ref, *, tf):
    d_ff = wd_ref.shape[0]
    x = x_ref[...]
    h = _rms(x, pre_g_ref[...]).astype(BF16)
    def gate_up(c0):
        return _dot(h, wgu_ref[:, c0:c0 + tf]), _dot(h, wgu_ref[:, d_ff + c0:d_ff + c0 + tf])

    acc = None
    nxt = gate_up(0)
    for c0 in range(0, d_ff, tf):
        gate, up = nxt
        if c0 + tf < d_ff:
            nxt = gate_up(c0 + tf)
        a = (gate * jax.nn.sigmoid(gate) * up).astype(BF16)
        part = _dot(a, wd_ref[c0:c0 + tf, :])
        acc = part if acc is None else acc + part
    o_ref[...] = x + 0.5 * _rms(acc, post_g_ref[...])


def _ffn(layer, x, pre_g, w_gu, w_down, post_g, *, tm, tf):
    m, d = x.shape
    d_ff = w_down.shape[1]
    resident = pl.Buffered(1)
    return pl.pallas_call(
        functools.partial(_ffn_kernel, tf=tf),
        grid=(m // tm,),
        in_specs=[
            pl.BlockSpec((tm, d), lambda i: (i, 0)),
            _layer_spec(layer, (1, d)),
            _layer_spec(layer, (d, 2 * d_ff), pipeline_mode=resident),
            _layer_spec(layer, (d_ff, d), pipeline_mode=resident),
            _layer_spec(layer, (1, d)),
        ],
        out_specs=pl.BlockSpec((tm, d), lambda i: (i, 0)),
        out_shape=jax.ShapeDtypeStruct((m, d), F32),
        compiler_params=_params("parallel"),
        name="ffn",
    )(x, pre_g, w_gu, w_down, post_g)


IN_QK = 2 * GLA_QK
IN_VR = 2 * GLA_V
IN_SU = S5_WIDTH
IN_FOX = 3 * FOX_W
IN_SMALL = LANES
OFF_VR = IN_QK
OFF_SU = OFF_VR + IN_VR
OFF_FOX = OFF_SU + IN_SU
OFF_SMALL = OFF_FOX + IN_FOX
IN_TOTAL = OFF_SMALL + IN_SMALL


def _inproj_kernel(x_ref, g_ref, w_ref, qk_ref, vr_ref, su_ref, fox_ref, small_ref):
    h = _rms(x_ref[...], g_ref[...]).astype(BF16)
    qk_ref[...] = _dot(h, w_ref[:, 0:OFF_VR])
    vr_ref[...] = _dot(h, w_ref[:, OFF_VR:OFF_SU]).astype(BF16)
    su_ref[...] = _dot(h, w_ref[:, OFF_SU:OFF_FOX])
    fox_ref[:, 0:FOX_W] = (_dot(h, w_ref[:, OFF_FOX:OFF_FOX + FOX_W]) * (FOX_DH ** -0.5 * LOG2E)).astype(BF16)
    fox_ref[:, FOX_W:IN_FOX] = _dot(h, w_ref[:, OFF_FOX + FOX_W:OFF_SMALL]).astype(BF16)
    small_ref[...] = _dot(h, w_ref[:, OFF_SMALL:IN_TOTAL])


def _inproj(layer, x, g, w, *, bsz, seq, tm):
    m, d = x.shape
    nl = seq // tm
    row = lambda b, l: (b * nl + l, 0)
    return pl.pallas_call(
        _inproj_kernel,
        grid=(bsz, nl),
        in_specs=[
            pl.BlockSpec((tm, d), row),
            _layer_spec(layer, (1, d)),
            _layer_spec(layer, (d, IN_TOTAL)),
        ],
        out_specs=[
            pl.BlockSpec((tm, IN_QK), row),
            pl.BlockSpec((tm, IN_VR), row),
            pl.BlockSpec((tm, IN_SU), lambda b, l: (l, b)),
            pl.BlockSpec((tm, IN_FOX), row),
            pl.BlockSpec((tm, IN_SMALL), row),
        ],
        out_shape=[
            jax.ShapeDtypeStruct((m, IN_QK), F32),
            jax.ShapeDtypeStruct((m, IN_VR), BF16),
            jax.ShapeDtypeStruct((seq, bsz * IN_SU), F32),
            jax.ShapeDtypeStruct((m, IN_FOX), BF16),
            jax.ShapeDtypeStruct((m, IN_SMALL), F32),
        ],
        compiler_params=_params("parallel", "parallel"),
        name="inproj",
    )(x, g, w)


def _gla_kernel(qk_ref, vr_ref, small_ref, wg_ref, bg_ref, ng_ref, o_ref, la_sc, o_sc):
    seq = qk_ref.shape[0]
    n_chunks = seq // CHUNK
    c, hq, hv = CHUNK, GLA_QK, GLA_V

    z = _dot(small_ref[...].astype(BF16), wg_ref[...]) + bg_ref[...]
    la_sc[...] = _log_sigmoid(z) * (1.0 / GLA_TAU)

    row = lax.broadcasted_iota(jnp.int32, (c, c), 0)
    col = lax.broadcasted_iota(jnp.int32, (c, c), 1)
    tri = (col <= row).astype(BF16)
    row_x = lax.broadcasted_iota(jnp.int32, (GLA_HEADS * c, c), 0)
    col_x = lax.broadcasted_iota(jnp.int32, (GLA_HEADS * c, c), 1)
    lower = col_x <= (row_x & (c - 1))
    lane_q = lax.broadcasted_iota(jnp.int32, (c, hq), 1)
    lane_v = lax.broadcasted_iota(jnp.int32, (c, hv), 1)
    q_head = [(lane_q >> DK_SHIFT) == h for h in range(GLA_HEADS)]
    v_head = [(lane_v >> DV_SHIFT) == h for h in range(GLA_HEADS)]
    st_row = lax.broadcasted_iota(jnp.int32, (hv, hq), 0)
    st_col = lax.broadcasted_iota(jnp.int32, (hv, hq), 1)
    same_head = (st_row >> DV_SHIFT) == (st_col >> DK_SHIFT)

    def expand(t):
        return jnp.concatenate([jnp.where(q_head[h], t, 0.0) for h in range(GLA_HEADS)], axis=0).astype(BF16)

    per_chunk = []
    for first in range(0, n_chunks, GLA_GROUP):
        group = range(first, min(first + GLA_GROUP, n_chunks))
        stage1 = []
        for i in group:
            hi, mid, lo = _split3(la_sc[i * c:(i + 1) * c, :])
            stage1.append(_dot(tri, hi) + _dot(tri, mid) + _dot(tri, lo))
        stage2 = []
        for i, g in zip(group, stage1):
            rows = slice(i * c, (i + 1) * c)
            g_last = g[c - 1:c, :]
            eg, ieg = jnp.exp(g), jnp.exp(-g)
            qc = qk_ref[rows, 0:hq] * (GLA_DK ** -0.5)
            kc = qk_ref[rows, hq:2 * hq]
            v = vr_ref[rows, 0:hv]
            q_fwd = qc * eg
            a_fwd = _dot_nt(expand(q_fwd), (kc * ieg).astype(BF16))
            a_bwd = _dot_nt(expand(qc * ieg), (kc * eg).astype(BF16))
            kw = (kc * jnp.exp(g_last - g)).astype(BF16)
            ds_t = _dot(v.astype(F32).T.astype(BF16), kw)
            stage2.append((a_fwd, a_bwd, v, q_fwd.astype(BF16), jnp.where(same_head, ds_t, 0.0), jnp.exp(g_last)))
        for a_fwd, a_bwd, v, q_fwd, ds_t, decay in stage2:
            attn = jnp.where(lower, a_fwd, a_bwd).astype(BF16)
            p = _dot(attn, v)
            o = jnp.where(v_head[0], p[0:c, :], 0.0)
            for h in range(1, GLA_HEADS):
                o = o + jnp.where(v_head[h], p[h * c:(h + 1) * c, :], 0.0)
            per_chunk.append((o, q_fwd, ds_t, decay))

    s_t = jnp.zeros((hv, hq), F32)
    states = []
    for _, _, ds_t, decay in per_chunk:
        states.append(s_t.astype(BF16))
        s_t = s_t * decay + ds_t

    for i, (o, q_fwd, _, _) in enumerate(per_chunk):
        o_sc[i * c:(i + 1) * c, :] = o + _dot_nt(q_fwd, states[i])

    o = o_sc[...]
    gr = lax.broadcasted_iota(jnp.int32, (hv, hv), 0) >> DV_SHIFT
    gc = lax.broadcasted_iota(jnp.int32, (hv, hv), 1) >> DV_SHIFT
    avg = jnp.where(gr == gc, 1.0 / GLA_DV, 0.0).astype(BF16)
    hi, mid, lo = _split3(o * o)
    ms = _dot(hi, avg) + _dot(mid, avg) + _dot(lo, avg)
    r = vr_ref[:, hv:2 * hv].astype(F32)
    o_ref[...] = (o * lax.rsqrt(ms + EPS) * ng_ref[...] * (r * jax.nn.sigmoid(r))).astype(BF16)


def _gla(layer, qk, vr, small, wg, bg, ng, *, bsz, seq):
    m = qk.shape[0]
    row = lambda b: (b, 0)
    return pl.pallas_call(
        _gla_kernel,
        grid=(bsz,),
        in_specs=[
            pl.BlockSpec((seq, IN_QK), row),
            pl.BlockSpec((seq, IN_VR), row),
            pl.BlockSpec((seq, IN_SMALL), row),
            _layer_spec(layer, (IN_SMALL, GLA_QK)),
            _layer_spec(layer, (1, GLA_QK)),
            _layer_spec(layer, (1, GLA_V)),
        ],
        out_specs=pl.BlockSpec((seq, GLA_V), row),
        out_shape=jax.ShapeDtypeStruct((m, GLA_V), BF16),
        scratch_shapes=[pltpu.VMEM((seq, GLA_QK), F32), pltpu.VMEM((seq, GLA_V), F32)],
        compiler_params=_params("parallel"),
        name="gla",
    )(qk, vr, small, wg, bg, ng)


def _foxprep_kernel(small_ref, b_ref, o_ref):
    seq = small_ref.shape[0]
    ff = small_ref[...].T[GLA_RANK:GLA_RANK + FOX_HEADS, :]
    x = _log_sigmoid(ff + b_ref[...])
    lane = lax.broadcasted_iota(jnp.int32, x.shape, 1)
    shift = 1
    while shift < seq:
        x = x + jnp.where(lane >= shift, pltpu.roll(x, shift, axis=1), 0.0)
        shift *= 2
    hi, mid, lo = (t.astype(F32) for t in _split3(x * LOG2E))
    pad = jnp.zeros((LANES - FOX_TERMS * FOX_HEADS, seq), F32)
    o_ref[...] = jnp.concatenate([hi, mid, lo, jnp.ones_like(x), pad], axis=0).T.astype(BF16)


def _foxprep(layer, small, bias, *, bsz, seq):
    return pl.pallas_call(
        _foxprep_kernel,
        grid=(bsz,),
        in_specs=[
            pl.BlockSpec((seq, IN_SMALL), lambda b: (b, 0)),
            _layer_spec(layer, (FOX_HEADS, 1)),
        ],
        out_specs=pl.BlockSpec((seq, LANES), lambda b: (b, 0)),
        out_shape=jax.ShapeDtypeStruct((bsz * seq, LANES), BF16),
        compiler_params=_params("parallel"),
        name="foxprep",
    )(small, bias)


def _fox_placement(head, base, *, key):
    r = lax.broadcasted_iota(jnp.int32, (LANES, LANES), 0)
    rel = lax.broadcasted_iota(jnp.int32, (LANES, LANES), 1) - base
    term = r >> HEADS_SHIFT
    mine = ((r & (FOX_HEADS - 1)) == head) & (term < FOX_TERMS)
    is_part = mine & (term < FOX_TERMS - 1)
    is_one = mine & (term == FOX_TERMS - 1)
    n = FOX_TERMS - 1
    if key:
        m = jnp.where(is_part & (rel == term + n), -1.0, jnp.where(is_one & (rel >= 0) & (rel < n), 1.0, 0.0))
    else:
        m = jnp.where(is_part & (rel == term), 1.0, jnp.where(is_one & (rel >= n) & (rel < 2 * n), 1.0, 0.0))
    return m.astype(BF16)


def _fox_kernel(q_ref, k_ref, v_ref, f_ref, o_ref, qx_sc, kx_sc, vx_sc, *, tk):
    seq = k_ref.shape[0]
    spare = (FOX_DH, 0)
    lane = lax.broadcasted_iota(jnp.int32, (seq, LANES), 1)
    q, k, v, f = q_ref[...], k_ref[...], v_ref[...], f_ref[...]
    for h in range(2):
        own = (lane < FOX_DH) if h == 0 else (lane >= FOX_DH)
        head = 2 * pl.program_id(1) + h
        qx_sc[h] = jnp.where(own, q, _dot(f, _fox_placement(head, spare[h], key=False)).astype(BF16))
        kx_sc[h] = jnp.where(own, k, _dot(f, _fox_placement(head, spare[h], key=True)).astype(BF16))
        vx_sc[h] = jnp.where(own, v, jnp.ones_like(v))

    row = lax.broadcasted_iota(jnp.int32, (tk, tk), 0)
    col = lax.broadcasted_iota(jnp.int32, (tk, tk), 1)
    causal = col <= row

    items = [(h, r0, c0) for c0 in range(0, seq, tk) for h in range(2) for r0 in range(c0, seq, tk)]

    def logits(item):
        h, r0, c0 = item
        s = _dot_nt(qx_sc[h, r0:r0 + tk, :], kx_sc[h, c0:c0 + tk, :])
        return jnp.where(causal, s, -jnp.inf) if r0 == c0 else s

    state = {}
    s_next = logits(items[0])
    for n, (h, r0, c0) in enumerate(items):
        s = s_next
        if n + 1 < len(items):
            s_next = logits(items[n + 1])
        vx = vx_sc[h, c0:c0 + tk, :]
        m_new = jnp.max(s, axis=1, keepdims=True)
        if c0 == 0:
            acc = _dot(jnp.exp2(s - m_new).astype(BF16), vx)
        else:
            m_old, acc_old = state[h, r0]
            m_new = jnp.maximum(m_old, m_new)
            acc = jnp.exp2(m_old - m_new) * acc_old + _dot(jnp.exp2(s - m_new).astype(BF16), vx)
        state[h, r0] = (m_new, acc)

    first = lax.broadcasted_iota(jnp.int32, (tk, LANES), 1) < FOX_DH
    for r0 in range(0, seq, tk):
        a0, a1 = state[0, r0][1], state[1, r0][1]
        o_ref[r0:r0 + tk, :] = jnp.where(first, a0 / pltpu.roll(a0, FOX_DH, axis=1),
                                         a1 / pltpu.roll(a1, FOX_DH, axis=1)).astype(BF16)


def _fox(fox, f, *, bsz, seq, tk):
    m = fox.shape[0]
    return pl.pallas_call(
        functools.partial(_fox_kernel, tk=tk),
        grid=(bsz, FOX_PAIRS),
        in_specs=[
            pl.BlockSpec((seq, LANES), lambda b, p: (b, p)),
            pl.BlockSpec((seq, LANES), lambda b, p: (b, FOX_PAIRS + p)),
            pl.BlockSpec((seq, LANES), lambda b, p: (b, 2 * FOX_PAIRS + p)),
            pl.BlockSpec((seq, LANES), lambda b, p: (b, 0)),
        ],
        out_specs=pl.BlockSpec((seq, LANES), lambda b, p: (b, p)),
        out_shape=jax.ShapeDtypeStruct((m, FOX_W), BF16),
        scratch_shapes=[pltpu.VMEM((2, seq, LANES), BF16)] * 3,
        compiler_params=_params("parallel", "parallel"),
        name="fox",
    )(fox, fox, fox, f)


def _s5prep_kernel(are_ref, aim_ref, ldt_ref, bre_ref, bim_ref, abre_ref, abim_ref, bbre_ref, bbim_ref):
    lam_re = jnp.minimum(are_ref[...], -1e-4)
    lam_im = aim_ref[...]
    dt = jnp.exp(ldt_ref[...])
    mag = jnp.exp(lam_re * dt)
    ab_re = mag * jnp.cos(lam_im * dt)
    ab_im = mag * jnp.sin(lam_im * dt)
    den = lam_re * lam_re + lam_im * lam_im
    z_re = ((ab_re - 1.0) * lam_re + ab_im * lam_im) / den
    z_im = (ab_im * lam_re - (ab_re - 1.0) * lam_im) / den
    br, bi = bre_ref[...], bim_ref[...]
    abre_ref[...] = ab_re
    abim_ref[...] = ab_im
    bbre_ref[...] = z_re * br - z_im * bi
    bbim_ref[...] = z_re * bi + z_im * br


def _s5prep(a_re, a_im, log_dt, b_re, b_im):
    depth = a_re.shape[0]
    col = lambda a: a.reshape(depth, S5_NSTATE, 1)
    ldt = jnp.broadcast_to(log_dt[:, :, None], (depth, S5_GROUPS, S5_STATE))
    vec = pl.BlockSpec((None, S5_NSTATE, 1), lambda l: (l, 0, 0))
    mat = pl.BlockSpec((None, S5_NSTATE, S5_CH), lambda l: (l, 0, 0))
    return pl.pallas_call(
        _s5prep_kernel,
        grid=(depth,),
        in_specs=[vec, vec, vec, mat, mat],
        out_specs=[vec, vec, mat, mat],
        out_shape=[jax.ShapeDtypeStruct((depth, S5_NSTATE, 1), F32)] * 2
        + [jax.ShapeDtypeStruct((depth, S5_NSTATE, S5_CH), F32)] * 2,
        compiler_params=_params("parallel"),
        name="s5prep",
    )(col(a_re), col(a_im), col(ldt), b_re.reshape(depth, S5_NSTATE, S5_CH), b_im.reshape(depth, S5_NSTATE, S5_CH))


def _s5_kernel(u_ref, abre_ref, abim_ref, bre_ref, bim_ref, cre_ref, cim_ref, d_ref, gw_ref, gb_ref, o_ref,
               sre_sc, sim_sc, *, bsz, ts):
    @pl.when(pl.program_id(0) == 0)
    def _():
        sre_sc[...] = jnp.zeros_like(sre_sc)
        sim_sc[...] = jnp.zeros_like(sim_sc)

    rows = ts * bsz
    n_sub = u_ref.shape[0] // rows
    a_re = jnp.broadcast_to(abre_ref[...], (bsz, S5_NSTATE))
    a_im = jnp.broadcast_to(abim_ref[...], (bsz, S5_NSTATE))

    bu = []
    for k in range(n_sub):
        ub = u_ref[k * rows:(k + 1) * rows, :].astype(BF16)
        bu.append((_dot(ub, bre_ref[...]), _dot(ub, bim_ref[...])))

    s_re, s_im = sre_sc[...], sim_sc[...]
    for k in range(n_sub):
        bu_re, bu_im = bu[k]
        x_re, x_im = [], []
        for t in range(ts):
            r = slice(t * bsz, (t + 1) * bsz)
            s_re, s_im = (a_re * s_re - a_im * s_im + bu_re[r, :], a_re * s_im + a_im * s_re + bu_im[r, :])
            x_re.append(s_re)
            x_im.append(s_im)
        x_re = jnp.concatenate(x_re, axis=0).astype(BF16)
        x_im = jnp.concatenate(x_im, axis=0).astype(BF16)
        u = u_ref[k * rows:(k + 1) * rows, :]
        y = _dot(x_re, cre_ref[...]) - _dot(x_im, cim_ref[...])
        y = jax.nn.gelu(y + d_ref[...] * u, approximate=True)
        o_ref[k * rows:(k + 1) * rows, :] = (
            y * jax.nn.sigmoid(_dot(y.astype(BF16), gw_ref[...]) + gb_ref[...])).astype(BF16)
    sre_sc[...] = s_re
    sim_sc[...] = s_im


def _s5(layer, u, ab_re, ab_im, b_re, b_im, c_re, c_im, d, glu_w, glu_b, *, bsz, seq, tb, ts):
    rows = tb * bsz
    return pl.pallas_call(
        functools.partial(_s5_kernel, bsz=bsz, ts=ts),
        grid=(seq // tb,),
        in_specs=[
            pl.BlockSpec((rows, S5_WIDTH), lambda i: (i, 0)),
            _layer_spec(layer, (1, S5_NSTATE)),
            _layer_spec(layer, (1, S5_NSTATE)),
            _layer_spec(layer, (S5_WIDTH, S5_NSTATE)),
            _layer_spec(layer, (S5_WIDTH, S5_NSTATE)),
            _layer_spec(layer, (S5_NSTATE, S5_WIDTH)),
            _layer_spec(layer, (S5_NSTATE, S5_WIDTH)),
            _layer_spec(layer, (1, S5_WIDTH)),
            _layer_spec(layer, (S5_WIDTH, S5_WIDTH)),
            _layer_spec(layer, (1, S5_WIDTH)),
        ],
        out_specs=pl.BlockSpec((rows, S5_WIDTH), lambda i: (i, 0)),
        out_shape=jax.ShapeDtypeStruct((seq * bsz, S5_WIDTH), BF16),
        scratch_shapes=[pltpu.VMEM((bsz, S5_NSTATE), F32), pltpu.VMEM((bsz, S5_NSTATE), F32)],
        compiler_params=_params("arbitrary"),
        name="s5",
    )(u, ab_re, ab_im, b_re, b_im, c_re, c_im, d, glu_w, glu_b)


def _merge_kernel(x_ref, gla_ref, s5_ref, fox_ref, pre_g_ref, wgate_ref, wgla_ref, ws5_ref, wfox_ref, wo_ref,
                  post_g_ref, o_ref):
    x = x_ref[...]
    d = x.shape[1]
    h = _rms(x, pre_g_ref[...]).astype(BF16)
    branches = ((gla_ref, wgla_ref), (s5_ref, ws5_ref), (fox_ref, wfox_ref))
    mix = None
    for n, (b_ref, w_ref) in enumerate(branches):
        gate = jax.nn.sigmoid(_dot(h, wgate_ref[:, n * d:(n + 1) * d]))
        term = gate * _dot(b_ref[...], w_ref[...])
        mix = term if mix is None else mix + term
    y = _dot(mix.astype(BF16), wo_ref[...])
    o_ref[...] = x + _rms(y, post_g_ref[...])


def _merge(layer, x, gla_o, s5_o, fox_o, pre_g, w_gate, w_gla, w_s5, w_fox, w_o, post_g, *, bsz, seq, tm):
    m, d = x.shape
    nl = seq // tm
    row = lambda b, l: (b * nl + l, 0)
    return pl.pallas_call(
        _merge_kernel,
        grid=(bsz, nl),
        in_specs=[
            pl.BlockSpec((tm, d), row),
            pl.BlockSpec((tm, GLA_V), row),
            pl.BlockSpec((tm, S5_WIDTH), lambda b, l: (l, b)),
            pl.BlockSpec((tm, FOX_W), row),
            _layer_spec(layer, (1, d)),
            _layer_spec(layer, (d, N_BRANCH * d)),
            _layer_spec(layer, (GLA_V, d)),
            _layer_spec(layer, (S5_WIDTH, d)),
            _layer_spec(layer, (FOX_W, d)),
            _layer_spec(layer, (d, d)),
            _layer_spec(layer, (1, d)),
        ],
        out_specs=pl.BlockSpec((tm, d), row),
        out_shape=jax.ShapeDtypeStruct((m, d), F32),
        compiler_params=_params("parallel", "parallel"),
        name="merge",
    )(x, gla_o, s5_o, fox_o, pre_g, w_gate, w_gla, w_s5, w_fox, w_o, post_g)


def _memkv_kernel(mem_ref, g_ref, w_ref, k_ref, v_ref):
    d = mem_ref.shape[1]
    h = _rms(mem_ref[...], g_ref[...]).astype(BF16)
    k_ref[...] = _dot(h, w_ref[:, 0:d]).astype(BF16)
    v_ref[...] = _dot(h, w_ref[:, d:2 * d]).astype(BF16)


def _memkv(layer, mem, g, w_kv, *, tm):
    m, d = mem.shape
    row = lambda i: (i, 0)
    return pl.pallas_call(
        _memkv_kernel,
        grid=(m // tm,),
        in_specs=[pl.BlockSpec((tm, d), row), _layer_spec(layer, (1, d)), _layer_spec(layer, (d, 2 * d))],
        out_specs=[pl.BlockSpec((tm, d), row)] * 2,
        out_shape=[jax.ShapeDtypeStruct((m, d), BF16)] * 2,
        compiler_params=_params("parallel"),
        name="memkv",
    )(mem, g, w_kv)


def _xattn_kernel(x_ref, k_ref, v_ref, pre_g_ref, wq_ref, wo_ref, post_g_ref, o_ref):
    x = x_ref[...]
    d = x.shape[1]
    dh = d // XA_HEADS
    h = _rms(x, pre_g_ref[...]).astype(BF16)
    q = (_dot(h, wq_ref[...]) * (dh ** -0.5)).astype(BF16)
    cols = [slice(n * dh, (n + 1) * dh) for n in range(XA_HEADS)]
    logits = [_dot_nt(q[:, c], k_ref[:, c]) for c in cols]
    probs = []
    for s in logits:
        e = jnp.exp(s - jnp.max(s, axis=1, keepdims=True))
        probs.append((e / jnp.sum(e, axis=1, keepdims=True)).astype(BF16))
    heads = [_dot(p, v_ref[:, c]).astype(BF16) for p, c in zip(probs, cols)]
    y = _dot(jnp.concatenate(heads, axis=1), wo_ref[...])
    o_ref[...] = x + _rms(y, post_g_ref[...])


def _xattn(layer, x, k, v, pre_g, w_q, w_o, post_g, *, bsz, seq, n_mem, tm):
    m, d = x.shape
    nl = seq // tm
    row = lambda b, l: (b * nl + l, 0)
    return pl.pallas_call(
        _xattn_kernel,
        grid=(bsz, nl),
        in_specs=[
            pl.BlockSpec((tm, d), row),
            pl.BlockSpec((n_mem, d), lambda b, l: (b, 0)),
            pl.BlockSpec((n_mem, d), lambda b, l: (b, 0)),
            _layer_spec(layer, (1, d)),
            _layer_spec(layer, (d, d)),
            _layer_spec(layer, (d, d)),
            _layer_spec(layer, (1, d)),
        ],
        out_specs=pl.BlockSpec((tm, d), row),
        out_shape=jax.ShapeDtypeStruct((m, d), F32),
        compiler_params=_params("parallel", "parallel"),
        name="xattn",
    )(x, k, v, pre_g, w_q, w_o, post_g)


def _tile(n, want):
    t = min(n, want)
    while n % t:
        t -= 1
    return t


def _block_diag(t):
    depth, g, r, c = t.shape
    eye = jnp.eye(g, dtype=t.dtype)
    return jnp.einsum("lgrc,gk->lgrkc", t, eye).reshape(depth, g * r, g * c)


def kernel(x, mem, ffn1_pre_g, ffn1_w_gu, ffn1_w_down, ffn1_post_g, mix_pre_g, w_in, gla_gate_w, gla_gate_b, gla_norm_g, w_gla_up, s5_a_re, s5_a_im, s5_log_dt, s5_b_re, s5_b_im, s5_c_re, s5_c_im, s5_d, s5_glu_w, s5_glu_b, w_s5_up, fox_f_b, w_fox_up, w_mix_out, mix_post_g, xa_pre_g, xa_mem_g, xa_w_q, xa_w_kv, xa_w_o, xa_post_g, ffn2_pre_g, ffn2_w_gu, ffn2_w_down, ffn2_post_g):
    bsz, seq, d = x.shape
    n_mem = mem.shape[1]
    depth = w_in.shape[0]
    d_ff = ffn1_w_down.shape[1]
    m = bsz * seq
    assert seq % CHUNK == 0 and d % LANES == 0

    tm_ffn = _tile(m, 1024)
    tf = _tile(d_ff // LANES, 2) * LANES
    tm = _tile(seq, 512)
    tk = _tile(seq, 512)
    ts = _tile(seq, 128)
    tb = 2 * ts if seq % (2 * ts) == 0 else ts

    bf = lambda a: a.astype(BF16)
    vec = lambda a: a.reshape(depth, 1, a.shape[-1])

    sizes = (GLA_QK, GLA_QK, GLA_V, GLA_V, GLA_RANK, S5_WIDTH, FOX_W, FOX_W, FOX_W, FOX_HEADS, N_BRANCH * d)
    offs = [0]
    for s in sizes:
        offs.append(offs[-1] + s)
    w_in_b = bf(w_in)
    seg = lambda i: w_in_b[:, :, offs[i]:offs[i + 1]]
    pad = jnp.zeros((depth, d, IN_SMALL - GLA_RANK - FOX_HEADS), BF16)
    w_proj = jnp.concatenate([seg(0), seg(1), seg(2), seg(3), seg(5), seg(6), seg(7), seg(8), seg(4), seg(9), pad],
                             axis=2)
    w_gate = seg(10)
    gla_wg = bf(jnp.concatenate([gla_gate_w, jnp.zeros((depth, IN_SMALL - GLA_RANK, GLA_QK), gla_gate_w.dtype)],
                                axis=1))

    ab_re, ab_im, bb_re, bb_im = _s5prep(s5_a_re, s5_a_im, s5_log_dt, s5_b_re, s5_b_im)
    grp = lambda t: t.reshape(depth, S5_GROUPS, S5_STATE, S5_CH)
    s5_bre = bf(_block_diag(grp(bb_re).transpose(0, 1, 3, 2)))
    s5_bim = bf(_block_diag(grp(bb_im).transpose(0, 1, 3, 2)))
    s5_cre = bf(_block_diag(s5_c_re.transpose(0, 1, 3, 2)))
    s5_cim = bf(_block_diag(s5_c_im.transpose(0, 1, 3, 2)))
    ab_re = ab_re.reshape(depth, 1, S5_NSTATE)
    ab_im = ab_im.reshape(depth, 1, S5_NSTATE)

    ffn1_gu, ffn1_down, ffn2_gu, ffn2_down = bf(ffn1_w_gu), bf(ffn1_w_down), bf(ffn2_w_gu), bf(ffn2_w_down)
    gla_up, s5_up, fox_up, mix_out = bf(w_gla_up), bf(w_s5_up), bf(w_fox_up), bf(w_mix_out)
    glu_w, xa_q, xa_kv, xa_o = bf(s5_glu_w), bf(xa_w_q), bf(xa_w_kv), bf(xa_w_o)
    fox_b = fox_f_b.reshape(depth, FOX_HEADS, 1)

    xs = x.reshape(m, d)
    mem2 = mem.reshape(bsz * n_mem, d)
    for l in range(depth):
        xs = _ffn(l, xs, vec(ffn1_pre_g), ffn1_gu, ffn1_down, vec(ffn1_post_g), tm=tm_ffn, tf=tf)

        qk, vr, su, fox, small = _inproj(l, xs, vec(mix_pre_g), w_proj, bsz=bsz, seq=seq, tm=tm)
        gla_o = _gla(l, qk, vr, small, gla_wg, vec(gla_gate_b), vec(gla_norm_g), bsz=bsz, seq=seq)
        f_cols = _foxprep(l, small, fox_b, bsz=bsz, seq=seq)
        fox_o = _fox(fox, f_cols, bsz=bsz, seq=seq, tk=tk)
        s5_o = _s5(l, su.reshape(seq * bsz, S5_WIDTH), ab_re, ab_im, s5_bre, s5_bim, s5_cre, s5_cim,
                   vec(s5_d), glu_w, vec(s5_glu_b), bsz=bsz, seq=seq, tb=tb, ts=ts)
        xs = _merge(l, xs, gla_o, s5_o.reshape(seq, bsz * S5_WIDTH), fox_o, vec(mix_pre_g), w_gate, gla_up,
                    s5_up, fox_up, mix_out, vec(mix_post_g), bsz=bsz, seq=seq, tm=tm)

        mk, mv = _memkv(l, mem2, vec(xa_mem_g), xa_kv, tm=_tile(bsz * n_mem, 512))
        xs = _xattn(l, xs, mk, mv, vec(xa_pre_g), xa_q, xa_o, vec(xa_post_g), bsz=bsz, seq=seq, n_mem=n_mem, tm=tm)

        xs = _ffn(l, xs, vec(ffn2_pre_g), ffn2_gu, ffn2_down, vec(ffn2_post_g), tm=tm_ffn, tf=tf)
    return xs.reshape(bsz, seq, d)
```

```python
import functools
import math

import jax
import jax.numpy as jnp
from jax import lax
from jax.experimental import pallas as pl
from jax.experimental.pallas import tpu as pltpu

F32 = jnp.float32
BF16 = jnp.bfloat16
EPS = 1e-6
LOG2E = math.log2(math.e)

LANES = 128
SUBLANES = 8
VMEM_LIMIT_BYTES = 56 * 1024 * 1024

CHUNK = 64
GLA_GROUP = 8
GLA_HEADS, GLA_DK, GLA_DV, GLA_RANK, GLA_TAU = 4, 32, 64, 16, 16.0
GLA_QK = GLA_HEADS * GLA_DK
GLA_V = GLA_HEADS * GLA_DV
DK_SHIFT = GLA_DK.bit_length() - 1
DV_SHIFT = GLA_DV.bit_length() - 1
S5_GROUPS, S5_CH, S5_STATE = 16, 16, 64
S5_WIDTH = S5_GROUPS * S5_CH
S5_NSTATE = S5_GROUPS * S5_STATE
FOX_HEADS, FOX_DH = 8, 64
FOX_W = FOX_HEADS * FOX_DH
FOX_PAIRS = FOX_W // LANES
FOX_TERMS = 4
HEADS_SHIFT = FOX_HEADS.bit_length() - 1
XA_HEADS = 4
N_BRANCH = 3

NT_DIMS = (((1,), (1,)), ((), ()))
RESIDENT = pl.Buffered(1)


def _params(*semantics):
    return pltpu.CompilerParams(dimension_semantics=semantics, vmem_limit_bytes=VMEM_LIMIT_BYTES)


def _layer_spec(layer, block, index_map=None, pipeline_mode=None):
    block = tuple(block)
    if index_map is None:
        index_map = lambda *_: (0,) * len(block)
    return pl.BlockSpec((None,) + block, lambda *ids: (layer,) + tuple(index_map(*ids)),
                        pipeline_mode=pipeline_mode)


def _rms(x, g):
    return x * lax.rsqrt(jnp.mean(x * x, axis=-1, keepdims=True) + EPS) * g


def _dot(a, b):
    return jnp.dot(a, b, preferred_element_type=F32)


def _dot_nt(a, b):
    return lax.dot_general(a, b, NT_DIMS, preferred_element_type=F32)


def _log_sigmoid(x):
    return jnp.minimum(x, 0.0) - jnp.log1p(jnp.exp(-jnp.abs(x)))


def _split3(x):
    hi = x.astype(BF16)
    r1 = x - hi.astype(F32)
    mid = r1.astype(BF16)
    lo = (r1 - mid.astype(F32)).astype(BF16)
    return hi, mid, lo


def _ffn_kernel(x_ref, pre_g_ref, wgu_ref, wd_ref, post_g_ref, o_ref, *, tf):
    d_ff = wd_ref.shape[0]
    x = x_ref[...]
    h = _rms(x, pre_g_ref[...]).astype(BF16)

    def gate_up(c0):
        return _dot(h, wgu_ref[:, c0:c0 + tf]), _dot(h, wgu_ref[:, d_ff + c0:d_ff + c0 + tf])

    acc = None
    nxt = gate_up(0)
    for c0 in range(0, d_ff, tf):
        gate, up = nxt
        if c0 + tf < d_ff:
            nxt = gate_up(c0 + tf)
        a = (gate * jax.nn.sigmoid(gate) * up).astype(BF16)
        part = _dot(a, wd_ref[c0:c0 + tf, :])
        acc = part if acc is None else acc + part
    o_ref[...] = x + 0.5 * _rms(acc, post_g_ref[...])


def _ffn(layer, x, pre_g, w_gu, w_down, post_g, *, tm, tf):
    m, d = x.shape
    d_ff = w_down.shape[1]
    return pl.pallas_call(
        functools.partial(_ffn_kernel, tf=tf),
        grid=(m // tm,),
        in_specs=[
            pl.BlockSpec((tm, d), lambda i: (i, 0)),
            _layer_spec(layer, (1, d)),
            _layer_spec(layer, (d, 2 * d_ff), pipeline_mode=RESIDENT),
            _layer_spec(layer, (d_ff, d), pipeline_mode=RESIDENT),
            _layer_spec(layer, (1, d)),
        ],
        out_specs=pl.BlockSpec((tm, d), lambda i: (i, 0)),
        out_shape=jax.ShapeDtypeStruct((m, d), F32),
        compiler_params=_params("parallel"),
        name="ffn",
    )(x, pre_g, w_gu, w_down, post_g)


IN_QK = 2 * GLA_QK
IN_VR = 2 * GLA_V
IN_SU = S5_WIDTH
IN_FOX = 3 * FOX_W
IN_SMALL = LANES
OFF_VR = IN_QK
OFF_SU = OFF_VR + IN_VR
OFF_FOX = OFF_SU + IN_SU
OFF_SMALL = OFF_FOX + IN_FOX
IN_TOTAL = OFF_SMALL + IN_SMALL


def _inproj_kernel(x_ref, g_ref, w_ref, qk_ref, vr_ref, su_ref, fox_ref, small_ref):
    h = _rms(x_ref[...], g_ref[...]).astype(BF16)
    qk_ref[...] = _dot(h, w_ref[:, 0:OFF_VR])
    vr_ref[...] = _dot(h, w_ref[:, OFF_VR:OFF_SU]).astype(BF16)
    su_ref[...] = _dot(h, w_ref[:, OFF_SU:OFF_FOX])
    fox_ref[:, 0:FOX_W] = (_dot(h, w_ref[:, OFF_FOX:OFF_FOX + FOX_W]) * (FOX_DH ** -0.5 * LOG2E)).astype(BF16)
    fox_ref[:, FOX_W:IN_FOX] = _dot(h, w_ref[:, OFF_FOX + FOX_W:OFF_SMALL]).astype(BF16)
    small_ref[...] = _dot(h, w_ref[:, OFF_SMALL:IN_TOTAL])


def _inproj(layer, x, g, w, *, bsz, seq, tm):
    m, d = x.shape
    nl = seq // tm
    row = lambda b, l: (b * nl + l, 0)
    return pl.pallas_call(
        _inproj_kernel,
        grid=(bsz, nl),
        in_specs=[
            pl.BlockSpec((tm, d), row),
            _layer_spec(layer, (1, d)),
            _layer_spec(layer, (d, IN_TOTAL), pipeline_mode=RESIDENT),
        ],
        out_specs=[
            pl.BlockSpec((tm, IN_QK), row),
            pl.BlockSpec((tm, IN_VR), row),
            pl.BlockSpec((tm, IN_SU), lambda b, l: (l, b)),
            pl.BlockSpec((tm, IN_FOX), row),
            pl.BlockSpec((tm, IN_SMALL), row),
        ],
        out_shape=[
            jax.ShapeDtypeStruct((m, IN_QK), F32),
            jax.ShapeDtypeStruct((m, IN_VR), BF16),
            jax.ShapeDtypeStruct((seq, bsz * IN_SU), F32),
            jax.ShapeDtypeStruct((m, IN_FOX), BF16),
            jax.ShapeDtypeStruct((m, IN_SMALL), F32),
        ],
        compiler_params=_params("parallel", "parallel"),
        name="inproj",
    )(x, g, w)


def _gla_kernel(qk_ref, vr_ref, small_ref, wg_ref, bg_ref, ng_ref, o_ref, la_sc, o_sc):
    seq = qk_ref.shape[0]
    n_chunks = seq // CHUNK
    c, hq, hv = CHUNK, GLA_QK, GLA_V

    z = _dot(small_ref[...].astype(BF16), wg_ref[...]) + bg_ref[...]
    la_sc[...] = _log_sigmoid(z) * (1.0 / GLA_TAU)

    row = lax.broadcasted_iota(jnp.int32, (c, c), 0)
    col = lax.broadcasted_iota(jnp.int32, (c, c), 1)
    tri = (col <= row).astype(BF16)
    row_x = lax.broadcasted_iota(jnp.int32, (GLA_HEADS * c, c), 0)
    col_x = lax.broadcasted_iota(jnp.int32, (GLA_HEADS * c, c), 1)
    lower = col_x <= (row_x & (c - 1))
    lane_q = lax.broadcasted_iota(jnp.int32, (c, hq), 1)
    lane_v = lax.broadcasted_iota(jnp.int32, (c, hv), 1)
    q_head = [(lane_q >> DK_SHIFT) == h for h in range(GLA_HEADS)]
    v_head = [(lane_v >> DV_SHIFT) == h for h in range(GLA_HEADS)]
    st_row = lax.broadcasted_iota(jnp.int32, (hv, hq), 0)
    st_col = lax.broadcasted_iota(jnp.int32, (hv, hq), 1)
    same_head = (st_row >> DV_SHIFT) == (st_col >> DK_SHIFT)

    def expand(t):
        return jnp.concatenate([jnp.where(q_head[h], t, 0.0) for h in range(GLA_HEADS)], axis=0).astype(BF16)

    per_chunk = []
    for first in range(0, n_chunks, GLA_GROUP):
        group = range(first, min(first + GLA_GROUP, n_chunks))
        stage1 = []
        for i in group:
            hi, mid, lo = _split3(la_sc[i * c:(i + 1) * c, :])
            stage1.append(_dot(tri, hi) + _dot(tri, mid) + _dot(tri, lo))
        stage2 = []
        for i, g in zip(group, stage1):
            rows = slice(i * c, (i + 1) * c)
            g_last = g[c - 1:c, :]
            eg, ieg = jnp.exp(g), jnp.exp(-g)
            qc = qk_ref[rows, 0:hq] * (GLA_DK ** -0.5)
            kc = qk_ref[rows, hq:2 * hq]
            v = vr_ref[rows, 0:hv]
            q_fwd = qc * eg
            a_fwd = _dot_nt(expand(q_fwd), (kc * ieg).astype(BF16))
            a_bwd = _dot_nt(expand(qc * ieg), (kc * eg).astype(BF16))
            kw = (kc * jnp.exp(g_last - g)).astype(BF16)
            ds_t = _dot(v.astype(F32).T.astype(BF16), kw)
            stage2.append((a_fwd, a_bwd, v, q_fwd.astype(BF16), jnp.where(same_head, ds_t, 0.0), jnp.exp(g_last)))
        for a_fwd, a_bwd, v, q_fwd, ds_t, decay in stage2:
            attn = jnp.where(lower, a_fwd, a_bwd).astype(BF16)
            p = _dot(attn, v)
            o = jnp.where(v_head[0], p[0:c, :], 0.0)
            for h in range(1, GLA_HEADS):
                o = o + jnp.where(v_head[h], p[h * c:(h + 1) * c, :], 0.0)
            per_chunk.append((o, q_fwd, ds_t, decay))

    s_t = jnp.zeros((hv, hq), F32)
    states = []
    for _, _, ds_t, decay in per_chunk:
        states.append(s_t.astype(BF16))
        s_t = s_t * decay + ds_t

    for i, (o, q_fwd, _, _) in enumerate(per_chunk):
        o_sc[i * c:(i + 1) * c, :] = o + _dot_nt(q_fwd, states[i])

    o = o_sc[...]
    gr = lax.broadcasted_iota(jnp.int32, (hv, hv), 0) >> DV_SHIFT
    gc = lax.broadcasted_iota(jnp.int32, (hv, hv), 1) >> DV_SHIFT
    avg = jnp.where(gr == gc, 1.0 / GLA_DV, 0.0).astype(BF16)
    hi, mid, lo = _split3(o * o)
    ms = _dot(hi, avg) + _dot(mid, avg) + _dot(lo, avg)
    r = vr_ref[:, hv:2 * hv].astype(F32)
    o_ref[...] = (o * lax.rsqrt(ms + EPS) * ng_ref[...] * (r * jax.nn.sigmoid(r))).astype(BF16)


def _gla(layer, qk, vr, small, wg, bg, ng, *, bsz, seq):
    m = qk.shape[0]
    row = lambda b: (b, 0)
    return pl.pallas_call(
        _gla_kernel,
        grid=(bsz,),
        in_specs=[
            pl.BlockSpec((seq, IN_QK), row),
            pl.BlockSpec((seq, IN_VR), row),
            pl.BlockSpec((seq, IN_SMALL), row),
            _layer_spec(layer, (IN_SMALL, GLA_QK)),
            _layer_spec(layer, (1, GLA_QK)),
            _layer_spec(layer, (1, GLA_V)),
        ],
        out_specs=pl.BlockSpec((seq, GLA_V), row),
        out_shape=jax.ShapeDtypeStruct((m, GLA_V), BF16),
        scratch_shapes=[pltpu.VMEM((seq, GLA_QK), F32), pltpu.VMEM((seq, GLA_V), F32)],
        compiler_params=_params("parallel"),
        name="gla",
    )(qk, vr, small, wg, bg, ng)


def _foxprep_kernel(small_ref, b_ref, o_ref, *, bsz):
    seq = small_ref.shape[0] // bsz
    ff = jnp.concatenate([small_ref[b * seq:(b + 1) * seq, :].T[GLA_RANK:GLA_RANK + FOX_HEADS, :]
                          for b in range(bsz)], axis=0)
    x = _log_sigmoid(ff + jnp.concatenate([b_ref[...]] * bsz, axis=0))
    lane = lax.broadcasted_iota(jnp.int32, x.shape, 1)
    shift = 1
    while shift < seq:
        x = x + jnp.where(lane >= shift, pltpu.roll(x, shift, axis=1), 0.0)
        shift *= 2
    hi, mid, lo = (t.astype(F32) for t in _split3(x * LOG2E))
    ones = jnp.ones((FOX_HEADS, seq), F32)
    pad = jnp.zeros((LANES - FOX_TERMS * FOX_HEADS, seq), F32)
    for b in range(bsz):
        mine = slice(b * FOX_HEADS, (b + 1) * FOX_HEADS)
        terms = jnp.concatenate([hi[mine], mid[mine], lo[mine], ones, pad], axis=0)
        o_ref[b * seq:(b + 1) * seq, :] = terms.T.astype(BF16)


def _foxprep(layer, small, bias, *, bsz, seq):
    whole = lambda i: (0, 0)
    return pl.pallas_call(
        functools.partial(_foxprep_kernel, bsz=bsz),
        grid=(1,),
        in_specs=[
            pl.BlockSpec((bsz * seq, IN_SMALL), whole),
            _layer_spec(layer, (FOX_HEADS, 1)),
        ],
        out_specs=pl.BlockSpec((bsz * seq, LANES), whole),
        out_shape=jax.ShapeDtypeStruct((bsz * seq, LANES), BF16),
        compiler_params=_params("arbitrary"),
        name="foxprep",
    )(small, bias)


def _fox_placement(head, base, *, key):
    r = lax.broadcasted_iota(jnp.int32, (LANES, LANES), 0)
    rel = lax.broadcasted_iota(jnp.int32, (LANES, LANES), 1) - base
    term = r >> HEADS_SHIFT
    mine = ((r & (FOX_HEADS - 1)) == head) & (term < FOX_TERMS)
    is_part = mine & (term < FOX_TERMS - 1)
    is_one = mine & (term == FOX_TERMS - 1)
    n = FOX_TERMS - 1
    if key:
        m = jnp.where(is_part & (rel == term + n), -1.0, jnp.where(is_one & (rel >= 0) & (rel < n), 1.0, 0.0))
    else:
        m = jnp.where(is_part & (rel == term), 1.0, jnp.where(is_one & (rel >= n) & (rel < 2 * n), 1.0, 0.0))
    return m.astype(BF16)


def _fox_kernel(q_ref, k_ref, v_ref, f_ref, o_ref, qx_sc, kx_sc, vx_sc, *, tk):
    seq = k_ref.shape[0]
    spare = (FOX_DH, 0)
    lane = lax.broadcasted_iota(jnp.int32, (seq, LANES), 1)
    q, k, v, f = q_ref[...], k_ref[...], v_ref[...], f_ref[...]
    for h in range(2):
        own = (lane < FOX_DH) if h == 0 else (lane >= FOX_DH)
        head = 2 * pl.program_id(1) + h
        qx_sc[h] = jnp.where(own, q, _dot(f, _fox_placement(head, spare[h], key=False)).astype(BF16))
        kx_sc[h] = jnp.where(own, k, _dot(f, _fox_placement(head, spare[h], key=True)).astype(BF16))
        vx_sc[h] = jnp.where(own, v, jnp.ones_like(v))

    tq = tk // 2
    row = lax.broadcasted_iota(jnp.int32, (tq, tq), 0)
    col = lax.broadcasted_iota(jnp.int32, (tq, tq), 1)
    causal = col <= row

    items = []
    for c0 in range(0, seq, tk):
        items += [(h, (c0, c0 + tk), (c0, c0 + tq)) for h in range(2)]
        items += [(h, (c0 + tq, c0 + tk), (c0 + tq, c0 + tk)) for h in range(2)]
        items += [(h, (r0, r0 + tk), (c0, c0 + tk)) for h in range(2) for r0 in range(c0 + tk, seq, tk)]

    def logits(item):
        h, (ra, rb), (ca, cb) = item
        return _dot_nt(qx_sc[h, ra:rb, :], kx_sc[h, ca:cb, :])

    state = {}
    s_next = logits(items[0])
    for n, (h, (ra, rb), (ca, cb)) in enumerate(items):
        s_item = s_next
        if n + 1 < len(items):
            s_next = logits(items[n + 1])
        probs = []
        for r0 in range(ra, rb, tq):
            s = s_item[r0 - ra:r0 - ra + tq, :]
            if r0 == ca:
                s = jnp.where(causal, s, -jnp.inf)
            m_new = jnp.max(s, axis=1, keepdims=True)
            if (h, r0) in state:
                m_old, acc_old = state[h, r0]
                m_new = jnp.maximum(m_old, m_new)
                scaled = jnp.exp2(m_old - m_new) * acc_old
            else:
                scaled = None
            state[h, r0] = (m_new, scaled)
            probs.append(jnp.exp2(s - m_new).astype(BF16))
        pv = _dot(jnp.concatenate(probs, axis=0), vx_sc[h, ca:cb, :])
        for i, r0 in enumerate(range(ra, rb, tq)):
            m_new, scaled = state[h, r0]
            part = pv[i * tq:(i + 1) * tq, :]
            state[h, r0] = (m_new, part if scaled is None else scaled + part)

    first = lax.broadcasted_iota(jnp.int32, (tq, LANES), 1) < FOX_DH
    for r0 in range(0, seq, tq):
        a0, a1 = state[0, r0][1], state[1, r0][1]
        o_ref[r0:r0 + tq, :] = jnp.where(first, a0 / pltpu.roll(a0, FOX_DH, axis=1),
                                         a1 / pltpu.roll(a1, FOX_DH, axis=1)).astype(BF16)


def _fox(fox, f, *, bsz, seq, tk):
    m = fox.shape[0]
    return pl.pallas_call(
        functools.partial(_fox_kernel, tk=tk),
        grid=(bsz, FOX_PAIRS),
        in_specs=[
            pl.BlockSpec((seq, LANES), lambda b, p: (b, p)),
            pl.BlockSpec((seq, LANES), lambda b, p: (b, FOX_PAIRS + p)),
            pl.BlockSpec((seq, LANES), lambda b, p: (b, 2 * FOX_PAIRS + p)),
            pl.BlockSpec((seq, LANES), lambda b, p: (b, 0)),
        ],
        out_specs=pl.BlockSpec((seq, LANES), lambda b, p: (b, p)),
        out_shape=jax.ShapeDtypeStruct((m, FOX_W), BF16),
        scratch_shapes=[pltpu.VMEM((2, seq, LANES), BF16)] * 3,
        compiler_params=_params("parallel", "parallel"),
        name="fox",
    )(fox, fox, fox, f)


def _s5prep_kernel(are_ref, aim_ref, ldt_ref, bre_ref, bim_ref, abre_ref, abim_ref, bbre_ref, bbim_ref):
    lam_re = jnp.minimum(are_ref[...], -1e-4)
    lam_im = aim_ref[...]
    dt = jnp.exp(ldt_ref[...])
    mag = jnp.exp(lam_re * dt)
    ab_re = mag * jnp.cos(lam_im * dt)
    ab_im = mag * jnp.sin(lam_im * dt)
    den = lam_re * lam_re + lam_im * lam_im
    z_re = ((ab_re - 1.0) * lam_re + ab_im * lam_im) / den
    z_im = (ab_im * lam_re - (ab_re - 1.0) * lam_im) / den
    br, bi = bre_ref[...], bim_ref[...]
    abre_ref[...] = ab_re
    abim_ref[...] = ab_im
    bbre_ref[...] = z_re * br - z_im * bi
    bbim_ref[...] = z_re * bi + z_im * br


def _s5prep(a_re, a_im, log_dt, b_re, b_im):
    depth = a_re.shape[0]
    col = lambda a: a.reshape(depth, S5_NSTATE, 1)
    ldt = jnp.broadcast_to(log_dt[:, :, None], (depth, S5_GROUPS, S5_STATE))
    vec = pl.BlockSpec((None, S5_NSTATE, 1), lambda l: (l, 0, 0))
    mat = pl.BlockSpec((None, S5_NSTATE, S5_CH), lambda l: (l, 0, 0))
    return pl.pallas_call(
        _s5prep_kernel,
        grid=(depth,),
        in_specs=[vec, vec, vec, mat, mat],
        out_specs=[vec, vec, mat, mat],
        out_shape=[jax.ShapeDtypeStruct((depth, S5_NSTATE, 1), F32)] * 2
        + [jax.ShapeDtypeStruct((depth, S5_NSTATE, S5_CH), F32)] * 2,
        compiler_params=_params("parallel"),
        name="s5prep",
    )(col(a_re), col(a_im), col(ldt), b_re.reshape(depth, S5_NSTATE, S5_CH), b_im.reshape(depth, S5_NSTATE, S5_CH))


def _s5_kernel(u_ref, abre_ref, abim_ref, bre_ref, bim_ref, cre_ref, cim_ref, d_ref, gw_ref, gb_ref, o_ref,
               sre_sc, sim_sc, *, bsz, ts):
    @pl.when(pl.program_id(0) == 0)
    def _():
        sre_sc[...] = jnp.zeros_like(sre_sc)
        sim_sc[...] = jnp.zeros_like(sim_sc)

    rows = ts * bsz
    n_sub = u_ref.shape[0] // rows
    a_re = jnp.broadcast_to(abre_ref[...], (bsz, S5_NSTATE))
    a_im = jnp.broadcast_to(abim_ref[...], (bsz, S5_NSTATE))

    bu = []
    for k in range(n_sub):
        ub = u_ref[k * rows:(k + 1) * rows, :].astype(BF16)
        bu.append((_dot(ub, bre_ref[...]), _dot(ub, bim_ref[...])))

    s_re, s_im = sre_sc[...], sim_sc[...]
    for k in range(n_sub):
        bu_re, bu_im = bu[k]
        x_re, x_im = [], []
        for t in range(ts):
            r = slice(t * bsz, (t + 1) * bsz)
            s_re, s_im = (a_re * s_re - a_im * s_im + bu_re[r, :], a_re * s_im + a_im * s_re + bu_im[r, :])
            x_re.append(s_re)
            x_im.append(s_im)
        x_re = jnp.concatenate(x_re, axis=0).astype(BF16)
        x_im = jnp.concatenate(x_im, axis=0).astype(BF16)
        u = u_ref[k * rows:(k + 1) * rows, :]
        y = _dot(x_re, cre_ref[...]) - _dot(x_im, cim_ref[...])
        y = jax.nn.gelu(y + d_ref[...] * u, approximate=True)
        o_ref[k * rows:(k + 1) * rows, :] = (
            y * jax.nn.sigmoid(_dot(y.astype(BF16), gw_ref[...]) + gb_ref[...])).astype(BF16)
    sre_sc[...] = s_re
    sim_sc[...] = s_im


def _s5(layer, u, ab_re, ab_im, b_re, b_im, c_re, c_im, d, glu_w, glu_b, *, bsz, seq, tb, ts):
    rows = tb * bsz
    return pl.pallas_call(
        functools.partial(_s5_kernel, bsz=bsz, ts=ts),
        grid=(seq // tb,),
        in_specs=[
            pl.BlockSpec((rows, S5_WIDTH), lambda i: (i, 0)),
            _layer_spec(layer, (1, S5_NSTATE)),
            _layer_spec(layer, (1, S5_NSTATE)),
            _layer_spec(layer, (S5_WIDTH, S5_NSTATE)),
            _layer_spec(layer, (S5_WIDTH, S5_NSTATE)),
            _layer_spec(layer, (S5_NSTATE, S5_WIDTH)),
            _layer_spec(layer, (S5_NSTATE, S5_WIDTH)),
            _layer_spec(layer, (1, S5_WIDTH)),
            _layer_spec(layer, (S5_WIDTH, S5_WIDTH)),
            _layer_spec(layer, (1, S5_WIDTH)),
        ],
        out_specs=pl.BlockSpec((rows, S5_WIDTH), lambda i: (i, 0)),
        out_shape=jax.ShapeDtypeStruct((seq * bsz, S5_WIDTH), BF16),
        scratch_shapes=[pltpu.VMEM((bsz, S5_NSTATE), F32), pltpu.VMEM((bsz, S5_NSTATE), F32)],
        compiler_params=_params("arbitrary"),
        name="s5",
    )(u, ab_re, ab_im, b_re, b_im, c_re, c_im, d, glu_w, glu_b)


def _merge_kernel(x_ref, gla_ref, s5_ref, fox_ref, pre_g_ref, wgate_ref, wgla_ref, ws5_ref, wfox_ref, wo_ref,
                  post_g_ref, o_ref):
    x = x_ref[...]
    d = x.shape[1]
    h = _rms(x, pre_g_ref[...]).astype(BF16)
    branches = ((gla_ref, wgla_ref), (s5_ref, ws5_ref), (fox_ref, wfox_ref))
    mix = None
    for n, (b_ref, w_ref) in enumerate(branches):
        gate = jax.nn.sigmoid(_dot(h, wgate_ref[:, n * d:(n + 1) * d]))
        term = gate * _dot(b_ref[...], w_ref[...])
        mix = term if mix is None else mix + term
    y = _dot(mix.astype(BF16), wo_ref[...])
    o_ref[...] = x + _rms(y, post_g_ref[...])


def _merge(layer, x, gla_o, s5_o, fox_o, pre_g, w_gate, w_gla, w_s5, w_fox, w_o, post_g, *, bsz, seq, tm):
    m, d = x.shape
    nl = seq // tm
    row = lambda b, l: (b * nl + l, 0)
    return pl.pallas_call(
        _merge_kernel,
        grid=(bsz, nl),
        in_specs=[
            pl.BlockSpec((tm, d), row),
            pl.BlockSpec((tm, GLA_V), row),
            pl.BlockSpec((tm, S5_WIDTH), lambda b, l: (l, b)),
            pl.BlockSpec((tm, FOX_W), row),
            _layer_spec(layer, (1, d)),
            _layer_spec(layer, (d, N_BRANCH * d), pipeline_mode=RESIDENT),
            _layer_spec(layer, (GLA_V, d), pipeline_mode=RESIDENT),
            _layer_spec(layer, (S5_WIDTH, d), pipeline_mode=RESIDENT),
            _layer_spec(layer, (FOX_W, d), pipeline_mode=RESIDENT),
            _layer_spec(layer, (d, d), pipeline_mode=RESIDENT),
            _layer_spec(layer, (1, d)),
        ],
        out_specs=pl.BlockSpec((tm, d), row),
        out_shape=jax.ShapeDtypeStruct((m, d), F32),
        compiler_params=_params("parallel", "parallel"),
        name="merge",
    )(x, gla_o, s5_o, fox_o, pre_g, w_gate, w_gla, w_s5, w_fox, w_o, post_g)


def _memkv_kernel(mem_ref, g_ref, w_ref, k_ref, v_ref):
    d = mem_ref.shape[1]
    h = _rms(mem_ref[...], g_ref[...]).astype(BF16)
    k_ref[...] = _dot(h, w_ref[:, 0:d]).astype(BF16)
    v_ref[...] = _dot(h, w_ref[:, d:2 * d]).astype(BF16)


def _memkv(layer, mem, g, w_kv, *, tm):
    m, d = mem.shape
    row = lambda i: (i, 0)
    return pl.pallas_call(
        _memkv_kernel,
        grid=(m // tm,),
        in_specs=[pl.BlockSpec((tm, d), row), _layer_spec(layer, (1, d)), _layer_spec(layer, (d, 2 * d))],
        out_specs=[pl.BlockSpec((tm, d), row)] * 2,
        out_shape=[jax.ShapeDtypeStruct((m, d), BF16)] * 2,
        compiler_params=_params("parallel"),
        name="memkv",
    )(mem, g, w_kv)


def _xattn_kernel(x_ref, k_ref, v_ref, pre_g_ref, wq_ref, wo_ref, post_g_ref, o_ref):
    x = x_ref[...]
    d = x.shape[1]
    dh = d // XA_HEADS
    h = _rms(x, pre_g_ref[...]).astype(BF16)
    q = (_dot(h, wq_ref[...]) * (dh ** -0.5)).astype(BF16)
    cols = [slice(n * dh, (n + 1) * dh) for n in range(XA_HEADS)]
    logits = [_dot_nt(q[:, c], k_ref[:, c]) for c in cols]
    probs = []
    for s in logits:
        e = jnp.exp(s - jnp.max(s, axis=1, keepdims=True))
        probs.append((e / jnp.sum(e, axis=1, keepdims=True)).astype(BF16))
    heads = [_dot(p, v_ref[:, c]).astype(BF16) for p, c in zip(probs, cols)]
    y = _dot(jnp.concatenate(heads, axis=1), wo_ref[...])
    o_ref[...] = x + _rms(y, post_g_ref[...])


def _xattn(layer, x, k, v, pre_g, w_q, w_o, post_g, *, bsz, seq, n_mem, tm):
    m, d = x.shape
    nl = seq // tm
    row = lambda b, l: (b * nl + l, 0)
    return pl.pallas_call(
        _xattn_kernel,
        grid=(bsz, nl),
        in_specs=[
            pl.BlockSpec((tm, d), row),
            pl.BlockSpec((n_mem, d), lambda b, l: (b, 0)),
            pl.BlockSpec((n_mem, d), lambda b, l: (b, 0)),
            _layer_spec(layer, (1, d)),
            _layer_spec(layer, (d, d), pipeline_mode=RESIDENT),
            _layer_spec(layer, (d, d), pipeline_mode=RESIDENT),
            _layer_spec(layer, (1, d)),
        ],
        out_specs=pl.BlockSpec((tm, d), row),
        out_shape=jax.ShapeDtypeStruct((m, d), F32),
        compiler_params=_params("parallel", "parallel"),
        name="xattn",
    )(x, k, v, pre_g, w_q, w_o, post_g)


def _tile(n, want):
    t = min(n, want)
    while n % t:
        t -= 1
    return t


def _block_diag(t):
    depth, g, r, c = t.shape
    eye = jnp.eye(g, dtype=t.dtype)
    return jnp.einsum("lgrc,gk->lgrkc", t, eye).reshape(depth, g * r, g * c)


def kernel(x, mem, ffn1_pre_g, ffn1_w_gu, ffn1_w_down, ffn1_post_g, mix_pre_g, w_in, gla_gate_w, gla_gate_b, gla_norm_g, w_gla_up, s5_a_re, s5_a_im, s5_log_dt, s5_b_re, s5_b_im, s5_c_re, s5_c_im, s5_d, s5_glu_w, s5_glu_b, w_s5_up, fox_f_b, w_fox_up, w_mix_out, mix_post_g, xa_pre_g, xa_mem_g, xa_w_q, xa_w_kv, xa_w_o, xa_post_g, ffn2_pre_g, ffn2_w_gu, ffn2_w_down, ffn2_post_g):
    bsz, seq, d = x.shape
    n_mem = mem.shape[1]
    depth = w_in.shape[0]
    d_ff = ffn1_w_down.shape[1]
    m = bsz * seq
    assert seq % CHUNK == 0 and d % LANES == 0

    tm_ffn = _tile(m, 1024)
    tf = _tile(d_ff // LANES, 2) * LANES
    tm = _tile(seq, 1024)
    tk = _tile(seq, 512)
    ts = _tile(seq, 128)
    tb = 2 * ts if seq % (2 * ts) == 0 else ts

    bf = lambda a: a.astype(BF16)
    vec = lambda a: a.reshape(depth, 1, a.shape[-1])

    sizes = (GLA_QK, GLA_QK, GLA_V, GLA_V, GLA_RANK, S5_WIDTH, FOX_W, FOX_W, FOX_W, FOX_HEADS, N_BRANCH * d)
    offs = [0]
    for s in sizes:
        offs.append(offs[-1] + s)
    w_in_b = bf(w_in)
    seg = lambda i: w_in_b[:, :, offs[i]:offs[i + 1]]
    pad = jnp.zeros((depth, d, IN_SMALL - GLA_RANK - FOX_HEADS), BF16)
    w_proj = jnp.concatenate([seg(0), seg(1), seg(2), seg(3), seg(5), seg(6), seg(7), seg(8), seg(4), seg(9), pad],
                             axis=2)
    w_gate = seg(10)
    gla_wg = bf(jnp.concatenate([gla_gate_w, jnp.zeros((depth, IN_SMALL - GLA_RANK, GLA_QK), gla_gate_w.dtype)],
                                axis=1))

    ab_re, ab_im, bb_re, bb_im = _s5prep(s5_a_re, s5_a_im, s5_log_dt, s5_b_re, s5_b_im)
    grp = lambda t: t.reshape(depth, S5_GROUPS, S5_STATE, S5_CH)
    s5_bre = bf(_block_diag(grp(bb_re).transpose(0, 1, 3, 2)))
    s5_bim = bf(_block_diag(grp(bb_im).transpose(0, 1, 3, 2)))
    s5_cre = bf(_block_diag(s5_c_re.transpose(0, 1, 3, 2)))
    s5_cim = bf(_block_diag(s5_c_im.transpose(0, 1, 3, 2)))
    ab_re = ab_re.reshape(depth, 1, S5_NSTATE)
    ab_im = ab_im.reshape(depth, 1, S5_NSTATE)

    ffn1_gu, ffn1_down, ffn2_gu, ffn2_down = bf(ffn1_w_gu), bf(ffn1_w_down), bf(ffn2_w_gu), bf(ffn2_w_down)
    gla_up, s5_up, fox_up, mix_out = bf(w_gla_up), bf(w_s5_up), bf(w_fox_up), bf(w_mix_out)
    glu_w, xa_q, xa_kv, xa_o = bf(s5_glu_w), bf(xa_w_q), bf(xa_w_kv), bf(xa_w_o)
    fox_b = fox_f_b.reshape(depth, FOX_HEADS, 1)

    xs = x.reshape(m, d)
    mem2 = mem.reshape(bsz * n_mem, d)
    for l in range(depth):
        xs = _ffn(l, xs, vec(ffn1_pre_g), ffn1_gu, ffn1_down, vec(ffn1_post_g), tm=tm_ffn, tf=tf)

        qk, vr, su, fox, small = _inproj(l, xs, vec(mix_pre_g), w_proj, bsz=bsz, seq=seq, tm=tm)
        gla_o = _gla(l, qk, vr, small, gla_wg, vec(gla_gate_b), vec(gla_norm_g), bsz=bsz, seq=seq)
        f_cols = _foxprep(l, small, fox_b, bsz=bsz, seq=seq)
        fox_o = _fox(fox, f_cols, bsz=bsz, seq=seq, tk=tk)
        s5_o = _s5(l, su.reshape(seq * bsz, S5_WIDTH), ab_re, ab_im, s5_bre, s5_bim, s5_cre, s5_cim,
                   vec(s5_d), glu_w, vec(s5_glu_b), bsz=bsz, seq=seq, tb=tb, ts=ts)
        xs = _merge(l, xs, gla_o, s5_o.reshape(seq, bsz * S5_WIDTH), fox_o, vec(mix_pre_g), w_gate, gla_up,
                    s5_up, fox_up, mix_out, vec(mix_post_g), bsz=bsz, seq=seq, tm=tm)

        mk, mv = _memkv(l, mem2, vec(xa_mem_g), xa_kv, tm=_tile(bsz * n_mem, 512))
        xs = _xattn(l, xs, mk, mv, vec(xa_pre_g), xa_q, xa_o, vec(xa_post_g), bsz=bsz, seq=seq, n_mem=n_mem, tm=tm)

        xs = _ffn(l, xs, vec(ffn2_pre_g), ffn2_gu, ffn2_down, vec(ffn2_post_g), tm=tm_ffn, tf=tf)
    return xs.reshape(bsz, seq, d)
```

```python
import functools
import math

import jax
import jax.numpy as jnp
from jax import lax
from jax.experimental import pallas as pl
from jax.experimental.pallas import tpu as pltpu

F32 = jnp.float32
BF16 = jnp.bfloat16
EPS = 1e-6
LOG2E = math.log2(math.e)

LANES = 128
SUBLANES = 8
VMEM_LIMIT_BYTES = 56 * 1024 * 1024

EDGE_BLOCKS = 4
CHUNK = 64
GLA_GROUP = 8
GLA_HEADS, GLA_DK, GLA_DV, GLA_RANK, GLA_TAU = 4, 32, 64, 16, 16.0
GLA_QK = GLA_HEADS * GLA_DK
GLA_V = GLA_HEADS * GLA_DV
DK_SHIFT = GLA_DK.bit_length() - 1
DV_SHIFT = GLA_DV.bit_length() - 1
S5_GROUPS, S5_CH, S5_STATE = 16, 16, 64
S5_WIDTH = S5_GROUPS * S5_CH
S5_NSTATE = S5_GROUPS * S5_STATE
FOX_HEADS, FOX_DH = 8, 64
FOX_W = FOX_HEADS * FOX_DH
FOX_PAIRS = FOX_W // LANES
FOX_TERMS = 4
HEADS_SHIFT = FOX_HEADS.bit_length() - 1
XA_HEADS = 4
N_BRANCH = 3

NT_DIMS = (((1,), (1,)), ((), ()))
RESIDENT = pl.Buffered(1)


def _params(*semantics):
    return pltpu.CompilerParams(dimension_semantics=semantics, vmem_limit_bytes=VMEM_LIMIT_BYTES)


def _layer_spec(layer, block, index_map=None, pipeline_mode=None):
    block = tuple(block)
    if index_map is None:
        index_map = lambda *_: (0,) * len(block)
    return pl.BlockSpec((None,) + block, lambda *ids: (layer,) + tuple(index_map(*ids)),
                        pipeline_mode=pipeline_mode)


def _rms(x, g):
    return x * lax.rsqrt(jnp.mean(x * x, axis=-1, keepdims=True) + EPS) * g


def _dot(a, b):
    return jnp.dot(a, b, preferred_element_type=F32)


def _dot_nt(a, b):
    return lax.dot_general(a, b, NT_DIMS, preferred_element_type=F32)


def _log_sigmoid(x):
    return jnp.minimum(x, 0.0) - jnp.log1p(jnp.exp(-jnp.abs(x)))


def _split3(x):
    hi = x.astype(BF16)
    r1 = x - hi.astype(F32)
    mid = r1.astype(BF16)
    lo = (r1 - mid.astype(F32)).astype(BF16)
    return hi, mid, lo


def _ffn_kernel(x_ref, pre_g_ref, wgu_ref, wd_ref, post_g_ref, o_ref, *, tf):
    d_ff = wd_ref.shape[0]
    tm = x_ref.shape[0]
    blocks = [slice(r0, r0 + tm // EDGE_BLOCKS) for r0 in range(0, tm, tm // EDGE_BLOCKS)]

    def gate_up(h, c0):
        return _dot(h, wgu_ref[:, c0:c0 + tf]), _dot(h, wgu_ref[:, d_ff + c0:d_ff + c0 + tf])

    hs, first = [], []
    for rows in blocks:
        hs.append(_rms(x_ref[rows, :], pre_g_ref[...]).astype(BF16))
        first.append(gate_up(hs[-1], 0))
    h = jnp.concatenate(hs, axis=0)
    nxt = tuple(jnp.concatenate(t, axis=0) for t in zip(*first))

    acc = None
    for c0 in range(0, d_ff - tf, tf):
        gate, up = nxt
        nxt = gate_up(h, c0 + tf)
        a = (gate * jax.nn.sigmoid(gate) * up).astype(BF16)
        part = _dot(a, wd_ref[c0:c0 + tf, :])
        acc = part if acc is None else acc + part

    gate, up = nxt
    a = (gate * jax.nn.sigmoid(gate) * up).astype(BF16)
    for rows in blocks:
        total = _dot(a[rows, :], wd_ref[d_ff - tf:d_ff, :])
        if acc is not None:
            total = acc[rows, :] + total
        o_ref[rows, :] = x_ref[rows, :] + 0.5 * _rms(total, post_g_ref[...])


def _ffn(layer, x, pre_g, w_gu, w_down, post_g, *, tm, tf):
    m, d = x.shape
    d_ff = w_down.shape[1]
    return pl.pallas_call(
        functools.partial(_ffn_kernel, tf=tf),
        grid=(m // tm,),
        in_specs=[
            pl.BlockSpec((tm, d), lambda i: (i, 0)),
            _layer_spec(layer, (1, d)),
            _layer_spec(layer, (d, 2 * d_ff), pipeline_mode=RESIDENT),
            _layer_spec(layer, (d_ff, d), pipeline_mode=RESIDENT),
            _layer_spec(layer, (1, d)),
        ],
        out_specs=pl.BlockSpec((tm, d), lambda i: (i, 0)),
        out_shape=jax.ShapeDtypeStruct((m, d), F32),
        compiler_params=_params("parallel"),
        name="ffn",
    )(x, pre_g, w_gu, w_down, post_g)


IN_QK = 2 * GLA_QK
IN_VR = 2 * GLA_V
IN_SU = S5_WIDTH
IN_FOX = 3 * FOX_W
IN_SMALL = LANES
OFF_VR = IN_QK
OFF_SU = OFF_VR + IN_VR
OFF_FOX = OFF_SU + IN_SU
OFF_SMALL = OFF_FOX + IN_FOX
IN_TOTAL = OFF_SMALL + IN_SMALL


def _inproj_kernel(x_ref, g_ref, w_ref, qk_ref, vr_ref, su_ref, fox_ref, small_ref):
    tm = x_ref.shape[0]
    hs = []
    for r0 in range(0, tm, tm // EDGE_BLOCKS):
        rows = slice(r0, r0 + tm // EDGE_BLOCKS)
        hs.append(_rms(x_ref[rows, :], g_ref[...]).astype(BF16))
        qk_ref[rows, :] = _dot(hs[-1], w_ref[:, 0:OFF_VR])
    h = jnp.concatenate(hs, axis=0)
    vr_ref[...] = _dot(h, w_ref[:, OFF_VR:OFF_SU]).astype(BF16)
    su_ref[...] = _dot(h, w_ref[:, OFF_SU:OFF_FOX])
    fox_ref[:, 0:FOX_W] = (_dot(h, w_ref[:, OFF_FOX:OFF_FOX + FOX_W]) * (FOX_DH ** -0.5 * LOG2E)).astype(BF16)
    fox_ref[:, FOX_W:IN_FOX] = _dot(h, w_ref[:, OFF_FOX + FOX_W:OFF_SMALL]).astype(BF16)
    small_ref[...] = _dot(h, w_ref[:, OFF_SMALL:IN_TOTAL])


def _inproj(layer, x, g, w, *, bsz, seq, tm):
    m, d = x.shape
    nl = seq // tm
    row = lambda b, l: (b * nl + l, 0)
    return pl.pallas_call(
        _inproj_kernel,
        grid=(bsz, nl),
        in_specs=[
            pl.BlockSpec((tm, d), row),
            _layer_spec(layer, (1, d)),
            _layer_spec(layer, (d, IN_TOTAL), pipeline_mode=RESIDENT),
        ],
        out_specs=[
            pl.BlockSpec((tm, IN_QK), row),
            pl.BlockSpec((tm, IN_VR), row),
            pl.BlockSpec((tm, IN_SU), lambda b, l: (l, b)),
            pl.BlockSpec((tm, IN_FOX), row),
            pl.BlockSpec((tm, IN_SMALL), row),
        ],
        out_shape=[
            jax.ShapeDtypeStruct((m, IN_QK), F32),
            jax.ShapeDtypeStruct((m, IN_VR), BF16),
            jax.ShapeDtypeStruct((seq, bsz * IN_SU), F32),
            jax.ShapeDtypeStruct((m, IN_FOX), BF16),
            jax.ShapeDtypeStruct((m, IN_SMALL), F32),
        ],
        compiler_params=_params("parallel", "parallel"),
        name="inproj",
    )(x, g, w)


def _gla_kernel(qk_ref, vr_ref, small_ref, wg_ref, bg_ref, ng_ref, o_ref, la_sc):
    seq = qk_ref.shape[0]
    n_chunks = seq // CHUNK
    c, hq, hv = CHUNK, GLA_QK, GLA_V

    z = _dot(small_ref[...].astype(BF16), wg_ref[...]) + bg_ref[...]
    la_sc[...] = _log_sigmoid(z) * (1.0 / GLA_TAU)

    row = lax.broadcasted_iota(jnp.int32, (c, c), 0)
    col = lax.broadcasted_iota(jnp.int32, (c, c), 1)
    tri = (col <= row).astype(BF16)
    row_x = lax.broadcasted_iota(jnp.int32, (GLA_HEADS * c, c), 0)
    col_x = lax.broadcasted_iota(jnp.int32, (GLA_HEADS * c, c), 1)
    lower = col_x <= (row_x & (c - 1))
    lane_q = lax.broadcasted_iota(jnp.int32, (c, hq), 1)
    lane_v = lax.broadcasted_iota(jnp.int32, (c, hv), 1)
    q_head = [(lane_q >> DK_SHIFT) == h for h in range(GLA_HEADS)]
    v_head = [(lane_v >> DV_SHIFT) == h for h in range(GLA_HEADS)]
    st_row = lax.broadcasted_iota(jnp.int32, (hv, hq), 0)
    st_col = lax.broadcasted_iota(jnp.int32, (hv, hq), 1)
    same_head = (st_row >> DV_SHIFT) == (st_col >> DK_SHIFT)

    def expand(t):
        return jnp.concatenate([jnp.where(q_head[h], t, 0.0) for h in range(GLA_HEADS)], axis=0).astype(BF16)

    gr = lax.broadcasted_iota(jnp.int32, (hv, hv), 0) >> DV_SHIFT
    gc = lax.broadcasted_iota(jnp.int32, (hv, hv), 1) >> DV_SHIFT
    avg = jnp.where(gr == gc, 1.0 / GLA_DV, 0.0).astype(BF16)

    def within_chunks(group):
        stage1 = []
        for i in group:
            hi, mid, lo = _split3(la_sc[i * c:(i + 1) * c, :])
            stage1.append(_dot(tri, hi) + _dot(tri, mid) + _dot(tri, lo))
        stage2 = []
        for i, g in zip(group, stage1):
            rows = slice(i * c, (i + 1) * c)
            g_last = g[c - 1:c, :]
            eg, ieg = jnp.exp(g), jnp.exp(-g)
            qc = qk_ref[rows, 0:hq] * (GLA_DK ** -0.5)
            kc = qk_ref[rows, hq:2 * hq]
            v = vr_ref[rows, 0:hv]
            q_fwd = qc * eg
            a_fwd = _dot_nt(expand(q_fwd), (kc * ieg).astype(BF16))
            a_bwd = _dot_nt(expand(qc * ieg), (kc * eg).astype(BF16))
            kw = (kc * jnp.exp(g_last - g)).astype(BF16)
            ds_t = _dot(v.astype(F32).T.astype(BF16), kw)
            stage2.append((a_fwd, a_bwd, v, q_fwd.astype(BF16), jnp.where(same_head, ds_t, 0.0), jnp.exp(g_last)))
        results = []
        for a_fwd, a_bwd, v, q_fwd, ds_t, decay in stage2:
            attn = jnp.where(lower, a_fwd, a_bwd).astype(BF16)
            p = _dot(attn, v)
            o = jnp.where(v_head[0], p[0:c, :], 0.0)
            for h in range(1, GLA_HEADS):
                o = o + jnp.where(v_head[h], p[h * c:(h + 1) * c, :], 0.0)
            results.append((o, q_fwd, ds_t, decay))
        return results

    def finish(group, results, s_t):
        outs = []
        for o, q_fwd, ds_t, decay in results:
            outs.append(o + _dot_nt(q_fwd, s_t.astype(BF16)))
            s_t = s_t * decay + ds_t
        o = jnp.concatenate(outs, axis=0)
        rows = slice(group[0] * c, (group[-1] + 1) * c)
        hi, mid, lo = _split3(o * o)
        ms = _dot(hi, avg) + _dot(mid, avg) + _dot(lo, avg)
        r = vr_ref[rows, hv:2 * hv].astype(F32)
        o_ref[rows, :] = (o * lax.rsqrt(ms + EPS) * ng_ref[...] * (r * jax.nn.sigmoid(r))).astype(BF16)
        return s_t

    groups = [range(first, min(first + GLA_GROUP, n_chunks)) for first in range(0, n_chunks, GLA_GROUP)]
    s_t = jnp.zeros((hv, hq), F32)
    pending = None
    for group in groups:
        results = within_chunks(group)
        if pending is not None:
            s_t = finish(*pending, s_t)
        pending = (group, results)
    finish(*pending, s_t)


def _gla(layer, qk, vr, small, wg, bg, ng, *, bsz, seq):
    m = qk.shape[0]
    row = lambda b: (b, 0)
    return pl.pallas_call(
        _gla_kernel,
        grid=(bsz,),
        in_specs=[
            pl.BlockSpec((seq, IN_QK), row),
            pl.BlockSpec((seq, IN_VR), row),
            pl.BlockSpec((seq, IN_SMALL), row),
            _layer_spec(layer, (IN_SMALL, GLA_QK)),
            _layer_spec(layer, (1, GLA_QK)),
            _layer_spec(layer, (1, GLA_V)),
        ],
        out_specs=pl.BlockSpec((seq, GLA_V), row),
        out_shape=jax.ShapeDtypeStruct((m, GLA_V), BF16),
        scratch_shapes=[pltpu.VMEM((seq, GLA_QK), F32)],
        compiler_params=_params("parallel"),
        name="gla",
    )(qk, vr, small, wg, bg, ng)


def _foxprep_kernel(small_ref, b_ref, o_ref, *, bsz):
    seq = small_ref.shape[0] // bsz
    ff = jnp.concatenate([small_ref[b * seq:(b + 1) * seq, :].T[GLA_RANK:GLA_RANK + FOX_HEADS, :]
                          for b in range(bsz)], axis=0)
    x = _log_sigmoid(ff + jnp.concatenate([b_ref[...]] * bsz, axis=0))
    lane = lax.broadcasted_iota(jnp.int32, x.shape, 1)
    shift = 1
    while shift < seq:
        x = x + jnp.where(lane >= shift, pltpu.roll(x, shift, axis=1), 0.0)
        shift *= 2
    hi, mid, lo = (t.astype(F32) for t in _split3(x * LOG2E))
    ones = jnp.ones((FOX_HEADS, seq), F32)
    pad = jnp.zeros((LANES - FOX_TERMS * FOX_HEADS, seq), F32)
    for b in range(bsz):
        mine = slice(b * FOX_HEADS, (b + 1) * FOX_HEADS)
        terms = jnp.concatenate([hi[mine], mid[mine], lo[mine], ones, pad], axis=0)
        o_ref[b * seq:(b + 1) * seq, :] = terms.T.astype(BF16)


def _foxprep(layer, small, bias, *, bsz, seq):
    whole = lambda i: (0, 0)
    return pl.pallas_call(
        functools.partial(_foxprep_kernel, bsz=bsz),
        grid=(1,),
        in_specs=[
            pl.BlockSpec((bsz * seq, IN_SMALL), whole),
            _layer_spec(layer, (FOX_HEADS, 1)),
        ],
        out_specs=pl.BlockSpec((bsz * seq, LANES), whole),
        out_shape=jax.ShapeDtypeStruct((bsz * seq, LANES), BF16),
        compiler_params=_params("arbitrary"),
        name="foxprep",
    )(small, bias)


def _fox_placement(head, base, *, key):
    r = lax.broadcasted_iota(jnp.int32, (LANES, LANES), 0)
    rel = lax.broadcasted_iota(jnp.int32, (LANES, LANES), 1) - base
    term = r >> HEADS_SHIFT
    mine = ((r & (FOX_HEADS - 1)) == head) & (term < FOX_TERMS)
    is_part = mine & (term < FOX_TERMS - 1)
    is_one = mine & (term == FOX_TERMS - 1)
    n = FOX_TERMS - 1
    if key:
        m = jnp.where(is_part & (rel == term + n), -1.0, jnp.where(is_one & (rel >= 0) & (rel < n), 1.0, 0.0))
    else:
        m = jnp.where(is_part & (rel == term), 1.0, jnp.where(is_one & (rel >= n) & (rel < 2 * n), 1.0, 0.0))
    return m.astype(BF16)


def _fox_kernel(q_ref, k_ref, v_ref, f_ref, o_ref, qx_sc, kx_sc, vx_sc, *, tk):
    seq = k_ref.shape[0]
    spare = (FOX_DH, 0)
    lane = lax.broadcasted_iota(jnp.int32, (seq, LANES), 1)
    q, k, v, f = q_ref[...], k_ref[...], v_ref[...], f_ref[...]
    for h in range(2):
        own = (lane < FOX_DH) if h == 0 else (lane >= FOX_DH)
        head = 2 * pl.program_id(1) + h
        qx_sc[h] = jnp.where(own, q, _dot(f, _fox_placement(head, spare[h], key=False)).astype(BF16))
        kx_sc[h] = jnp.where(own, k, _dot(f, _fox_placement(head, spare[h], key=True)).astype(BF16))
        vx_sc[h] = jnp.where(own, v, jnp.ones_like(v))

    row = lax.broadcasted_iota(jnp.int32, (tk, tk), 0)
    col = lax.broadcasted_iota(jnp.int32, (tk, tk), 1)
    causal = col <= row

    items = [(h, r0, c0) for c0 in range(0, seq, tk) for h in range(2) for r0 in range(c0, seq, tk)]

    def logits(item):
        h, r0, c0 = item
        s = _dot_nt(qx_sc[h, r0:r0 + tk, :], kx_sc[h, c0:c0 + tk, :])
        return jnp.where(causal, s, -jnp.inf) if r0 == c0 else s

    state = {}
    s_next = logits(items[0])
    for n, (h, r0, c0) in enumerate(items):
        s = s_next
        if n + 1 < len(items):
            s_next = logits(items[n + 1])
        vx = vx_sc[h, c0:c0 + tk, :]
        m_new = jnp.max(s, axis=1, keepdims=True)
        if c0 == 0:
            acc = _dot(jnp.exp2(s - m_new).astype(BF16), vx)
        else:
            m_old, acc_old = state[h, r0]
            m_new = jnp.maximum(m_old, m_new)
            acc = jnp.exp2(m_old - m_new) * acc_old + _dot(jnp.exp2(s - m_new).astype(BF16), vx)
        state[h, r0] = (m_new, acc)

    first = lax.broadcasted_iota(jnp.int32, (tk, LANES), 1) < FOX_DH
    for r0 in range(0, seq, tk):
        a0, a1 = state[0, r0][1], state[1, r0][1]
        o_ref[r0:r0 + tk, :] = jnp.where(first, a0 / pltpu.roll(a0, FOX_DH, axis=1),
                                         a1 / pltpu.roll(a1, FOX_DH, axis=1)).astype(BF16)


def _fox(fox, f, *, bsz, seq, tk):
    m = fox.shape[0]
    return pl.pallas_call(
        functools.partial(_fox_kernel, tk=tk),
        grid=(bsz, FOX_PAIRS),
        in_specs=[
            pl.BlockSpec((seq, LANES), lambda b, p: (b, p)),
            pl.BlockSpec((seq, LANES), lambda b, p: (b, FOX_PAIRS + p)),
            pl.BlockSpec((seq, LANES), lambda b, p: (b, 2 * FOX_PAIRS + p)),
            pl.BlockSpec((seq, LANES), lambda b, p: (b, 0)),
        ],
        out_specs=pl.BlockSpec((seq, LANES), lambda b, p: (b, p)),
        out_shape=jax.ShapeDtypeStruct((m, FOX_W), BF16),
        scratch_shapes=[pltpu.VMEM((2, seq, LANES), BF16)] * 3,
        compiler_params=_params("parallel", "parallel"),
        name="fox",
    )(fox, fox, fox, f)


def _s5prep_kernel(are_ref, aim_ref, ldt_ref, bre_ref, bim_ref, abre_ref, abim_ref, bbre_ref, bbim_ref):
    lam_re = jnp.minimum(are_ref[...], -1e-4)
    lam_im = aim_ref[...]
    dt = jnp.exp(ldt_ref[...])
    mag = jnp.exp(lam_re * dt)
    ab_re = mag * jnp.cos(lam_im * dt)
    ab_im = mag * jnp.sin(lam_im * dt)
    den = lam_re * lam_re + lam_im * lam_im
    z_re = ((ab_re - 1.0) * lam_re + ab_im * lam_im) / den
    z_im = (ab_im * lam_re - (ab_re - 1.0) * lam_im) / den
    br, bi = bre_ref[...], bim_ref[...]
    abre_ref[...] = ab_re
    abim_ref[...] = ab_im
    bbre_ref[...] = z_re * br - z_im * bi
    bbim_ref[...] = z_re * bi + z_im * br


def _s5prep(a_re, a_im, log_dt, b_re, b_im):
    depth = a_re.shape[0]
    col = lambda a: a.reshape(depth, S5_NSTATE, 1)
    ldt = jnp.broadcast_to(log_dt[:, :, None], (depth, S5_GROUPS, S5_STATE))
    vec = pl.BlockSpec((None, S5_NSTATE, 1), lambda l: (l, 0, 0))
    mat = pl.BlockSpec((None, S5_NSTATE, S5_CH), lambda l: (l, 0, 0))
    return pl.pallas_call(
        _s5prep_kernel,
        grid=(depth,),
        in_specs=[vec, vec, vec, mat, mat],
        out_specs=[vec, vec, mat, mat],
        out_shape=[jax.ShapeDtypeStruct((depth, S5_NSTATE, 1), F32)] * 2
        + [jax.ShapeDtypeStruct((depth, S5_NSTATE, S5_CH), F32)] * 2,
        compiler_params=_params("parallel"),
        name="s5prep",
    )(col(a_re), col(a_im), col(ldt), b_re.reshape(depth, S5_NSTATE, S5_CH), b_im.reshape(depth, S5_NSTATE, S5_CH))


def _s5_kernel(u_ref, abre_ref, abim_ref, bre_ref, bim_ref, cre_ref, cim_ref, d_ref, gw_ref, gb_ref, o_ref,
               sre_sc, sim_sc, *, bsz, ts):
    @pl.when(pl.program_id(0) == 0)
    def _():
        sre_sc[...] = jnp.zeros_like(sre_sc)
        sim_sc[...] = jnp.zeros_like(sim_sc)

    rows = ts * bsz
    n_sub = u_ref.shape[0] // rows
    a_re = jnp.broadcast_to(abre_ref[...], (bsz, S5_NSTATE))
    a_im = jnp.broadcast_to(abim_ref[...], (bsz, S5_NSTATE))

    bu = []
    for k in range(n_sub):
        ub = u_ref[k * rows:(k + 1) * rows, :].astype(BF16)
        bu.append((_dot(ub, bre_ref[...]), _dot(ub, bim_ref[...])))

    s_re, s_im = sre_sc[...], sim_sc[...]
    for k in range(n_sub):
        bu_re, bu_im = bu[k]
        x_re, x_im = [], []
        for t in range(ts):
            r = slice(t * bsz, (t + 1) * bsz)
            s_re, s_im = (a_re * s_re - a_im * s_im + bu_re[r, :], a_re * s_im + a_im * s_re + bu_im[r, :])
            x_re.append(s_re)
            x_im.append(s_im)
        x_re = jnp.concatenate(x_re, axis=0).astype(BF16)
        x_im = jnp.concatenate(x_im, axis=0).astype(BF16)
        u = u_ref[k * rows:(k + 1) * rows, :]
        y = _dot(x_re, cre_ref[...]) - _dot(x_im, cim_ref[...])
        y = jax.nn.gelu(y + d_ref[...] * u, approximate=True)
        o_ref[k * rows:(k + 1) * rows, :] = (
            y * jax.nn.sigmoid(_dot(y.astype(BF16), gw_ref[...]) + gb_ref[...])).astype(BF16)
    sre_sc[...] = s_re
    sim_sc[...] = s_im


def _s5(layer, u, ab_re, ab_im, b_re, b_im, c_re, c_im, d, glu_w, glu_b, *, bsz, seq, tb, ts):
    rows = tb * bsz
    return pl.pallas_call(
        functools.partial(_s5_kernel, bsz=bsz, ts=ts),
        grid=(seq // tb,),
        in_specs=[
            pl.BlockSpec((rows, S5_WIDTH), lambda i: (i, 0)),
            _layer_spec(layer, (1, S5_NSTATE)),
            _layer_spec(layer, (1, S5_NSTATE)),
            _layer_spec(layer, (S5_WIDTH, S5_NSTATE)),
            _layer_spec(layer, (S5_WIDTH, S5_NSTATE)),
            _layer_spec(layer, (S5_NSTATE, S5_WIDTH)),
            _layer_spec(layer, (S5_NSTATE, S5_WIDTH)),
            _layer_spec(layer, (1, S5_WIDTH)),
            _layer_spec(layer, (S5_WIDTH, S5_WIDTH)),
            _layer_spec(layer, (1, S5_WIDTH)),
        ],
        out_specs=pl.BlockSpec((rows, S5_WIDTH), lambda i: (i, 0)),
        out_shape=jax.ShapeDtypeStruct((seq * bsz, S5_WIDTH), BF16),
        scratch_shapes=[pltpu.VMEM((bsz, S5_NSTATE), F32), pltpu.VMEM((bsz, S5_NSTATE), F32)],
        compiler_params=_params("arbitrary"),
        name="s5",
    )(u, ab_re, ab_im, b_re, b_im, c_re, c_im, d, glu_w, glu_b)


def _merge_kernel(x_ref, gla_ref, s5_ref, fox_ref, pre_g_ref, wgate_ref, wgla_ref, ws5_ref, wfox_ref, wo_ref,
                  post_g_ref, o_ref):
    tm, d = x_ref.shape
    blocks = [slice(r0, r0 + tm // EDGE_BLOCKS) for r0 in range(0, tm, tm // EDGE_BLOCKS)]
    branches = ((gla_ref, wgla_ref), (s5_ref, ws5_ref), (fox_ref, wfox_ref))

    hs, first = [], []
    for rows in blocks:
        hs.append(_rms(x_ref[rows, :], pre_g_ref[...]).astype(BF16))
        first.append(_dot(hs[-1], wgate_ref[:, 0:d]))
    h = jnp.concatenate(hs, axis=0)
    logits = jnp.concatenate(first, axis=0)
    mix = None
    for n, (b_ref, w_ref) in enumerate(branches[:-1]):
        term = jax.nn.sigmoid(logits) * _dot(b_ref[...], w_ref[...])
        mix = term if mix is None else mix + term
        logits = _dot(h, wgate_ref[:, (n + 1) * d:(n + 2) * d])
    up = _dot(branches[-1][0][...], branches[-1][1][...])

    for rows in blocks:
        mixed = mix[rows, :] + jax.nn.sigmoid(logits[rows, :]) * up[rows, :]
        y = _dot(mixed.astype(BF16), wo_ref[...])
        o_ref[rows, :] = x_ref[rows, :] + _rms(y, post_g_ref[...])


def _merge(layer, x, gla_o, s5_o, fox_o, pre_g, w_gate, w_gla, w_s5, w_fox, w_o, post_g, *, bsz, seq, tm):
    m, d = x.shape
    nl = seq // tm
    row = lambda b, l: (b * nl + l, 0)
    return pl.pallas_call(
        _merge_kernel,
        grid=(bsz, nl),
        in_specs=[
            pl.BlockSpec((tm, d), row),
            pl.BlockSpec((tm, GLA_V), row),
            pl.BlockSpec((tm, S5_WIDTH), lambda b, l: (l, b)),
            pl.BlockSpec((tm, FOX_W), row),
            _layer_spec(layer, (1, d)),
            _layer_spec(layer, (d, N_BRANCH * d), pipeline_mode=RESIDENT),
            _layer_spec(layer, (GLA_V, d), pipeline_mode=RESIDENT),
            _layer_spec(layer, (S5_WIDTH, d), pipeline_mode=RESIDENT),
            _layer_spec(layer, (FOX_W, d), pipeline_mode=RESIDENT),
            _layer_spec(layer, (d, d), pipeline_mode=RESIDENT),
            _layer_spec(layer, (1, d)),
        ],
        out_specs=pl.BlockSpec((tm, d), row),
        out_shape=jax.ShapeDtypeStruct((m, d), F32),
        compiler_params=_params("parallel", "parallel"),
        name="merge",
    )(x, gla_o, s5_o, fox_o, pre_g, w_gate, w_gla, w_s5, w_fox, w_o, post_g)


def _memkv_kernel(mem_ref, g_ref, w_ref, k_ref, v_ref):
    d = mem_ref.shape[1]
    h = _rms(mem_ref[...], g_ref[...]).astype(BF16)
    k_ref[...] = _dot(h, w_ref[:, 0:d]).astype(BF16)
    v_ref[...] = _dot(h, w_ref[:, d:2 * d]).astype(BF16)


def _memkv(mem, g, w_kv, *, tm):
    m, d = mem.shape
    depth = w_kv.shape[0]
    per_layer = lambda block: pl.BlockSpec((None,) + block, lambda l, i: (l, 0, 0))
    out = pl.BlockSpec((None, tm, d), lambda l, i: (l, i, 0))
    return pl.pallas_call(
        _memkv_kernel,
        grid=(depth, m // tm),
        in_specs=[pl.BlockSpec((tm, d), lambda l, i: (i, 0)), per_layer((1, d)), per_layer((d, 2 * d))],
        out_specs=[out, out],
        out_shape=[jax.ShapeDtypeStruct((depth, m, d), BF16)] * 2,
        compiler_params=_params("parallel", "parallel"),
        name="memkv",
    )(mem, g, w_kv)


def _xattn_kernel(x_ref, k_ref, v_ref, pre_g_ref, wq_ref, wo_ref, post_g_ref, o_ref):
    tm, d = x_ref.shape
    dh = d // XA_HEADS
    blocks = [slice(r0, r0 + tm // EDGE_BLOCKS) for r0 in range(0, tm, tm // EDGE_BLOCKS)]
    cols = [slice(n * dh, (n + 1) * dh) for n in range(XA_HEADS)]

    def softmax(s):
        e = jnp.exp(s - jnp.max(s, axis=1, keepdims=True))
        return (e / jnp.sum(e, axis=1, keepdims=True)).astype(BF16)

    qs = []
    for rows in blocks:
        h = _rms(x_ref[rows, :], pre_g_ref[...]).astype(BF16)
        qs.append((_dot(h, wq_ref[...]) * (dh ** -0.5)).astype(BF16))
    logits = [[_dot_nt(q[:, c], k_ref[:, c]) for c in cols] for q in qs]
    probs = [[softmax(s) for s in per_head] for per_head in logits]
    heads = [jnp.concatenate([_dot(p, v_ref[:, c]).astype(BF16) for p, c in zip(per_head, cols)], axis=1)
             for per_head in probs]
    for rows, attended in zip(blocks, heads):
        y = _dot(attended, wo_ref[...])
        o_ref[rows, :] = x_ref[rows, :] + _rms(y, post_g_ref[...])


def _xattn(layer, x, k, v, pre_g, w_q, w_o, post_g, *, bsz, seq, n_mem, tm):
    m, d = x.shape
    nl = seq // tm
    row = lambda b, l: (b * nl + l, 0)
    return pl.pallas_call(
        _xattn_kernel,
        grid=(bsz, nl),
        in_specs=[
            pl.BlockSpec((tm, d), row),
            _layer_spec(layer, (n_mem, d), lambda b, l: (b, 0)),
            _layer_spec(layer, (n_mem, d), lambda b, l: (b, 0)),
            _layer_spec(layer, (1, d)),
            _layer_spec(layer, (d, d), pipeline_mode=RESIDENT),
            _layer_spec(layer, (d, d), pipeline_mode=RESIDENT),
            _layer_spec(layer, (1, d)),
        ],
        out_specs=pl.BlockSpec((tm, d), row),
        out_shape=jax.ShapeDtypeStruct((m, d), F32),
        compiler_params=_params("parallel", "parallel"),
        name="xattn",
    )(x, k, v, pre_g, w_q, w_o, post_g)


def _tile(n, want):
    t = min(n, want)
    while n % t:
        t -= 1
    return t


def _block_diag(t):
    depth, g, r, c = t.shape
    eye = jnp.eye(g, dtype=t.dtype)
    return jnp.einsum("lgrc,gk->lgrkc", t, eye).reshape(depth, g * r, g * c)


def kernel(x, mem, ffn1_pre_g, ffn1_w_gu, ffn1_w_down, ffn1_post_g, mix_pre_g, w_in, gla_gate_w, gla_gate_b, gla_norm_g, w_gla_up, s5_a_re, s5_a_im, s5_log_dt, s5_b_re, s5_b_im, s5_c_re, s5_c_im, s5_d, s5_glu_w, s5_glu_b, w_s5_up, fox_f_b, w_fox_up, w_mix_out, mix_post_g, xa_pre_g, xa_mem_g, xa_w_q, xa_w_kv, xa_w_o, xa_post_g, ffn2_pre_g, ffn2_w_gu, ffn2_w_down, ffn2_post_g):
    bsz, seq, d = x.shape
    n_mem = mem.shape[1]
    depth = w_in.shape[0]
    d_ff = ffn1_w_down.shape[1]
    m = bsz * seq
    assert seq % CHUNK == 0 and d % LANES == 0

    tm_ffn = _tile(m, 1024)
    tf = _tile(d_ff // LANES, 2) * LANES
    tm = _tile(seq, 1024)
    tk = _tile(seq, 512)
    ts = _tile(seq, 128)
    tb = 2 * ts if seq % (2 * ts) == 0 else ts

    bf = lambda a: a.astype(BF16)
    vec = lambda a: a.reshape(depth, 1, a.shape[-1])

    sizes = (GLA_QK, GLA_QK, GLA_V, GLA_V, GLA_RANK, S5_WIDTH, FOX_W, FOX_W, FOX_W, FOX_HEADS, N_BRANCH * d)
    offs = [0]
    for s in sizes:
        offs.append(offs[-1] + s)
    w_in_b = bf(w_in)
    seg = lambda i: w_in_b[:, :, offs[i]:offs[i + 1]]
    pad = jnp.zeros((depth, d, IN_SMALL - GLA_RANK - FOX_HEADS), BF16)
    w_proj = jnp.concatenate([seg(0), seg(1), seg(2), seg(3), seg(5), seg(6), seg(7), seg(8), seg(4), seg(9), pad],
                             axis=2)
    w_gate = seg(10)
    gla_wg = bf(jnp.concatenate([gla_gate_w, jnp.zeros((depth, IN_SMALL - GLA_RANK, GLA_QK), gla_gate_w.dtype)],
                                axis=1))

    ab_re, ab_im, bb_re, bb_im = _s5prep(s5_a_re, s5_a_im, s5_log_dt, s5_b_re, s5_b_im)
    grp = lambda t: t.reshape(depth, S5_GROUPS, S5_STATE, S5_CH)
    s5_bre = bf(_block_diag(grp(bb_re).transpose(0, 1, 3, 2)))
    s5_bim = bf(_block_diag(grp(bb_im).transpose(0, 1, 3, 2)))
    s5_cre = bf(_block_diag(s5_c_re.transpose(0, 1, 3, 2)))
    s5_cim = bf(_block_diag(s5_c_im.transpose(0, 1, 3, 2)))
    ab_re = ab_re.reshape(depth, 1, S5_NSTATE)
    ab_im = ab_im.reshape(depth, 1, S5_NSTATE)

    ffn1_gu, ffn1_down, ffn2_gu, ffn2_down = bf(ffn1_w_gu), bf(ffn1_w_down), bf(ffn2_w_gu), bf(ffn2_w_down)
    gla_up, s5_up, fox_up, mix_out = bf(w_gla_up), bf(w_s5_up), bf(w_fox_up), bf(w_mix_out)
    glu_w, xa_q, xa_kv, xa_o = bf(s5_glu_w), bf(xa_w_q), bf(xa_w_kv), bf(xa_w_o)
    fox_b = fox_f_b.reshape(depth, FOX_HEADS, 1)

    xs = x.reshape(m, d)
    mk, mv = _memkv(mem.reshape(bsz * n_mem, d), vec(xa_mem_g), xa_kv, tm=_tile(bsz * n_mem, 512))
    for l in range(depth):
        xs = _ffn(l, xs, vec(ffn1_pre_g), ffn1_gu, ffn1_down, vec(ffn1_post_g), tm=tm_ffn, tf=tf)

        qk, vr, su, fox, small = _inproj(l, xs, vec(mix_pre_g), w_proj, bsz=bsz, seq=seq, tm=tm)
        gla_o = _gla(l, qk, vr, small, gla_wg, vec(gla_gate_b), vec(gla_norm_g), bsz=bsz, seq=seq)
        f_cols = _foxprep(l, small, fox_b, bsz=bsz, seq=seq)
        fox_o = _fox(fox, f_cols, bsz=bsz, seq=seq, tk=tk)
        s5_o = _s5(l, su.reshape(seq * bsz, S5_WIDTH), ab_re, ab_im, s5_bre, s5_bim, s5_cre, s5_cim,
                   vec(s5_d), glu_w, vec(s5_glu_b), bsz=bsz, seq=seq, tb=tb, ts=ts)
        xs = _merge(l, xs, gla_o, s5_o.reshape(seq, bsz * S5_WIDTH), fox_o, vec(mix_pre_g), w_gate, gla_up,
                    s5_up, fox_up, mix_out, vec(mix_post_g), bsz=bsz, seq=seq, tm=tm)

        xs = _xattn(l, xs, mk, mv, vec(xa_pre_g), xa_q, xa_o, vec(xa_post_g), bsz=bsz, seq=seq, n_mem=n_mem, tm=tm)

        xs = _ffn(l, xs, vec(ffn2_pre_g), ffn2_gu, ffn2_down, vec(ffn2_post_g), tm=tm_ffn, tf=tf)
    return xs.reshape(bsz, seq, d)
```

```python
import functools
import math

import jax
import jax.numpy as jnp
from jax import lax
from jax.experimental import pallas as pl
from jax.experimental.pallas import tpu as pltpu

F32 = jnp.float32
BF16 = jnp.bfloat16
EPS = 1e-6
LOG2E = math.log2(math.e)

LANES = 128
SUBLANES = 8
VMEM_LIMIT_BYTES = 56 * 1024 * 1024

EDGE_BLOCKS = 4
CHUNK = 64
GLA_GROUP = 8
GLA_HEADS, GLA_DK, GLA_DV, GLA_RANK, GLA_TAU = 4, 32, 64, 16, 16.0
GLA_QK = GLA_HEADS * GLA_DK
GLA_V = GLA_HEADS * GLA_DV
DK_SHIFT = GLA_DK.bit_length() - 1
DV_SHIFT = GLA_DV.bit_length() - 1
S5_GROUPS, S5_CH, S5_STATE = 16, 16, 64
S5_WIDTH = S5_GROUPS * S5_CH
S5_NSTATE = S5_GROUPS * S5_STATE
FOX_HEADS, FOX_DH = 8, 64
FOX_W = FOX_HEADS * FOX_DH
FOX_PAIRS = FOX_W // LANES
FOX_TERMS = 4
HEADS_SHIFT = FOX_HEADS.bit_length() - 1
XA_HEADS = 4
N_BRANCH = 3

NT_DIMS = (((1,), (1,)), ((), ()))
RESIDENT = pl.Buffered(1)


def _params(*semantics):
    return pltpu.CompilerParams(dimension_semantics=semantics, vmem_limit_bytes=VMEM_LIMIT_BYTES)


def _layer_spec(layer, block, index_map=None, pipeline_mode=None):
    block = tuple(block)
    if index_map is None:
        index_map = lambda *_: (0,) * len(block)
    return pl.BlockSpec((None,) + block, lambda *ids: (layer,) + tuple(index_map(*ids)),
                        pipeline_mode=pipeline_mode)


def _rms(x, g):
    return x * lax.rsqrt(jnp.mean(x * x, axis=-1, keepdims=True) + EPS) * g


def _dot(a, b):
    return jnp.dot(a, b, preferred_element_type=F32)


def _dot_nt(a, b):
    return lax.dot_general(a, b, NT_DIMS, preferred_element_type=F32)


def _log_sigmoid(x):
    return jnp.minimum(x, 0.0) - jnp.log1p(jnp.exp(-jnp.abs(x)))


def _split3(x):
    hi = x.astype(BF16)
    r1 = x - hi.astype(F32)
    mid = r1.astype(BF16)
    lo = (r1 - mid.astype(F32)).astype(BF16)
    return hi, mid, lo


def _ffn_kernel(x_ref, pre_g_ref, wgu_ref, wd_ref, post_g_ref, o_ref, *, tf):
    d_ff = wd_ref.shape[0]
    tm = x_ref.shape[0]
    blocks = [slice(r0, r0 + tm // EDGE_BLOCKS) for r0 in range(0, tm, tm // EDGE_BLOCKS)]

    def gate_up(h, c0):
        return _dot(h, wgu_ref[:, c0:c0 + tf]), _dot(h, wgu_ref[:, d_ff + c0:d_ff + c0 + tf])

    hs, first = [], []
    for rows in blocks:
        hs.append(_rms(x_ref[rows, :], pre_g_ref[...]).astype(BF16))
        first.append(gate_up(hs[-1], 0))
    h = jnp.concatenate(hs, axis=0)
    nxt = tuple(jnp.concatenate(t, axis=0) for t in zip(*first))

    acc = None
    for c0 in range(0, d_ff - tf, tf):
        gate, up = nxt
        nxt = gate_up(h, c0 + tf)
        a = (gate * jax.nn.sigmoid(gate) * up).astype(BF16)
        part = _dot(a, wd_ref[c0:c0 + tf, :])
        acc = part if acc is None else acc + part

    gate, up = nxt
    a = (gate * jax.nn.sigmoid(gate) * up).astype(BF16)
    for rows in blocks:
        total = _dot(a[rows, :], wd_ref[d_ff - tf:d_ff, :])
        if acc is not None:
            total = acc[rows, :] + total
        o_ref[rows, :] = x_ref[rows, :] + 0.5 * _rms(total, post_g_ref[...])


def _ffn(layer, x, pre_g, w_gu, w_down, post_g, *, tm, tf):
    m, d = x.shape
    d_ff = w_down.shape[1]
    return pl.pallas_call(
        functools.partial(_ffn_kernel, tf=tf),
        grid=(m // tm,),
        in_specs=[
            pl.BlockSpec((tm, d), lambda i: (i, 0)),
            _layer_spec(layer, (1, d)),
            _layer_spec(layer, (d, 2 * d_ff), pipeline_mode=RESIDENT),
            _layer_spec(layer, (d_ff, d), pipeline_mode=RESIDENT),
            _layer_spec(layer, (1, d)),
        ],
        out_specs=pl.BlockSpec((tm, d), lambda i: (i, 0)),
        out_shape=jax.ShapeDtypeStruct((m, d), F32),
        compiler_params=_params("parallel"),
        name="ffn",
    )(x, pre_g, w_gu, w_down, post_g)


REPACK_GROUP = 4


def _repack_kernel(a_ref, b_ref, o_ref, *, base, aligned, shift, n_cols):
    first = pl.program_id(1) * REPACK_GROUP
    wide = jnp.concatenate([a_ref[...], b_ref[...]], axis=1)
    col = (base + first) * LANES + lax.broadcasted_iota(jnp.int32, wide.shape, 1)
    wide = jnp.where(col < n_cols, wide, 0.0).astype(BF16)
    src = lax.broadcasted_iota(jnp.int32, (2 * LANES, LANES), 0)
    dst = lax.broadcasted_iota(jnp.int32, (2 * LANES, LANES), 1)
    for k in range(REPACK_GROUP):
        s = jnp.where(first + k < aligned, 0, shift)
        pick = (src == dst + s).astype(BF16)
        o_ref[:, k * LANES:(k + 1) * LANES] = _dot(wide[:, k * LANES:(k + 2) * LANES], pick).astype(BF16)


def _repack(w, *, base, n_blocks, aligned, shift):
    depth, d, n_cols = w.shape
    group = REPACK_GROUP
    assert base % group == 0 and n_blocks % group == 0
    last = pl.cdiv(n_cols, LANES) - 1
    return pl.pallas_call(
        functools.partial(_repack_kernel, base=base, aligned=aligned, shift=shift, n_cols=n_cols),
        grid=(depth, n_blocks // group),
        in_specs=[
            pl.BlockSpec((None, d, group * LANES), lambda l, j: (l, 0, base // group + j)),
            pl.BlockSpec((None, d, LANES), lambda l, j: (l, 0, jnp.minimum(base + (j + 1) * group, last))),
        ],
        out_specs=pl.BlockSpec((None, d, group * LANES), lambda l, j: (l, 0, j)),
        out_shape=jax.ShapeDtypeStruct((depth, d, n_blocks * LANES), BF16),
        compiler_params=_params("parallel", "parallel"),
        name="repack",
    )(w, w)


IN_QK = 2 * GLA_QK
IN_VR = 2 * GLA_V
IN_SU = S5_WIDTH
IN_FOX = 3 * FOX_W
IN_SMALL = LANES
OFF_VR = IN_QK
OFF_SU = OFF_VR + IN_VR
OFF_FOX = OFF_SU + IN_SU
OFF_SMALL = OFF_FOX + IN_FOX


def _inproj_kernel(x_ref, g_ref, w_ref, ws_ref, qk_ref, vr_ref, su_ref, fox_ref, small_ref):
    tm = x_ref.shape[0]
    hs = []
    for r0 in range(0, tm, tm // EDGE_BLOCKS):
        rows = slice(r0, r0 + tm // EDGE_BLOCKS)
        hs.append(_rms(x_ref[rows, :], g_ref[...]).astype(BF16))
        qk_ref[rows, :] = _dot(hs[-1], w_ref[:, 0:OFF_VR])
    h = jnp.concatenate(hs, axis=0)
    vr_ref[...] = _dot(h, w_ref[:, OFF_VR:OFF_SU]).astype(BF16)
    su_ref[...] = _dot(h, w_ref[:, OFF_SU:OFF_FOX])
    fox_ref[:, 0:FOX_W] = (_dot(h, w_ref[:, OFF_FOX:OFF_FOX + FOX_W]) * (FOX_DH ** -0.5 * LOG2E)).astype(BF16)
    fox_ref[:, FOX_W:IN_FOX] = _dot(h, w_ref[:, OFF_FOX + FOX_W:OFF_SMALL]).astype(BF16)
    small_ref[...] = _dot(h, ws_ref[...])


def _inproj(layer, x, g, w, w_small, *, bsz, seq, tm):
    m, d = x.shape
    nl = seq // tm
    row = lambda b, l: (b * nl + l, 0)
    return pl.pallas_call(
        _inproj_kernel,
        grid=(bsz, nl),
        in_specs=[
            pl.BlockSpec((tm, d), row),
            _layer_spec(layer, (1, d)),
            _layer_spec(layer, (d, OFF_SMALL), pipeline_mode=RESIDENT),
            _layer_spec(layer, (d, IN_SMALL), pipeline_mode=RESIDENT),
        ],
        out_specs=[
            pl.BlockSpec((tm, IN_QK), row),
            pl.BlockSpec((tm, IN_VR), row),
            pl.BlockSpec((tm, IN_SU), lambda b, l: (l, b)),
            pl.BlockSpec((tm, IN_FOX), row),
            pl.BlockSpec((tm, IN_SMALL), row),
        ],
        out_shape=[
            jax.ShapeDtypeStruct((m, IN_QK), F32),
            jax.ShapeDtypeStruct((m, IN_VR), BF16),
            jax.ShapeDtypeStruct((seq, bsz * IN_SU), F32),
            jax.ShapeDtypeStruct((m, IN_FOX), BF16),
            jax.ShapeDtypeStruct((m, IN_SMALL), F32),
        ],
        compiler_params=_params("parallel", "parallel"),
        name="inproj",
    )(x, g, w, w_small)


def _gla_kernel(qk_ref, vr_ref, small_ref, wg_ref, bg_ref, ng_ref, o_ref, la_sc):
    seq = qk_ref.shape[0]
    n_chunks = seq // CHUNK
    c, hq, hv = CHUNK, GLA_QK, GLA_V

    z = _dot(small_ref[...].astype(BF16), wg_ref[...]) + bg_ref[...]
    la_sc[...] = _log_sigmoid(z) * (1.0 / GLA_TAU)

    row = lax.broadcasted_iota(jnp.int32, (c, c), 0)
    col = lax.broadcasted_iota(jnp.int32, (c, c), 1)
    tri = (col <= row).astype(BF16)
    row_x = lax.broadcasted_iota(jnp.int32, (GLA_HEADS * c, c), 0)
    col_x = lax.broadcasted_iota(jnp.int32, (GLA_HEADS * c, c), 1)
    lower = col_x <= (row_x & (c - 1))
    lane_q = lax.broadcasted_iota(jnp.int32, (c, hq), 1)
    lane_v = lax.broadcasted_iota(jnp.int32, (c, hv), 1)
    q_head = [(lane_q >> DK_SHIFT) == h for h in range(GLA_HEADS)]
    v_head = [(lane_v >> DV_SHIFT) == h for h in range(GLA_HEADS)]
    st_row = lax.broadcasted_iota(jnp.int32, (hv, hq), 0)
    st_col = lax.broadcasted_iota(jnp.int32, (hv, hq), 1)
    same_head = (st_row >> DV_SHIFT) == (st_col >> DK_SHIFT)

    def expand(t):
        return jnp.concatenate([jnp.where(q_head[h], t, 0.0) for h in range(GLA_HEADS)], axis=0).astype(BF16)

    gr = lax.broadcasted_iota(jnp.int32, (hv, hv), 0) >> DV_SHIFT
    gc = lax.broadcasted_iota(jnp.int32, (hv, hv), 1) >> DV_SHIFT
    avg = jnp.where(gr == gc, 1.0 / GLA_DV, 0.0).astype(BF16)

    def within_chunks(group):
        stage1 = []
        for i in group:
            hi, mid, lo = _split3(la_sc[i * c:(i + 1) * c, :])
            stage1.append(_dot(tri, hi) + _dot(tri, mid) + _dot(tri, lo))
        stage2 = []
        for i, g in zip(group, stage1):
            rows = slice(i * c, (i + 1) * c)
            g_last = g[c - 1:c, :]
            eg, ieg = jnp.exp(g), jnp.exp(-g)
            qc = qk_ref[rows, 0:hq] * (GLA_DK ** -0.5)
            kc = qk_ref[rows, hq:2 * hq]
            v = vr_ref[rows, 0:hv]
            q_fwd = qc * eg
            a_fwd = _dot_nt(expand(q_fwd), (kc * ieg).astype(BF16))
            a_bwd = _dot_nt(expand(qc * ieg), (kc * eg).astype(BF16))
            kw = (kc * jnp.exp(g_last - g)).astype(BF16)
            ds_t = _dot(v.astype(F32).T.astype(BF16), kw)
            stage2.append((a_fwd, a_bwd, v, q_fwd.astype(BF16), jnp.where(same_head, ds_t, 0.0), jnp.exp(g_last)))
        results = []
        for a_fwd, a_bwd, v, q_fwd, ds_t, decay in stage2:
            attn = jnp.where(lower, a_fwd, a_bwd).astype(BF16)
            p = _dot(attn, v)
            o = jnp.where(v_head[0], p[0:c, :], 0.0)
            for h in range(1, GLA_HEADS):
                o = o + jnp.where(v_head[h], p[h * c:(h + 1) * c, :], 0.0)
            results.append((o, q_fwd, ds_t, decay))
        return results

    def finish(group, results, s_t):
        outs = []
        for o, q_fwd, ds_t, decay in results:
            outs.append(o + _dot_nt(q_fwd, s_t.astype(BF16)))
            s_t = s_t * decay + ds_t
        o = jnp.concatenate(outs, axis=0)
        rows = slice(group[0] * c, (group[-1] + 1) * c)
        hi, mid, lo = _split3(o * o)
        ms = _dot(hi, avg) + _dot(mid, avg) + _dot(lo, avg)
        r = vr_ref[rows, hv:2 * hv].astype(F32)
        o_ref[rows, :] = (o * lax.rsqrt(ms + EPS) * ng_ref[...] * (r * jax.nn.sigmoid(r))).astype(BF16)
        return s_t

    groups = [range(first, min(first + GLA_GROUP, n_chunks)) for first in range(0, n_chunks, GLA_GROUP)]
    s_t = jnp.zeros((hv, hq), F32)
    pending = None
    for group in groups:
        results = within_chunks(group)
        if pending is not None:
            s_t = finish(*pending, s_t)
        pending = (group, results)
    finish(*pending, s_t)


def _gla(layer, qk, vr, small, wg, bg, ng, *, bsz, seq):
    m = qk.shape[0]
    row = lambda b: (b, 0)
    return pl.pallas_call(
        _gla_kernel,
        grid=(bsz,),
        in_specs=[
            pl.BlockSpec((seq, IN_QK), row),
            pl.BlockSpec((seq, IN_VR), row),
            pl.BlockSpec((seq, IN_SMALL), row),
            _layer_spec(layer, (IN_SMALL, GLA_QK)),
            _layer_spec(layer, (1, GLA_QK)),
            _layer_spec(layer, (1, GLA_V)),
        ],
        out_specs=pl.BlockSpec((seq, GLA_V), row),
        out_shape=jax.ShapeDtypeStruct((m, GLA_V), BF16),
        scratch_shapes=[pltpu.VMEM((seq, GLA_QK), F32)],
        compiler_params=_params("parallel"),
        name="gla",
    )(qk, vr, small, wg, bg, ng)


def _foxprep_kernel(small_ref, b_ref, o_ref, *, bsz):
    seq = small_ref.shape[0] // bsz
    ff = jnp.concatenate([small_ref[b * seq:(b + 1) * seq, :].T[GLA_RANK:GLA_RANK + FOX_HEADS, :]
                          for b in range(bsz)], axis=0)
    x = _log_sigmoid(ff + jnp.concatenate([b_ref[...]] * bsz, axis=0))
    lane = lax.broadcasted_iota(jnp.int32, x.shape, 1)
    shift = 1
    while shift < seq:
        x = x + jnp.where(lane >= shift, pltpu.roll(x, shift, axis=1), 0.0)
        shift *= 2
    hi, mid, lo = (t.astype(F32) for t in _split3(x * LOG2E))
    ones = jnp.ones((FOX_HEADS, seq), F32)
    pad = jnp.zeros((LANES - FOX_TERMS * FOX_HEADS, seq), F32)
    for b in range(bsz):
        mine = slice(b * FOX_HEADS, (b + 1) * FOX_HEADS)
        terms = jnp.concatenate([hi[mine], mid[mine], lo[mine], ones, pad], axis=0)
        o_ref[b * seq:(b + 1) * seq, :] = terms.T.astype(BF16)


def _foxprep(layer, small, bias, *, bsz, seq):
    whole = lambda i: (0, 0)
    return pl.pallas_call(
        functools.partial(_foxprep_kernel, bsz=bsz),
        grid=(1,),
        in_specs=[
            pl.BlockSpec((bsz * seq, IN_SMALL), whole),
            _layer_spec(layer, (FOX_HEADS, 1)),
        ],
        out_specs=pl.BlockSpec((bsz * seq, LANES), whole),
        out_shape=jax.ShapeDtypeStruct((bsz * seq, LANES), BF16),
        compiler_params=_params("arbitrary"),
        name="foxprep",
    )(small, bias)


def _fox_placement(head, base, *, key):
    r = lax.broadcasted_iota(jnp.int32, (LANES, LANES), 0)
    rel = lax.broadcasted_iota(jnp.int32, (LANES, LANES), 1) - base
    term = r >> HEADS_SHIFT
    mine = ((r & (FOX_HEADS - 1)) == head) & (term < FOX_TERMS)
    is_part = mine & (term < FOX_TERMS - 1)
    is_one = mine & (term == FOX_TERMS - 1)
    n = FOX_TERMS - 1
    if key:
        m = jnp.where(is_part & (rel == term + n), -1.0, jnp.where(is_one & (rel >= 0) & (rel < n), 1.0, 0.0))
    else:
        m = jnp.where(is_part & (rel == term), 1.0, jnp.where(is_one & (rel >= n) & (rel < 2 * n), 1.0, 0.0))
    return m.astype(BF16)


def _fox_kernel(q_ref, k_ref, v_ref, f_ref, o_ref, qx_sc, kx_sc, vx_sc, *, tk):
    seq = k_ref.shape[0]
    spare = (FOX_DH, 0)
    lane = lax.broadcasted_iota(jnp.int32, (seq, LANES), 1)
    q, k, v, f = q_ref[...], k_ref[...], v_ref[...], f_ref[...]
    for h in range(2):
        own = (lane < FOX_DH) if h == 0 else (lane >= FOX_DH)
        head = 2 * pl.program_id(1) + h
        qx_sc[h] = jnp.where(own, q, _dot(f, _fox_placement(head, spare[h], key=False)).astype(BF16))
        kx_sc[h] = jnp.where(own, k, _dot(f, _fox_placement(head, spare[h], key=True)).astype(BF16))
        vx_sc[h] = jnp.where(own, v, jnp.ones_like(v))

    row = lax.broadcasted_iota(jnp.int32, (tk, tk), 0)
    col = lax.broadcasted_iota(jnp.int32, (tk, tk), 1)
    causal = col <= row

    items = [(h, r0, c0) for c0 in range(0, seq, tk) for h in range(2) for r0 in range(c0, seq, tk)]

    def logits(item):
        h, r0, c0 = item
        s = _dot_nt(qx_sc[h, r0:r0 + tk, :], kx_sc[h, c0:c0 + tk, :])
        return jnp.where(causal, s, -jnp.inf) if r0 == c0 else s

    state = {}
    s_next = logits(items[0])
    for n, (h, r0, c0) in enumerate(items):
        s = s_next
        if n + 1 < len(items):
            s_next = logits(items[n + 1])
        vx = vx_sc[h, c0:c0 + tk, :]
        m_new = jnp.max(s, axis=1, keepdims=True)
        if c0 == 0:
            acc = _dot(jnp.exp2(s - m_new).astype(BF16), vx)
        else:
            m_old, acc_old = state[h, r0]
            m_new = jnp.maximum(m_old, m_new)
            acc = jnp.exp2(m_old - m_new) * acc_old + _dot(jnp.exp2(s - m_new).astype(BF16), vx)
        state[h, r0] = (m_new, acc)

    first = lax.broadcasted_iota(jnp.int32, (tk, LANES), 1) < FOX_DH
    for r0 in range(0, seq, tk):
        a0, a1 = state[0, r0][1], state[1, r0][1]
        o_ref[r0:r0 + tk, :] = jnp.where(first, a0 / pltpu.roll(a0, FOX_DH, axis=1),
                                         a1 / pltpu.roll(a1, FOX_DH, axis=1)).astype(BF16)


def _fox(fox, f, *, bsz, seq, tk):
    m = fox.shape[0]
    return pl.pallas_call(
        functools.partial(_fox_kernel, tk=tk),
        grid=(bsz, FOX_PAIRS),
        in_specs=[
            pl.BlockSpec((seq, LANES), lambda b, p: (b, p)),
            pl.BlockSpec((seq, LANES), lambda b, p: (b, FOX_PAIRS + p)),
            pl.BlockSpec((seq, LANES), lambda b, p: (b, 2 * FOX_PAIRS + p)),
            pl.BlockSpec((seq, LANES), lambda b, p: (b, 0)),
        ],
        out_specs=pl.BlockSpec((seq, LANES), lambda b, p: (b, p)),
        out_shape=jax.ShapeDtypeStruct((m, FOX_W), BF16),
        scratch_shapes=[pltpu.VMEM((2, seq, LANES), BF16)] * 3,
        compiler_params=_params("parallel", "parallel"),
        name="fox",
    )(fox, fox, fox, f)


def _s5prep_kernel(are_ref, aim_ref, ldt_ref, bre_ref, bim_ref, abre_ref, abim_ref, bbre_ref, bbim_ref):
    lam_re = jnp.minimum(are_ref[...], -1e-4)
    lam_im = aim_ref[...]
    dt = jnp.exp(ldt_ref[...])
    mag = jnp.exp(lam_re * dt)
    ab_re = mag * jnp.cos(lam_im * dt)
    ab_im = mag * jnp.sin(lam_im * dt)
    den = lam_re * lam_re + lam_im * lam_im
    z_re = ((ab_re - 1.0) * lam_re + ab_im * lam_im) / den
    z_im = (ab_im * lam_re - (ab_re - 1.0) * lam_im) / den
    br, bi = bre_ref[...], bim_ref[...]
    abre_ref[...] = ab_re
    abim_ref[...] = ab_im
    bbre_ref[...] = z_re * br - z_im * bi
    bbim_ref[...] = z_re * bi + z_im * br


def _s5prep(a_re, a_im, log_dt, b_re, b_im):
    depth = a_re.shape[0]
    col = lambda a: a.reshape(depth, S5_NSTATE, 1)
    ldt = jnp.broadcast_to(log_dt[:, :, None], (depth, S5_GROUPS, S5_STATE))
    vec = pl.BlockSpec((None, S5_NSTATE, 1), lambda l: (l, 0, 0))
    mat = pl.BlockSpec((None, S5_NSTATE, S5_CH), lambda l: (l, 0, 0))
    return pl.pallas_call(
        _s5prep_kernel,
        grid=(depth,),
        in_specs=[vec, vec, vec, mat, mat],
        out_specs=[vec, vec, mat, mat],
        out_shape=[jax.ShapeDtypeStruct((depth, S5_NSTATE, 1), F32)] * 2
        + [jax.ShapeDtypeStruct((depth, S5_NSTATE, S5_CH), F32)] * 2,
        compiler_params=_params("parallel"),
        name="s5prep",
    )(col(a_re), col(a_im), col(ldt), b_re.reshape(depth, S5_NSTATE, S5_CH), b_im.reshape(depth, S5_NSTATE, S5_CH))


def _s5_kernel(u_ref, abre_ref, abim_ref, bre_ref, bim_ref, cre_ref, cim_ref, d_ref, gw_ref, gb_ref, o_ref,
               sre_sc, sim_sc, *, bsz, ts):
    @pl.when(pl.program_id(0) == 0)
    def _():
        sre_sc[...] = jnp.zeros_like(sre_sc)
        sim_sc[...] = jnp.zeros_like(sim_sc)

    rows = ts * bsz
    n_sub = u_ref.shape[0] // rows
    a_re = jnp.broadcast_to(abre_ref[...], (bsz, S5_NSTATE))
    a_im = jnp.broadcast_to(abim_ref[...], (bsz, S5_NSTATE))

    bu = []
    for k in range(n_sub):
        ub = u_ref[k * rows:(k + 1) * rows, :].astype(BF16)
        bu.append((_dot(ub, bre_ref[...]), _dot(ub, bim_ref[...])))

    s_re, s_im = sre_sc[...], sim_sc[...]
    for k in range(n_sub):
        bu_re, bu_im = bu[k]
        x_re, x_im = [], []
        for t in range(ts):
            r = slice(t * bsz, (t + 1) * bsz)
            s_re, s_im = (a_re * s_re - a_im * s_im + bu_re[r, :], a_re * s_im + a_im * s_re + bu_im[r, :])
            x_re.append(s_re)
            x_im.append(s_im)
        x_re = jnp.concatenate(x_re, axis=0).astype(BF16)
        x_im = jnp.concatenate(x_im, axis=0).astype(BF16)
        u = u_ref[k * rows:(k + 1) * rows, :]
        y = _dot(x_re, cre_ref[...]) - _dot(x_im, cim_ref[...])
        y = jax.nn.gelu(y + d_ref[...] * u, approximate=True)
        o_ref[k * rows:(k + 1) * rows, :] = (
            y * jax.nn.sigmoid(_dot(y.astype(BF16), gw_ref[...]) + gb_ref[...])).astype(BF16)
    sre_sc[...] = s_re
    sim_sc[...] = s_im


def _s5(layer, u, ab_re, ab_im, b_re, b_im, c_re, c_im, d, glu_w, glu_b, *, bsz, seq, tb, ts):
    rows = tb * bsz
    return pl.pallas_call(
        functools.partial(_s5_kernel, bsz=bsz, ts=ts),
        grid=(seq // tb,),
        in_specs=[
            pl.BlockSpec((rows, S5_WIDTH), lambda i: (i, 0)),
            _layer_spec(layer, (1, S5_NSTATE)),
            _layer_spec(layer, (1, S5_NSTATE)),
            _layer_spec(layer, (S5_WIDTH, S5_NSTATE)),
            _layer_spec(layer, (S5_WIDTH, S5_NSTATE)),
            _layer_spec(layer, (S5_NSTATE, S5_WIDTH)),
            _layer_spec(layer, (S5_NSTATE, S5_WIDTH)),
            _layer_spec(layer, (1, S5_WIDTH)),
            _layer_spec(layer, (S5_WIDTH, S5_WIDTH)),
            _layer_spec(layer, (1, S5_WIDTH)),
        ],
        out_specs=pl.BlockSpec((rows, S5_WIDTH), lambda i: (i, 0)),
        out_shape=jax.ShapeDtypeStruct((seq * bsz, S5_WIDTH), BF16),
        scratch_shapes=[pltpu.VMEM((bsz, S5_NSTATE), F32), pltpu.VMEM((bsz, S5_NSTATE), F32)],
        compiler_params=_params("arbitrary"),
        name="s5",
    )(u, ab_re, ab_im, b_re, b_im, c_re, c_im, d, glu_w, glu_b)


def _merge_kernel(x_ref, gla_ref, s5_ref, fox_ref, pre_g_ref, wgate_ref, wgla_ref, ws5_ref, wfox_ref, wo_ref,
                  post_g_ref, o_ref):
    tm, d = x_ref.shape
    blocks = [slice(r0, r0 + tm // EDGE_BLOCKS) for r0 in range(0, tm, tm // EDGE_BLOCKS)]
    branches = ((gla_ref, wgla_ref), (s5_ref, ws5_ref), (fox_ref, wfox_ref))

    hs, first = [], []
    for rows in blocks:
        hs.append(_rms(x_ref[rows, :], pre_g_ref[...]).astype(BF16))
        first.append(_dot(hs[-1], wgate_ref[:, 0:d]))
    h = jnp.concatenate(hs, axis=0)
    logits = jnp.concatenate(first, axis=0)
    mix = None
    for n, (b_ref, w_ref) in enumerate(branches[:-1]):
        term = jax.nn.sigmoid(logits) * _dot(b_ref[...], w_ref[...])
        mix = term if mix is None else mix + term
        logits = _dot(h, wgate_ref[:, (n + 1) * d:(n + 2) * d])
    up = _dot(branches[-1][0][...], branches[-1][1][...])

    for rows in blocks:
        mixed = mix[rows, :] + jax.nn.sigmoid(logits[rows, :]) * up[rows, :]
        y = _dot(mixed.astype(BF16), wo_ref[...])
        o_ref[rows, :] = x_ref[rows, :] + _rms(y, post_g_ref[...])


def _merge(layer, x, gla_o, s5_o, fox_o, pre_g, w_gate, w_gla, w_s5, w_fox, w_o, post_g, *, bsz, seq, tm):
    m, d = x.shape
    nl = seq // tm
    row = lambda b, l: (b * nl + l, 0)
    return pl.pallas_call(
        _merge_kernel,
        grid=(bsz, nl),
        in_specs=[
            pl.BlockSpec((tm, d), row),
            pl.BlockSpec((tm, GLA_V), row),
            pl.BlockSpec((tm, S5_WIDTH), lambda b, l: (l, b)),
            pl.BlockSpec((tm, FOX_W), row),
            _layer_spec(layer, (1, d)),
            _layer_spec(layer, (d, N_BRANCH * d), pipeline_mode=RESIDENT),
            _layer_spec(layer, (GLA_V, d), pipeline_mode=RESIDENT),
            _layer_spec(layer, (S5_WIDTH, d), pipeline_mode=RESIDENT),
            _layer_spec(layer, (FOX_W, d), pipeline_mode=RESIDENT),
            _layer_spec(layer, (d, d), pipeline_mode=RESIDENT),
            _layer_spec(layer, (1, d)),
        ],
        out_specs=pl.BlockSpec((tm, d), row),
        out_shape=jax.ShapeDtypeStruct((m, d), F32),
        compiler_params=_params("parallel", "parallel"),
        name="merge",
    )(x, gla_o, s5_o, fox_o, pre_g, w_gate, w_gla, w_s5, w_fox, w_o, post_g)


def _memkv_kernel(mem_ref, g_ref, w_ref, k_ref, v_ref):
    d = mem_ref.shape[1]
    h = _rms(mem_ref[...], g_ref[...]).astype(BF16)
    k_ref[...] = _dot(h, w_ref[:, 0:d]).astype(BF16)
    v_ref[...] = _dot(h, w_ref[:, d:2 * d]).astype(BF16)


def _memkv(mem, g, w_kv, *, tm):
    m, d = mem.shape
    depth = w_kv.shape[0]
    per_layer = lambda block: pl.BlockSpec((None,) + block, lambda l, i: (l, 0, 0))
    out = pl.BlockSpec((None, tm, d), lambda l, i: (l, i, 0))
    return pl.pallas_call(
        _memkv_kernel,
        grid=(depth, m // tm),
        in_specs=[pl.BlockSpec((tm, d), lambda l, i: (i, 0)), per_layer((1, d)), per_layer((d, 2 * d))],
        out_specs=[out, out],
        out_shape=[jax.ShapeDtypeStruct((depth, m, d), BF16)] * 2,
        compiler_params=_params("parallel", "parallel"),
        name="memkv",
    )(mem, g, w_kv)


def _xattn_kernel(x_ref, k_ref, v_ref, pre_g_ref, wq_ref, wo_ref, post_g_ref, o_ref):
    tm, d = x_ref.shape
    dh = d // XA_HEADS
    blocks = [slice(r0, r0 + tm // EDGE_BLOCKS) for r0 in range(0, tm, tm // EDGE_BLOCKS)]
    cols = [slice(n * dh, (n + 1) * dh) for n in range(XA_HEADS)]

    def softmax(s):
        e = jnp.exp(s - jnp.max(s, axis=1, keepdims=True))
        return (e / jnp.sum(e, axis=1, keepdims=True)).astype(BF16)

    qs = []
    for rows in blocks:
        h = _rms(x_ref[rows, :], pre_g_ref[...]).astype(BF16)
        qs.append((_dot(h, wq_ref[...]) * (dh ** -0.5)).astype(BF16))
    logits = [[_dot_nt(q[:, c], k_ref[:, c]) for c in cols] for q in qs]
    probs = [[softmax(s) for s in per_head] for per_head in logits]
    heads = [jnp.concatenate([_dot(p, v_ref[:, c]).astype(BF16) for p, c in zip(per_head, cols)], axis=1)
             for per_head in probs]
    for rows, attended in zip(blocks, heads):
        y = _dot(attended, wo_ref[...])
        o_ref[rows, :] = x_ref[rows, :] + _rms(y, post_g_ref[...])


def _xattn(layer, x, k, v, pre_g, w_q, w_o, post_g, *, bsz, seq, n_mem, tm):
    m, d = x.shape
    nl = seq // tm
    row = lambda b, l: (b * nl + l, 0)
    return pl.pallas_call(
        _xattn_kernel,
        grid=(bsz, nl),
        in_specs=[
            pl.BlockSpec((tm, d), row),
            _layer_spec(layer, (n_mem, d), lambda b, l: (b, 0)),
            _layer_spec(layer, (n_mem, d), lambda b, l: (b, 0)),
            _layer_spec(layer, (1, d)),
            _layer_spec(layer, (d, d), pipeline_mode=RESIDENT),
            _layer_spec(layer, (d, d), pipeline_mode=RESIDENT),
            _layer_spec(layer, (1, d)),
        ],
        out_specs=pl.BlockSpec((tm, d), row),
        out_shape=jax.ShapeDtypeStruct((m, d), F32),
        compiler_params=_params("parallel", "parallel"),
        name="xattn",
    )(x, k, v, pre_g, w_q, w_o, post_g)


def _tile(n, want):
    t = min(n, want)
    while n % t:
        t -= 1
    return t


def _block_diag(t):
    depth, g, r, c = t.shape
    eye = jnp.eye(g, dtype=t.dtype)
    return jnp.einsum("lgrc,gk->lgrkc", t, eye).reshape(depth, g * r, g * c)


def kernel(x, mem, ffn1_pre_g, ffn1_w_gu, ffn1_w_down, ffn1_post_g, mix_pre_g, w_in, gla_gate_w, gla_gate_b, gla_norm_g, w_gla_up, s5_a_re, s5_a_im, s5_log_dt, s5_b_re, s5_b_im, s5_c_re, s5_c_im, s5_d, s5_glu_w, s5_glu_b, w_s5_up, fox_f_b, w_fox_up, w_mix_out, mix_post_g, xa_pre_g, xa_mem_g, xa_w_q, xa_w_kv, xa_w_o, xa_post_g, ffn2_pre_g, ffn2_w_gu, ffn2_w_down, ffn2_post_g):
    bsz, seq, d = x.shape
    n_mem = mem.shape[1]
    depth = w_in.shape[0]
    d_ff = ffn1_w_down.shape[1]
    m = bsz * seq
    assert seq % CHUNK == 0 and d % LANES == 0

    tm_ffn = _tile(m, 1024)
    tf = _tile(d_ff // LANES, 2) * LANES
    tm = _tile(seq, 1024)
    tk = _tile(seq, 512)
    ts = _tile(seq, 128)
    tb = 2 * ts if seq % (2 * ts) == 0 else ts

    bf = lambda a: a.astype(BF16)
    vec = lambda a: a.reshape(depth, 1, a.shape[-1])

    aligned = 2 * GLA_QK + 2 * GLA_V
    mid = S5_WIDTH + 3 * FOX_W
    assert aligned % LANES == 0 and mid % LANES == 0 and aligned + mid == OFF_SMALL
    w_main = _repack(w_in, base=0, n_blocks=OFF_SMALL // LANES, aligned=aligned // LANES, shift=GLA_RANK)
    gates_at = aligned + GLA_RANK + mid + FOX_HEADS
    w_gate = _repack(w_in, base=gates_at // LANES, n_blocks=N_BRANCH * d // LANES, aligned=0, shift=gates_at % LANES)
    w_small = bf(jnp.concatenate(
        [w_in[:, :, aligned:aligned + GLA_RANK], w_in[:, :, gates_at - FOX_HEADS:gates_at],
         jnp.zeros((depth, d, IN_SMALL - GLA_RANK - FOX_HEADS), w_in.dtype)], axis=2))
    gla_wg = bf(jnp.concatenate([gla_gate_w, jnp.zeros((depth, IN_SMALL - GLA_RANK, GLA_QK), gla_gate_w.dtype)],
                                axis=1))

    ab_re, ab_im, bb_re, bb_im = _s5prep(s5_a_re, s5_a_im, s5_log_dt, s5_b_re, s5_b_im)
    grp = lambda t: t.reshape(depth, S5_GROUPS, S5_STATE, S5_CH)
    s5_bre = bf(_block_diag(grp(bb_re).transpose(0, 1, 3, 2)))
    s5_bim = bf(_block_diag(grp(bb_im).transpose(0, 1, 3, 2)))
    s5_cre = bf(_block_diag(s5_c_re.transpose(0, 1, 3, 2)))
    s5_cim = bf(_block_diag(s5_c_im.transpose(0, 1, 3, 2)))
    ab_re = ab_re.reshape(depth, 1, S5_NSTATE)
    ab_im = ab_im.reshape(depth, 1, S5_NSTATE)

    ffn1_gu, ffn1_down, ffn2_gu, ffn2_down = bf(ffn1_w_gu), bf(ffn1_w_down), bf(ffn2_w_gu), bf(ffn2_w_down)
    gla_up, s5_up, fox_up, mix_out = bf(w_gla_up), bf(w_s5_up), bf(w_fox_up), bf(w_mix_out)
    glu_w, xa_q, xa_kv, xa_o = bf(s5_glu_w), bf(xa_w_q), bf(xa_w_kv), bf(xa_w_o)
    fox_b = fox_f_b.reshape(depth, FOX_HEADS, 1)

    xs = x.reshape(m, d)
    mk, mv = _memkv(mem.reshape(bsz * n_mem, d), vec(xa_mem_g), xa_kv, tm=_tile(bsz * n_mem, 512))
    for l in range(depth):
        xs = _ffn(l, xs, vec(ffn1_pre_g), ffn1_gu, ffn1_down, vec(ffn1_post_g), tm=tm_ffn, tf=tf)

        qk, vr, su, fox, small = _inproj(l, xs, vec(mix_pre_g), w_main, w_small, bsz=bsz, seq=seq, tm=tm)
        gla_o = _gla(l, qk, vr, small, gla_wg, vec(gla_gate_b), vec(gla_norm_g), bsz=bsz, seq=seq)
        f_cols = _foxprep(l, small, fox_b, bsz=bsz, seq=seq)
        fox_o = _fox(fox, f_cols, bsz=bsz, seq=seq, tk=tk)
        s5_o = _s5(l, su.reshape(seq * bsz, S5_WIDTH), ab_re, ab_im, s5_bre, s5_bim, s5_cre, s5_cim,
                   vec(s5_d), glu_w, vec(s5_glu_b), bsz=bsz, seq=seq, tb=tb, ts=ts)
        xs = _merge(l, xs, gla_o, s5_o.reshape(seq, bsz * S5_WIDTH), fox_o, vec(mix_pre_g), w_gate, gla_up,
                    s5_up, fox_up, mix_out, vec(mix_post_g), bsz=bsz, seq=seq, tm=tm)

        xs = _xattn(l, xs, mk, mv, vec(xa_pre_g), xa_q, xa_o, vec(xa_post_g), bsz=bsz, seq=seq, n_mem=n_mem, tm=tm)

        xs = _ffn(l, xs, vec(ffn2_pre_g), ffn2_gu, ffn2_down, vec(ffn2_post_g), tm=tm_ffn, tf=tf)
    return xs.reshape(bsz, seq, d)
```

```python
import functools
import math

import jax
import jax.numpy as jnp
from jax import lax
from jax.experimental import pallas as pl
from jax.experimental.pallas import tpu as pltpu

F32 = jnp.float32
BF16 = jnp.bfloat16
EPS = 1e-6
LOG2E = math.log2(math.e)

LANES = 128
SUBLANES = 8
VMEM_LIMIT_BYTES = 56 * 1024 * 1024

EDGE_BLOCKS = 4
CHUNK = 64
GLA_GROUP = 8
GLA_HEADS, GLA_DK, GLA_DV, GLA_RANK, GLA_TAU = 4, 32, 64, 16, 16.0
GLA_QK = GLA_HEADS * GLA_DK
GLA_V = GLA_HEADS * GLA_DV
DK_SHIFT = GLA_DK.bit_length() - 1
DV_SHIFT = GLA_DV.bit_length() - 1
S5_GROUPS, S5_CH, S5_STATE = 16, 16, 64
S5_WIDTH = S5_GROUPS * S5_CH
S5_NSTATE = S5_GROUPS * S5_STATE
FOX_HEADS, FOX_DH = 8, 64
FOX_W = FOX_HEADS * FOX_DH
FOX_PAIRS = FOX_W // LANES
FOX_TERMS = 4
HEADS_SHIFT = FOX_HEADS.bit_length() - 1
XA_HEADS = 4
N_BRANCH = 3

NT_DIMS = (((1,), (1,)), ((), ()))
RESIDENT = pl.Buffered(1)


def _params(*semantics):
    return pltpu.CompilerParams(dimension_semantics=semantics, vmem_limit_bytes=VMEM_LIMIT_BYTES)


def _layer_spec(layer, block, index_map=None, pipeline_mode=None):
    block = tuple(block)
    if index_map is None:
        index_map = lambda *_: (0,) * len(block)
    return pl.BlockSpec((None,) + block, lambda *ids: (layer,) + tuple(index_map(*ids)),
                        pipeline_mode=pipeline_mode)


def _rms(x, g):
    return x * lax.rsqrt(jnp.mean(x * x, axis=-1, keepdims=True) + EPS) * g


def _dot(a, b):
    return jnp.dot(a, b, preferred_element_type=F32)


def _dot_nt(a, b):
    return lax.dot_general(a, b, NT_DIMS, preferred_element_type=F32)


def _log_sigmoid(x):
    return jnp.minimum(x, 0.0) - jnp.log1p(jnp.exp(-jnp.abs(x)))


def _split3(x):
    hi = x.astype(BF16)
    r1 = x - hi.astype(F32)
    mid = r1.astype(BF16)
    lo = (r1 - mid.astype(F32)).astype(BF16)
    return hi, mid, lo


def _ffn_kernel(x_ref, pre_g_ref, wgu_ref, wd_ref, post_g_ref, o_ref, *, tf):
    d_ff = wd_ref.shape[0]
    tm = x_ref.shape[0]
    blocks = [slice(r0, r0 + tm // EDGE_BLOCKS) for r0 in range(0, tm, tm // EDGE_BLOCKS)]

    def gate_up(h, c0):
        return _dot(h, wgu_ref[:, c0:c0 + tf]), _dot(h, wgu_ref[:, d_ff + c0:d_ff + c0 + tf])

    hs, first = [], []
    for rows in blocks:
        hs.append(_rms(x_ref[rows, :], pre_g_ref[...]).astype(BF16))
        first.append(gate_up(hs[-1], 0))
    h = jnp.concatenate(hs, axis=0)
    nxt = tuple(jnp.concatenate(t, axis=0) for t in zip(*first))

    acc = None
    for c0 in range(0, d_ff - tf, tf):
        gate, up = nxt
        nxt = gate_up(h, c0 + tf)
        a = (gate * jax.nn.sigmoid(gate) * up).astype(BF16)
        part = _dot(a, wd_ref[c0:c0 + tf, :])
        acc = part if acc is None else acc + part

    gate, up = nxt
    a = (gate * jax.nn.sigmoid(gate) * up).astype(BF16)
    for rows in blocks:
        total = _dot(a[rows, :], wd_ref[d_ff - tf:d_ff, :])
        if acc is not None:
            total = acc[rows, :] + total
        o_ref[rows, :] = x_ref[rows, :] + 0.5 * _rms(total, post_g_ref[...])


def _ffn(layer, x, pre_g, w_gu, w_down, post_g, *, tm, tf):
    m, d = x.shape
    d_ff = w_down.shape[1]
    return pl.pallas_call(
        functools.partial(_ffn_kernel, tf=tf),
        grid=(m // tm,),
        in_specs=[
            pl.BlockSpec((tm, d), lambda i: (i, 0)),
            _layer_spec(layer, (1, d)),
            _layer_spec(layer, (d, 2 * d_ff), pipeline_mode=RESIDENT),
            _layer_spec(layer, (d_ff, d), pipeline_mode=RESIDENT),
            _layer_spec(layer, (1, d)),
        ],
        out_specs=pl.BlockSpec((tm, d), lambda i: (i, 0)),
        out_shape=jax.ShapeDtypeStruct((m, d), F32),
        compiler_params=_params("parallel"),
        name="ffn",
    )(x, pre_g, w_gu, w_down, post_g)


REPACK_GROUP = 4
REPACK_EDGE = 32


def _repack_kernel(a_ref, edge_ref, o_ref, *, aligned, shift):
    first = pl.program_id(1) * REPACK_GROUP
    wide = jnp.concatenate([a_ref[...], edge_ref[...]], axis=1).astype(BF16)
    src = lax.broadcasted_iota(jnp.int32, (2 * LANES, LANES), 0)
    dst = lax.broadcasted_iota(jnp.int32, (2 * LANES, LANES), 1)
    for k in range(REPACK_GROUP):
        s = jnp.where(first + k < aligned, 0, shift)
        pick = (src == dst + s).astype(BF16)
        o_ref[:, k * LANES:(k + 1) * LANES] = _dot(wide[:, k * LANES:(k + 2) * LANES], pick).astype(BF16)


def _repack(w, edges, *, base, n_blocks, aligned, shift):
    depth, d, _ = w.shape
    group = REPACK_GROUP
    assert base % group == 0 and n_blocks % group == 0 and shift <= REPACK_EDGE
    return pl.pallas_call(
        functools.partial(_repack_kernel, aligned=aligned, shift=shift),
        grid=(depth, n_blocks // group),
        in_specs=[
            pl.BlockSpec((None, d, group * LANES), lambda l, j: (l, 0, base // group + j)),
            pl.BlockSpec((None, d, LANES), lambda l, j: (l, 0, base // group + j)),
        ],
        out_specs=pl.BlockSpec((None, d, group * LANES), lambda l, j: (l, 0, j)),
        out_shape=jax.ShapeDtypeStruct((depth, d, n_blocks * LANES), BF16),
        compiler_params=_params("parallel", "parallel"),
        name="repack",
    )(w, edges)


def _smallpack_kernel(down_ref, forget_ref, o_ref):
    lane = lax.broadcasted_iota(jnp.int32, o_ref.shape, 1)
    keep = jnp.where(lane < GLA_RANK, down_ref[...], jnp.where(lane < GLA_RANK + FOX_HEADS, forget_ref[...], 0.0))
    o_ref[...] = keep.astype(BF16)


def _smallpack(w, edges, *, down_block, forget_group):
    depth, d, _ = w.shape
    return pl.pallas_call(
        _smallpack_kernel,
        grid=(depth,),
        in_specs=[pl.BlockSpec((None, d, LANES), lambda l: (l, 0, down_block)),
                  pl.BlockSpec((None, d, LANES), lambda l: (l, 0, forget_group - 1))],
        out_specs=pl.BlockSpec((None, d, LANES), lambda l: (l, 0, 0)),
        out_shape=jax.ShapeDtypeStruct((depth, d, LANES), BF16),
        compiler_params=_params("parallel"),
        name="smallpack",
    )(w, edges)


IN_QK = 2 * GLA_QK
IN_VR = 2 * GLA_V
IN_SU = S5_WIDTH
IN_FOX = 3 * FOX_W
IN_SMALL = LANES
OFF_VR = IN_QK
OFF_SU = OFF_VR + IN_VR
OFF_FOX = OFF_SU + IN_SU
OFF_SMALL = OFF_FOX + IN_FOX


def _inproj_kernel(x_ref, g_ref, w_ref, ws_ref, qk_ref, vr_ref, su_ref, fox_ref, small_ref):
    tm = x_ref.shape[0]
    hs = []
    for r0 in range(0, tm, tm // EDGE_BLOCKS):
        rows = slice(r0, r0 + tm // EDGE_BLOCKS)
        hs.append(_rms(x_ref[rows, :], g_ref[...]).astype(BF16))
        qk_ref[rows, :] = _dot(hs[-1], w_ref[:, 0:OFF_VR])
    h = jnp.concatenate(hs, axis=0)
    vr_ref[...] = _dot(h, w_ref[:, OFF_VR:OFF_SU]).astype(BF16)
    su_ref[...] = _dot(h, w_ref[:, OFF_SU:OFF_FOX])
    fox_ref[:, 0:FOX_W] = (_dot(h, w_ref[:, OFF_FOX:OFF_FOX + FOX_W]) * (FOX_DH ** -0.5 * LOG2E)).astype(BF16)
    fox_ref[:, FOX_W:IN_FOX] = _dot(h, w_ref[:, OFF_FOX + FOX_W:OFF_SMALL]).astype(BF16)
    small_ref[...] = _dot(h, ws_ref[...])


def _inproj(layer, x, g, w, w_small, *, bsz, seq, tm):
    m, d = x.shape
    nl = seq // tm
    row = lambda b, l: (b * nl + l, 0)
    return pl.pallas_call(
        _inproj_kernel,
        grid=(bsz, nl),
        in_specs=[
            pl.BlockSpec((tm, d), row),
            _layer_spec(layer, (1, d)),
            _layer_spec(layer, (d, OFF_SMALL), pipeline_mode=RESIDENT),
            _layer_spec(layer, (d, IN_SMALL), pipeline_mode=RESIDENT),
        ],
        out_specs=[
            pl.BlockSpec((tm, IN_QK), row),
            pl.BlockSpec((tm, IN_VR), row),
            pl.BlockSpec((tm, IN_SU), lambda b, l: (l, b)),
            pl.BlockSpec((tm, IN_FOX), row),
            pl.BlockSpec((tm, IN_SMALL), row),
        ],
        out_shape=[
            jax.ShapeDtypeStruct((m, IN_QK), F32),
            jax.ShapeDtypeStruct((m, IN_VR), BF16),
            jax.ShapeDtypeStruct((seq, bsz * IN_SU), F32),
            jax.ShapeDtypeStruct((m, IN_FOX), BF16),
            jax.ShapeDtypeStruct((m, IN_SMALL), F32),
        ],
        compiler_params=_params("parallel", "parallel"),
        name="inproj",
    )(x, g, w, w_small)


def _gla_kernel(qk_ref, vr_ref, small_ref, wg_ref, bg_ref, ng_ref, o_ref, la_sc):
    seq = qk_ref.shape[0]
    n_chunks = seq // CHUNK
    c, hq, hv = CHUNK, GLA_QK, GLA_V

    z = _dot(small_ref[...].astype(BF16), wg_ref[...]) + bg_ref[...]
    la_sc[...] = _log_sigmoid(z) * (1.0 / GLA_TAU)

    row = lax.broadcasted_iota(jnp.int32, (c, c), 0)
    col = lax.broadcasted_iota(jnp.int32, (c, c), 1)
    tri = (col <= row).astype(BF16)
    row_x = lax.broadcasted_iota(jnp.int32, (GLA_HEADS * c, c), 0)
    col_x = lax.broadcasted_iota(jnp.int32, (GLA_HEADS * c, c), 1)
    lower = col_x <= (row_x & (c - 1))
    lane_q = lax.broadcasted_iota(jnp.int32, (c, hq), 1)
    lane_v = lax.broadcasted_iota(jnp.int32, (c, hv), 1)
    q_head = [(lane_q >> DK_SHIFT) == h for h in range(GLA_HEADS)]
    v_head = [(lane_v >> DV_SHIFT) == h for h in range(GLA_HEADS)]
    st_row = lax.broadcasted_iota(jnp.int32, (hv, hq), 0)
    st_col = lax.broadcasted_iota(jnp.int32, (hv, hq), 1)
    same_head = (st_row >> DV_SHIFT) == (st_col >> DK_SHIFT)

    def expand(t):
        return jnp.concatenate([jnp.where(q_head[h], t, 0.0) for h in range(GLA_HEADS)], axis=0).astype(BF16)

    gr = lax.broadcasted_iota(jnp.int32, (hv, hv), 0) >> DV_SHIFT
    gc = lax.broadcasted_iota(jnp.int32, (hv, hv), 1) >> DV_SHIFT
    avg = jnp.where(gr == gc, 1.0 / GLA_DV, 0.0).astype(BF16)

    def within_chunks(group):
        stage1 = []
        for i in group:
            hi, mid, lo = _split3(la_sc[i * c:(i + 1) * c, :])
            stage1.append(_dot(tri, hi) + _dot(tri, mid) + _dot(tri, lo))
        stage2 = []
        for i, g in zip(group, stage1):
            rows = slice(i * c, (i + 1) * c)
            g_last = g[c - 1:c, :]
            eg, ieg = jnp.exp(g), jnp.exp(-g)
            qc = qk_ref[rows, 0:hq] * (GLA_DK ** -0.5)
            kc = qk_ref[rows, hq:2 * hq]
            v = vr_ref[rows, 0:hv]
            q_fwd = qc * eg
            a_fwd = _dot_nt(expand(q_fwd), (kc * ieg).astype(BF16))
            a_bwd = _dot_nt(expand(qc * ieg), (kc * eg).astype(BF16))
            kw = (kc * jnp.exp(g_last - g)).astype(BF16)
            ds_t = _dot(v.astype(F32).T.astype(BF16), kw)
            stage2.append((a_fwd, a_bwd, v, q_fwd.astype(BF16), jnp.where(same_head, ds_t, 0.0), jnp.exp(g_last)))
        results = []
        for a_fwd, a_bwd, v, q_fwd, ds_t, decay in stage2:
            attn = jnp.where(lower, a_fwd, a_bwd).astype(BF16)
            p = _dot(attn, v)
            o = jnp.where(v_head[0], p[0:c, :], 0.0)
            for h in range(1, GLA_HEADS):
                o = o + jnp.where(v_head[h], p[h * c:(h + 1) * c, :], 0.0)
            results.append((o, q_fwd, ds_t, decay))
        return results

    def finish(group, results, s_t):
        outs = []
        for o, q_fwd, ds_t, decay in results:
            outs.append(o + _dot_nt(q_fwd, s_t.astype(BF16)))
            s_t = s_t * decay + ds_t
        o = jnp.concatenate(outs, axis=0)
        rows = slice(group[0] * c, (group[-1] + 1) * c)
        hi, mid, lo = _split3(o * o)
        ms = _dot(hi, avg) + _dot(mid, avg) + _dot(lo, avg)
        r = vr_ref[rows, hv:2 * hv].astype(F32)
        o_ref[rows, :] = (o * lax.rsqrt(ms + EPS) * ng_ref[...] * (r * jax.nn.sigmoid(r))).astype(BF16)
        return s_t

    groups = [range(first, min(first + GLA_GROUP, n_chunks)) for first in range(0, n_chunks, GLA_GROUP)]
    s_t = jnp.zeros((hv, hq), F32)
    pending = None
    for group in groups:
        results = within_chunks(group)
        if pending is not None:
            s_t = finish(*pending, s_t)
        pending = (group, results)
    finish(*pending, s_t)


def _gla(layer, qk, vr, small, wg, bg, ng, *, bsz, seq):
    m = qk.shape[0]
    row = lambda b: (b, 0)
    return pl.pallas_call(
        _gla_kernel,
        grid=(bsz,),
        in_specs=[
            pl.BlockSpec((seq, IN_QK), row),
            pl.BlockSpec((seq, IN_VR), row),
            pl.BlockSpec((seq, IN_SMALL), row),
            _layer_spec(layer, (IN_SMALL, GLA_QK)),
            _layer_spec(layer, (1, GLA_QK)),
            _layer_spec(layer, (1, GLA_V)),
        ],
        out_specs=pl.BlockSpec((seq, GLA_V), row),
        out_shape=jax.ShapeDtypeStruct((m, GLA_V), BF16),
        scratch_shapes=[pltpu.VMEM((seq, GLA_QK), F32)],
        compiler_params=_params("parallel"),
        name="gla",
    )(qk, vr, small, wg, bg, ng)


def _foxprep_kernel(small_ref, b_ref, o_ref, *, bsz):
    seq = small_ref.shape[0] // bsz
    ff = jnp.concatenate([small_ref[b * seq:(b + 1) * seq, :].T[GLA_RANK:GLA_RANK + FOX_HEADS, :]
                          for b in range(bsz)], axis=0)
    x = _log_sigmoid(ff + jnp.concatenate([b_ref[...]] * bsz, axis=0))
    lane = lax.broadcasted_iota(jnp.int32, x.shape, 1)
    shift = 1
    while shift < seq:
        x = x + jnp.where(lane >= shift, pltpu.roll(x, shift, axis=1), 0.0)
        shift *= 2
    hi, mid, lo = (t.astype(F32) for t in _split3(x * LOG2E))
    ones = jnp.ones((FOX_HEADS, seq), F32)
    pad = jnp.zeros((LANES - FOX_TERMS * FOX_HEADS, seq), F32)
    for b in range(bsz):
        mine = slice(b * FOX_HEADS, (b + 1) * FOX_HEADS)
        terms = jnp.concatenate([hi[mine], mid[mine], lo[mine], ones, pad], axis=0)
        o_ref[b * seq:(b + 1) * seq, :] = terms.T.astype(BF16)


def _foxprep(layer, small, bias, *, bsz, seq):
    whole = lambda i: (0, 0)
    return pl.pallas_call(
        functools.partial(_foxprep_kernel, bsz=bsz),
        grid=(1,),
        in_specs=[
            pl.BlockSpec((bsz * seq, IN_SMALL), whole),
            _layer_spec(layer, (FOX_HEADS, 1)),
        ],
        out_specs=pl.BlockSpec((bsz * seq, LANES), whole),
        out_shape=jax.ShapeDtypeStruct((bsz * seq, LANES), BF16),
        compiler_params=_params("arbitrary"),
        name="foxprep",
    )(small, bias)


def _fox_placement(head, base, *, key):
    r = lax.broadcasted_iota(jnp.int32, (LANES, LANES), 0)
    rel = lax.broadcasted_iota(jnp.int32, (LANES, LANES), 1) - base
    term = r >> HEADS_SHIFT
    mine = ((r & (FOX_HEADS - 1)) == head) & (term < FOX_TERMS)
    is_part = mine & (term < FOX_TERMS - 1)
    is_one = mine & (term == FOX_TERMS - 1)
    n = FOX_TERMS - 1
    if key:
        m = jnp.where(is_part & (rel == term + n), -1.0, jnp.where(is_one & (rel >= 0) & (rel < n), 1.0, 0.0))
    else:
        m = jnp.where(is_part & (rel == term), 1.0, jnp.where(is_one & (rel >= n) & (rel < 2 * n), 1.0, 0.0))
    return m.astype(BF16)


def _fox_kernel(q_ref, k_ref, v_ref, f_ref, o_ref, qx_sc, kx_sc, vx_sc, *, tk):
    seq = k_ref.shape[0]
    spare = (FOX_DH, 0)
    lane = lax.broadcasted_iota(jnp.int32, (seq, LANES), 1)
    q, k, v, f = q_ref[...], k_ref[...], v_ref[...], f_ref[...]
    for h in range(2):
        own = (lane < FOX_DH) if h == 0 else (lane >= FOX_DH)
        head = 2 * pl.program_id(1) + h
        qx_sc[h] = jnp.where(own, q, _dot(f, _fox_placement(head, spare[h], key=False)).astype(BF16))
        kx_sc[h] = jnp.where(own, k, _dot(f, _fox_placement(head, spare[h], key=True)).astype(BF16))
        vx_sc[h] = jnp.where(own, v, jnp.ones_like(v))

    row = lax.broadcasted_iota(jnp.int32, (tk, tk), 0)
    col = lax.broadcasted_iota(jnp.int32, (tk, tk), 1)
    causal = col <= row

    items = [(h, r0, c0) for c0 in range(0, seq, tk) for h in range(2) for r0 in range(c0, seq, tk)]

    def logits(item):
        h, r0, c0 = item
        s = _dot_nt(qx_sc[h, r0:r0 + tk, :], kx_sc[h, c0:c0 + tk, :])
        return jnp.where(causal, s, -jnp.inf) if r0 == c0 else s

    state = {}
    s_next = logits(items[0])
    for n, (h, r0, c0) in enumerate(items):
        s = s_next
        if n + 1 < len(items):
            s_next = logits(items[n + 1])
        vx = vx_sc[h, c0:c0 + tk, :]
        m_new = jnp.max(s, axis=1, keepdims=True)
        if c0 == 0:
            acc = _dot(jnp.exp2(s - m_new).astype(BF16), vx)
        else:
            m_old, acc_old = state[h, r0]
            m_new = jnp.maximum(m_old, m_new)
            acc = jnp.exp2(m_old - m_new) * acc_old + _dot(jnp.exp2(s - m_new).astype(BF16), vx)
        state[h, r0] = (m_new, acc)

    first = lax.broadcasted_iota(jnp.int32, (tk, LANES), 1) < FOX_DH
    for r0 in range(0, seq, tk):
        a0, a1 = state[0, r0][1], state[1, r0][1]
        o_ref[r0:r0 + tk, :] = jnp.where(first, a0 / pltpu.roll(a0, FOX_DH, axis=1),
                                         a1 / pltpu.roll(a1, FOX_DH, axis=1)).astype(BF16)


def _fox(fox, f, *, bsz, seq, tk):
    m = fox.shape[0]
    return pl.pallas_call(
        functools.partial(_fox_kernel, tk=tk),
        grid=(bsz, FOX_PAIRS),
        in_specs=[
            pl.BlockSpec((seq, LANES), lambda b, p: (b, p)),
            pl.BlockSpec((seq, LANES), lambda b, p: (b, FOX_PAIRS + p)),
            pl.BlockSpec((seq, LANES), lambda b, p: (b, 2 * FOX_PAIRS + p)),
            pl.BlockSpec((seq, LANES), lambda b, p: (b, 0)),
        ],
        out_specs=pl.BlockSpec((seq, LANES), lambda b, p: (b, p)),
        out_shape=jax.ShapeDtypeStruct((m, FOX_W), BF16),
        scratch_shapes=[pltpu.VMEM((2, seq, LANES), BF16)] * 3,
        compiler_params=_params("parallel", "parallel"),
        name="fox",
    )(fox, fox, fox, f)


def _s5prep_kernel(are_ref, aim_ref, ldt_ref, bre_ref, bim_ref, abre_ref, abim_ref, bbre_ref, bbim_ref):
    lam_re = jnp.minimum(are_ref[...], -1e-4)
    lam_im = aim_ref[...]
    dt = jnp.exp(ldt_ref[...])
    mag = jnp.exp(lam_re * dt)
    ab_re = mag * jnp.cos(lam_im * dt)
    ab_im = mag * jnp.sin(lam_im * dt)
    den = lam_re * lam_re + lam_im * lam_im
    z_re = ((ab_re - 1.0) * lam_re + ab_im * lam_im) / den
    z_im = (ab_im * lam_re - (ab_re - 1.0) * lam_im) / den
    br, bi = bre_ref[...], bim_ref[...]
    abre_ref[...] = ab_re
    abim_ref[...] = ab_im
    bbre_ref[...] = z_re * br - z_im * bi
    bbim_ref[...] = z_re * bi + z_im * br


def _s5prep(a_re, a_im, log_dt, b_re, b_im):
    depth = a_re.shape[0]
    col = lambda a: a.reshape(depth, S5_NSTATE, 1)
    ldt = jnp.broadcast_to(log_dt[:, :, None], (depth, S5_GROUPS, S5_STATE))
    vec = pl.BlockSpec((None, S5_NSTATE, 1), lambda l: (l, 0, 0))
    mat = pl.BlockSpec((None, S5_NSTATE, S5_CH), lambda l: (l, 0, 0))
    return pl.pallas_call(
        _s5prep_kernel,
        grid=(depth,),
        in_specs=[vec, vec, vec, mat, mat],
        out_specs=[vec, vec, mat, mat],
        out_shape=[jax.ShapeDtypeStruct((depth, S5_NSTATE, 1), F32)] * 2
        + [jax.ShapeDtypeStruct((depth, S5_NSTATE, S5_CH), F32)] * 2,
        compiler_params=_params("parallel"),
        name="s5prep",
    )(col(a_re), col(a_im), col(ldt), b_re.reshape(depth, S5_NSTATE, S5_CH), b_im.reshape(depth, S5_NSTATE, S5_CH))


def _s5_kernel(u_ref, abre_ref, abim_ref, bre_ref, bim_ref, cre_ref, cim_ref, d_ref, gw_ref, gb_ref, o_ref,
               sre_sc, sim_sc, *, bsz, ts):
    @pl.when(pl.program_id(0) == 0)
    def _():
        sre_sc[...] = jnp.zeros_like(sre_sc)
        sim_sc[...] = jnp.zeros_like(sim_sc)

    rows = ts * bsz
    n_sub = u_ref.shape[0] // rows
    a_re = jnp.broadcast_to(abre_ref[...], (bsz, S5_NSTATE))
    a_im = jnp.broadcast_to(abim_ref[...], (bsz, S5_NSTATE))

    bu = []
    for k in range(n_sub):
        ub = u_ref[k * rows:(k + 1) * rows, :].astype(BF16)
        bu.append((_dot(ub, bre_ref[...]), _dot(ub, bim_ref[...])))

    s_re, s_im = sre_sc[...], sim_sc[...]
    for k in range(n_sub):
        bu_re, bu_im = bu[k]
        x_re, x_im = [], []
        for t in range(ts):
            r = slice(t * bsz, (t + 1) * bsz)
            s_re, s_im = (a_re * s_re - a_im * s_im + bu_re[r, :], a_re * s_im + a_im * s_re + bu_im[r, :])
            x_re.append(s_re)
            x_im.append(s_im)
        x_re = jnp.concatenate(x_re, axis=0).astype(BF16)
        x_im = jnp.concatenate(x_im, axis=0).astype(BF16)
        u = u_ref[k * rows:(k + 1) * rows, :]
        y = _dot(x_re, cre_ref[...]) - _dot(x_im, cim_ref[...])
        y = jax.nn.gelu(y + d_ref[...] * u, approximate=True)
        o_ref[k * rows:(k + 1) * rows, :] = (
            y * jax.nn.sigmoid(_dot(y.astype(BF16), gw_ref[...]) + gb_ref[...])).astype(BF16)
    sre_sc[...] = s_re
    sim_sc[...] = s_im


def _s5(layer, u, ab_re, ab_im, b_re, b_im, c_re, c_im, d, glu_w, glu_b, *, bsz, seq, tb, ts):
    rows = tb * bsz
    return pl.pallas_call(
        functools.partial(_s5_kernel, bsz=bsz, ts=ts),
        grid=(seq // tb,),
        in_specs=[
            pl.BlockSpec((rows, S5_WIDTH), lambda i: (i, 0)),
            _layer_spec(layer, (1, S5_NSTATE)),
            _layer_spec(layer, (1, S5_NSTATE)),
            _layer_spec(layer, (S5_WIDTH, S5_NSTATE)),
            _layer_spec(layer, (S5_WIDTH, S5_NSTATE)),
            _layer_spec(layer, (S5_NSTATE, S5_WIDTH)),
            _layer_spec(layer, (S5_NSTATE, S5_WIDTH)),
            _layer_spec(layer, (1, S5_WIDTH)),
            _layer_spec(layer, (S5_WIDTH, S5_WIDTH)),
            _layer_spec(layer, (1, S5_WIDTH)),
        ],
        out_specs=pl.BlockSpec((rows, S5_WIDTH), lambda i: (i, 0)),
        out_shape=jax.ShapeDtypeStruct((seq * bsz, S5_WIDTH), BF16),
        scratch_shapes=[pltpu.VMEM((bsz, S5_NSTATE), F32), pltpu.VMEM((bsz, S5_NSTATE), F32)],
        compiler_params=_params("arbitrary"),
        name="s5",
    )(u, ab_re, ab_im, b_re, b_im, c_re, c_im, d, glu_w, glu_b)


def _merge_kernel(x_ref, gla_ref, s5_ref, fox_ref, pre_g_ref, wgate_ref, wgla_ref, ws5_ref, wfox_ref, wo_ref,
                  post_g_ref, o_ref):
    tm, d = x_ref.shape
    blocks = [slice(r0, r0 + tm // EDGE_BLOCKS) for r0 in range(0, tm, tm // EDGE_BLOCKS)]
    branches = ((gla_ref, wgla_ref), (s5_ref, ws5_ref), (fox_ref, wfox_ref))

    hs, first = [], []
    for rows in blocks:
        hs.append(_rms(x_ref[rows, :], pre_g_ref[...]).astype(BF16))
        first.append(_dot(hs[-1], wgate_ref[:, 0:d]))
    h = jnp.concatenate(hs, axis=0)
    logits = jnp.concatenate(first, axis=0)
    mix = None
    for n, (b_ref, w_ref) in enumerate(branches[:-1]):
        term = jax.nn.sigmoid(logits) * _dot(b_ref[...], w_ref[...])
        mix = term if mix is None else mix + term
        logits = _dot(h, wgate_ref[:, (n + 1) * d:(n + 2) * d])
    up = _dot(branches[-1][0][...], branches[-1][1][...])

    for rows in blocks:
        mixed = mix[rows, :] + jax.nn.sigmoid(logits[rows, :]) * up[rows, :]
        y = _dot(mixed.astype(BF16), wo_ref[...])
        o_ref[rows, :] = x_ref[rows, :] + _rms(y, post_g_ref[...])


def _merge(layer, x, gla_o, s5_o, fox_o, pre_g, w_gate, w_gla, w_s5, w_fox, w_o, post_g, *, bsz, seq, tm):
    m, d = x.shape
    nl = seq // tm
    row = lambda b, l: (b * nl + l, 0)
    return pl.pallas_call(
        _merge_kernel,
        grid=(bsz, nl),
        in_specs=[
            pl.BlockSpec((tm, d), row),
            pl.BlockSpec((tm, GLA_V), row),
            pl.BlockSpec((tm, S5_WIDTH), lambda b, l: (l, b)),
            pl.BlockSpec((tm, FOX_W), row),
            _layer_spec(layer, (1, d)),
            _layer_spec(layer, (d, N_BRANCH * d), pipeline_mode=RESIDENT),
            _layer_spec(layer, (GLA_V, d), pipeline_mode=RESIDENT),
            _layer_spec(layer, (S5_WIDTH, d), pipeline_mode=RESIDENT),
            _layer_spec(layer, (FOX_W, d), pipeline_mode=RESIDENT),
            _layer_spec(layer, (d, d), pipeline_mode=RESIDENT),
            _layer_spec(layer, (1, d)),
        ],
        out_specs=pl.BlockSpec((tm, d), row),
        out_shape=jax.ShapeDtypeStruct((m, d), F32),
        compiler_params=_params("parallel", "parallel"),
        name="merge",
    )(x, gla_o, s5_o, fox_o, pre_g, w_gate, w_gla, w_s5, w_fox, w_o, post_g)


def _memkv_kernel(mem_ref, g_ref, w_ref, k_ref, v_ref):
    d = mem_ref.shape[1]
    h = _rms(mem_ref[...], g_ref[...]).astype(BF16)
    k_ref[...] = _dot(h, w_ref[:, 0:d]).astype(BF16)
    v_ref[...] = _dot(h, w_ref[:, d:2 * d]).astype(BF16)


def _memkv(mem, g, w_kv, *, tm):
    m, d = mem.shape
    depth = w_kv.shape[0]
    per_layer = lambda block: pl.BlockSpec((None,) + block, lambda l, i: (l, 0, 0))
    out = pl.BlockSpec((None, tm, d), lambda l, i: (l, i, 0))
    return pl.pallas_call(
        _memkv_kernel,
        grid=(depth, m // tm),
        in_specs=[pl.BlockSpec((tm, d), lambda l, i: (i, 0)), per_layer((1, d)), per_layer((d, 2 * d))],
        out_specs=[out, out],
        out_shape=[jax.ShapeDtypeStruct((depth, m, d), BF16)] * 2,
        compiler_params=_params("parallel", "parallel"),
        name="memkv",
    )(mem, g, w_kv)


def _xattn_kernel(x_ref, k_ref, v_ref, pre_g_ref, wq_ref, wo_ref, post_g_ref, o_ref):
    tm, d = x_ref.shape
    dh = d // XA_HEADS
    blocks = [slice(r0, r0 + tm // EDGE_BLOCKS) for r0 in range(0, tm, tm // EDGE_BLOCKS)]
    cols = [slice(n * dh, (n + 1) * dh) for n in range(XA_HEADS)]

    def softmax(s):
        e = jnp.exp(s - jnp.max(s, axis=1, keepdims=True))
        return (e / jnp.sum(e, axis=1, keepdims=True)).astype(BF16)

    qs = []
    for rows in blocks:
        h = _rms(x_ref[rows, :], pre_g_ref[...]).astype(BF16)
        qs.append((_dot(h, wq_ref[...]) * (dh ** -0.5)).astype(BF16))
    logits = [[_dot_nt(q[:, c], k_ref[:, c]) for c in cols] for q in qs]
    probs = [[softmax(s) for s in per_head] for per_head in logits]
    heads = [jnp.concatenate([_dot(p, v_ref[:, c]).astype(BF16) for p, c in zip(per_head, cols)], axis=1)
             for per_head in probs]
    for rows, attended in zip(blocks, heads):
        y = _dot(attended, wo_ref[...])
        o_ref[rows, :] = x_ref[rows, :] + _rms(y, post_g_ref[...])


def _xattn(layer, x, k, v, pre_g, w_q, w_o, post_g, *, bsz, seq, n_mem, tm):
    m, d = x.shape
    nl = seq // tm
    row = lambda b, l: (b * nl + l, 0)
    return pl.pallas_call(
        _xattn_kernel,
        grid=(bsz, nl),
        in_specs=[
            pl.BlockSpec((tm, d), row),
            _layer_spec(layer, (n_mem, d), lambda b, l: (b, 0)),
            _layer_spec(layer, (n_mem, d), lambda b, l: (b, 0)),
            _layer_spec(layer, (1, d)),
            _layer_spec(layer, (d, d), pipeline_mode=RESIDENT),
            _layer_spec(layer, (d, d), pipeline_mode=RESIDENT),
            _layer_spec(layer, (1, d)),
        ],
        out_specs=pl.BlockSpec((tm, d), row),
        out_shape=jax.ShapeDtypeStruct((m, d), F32),
        compiler_params=_params("parallel", "parallel"),
        name="xattn",
    )(x, k, v, pre_g, w_q, w_o, post_g)


def _tile(n, want):
    t = min(n, want)
    while n % t:
        t -= 1
    return t


def _block_diag(t):
    depth, g, r, c = t.shape
    eye = jnp.eye(g, dtype=t.dtype)
    return jnp.einsum("lgrc,gk->lgrkc", t, eye).reshape(depth, g * r, g * c)


def kernel(x, mem, ffn1_pre_g, ffn1_w_gu, ffn1_w_down, ffn1_post_g, mix_pre_g, w_in, gla_gate_w, gla_gate_b, gla_norm_g, w_gla_up, s5_a_re, s5_a_im, s5_log_dt, s5_b_re, s5_b_im, s5_c_re, s5_c_im, s5_d, s5_glu_w, s5_glu_b, w_s5_up, fox_f_b, w_fox_up, w_mix_out, mix_post_g, xa_pre_g, xa_mem_g, xa_w_q, xa_w_kv, xa_w_o, xa_post_g, ffn2_pre_g, ffn2_w_gu, ffn2_w_down, ffn2_post_g):
    bsz, seq, d = x.shape
    n_mem = mem.shape[1]
    depth = w_in.shape[0]
    d_ff = ffn1_w_down.shape[1]
    m = bsz * seq
    assert seq % CHUNK == 0 and d % LANES == 0

    tm_ffn = _tile(m, 1024)
    tf = _tile(d_ff // LANES, 2) * LANES
    tm = _tile(seq, 1024)
    tk = _tile(seq, 512)
    ts = _tile(seq, 128)
    tb = 2 * ts if seq % (2 * ts) == 0 else ts

    bf = lambda a: a.astype(BF16)
    vec = lambda a: a.reshape(depth, 1, a.shape[-1])

    aligned = 2 * GLA_QK + 2 * GLA_V
    mid = S5_WIDTH + 3 * FOX_W
    assert aligned % LANES == 0 and mid % LANES == 0 and aligned + mid == OFF_SMALL
    gates_at = aligned + GLA_RANK + mid + FOX_HEADS
    width = REPACK_GROUP * LANES
    n_groups = w_in.shape[2] // width
    body = w_in[:, :, width:n_groups * width].reshape(depth, d, n_groups - 1, width)[..., :REPACK_EDGE]
    tail = w_in[:, :, n_groups * width:n_groups * width + REPACK_EDGE]
    tail = jnp.pad(tail, ((0, 0), (0, 0), (0, REPACK_EDGE - tail.shape[2])))
    edges = jnp.concatenate([body, tail[:, :, None, :]], axis=2)
    edges = jnp.pad(edges, ((0, 0), (0, 0), (0, 0), (0, LANES - REPACK_EDGE))).reshape(depth, d, n_groups * LANES)
    w_main = _repack(w_in, edges, base=0, n_blocks=OFF_SMALL // LANES, aligned=aligned // LANES, shift=GLA_RANK)
    w_gate = _repack(w_in, edges, base=gates_at // LANES, n_blocks=N_BRANCH * d // LANES, aligned=0,
                     shift=gates_at % LANES)
    forget_at = gates_at - FOX_HEADS
    assert forget_at % width == GLA_RANK and GLA_RANK + FOX_HEADS <= REPACK_EDGE and IN_SMALL == LANES
    w_small = _smallpack(w_in, edges, down_block=aligned // LANES, forget_group=forget_at // width)
    gla_wg = bf(jnp.concatenate([gla_gate_w, jnp.zeros((depth, IN_SMALL - GLA_RANK, GLA_QK), gla_gate_w.dtype)],
                                axis=1))

    ab_re, ab_im, bb_re, bb_im = _s5prep(s5_a_re, s5_a_im, s5_log_dt, s5_b_re, s5_b_im)
    grp = lambda t: t.reshape(depth, S5_GROUPS, S5_STATE, S5_CH)
    s5_bre = bf(_block_diag(grp(bb_re).transpose(0, 1, 3, 2)))
    s5_bim = bf(_block_diag(grp(bb_im).transpose(0, 1, 3, 2)))
    s5_cre = bf(_block_diag(s5_c_re.transpose(0, 1, 3, 2)))
    s5_cim = bf(_block_diag(s5_c_im.transpose(0, 1, 3, 2)))
    ab_re = ab_re.reshape(depth, 1, S5_NSTATE)
    ab_im = ab_im.reshape(depth, 1, S5_NSTATE)

    ffn1_gu, ffn1_down, ffn2_gu, ffn2_down = bf(ffn1_w_gu), bf(ffn1_w_down), bf(ffn2_w_gu), bf(ffn2_w_down)
    gla_up, s5_up, fox_up, mix_out = bf(w_gla_up), bf(w_s5_up), bf(w_fox_up), bf(w_mix_out)
    glu_w, xa_q, xa_kv, xa_o = bf(s5_glu_w), bf(xa_w_q), bf(xa_w_kv), bf(xa_w_o)
    fox_b = fox_f_b.reshape(depth, FOX_HEADS, 1)

    xs = x.reshape(m, d)
    mk, mv = _memkv(mem.reshape(bsz * n_mem, d), vec(xa_mem_g), xa_kv, tm=_tile(bsz * n_mem, 512))
    for l in range(depth):
        xs = _ffn(l, xs, vec(ffn1_pre_g), ffn1_gu, ffn1_down, vec(ffn1_post_g), tm=tm_ffn, tf=tf)

        qk, vr, su, fox, small = _inproj(l, xs, vec(mix_pre_g), w_main, w_small, bsz=bsz, seq=seq, tm=tm)
        gla_o = _gla(l, qk, vr, small, gla_wg, vec(gla_gate_b), vec(gla_norm_g), bsz=bsz, seq=seq)
        f_cols = _foxprep(l, small, fox_b, bsz=bsz, seq=seq)
        fox_o = _fox(fox, f_cols, bsz=bsz, seq=seq, tk=tk)
        s5_o = _s5(l, su.reshape(seq * bsz, S5_WIDTH), ab_re, ab_im, s5_bre, s5_bim, s5_cre, s5_cim,
                   vec(s5_d), glu_w, vec(s5_glu_b), bsz=bsz, seq=seq, tb=tb, ts=ts)
        xs = _merge(l, xs, gla_o, s5_o.reshape(seq, bsz * S5_WIDTH), fox_o, vec(mix_pre_g), w_gate, gla_up,
                    s5_up, fox_up, mix_out, vec(mix_post_g), bsz=bsz, seq=seq, tm=tm)

        xs = _xattn(l, xs, mk, mv, vec(xa_pre_g), xa_q, xa_o, vec(xa_post_g), bsz=bsz, seq=seq, n_mem=n_mem, tm=tm)

        xs = _ffn(l, xs, vec(ffn2_pre_g), ffn2_gu, ffn2_down, vec(ffn2_post_g), tm=tm_ffn, tf=tf)
    return xs.reshape(bsz, seq, d)
```

```python
import functools
import math

import jax
import jax.numpy as jnp
from jax import lax
from jax.experimental import pallas as pl
from jax.experimental.pallas import tpu as pltpu

F32 = jnp.float32
BF16 = jnp.bfloat16
EPS = 1e-6
LOG2E = math.log2(math.e)

LANES = 128
SUBLANES = 8
VMEM_LIMIT_BYTES = 56 * 1024 * 1024

EDGE_BLOCKS = 4
CHUNK = 64
GLA_GROUP = 8
GLA_HEADS, GLA_DK, GLA_DV, GLA_RANK, GLA_TAU = 4, 32, 64, 16, 16.0
GLA_QK = GLA_HEADS * GLA_DK
GLA_V = GLA_HEADS * GLA_DV
DK_SHIFT = GLA_DK.bit_length() - 1
DV_SHIFT = GLA_DV.bit_length() - 1
S5_GROUPS, S5_CH, S5_STATE = 16, 16, 64
S5_WIDTH = S5_GROUPS * S5_CH
S5_NSTATE = S5_GROUPS * S5_STATE
FOX_HEADS, FOX_DH = 8, 64
FOX_W = FOX_HEADS * FOX_DH
FOX_PAIRS = FOX_W // LANES
FOX_TERMS = 4
HEADS_SHIFT = FOX_HEADS.bit_length() - 1
XA_HEADS = 4
N_BRANCH = 3

NT_DIMS = (((1,), (1,)), ((), ()))
RESIDENT = pl.Buffered(1)


def _params(*semantics):
    return pltpu.CompilerParams(dimension_semantics=semantics, vmem_limit_bytes=VMEM_LIMIT_BYTES)


def _layer_spec(layer, block, index_map=None, pipeline_mode=None):
    block = tuple(block)
    if index_map is None:
        index_map = lambda *_: (0,) * len(block)
    return pl.BlockSpec((None,) + block, lambda *ids: (layer,) + tuple(index_map(*ids)),
                        pipeline_mode=pipeline_mode)


def _rms(x, g):
    return x * lax.rsqrt(jnp.mean(x * x, axis=-1, keepdims=True) + EPS) * g


def _dot(a, b):
    return jnp.dot(a, b, preferred_element_type=F32)


def _dot_nt(a, b):
    return lax.dot_general(a, b, NT_DIMS, preferred_element_type=F32)


def _log_sigmoid(x):
    return jnp.minimum(x, 0.0) - jnp.log1p(jnp.exp(-jnp.abs(x)))


def _split3(x):
    hi = x.astype(BF16)
    r1 = x - hi.astype(F32)
    mid = r1.astype(BF16)
    lo = (r1 - mid.astype(F32)).astype(BF16)
    return hi, mid, lo


def _cast_specs(layer, stacked, n_steps, step):
    _, rows, cols = stacked.shape
    slab = rows // n_steps
    assert slab * n_steps == rows and slab % (2 * SUBLANES) == 0
    return (pl.BlockSpec((None, slab, cols), lambda *ids: (layer, step(*ids), 0)),
            pl.BlockSpec((slab, cols), lambda *ids: (step(*ids), 0)),
            jax.ShapeDtypeStruct((rows, cols), BF16))


def _cast_slabs(refs):
    half = len(refs) // 2
    for src, dst in zip(refs[:half], refs[half:]):
        dst[...] = src[...].astype(BF16)


def _ffn_kernel(x_ref, pre_g_ref, wgu_ref, wd_ref, post_g_ref, *rest, tf):
    o_ref, cast_refs = rest[len(rest) // 2], rest[:len(rest) // 2] + rest[len(rest) // 2 + 1:]
    _cast_slabs(cast_refs)
    d_ff = wd_ref.shape[0]
    tm = x_ref.shape[0]
    blocks = [slice(r0, r0 + tm // EDGE_BLOCKS) for r0 in range(0, tm, tm // EDGE_BLOCKS)]

    def gate_up(h, c0):
        return _dot(h, wgu_ref[:, c0:c0 + tf]), _dot(h, wgu_ref[:, d_ff + c0:d_ff + c0 + tf])

    hs, first = [], []
    for rows in blocks:
        hs.append(_rms(x_ref[rows, :], pre_g_ref[...]).astype(BF16))
        first.append(gate_up(hs[-1], 0))
    h = jnp.concatenate(hs, axis=0)
    nxt = tuple(jnp.concatenate(t, axis=0) for t in zip(*first))

    acc = None
    for c0 in range(0, d_ff - tf, tf):
        gate, up = nxt
        nxt = gate_up(h, c0 + tf)
        a = (gate * jax.nn.sigmoid(gate) * up).astype(BF16)
        part = _dot(a, wd_ref[c0:c0 + tf, :])
        acc = part if acc is None else acc + part

    gate, up = nxt
    a = (gate * jax.nn.sigmoid(gate) * up).astype(BF16)
    for rows in blocks:
        total = _dot(a[rows, :], wd_ref[d_ff - tf:d_ff, :])
        if acc is not None:
            total = acc[rows, :] + total
        o_ref[rows, :] = x_ref[rows, :] + 0.5 * _rms(total, post_g_ref[...])


def _ffn(layer, x, pre_g, w_gu, w_down, post_g, *, tm, tf, cast=None):
    m, d = x.shape
    d_ff = w_down.shape[0]
    n_steps = m // tm
    extra = [_cast_specs(cast[0], w, n_steps, lambda i: i) for w in cast[1:]] if cast else []
    outs = pl.pallas_call(
        functools.partial(_ffn_kernel, tf=tf),
        grid=(n_steps,),
        in_specs=[
            pl.BlockSpec((tm, d), lambda i: (i, 0)),
            _layer_spec(layer, (1, d)),
            pl.BlockSpec((d, 2 * d_ff), lambda i: (0, 0), pipeline_mode=RESIDENT),
            pl.BlockSpec((d_ff, d), lambda i: (0, 0), pipeline_mode=RESIDENT),
            _layer_spec(layer, (1, d)),
        ] + [e[0] for e in extra],
        out_specs=[pl.BlockSpec((tm, d), lambda i: (i, 0))] + [e[1] for e in extra],
        out_shape=[jax.ShapeDtypeStruct((m, d), F32)] + [e[2] for e in extra],
        compiler_params=_params("parallel"),
        name="ffn",
    )(x, pre_g, w_gu, w_down, post_g, *(cast[1:] if cast else ()))
    return outs[0], tuple(outs[1:])


IN_QK = 2 * GLA_QK
IN_VR = 2 * GLA_V
IN_SU = S5_WIDTH
IN_FOX = 3 * FOX_W
IN_SMALL = LANES
OFF_VR = IN_QK
OFF_SU = OFF_VR + IN_VR
OFF_FOX = OFF_SU + IN_SU
OFF_SMALL = OFF_FOX + IN_FOX
IN_TOTAL = OFF_SMALL + IN_SMALL


def _inproj_kernel(x_ref, g_ref, w_ref, qk_ref, vr_ref, su_ref, fox_ref, small_ref):
    tm = x_ref.shape[0]
    hs = []
    for r0 in range(0, tm, tm // EDGE_BLOCKS):
        rows = slice(r0, r0 + tm // EDGE_BLOCKS)
        hs.append(_rms(x_ref[rows, :], g_ref[...]).astype(BF16))
        qk_ref[rows, :] = _dot(hs[-1], w_ref[:, 0:OFF_VR])
    h = jnp.concatenate(hs, axis=0)
    vr_ref[...] = _dot(h, w_ref[:, OFF_VR:OFF_SU]).astype(BF16)
    su_ref[...] = _dot(h, w_ref[:, OFF_SU:OFF_FOX])
    fox_ref[:, 0:FOX_W] = (_dot(h, w_ref[:, OFF_FOX:OFF_FOX + FOX_W]) * (FOX_DH ** -0.5 * LOG2E)).astype(BF16)
    fox_ref[:, FOX_W:IN_FOX] = _dot(h, w_ref[:, OFF_FOX + FOX_W:OFF_SMALL]).astype(BF16)
    small_ref[...] = _dot(h, w_ref[:, OFF_SMALL:IN_TOTAL])


def _inproj(layer, x, g, w, *, bsz, seq, tm):
    m, d = x.shape
    nl = seq // tm
    row = lambda b, l: (b * nl + l, 0)
    return pl.pallas_call(
        _inproj_kernel,
        grid=(bsz, nl),
        in_specs=[
            pl.BlockSpec((tm, d), row),
            _layer_spec(layer, (1, d)),
            _layer_spec(layer, (d, IN_TOTAL), pipeline_mode=RESIDENT),
        ],
        out_specs=[
            pl.BlockSpec((tm, IN_QK), row),
            pl.BlockSpec((tm, IN_VR), row),
            pl.BlockSpec((tm, IN_SU), lambda b, l: (l, b)),
            pl.BlockSpec((tm, IN_FOX), row),
            pl.BlockSpec((tm, IN_SMALL), row),
        ],
        out_shape=[
            jax.ShapeDtypeStruct((m, IN_QK), F32),
            jax.ShapeDtypeStruct((m, IN_VR), BF16),
            jax.ShapeDtypeStruct((seq, bsz * IN_SU), F32),
            jax.ShapeDtypeStruct((m, IN_FOX), BF16),
            jax.ShapeDtypeStruct((m, IN_SMALL), F32),
        ],
        compiler_params=_params("parallel", "parallel"),
        name="inproj",
    )(x, g, w)


def _gla_kernel(qk_ref, vr_ref, small_ref, wg_ref, bg_ref, ng_ref, o_ref, la_sc):
    seq = qk_ref.shape[0]
    n_chunks = seq // CHUNK
    c, hq, hv = CHUNK, GLA_QK, GLA_V

    z = _dot(small_ref[...].astype(BF16), wg_ref[...]) + bg_ref[...]
    la_sc[...] = _log_sigmoid(z) * (1.0 / GLA_TAU)

    row = lax.broadcasted_iota(jnp.int32, (c, c), 0)
    col = lax.broadcasted_iota(jnp.int32, (c, c), 1)
    tri = (col <= row).astype(BF16)
    row_x = lax.broadcasted_iota(jnp.int32, (GLA_HEADS * c, c), 0)
    col_x = lax.broadcasted_iota(jnp.int32, (GLA_HEADS * c, c), 1)
    lower = col_x <= (row_x & (c - 1))
    lane_q = lax.broadcasted_iota(jnp.int32, (c, hq), 1)
    lane_v = lax.broadcasted_iota(jnp.int32, (c, hv), 1)
    q_head = [(lane_q >> DK_SHIFT) == h for h in range(GLA_HEADS)]
    v_head = [(lane_v >> DV_SHIFT) == h for h in range(GLA_HEADS)]
    st_row = lax.broadcasted_iota(jnp.int32, (hv, hq), 0)
    st_col = lax.broadcasted_iota(jnp.int32, (hv, hq), 1)
    same_head = (st_row >> DV_SHIFT) == (st_col >> DK_SHIFT)

    def expand(t):
        return jnp.concatenate([jnp.where(q_head[h], t, 0.0) for h in range(GLA_HEADS)], axis=0).astype(BF16)

    gr = lax.broadcasted_iota(jnp.int32, (hv, hv), 0) >> DV_SHIFT
    gc = lax.broadcasted_iota(jnp.int32, (hv, hv), 1) >> DV_SHIFT
    avg = jnp.where(gr == gc, 1.0 / GLA_DV, 0.0).astype(BF16)

    def within_chunks(group):
        stage1 = []
        for i in group:
            hi, mid, lo = _split3(la_sc[i * c:(i + 1) * c, :])
            stage1.append(_dot(tri, hi) + _dot(tri, mid) + _dot(tri, lo))
        stage2 = []
        for i, g in zip(group, stage1):
            rows = slice(i * c, (i + 1) * c)
            g_last = g[c - 1:c, :]
            eg, ieg = jnp.exp(g), jnp.exp(-g)
            qc = qk_ref[rows, 0:hq] * (GLA_DK ** -0.5)
            kc = qk_ref[rows, hq:2 * hq]
            v = vr_ref[rows, 0:hv]
            q_fwd = qc * eg
            a_fwd = _dot_nt(expand(q_fwd), (kc * ieg).astype(BF16))
            a_bwd = _dot_nt(expand(qc * ieg), (kc * eg).astype(BF16))
            kw = (kc * jnp.exp(g_last - g)).astype(BF16)
            ds_t = _dot(v.astype(F32).T.astype(BF16), kw)
            stage2.append((a_fwd, a_bwd, v, q_fwd.astype(BF16), jnp.where(same_head, ds_t, 0.0), jnp.exp(g_last)))
        results = []
        for a_fwd, a_bwd, v, q_fwd, ds_t, decay in stage2:
            attn = jnp.where(lower, a_fwd, a_bwd).astype(BF16)
            p = _dot(attn, v)
            o = jnp.where(v_head[0], p[0:c, :], 0.0)
            for h in range(1, GLA_HEADS):
                o = o + jnp.where(v_head[h], p[h * c:(h + 1) * c, :], 0.0)
            results.append((o, q_fwd, ds_t, decay))
        return results

    def finish(group, results, s_t):
        outs = []
        for o, q_fwd, ds_t, decay in results:
            outs.append(o + _dot_nt(q_fwd, s_t.astype(BF16)))
            s_t = s_t * decay + ds_t
        o = jnp.concatenate(outs, axis=0)
        rows = slice(group[0] * c, (group[-1] + 1) * c)
        hi, mid, lo = _split3(o * o)
        ms = _dot(hi, avg) + _dot(mid, avg) + _dot(lo, avg)
        r = vr_ref[rows, hv:2 * hv].astype(F32)
        o_ref[rows, :] = (o * lax.rsqrt(ms + EPS) * ng_ref[...] * (r * jax.nn.sigmoid(r))).astype(BF16)
        return s_t

    groups = [range(first, min(first + GLA_GROUP, n_chunks)) for first in range(0, n_chunks, GLA_GROUP)]
    s_t = jnp.zeros((hv, hq), F32)
    pending = None
    for group in groups:
        results = within_chunks(group)
        if pending is not None:
            s_t = finish(*pending, s_t)
        pending = (group, results)
    finish(*pending, s_t)


def _gla(layer, qk, vr, small, wg, bg, ng, *, bsz, seq):
    m = qk.shape[0]
    row = lambda b: (b, 0)
    return pl.pallas_call(
        _gla_kernel,
        grid=(bsz,),
        in_specs=[
            pl.BlockSpec((seq, IN_QK), row),
            pl.BlockSpec((seq, IN_VR), row),
            pl.BlockSpec((seq, IN_SMALL), row),
            _layer_spec(layer, (IN_SMALL, GLA_QK)),
            _layer_spec(layer, (1, GLA_QK)),
            _layer_spec(layer, (1, GLA_V)),
        ],
        out_specs=pl.BlockSpec((seq, GLA_V), row),
        out_shape=jax.ShapeDtypeStruct((m, GLA_V), BF16),
        scratch_shapes=[pltpu.VMEM((seq, GLA_QK), F32)],
        compiler_params=_params("parallel"),
        name="gla",
    )(qk, vr, small, wg, bg, ng)


def _foxprep_kernel(small_ref, b_ref, o_ref, *, bsz):
    seq = small_ref.shape[0] // bsz
    ff = jnp.concatenate([small_ref[b * seq:(b + 1) * seq, :].T[GLA_RANK:GLA_RANK + FOX_HEADS, :]
                          for b in range(bsz)], axis=0)
    x = _log_sigmoid(ff + jnp.concatenate([b_ref[...]] * bsz, axis=0))
    lane = lax.broadcasted_iota(jnp.int32, x.shape, 1)
    shift = 1
    while shift < seq:
        x = x + jnp.where(lane >= shift, pltpu.roll(x, shift, axis=1), 0.0)
        shift *= 2
    hi, mid, lo = (t.astype(F32) for t in _split3(x * LOG2E))
    ones = jnp.ones((FOX_HEADS, seq), F32)
    pad = jnp.zeros((LANES - FOX_TERMS * FOX_HEADS, seq), F32)
    for b in range(bsz):
        mine = slice(b * FOX_HEADS, (b + 1) * FOX_HEADS)
        terms = jnp.concatenate([hi[mine], mid[mine], lo[mine], ones, pad], axis=0)
        o_ref[b * seq:(b + 1) * seq, :] = terms.T.astype(BF16)


def _foxprep(layer, small, bias, *, bsz, seq):
    whole = lambda i: (0, 0)
    return pl.pallas_call(
        functools.partial(_foxprep_kernel, bsz=bsz),
        grid=(1,),
        in_specs=[
            pl.BlockSpec((bsz * seq, IN_SMALL), whole),
            _layer_spec(layer, (FOX_HEADS, 1)),
        ],
        out_specs=pl.BlockSpec((bsz * seq, LANES), whole),
        out_shape=jax.ShapeDtypeStruct((bsz * seq, LANES), BF16),
        compiler_params=_params("arbitrary"),
        name="foxprep",
    )(small, bias)


def _fox_placement(head, base, *, key):
    r = lax.broadcasted_iota(jnp.int32, (LANES, LANES), 0)
    rel = lax.broadcasted_iota(jnp.int32, (LANES, LANES), 1) - base
    term = r >> HEADS_SHIFT
    mine = ((r & (FOX_HEADS - 1)) == head) & (term < FOX_TERMS)
    is_part = mine & (term < FOX_TERMS - 1)
    is_one = mine & (term == FOX_TERMS - 1)
    n = FOX_TERMS - 1
    if key:
        m = jnp.where(is_part & (rel == term + n), -1.0, jnp.where(is_one & (rel >= 0) & (rel < n), 1.0, 0.0))
    else:
        m = jnp.where(is_part & (rel == term), 1.0, jnp.where(is_one & (rel >= n) & (rel < 2 * n), 1.0, 0.0))
    return m.astype(BF16)


def _fox_kernel(q_ref, k_ref, v_ref, f_ref, o_ref, qx_sc, kx_sc, vx_sc, *, tk):
    seq = k_ref.shape[0]
    spare = (FOX_DH, 0)
    lane = lax.broadcasted_iota(jnp.int32, (seq, LANES), 1)
    q, k, v, f = q_ref[...], k_ref[...], v_ref[...], f_ref[...]
    for h in range(2):
        own = (lane < FOX_DH) if h == 0 else (lane >= FOX_DH)
        head = 2 * pl.program_id(1) + h
        qx_sc[h] = jnp.where(own, q, _dot(f, _fox_placement(head, spare[h], key=False)).astype(BF16))
        kx_sc[h] = jnp.where(own, k, _dot(f, _fox_placement(head, spare[h], key=True)).astype(BF16))
        vx_sc[h] = jnp.where(own, v, jnp.ones_like(v))

    row = lax.broadcasted_iota(jnp.int32, (tk, tk), 0)
    col = lax.broadcasted_iota(jnp.int32, (tk, tk), 1)
    causal = col <= row

    items = [(h, r0, c0) for c0 in range(0, seq, tk) for h in range(2) for r0 in range(c0, seq, tk)]

    def logits(item):
        h, r0, c0 = item
        s = _dot_nt(qx_sc[h, r0:r0 + tk, :], kx_sc[h, c0:c0 + tk, :])
        return jnp.where(causal, s, -jnp.inf) if r0 == c0 else s

    state = {}
    s_next = logits(items[0])
    for n, (h, r0, c0) in enumerate(items):
        s = s_next
        if n + 1 < len(items):
            s_next = logits(items[n + 1])
        vx = vx_sc[h, c0:c0 + tk, :]
        m_new = jnp.max(s, axis=1, keepdims=True)
        if c0 == 0:
            acc = _dot(jnp.exp2(s - m_new).astype(BF16), vx)
        else:
            m_old, acc_old = state[h, r0]
            m_new = jnp.maximum(m_old, m_new)
            acc = jnp.exp2(m_old - m_new) * acc_old + _dot(jnp.exp2(s - m_new).astype(BF16), vx)
        state[h, r0] = (m_new, acc)

    first = lax.broadcasted_iota(jnp.int32, (tk, LANES), 1) < FOX_DH
    for r0 in range(0, seq, tk):
        a0, a1 = state[0, r0][1], state[1, r0][1]
        o_ref[r0:r0 + tk, :] = jnp.where(first, a0 / pltpu.roll(a0, FOX_DH, axis=1),
                                         a1 / pltpu.roll(a1, FOX_DH, axis=1)).astype(BF16)


def _fox(fox, f, *, bsz, seq, tk):
    m = fox.shape[0]
    return pl.pallas_call(
        functools.partial(_fox_kernel, tk=tk),
        grid=(bsz, FOX_PAIRS),
        in_specs=[
            pl.BlockSpec((seq, LANES), lambda b, p: (b, p)),
            pl.BlockSpec((seq, LANES), lambda b, p: (b, FOX_PAIRS + p)),
            pl.BlockSpec((seq, LANES), lambda b, p: (b, 2 * FOX_PAIRS + p)),
            pl.BlockSpec((seq, LANES), lambda b, p: (b, 0)),
        ],
        out_specs=pl.BlockSpec((seq, LANES), lambda b, p: (b, p)),
        out_shape=jax.ShapeDtypeStruct((m, FOX_W), BF16),
        scratch_shapes=[pltpu.VMEM((2, seq, LANES), BF16)] * 3,
        compiler_params=_params("parallel", "parallel"),
        name="fox",
    )(fox, fox, fox, f)


def _s5prep_kernel(are_ref, aim_ref, ldt_ref, bre_ref, bim_ref, abre_ref, abim_ref, bbre_ref, bbim_ref):
    lam_re = jnp.minimum(are_ref[...], -1e-4)
    lam_im = aim_ref[...]
    dt = jnp.exp(ldt_ref[...])
    mag = jnp.exp(lam_re * dt)
    ab_re = mag * jnp.cos(lam_im * dt)
    ab_im = mag * jnp.sin(lam_im * dt)
    den = lam_re * lam_re + lam_im * lam_im
    z_re = ((ab_re - 1.0) * lam_re + ab_im * lam_im) / den
    z_im = (ab_im * lam_re - (ab_re - 1.0) * lam_im) / den
    br, bi = bre_ref[...], bim_ref[...]
    abre_ref[...] = ab_re
    abim_ref[...] = ab_im
    bbre_ref[...] = z_re * br - z_im * bi
    bbim_ref[...] = z_re * bi + z_im * br


def _s5prep(a_re, a_im, log_dt, b_re, b_im):
    depth = a_re.shape[0]
    col = lambda a: a.reshape(depth, S5_NSTATE, 1)
    ldt = jnp.broadcast_to(log_dt[:, :, None], (depth, S5_GROUPS, S5_STATE))
    vec = pl.BlockSpec((None, S5_NSTATE, 1), lambda l: (l, 0, 0))
    mat = pl.BlockSpec((None, S5_NSTATE, S5_CH), lambda l: (l, 0, 0))
    return pl.pallas_call(
        _s5prep_kernel,
        grid=(depth,),
        in_specs=[vec, vec, vec, mat, mat],
        out_specs=[vec, vec, mat, mat],
        out_shape=[jax.ShapeDtypeStruct((depth, S5_NSTATE, 1), F32)] * 2
        + [jax.ShapeDtypeStruct((depth, S5_NSTATE, S5_CH), F32)] * 2,
        compiler_params=_params("parallel"),
        name="s5prep",
    )(col(a_re), col(a_im), col(ldt), b_re.reshape(depth, S5_NSTATE, S5_CH), b_im.reshape(depth, S5_NSTATE, S5_CH))


def _s5_kernel(u_ref, abre_ref, abim_ref, bre_ref, bim_ref, cre_ref, cim_ref, d_ref, gw_ref, gb_ref, o_ref,
               sre_sc, sim_sc, *, bsz, ts):
    @pl.when(pl.program_id(0) == 0)
    def _():
        sre_sc[...] = jnp.zeros_like(sre_sc)
        sim_sc[...] = jnp.zeros_like(sim_sc)

    rows = ts * bsz
    n_sub = u_ref.shape[0] // rows
    a_re = jnp.broadcast_to(abre_ref[...], (bsz, S5_NSTATE))
    a_im = jnp.broadcast_to(abim_ref[...], (bsz, S5_NSTATE))

    bu = []
    for k in range(n_sub):
        ub = u_ref[k * rows:(k + 1) * rows, :].astype(BF16)
        bu.append((_dot(ub, bre_ref[...]), _dot(ub, bim_ref[...])))

    s_re, s_im = sre_sc[...], sim_sc[...]
    for k in range(n_sub):
        bu_re, bu_im = bu[k]
        x_re, x_im = [], []
        for t in range(ts):
            r = slice(t * bsz, (t + 1) * bsz)
            s_re, s_im = (a_re * s_re - a_im * s_im + bu_re[r, :], a_re * s_im + a_im * s_re + bu_im[r, :])
            x_re.append(s_re)
            x_im.append(s_im)
        x_re = jnp.concatenate(x_re, axis=0).astype(BF16)
        x_im = jnp.concatenate(x_im, axis=0).astype(BF16)
        u = u_ref[k * rows:(k + 1) * rows, :]
        y = _dot(x_re, cre_ref[...]) - _dot(x_im, cim_ref[...])
        y = jax.nn.gelu(y + d_ref[...] * u, approximate=True)
        o_ref[k * rows:(k + 1) * rows, :] = (
            y * jax.nn.sigmoid(_dot(y.astype(BF16), gw_ref[...]) + gb_ref[...])).astype(BF16)
    sre_sc[...] = s_re
    sim_sc[...] = s_im


def _s5(layer, u, ab_re, ab_im, b_re, b_im, c_re, c_im, d, glu_w, glu_b, *, bsz, seq, tb, ts):
    rows = tb * bsz
    return pl.pallas_call(
        functools.partial(_s5_kernel, bsz=bsz, ts=ts),
        grid=(seq // tb,),
        in_specs=[
            pl.BlockSpec((rows, S5_WIDTH), lambda i: (i, 0)),
            _layer_spec(layer, (1, S5_NSTATE)),
            _layer_spec(layer, (1, S5_NSTATE)),
            _layer_spec(layer, (S5_WIDTH, S5_NSTATE)),
            _layer_spec(layer, (S5_WIDTH, S5_NSTATE)),
            _layer_spec(layer, (S5_NSTATE, S5_WIDTH)),
            _layer_spec(layer, (S5_NSTATE, S5_WIDTH)),
            _layer_spec(layer, (1, S5_WIDTH)),
            _layer_spec(layer, (S5_WIDTH, S5_WIDTH)),
            _layer_spec(layer, (1, S5_WIDTH)),
        ],
        out_specs=pl.BlockSpec((rows, S5_WIDTH), lambda i: (i, 0)),
        out_shape=jax.ShapeDtypeStruct((seq * bsz, S5_WIDTH), BF16),
        scratch_shapes=[pltpu.VMEM((bsz, S5_NSTATE), F32), pltpu.VMEM((bsz, S5_NSTATE), F32)],
        compiler_params=_params("arbitrary"),
        name="s5",
    )(u, ab_re, ab_im, b_re, b_im, c_re, c_im, d, glu_w, glu_b)


def _merge_kernel(x_ref, gla_ref, s5_ref, fox_ref, pre_g_ref, wgate_ref, wgla_ref, ws5_ref, wfox_ref, wo_ref,
                  post_g_ref, o_ref):
    tm, d = x_ref.shape
    blocks = [slice(r0, r0 + tm // EDGE_BLOCKS) for r0 in range(0, tm, tm // EDGE_BLOCKS)]
    branches = ((gla_ref, wgla_ref), (s5_ref, ws5_ref), (fox_ref, wfox_ref))

    hs, first = [], []
    for rows in blocks:
        hs.append(_rms(x_ref[rows, :], pre_g_ref[...]).astype(BF16))
        first.append(_dot(hs[-1], wgate_ref[:, 0:d]))
    h = jnp.concatenate(hs, axis=0)
    logits = jnp.concatenate(first, axis=0)
    mix = None
    for n, (b_ref, w_ref) in enumerate(branches[:-1]):
        term = jax.nn.sigmoid(logits) * _dot(b_ref[...], w_ref[...])
        mix = term if mix is None else mix + term
        logits = _dot(h, wgate_ref[:, (n + 1) * d:(n + 2) * d])
    up = _dot(branches[-1][0][...], branches[-1][1][...])

    for rows in blocks:
        mixed = mix[rows, :] + jax.nn.sigmoid(logits[rows, :]) * up[rows, :]
        y = _dot(mixed.astype(BF16), wo_ref[...])
        o_ref[rows, :] = x_ref[rows, :] + _rms(y, post_g_ref[...])


def _merge(layer, x, gla_o, s5_o, fox_o, pre_g, w_gate, w_gla, w_s5, w_fox, w_o, post_g, *, bsz, seq, tm):
    m, d = x.shape
    nl = seq // tm
    row = lambda b, l: (b * nl + l, 0)
    return pl.pallas_call(
        _merge_kernel,
        grid=(bsz, nl),
        in_specs=[
            pl.BlockSpec((tm, d), row),
            pl.BlockSpec((tm, GLA_V), row),
            pl.BlockSpec((tm, S5_WIDTH), lambda b, l: (l, b)),
            pl.BlockSpec((tm, FOX_W), row),
            _layer_spec(layer, (1, d)),
            _layer_spec(layer, (d, N_BRANCH * d), pipeline_mode=RESIDENT),
            _layer_spec(layer, (GLA_V, d), pipeline_mode=RESIDENT),
            _layer_spec(layer, (S5_WIDTH, d), pipeline_mode=RESIDENT),
            _layer_spec(layer, (FOX_W, d), pipeline_mode=RESIDENT),
            _layer_spec(layer, (d, d), pipeline_mode=RESIDENT),
            _layer_spec(layer, (1, d)),
        ],
        out_specs=pl.BlockSpec((tm, d), row),
        out_shape=jax.ShapeDtypeStruct((m, d), F32),
        compiler_params=_params("parallel", "parallel"),
        name="merge",
    )(x, gla_o, s5_o, fox_o, pre_g, w_gate, w_gla, w_s5, w_fox, w_o, post_g)


def _memkv_kernel(mem_ref, g_ref, w_ref, *rest):
    n_cast = (len(rest) - 2) // 2
    k_ref, v_ref = rest[n_cast], rest[n_cast + 1]
    _cast_slabs(rest[:n_cast] + rest[n_cast + 2:])
    d = mem_ref.shape[1]
    h = _rms(mem_ref[...], g_ref[...]).astype(BF16)
    k_ref[...] = _dot(h, w_ref[:, 0:d]).astype(BF16)
    v_ref[...] = _dot(h, w_ref[:, d:2 * d]).astype(BF16)


def _memkv(mem, g, w_kv, *, tm, cast):
    m, d = mem.shape
    depth = w_kv.shape[0]
    tiles = m // tm
    per_layer = lambda block: pl.BlockSpec((None,) + block, lambda l, i: (l, 0, 0))
    out = pl.BlockSpec((None, tm, d), lambda l, i: (l, i, 0))
    extra = [_cast_specs(cast[0], w, depth * tiles, lambda l, i: l * tiles + i) for w in cast[1:]]
    outs = pl.pallas_call(
        _memkv_kernel,
        grid=(depth, tiles),
        in_specs=[pl.BlockSpec((tm, d), lambda l, i: (i, 0)), per_layer((1, d)), per_layer((d, 2 * d))]
        + [e[0] for e in extra],
        out_specs=[out, out] + [e[1] for e in extra],
        out_shape=[jax.ShapeDtypeStruct((depth, m, d), BF16)] * 2 + [e[2] for e in extra],
        compiler_params=_params("parallel", "parallel"),
        name="memkv",
    )(mem, g, w_kv, *cast[1:])
    return outs[0], outs[1], tuple(outs[2:])


def _xattn_kernel(x_ref, k_ref, v_ref, pre_g_ref, wq_ref, wo_ref, post_g_ref, *rest):
    o_ref, cast_refs = rest[len(rest) // 2], rest[:len(rest) // 2] + rest[len(rest) // 2 + 1:]
    _cast_slabs(cast_refs)
    tm, d = x_ref.shape
    dh = d // XA_HEADS
    blocks = [slice(r0, r0 + tm // EDGE_BLOCKS) for r0 in range(0, tm, tm // EDGE_BLOCKS)]
    cols = [slice(n * dh, (n + 1) * dh) for n in range(XA_HEADS)]

    def softmax(s):
        e = jnp.exp(s - jnp.max(s, axis=1, keepdims=True))
        return (e / jnp.sum(e, axis=1, keepdims=True)).astype(BF16)

    qs = []
    for rows in blocks:
        h = _rms(x_ref[rows, :], pre_g_ref[...]).astype(BF16)
        qs.append((_dot(h, wq_ref[...]) * (dh ** -0.5)).astype(BF16))
    logits = [[_dot_nt(q[:, c], k_ref[:, c]) for c in cols] for q in qs]
    probs = [[softmax(s) for s in per_head] for per_head in logits]
    heads = [jnp.concatenate([_dot(p, v_ref[:, c]).astype(BF16) for p, c in zip(per_head, cols)], axis=1)
             for per_head in probs]
    for rows, attended in zip(blocks, heads):
        y = _dot(attended, wo_ref[...])
        o_ref[rows, :] = x_ref[rows, :] + _rms(y, post_g_ref[...])


def _xattn(layer, x, k, v, pre_g, w_q, w_o, post_g, *, bsz, seq, n_mem, tm, cast):
    m, d = x.shape
    nl = seq // tm
    row = lambda b, l: (b * nl + l, 0)
    extra = [_cast_specs(cast[0], w, bsz * nl, lambda b, l: b * nl + l) for w in cast[1:]]
    outs = pl.pallas_call(
        _xattn_kernel,
        grid=(bsz, nl),
        in_specs=[
            pl.BlockSpec((tm, d), row),
            _layer_spec(layer, (n_mem, d), lambda b, l: (b, 0)),
            _layer_spec(layer, (n_mem, d), lambda b, l: (b, 0)),
            _layer_spec(layer, (1, d)),
            _layer_spec(layer, (d, d), pipeline_mode=RESIDENT),
            _layer_spec(layer, (d, d), pipeline_mode=RESIDENT),
            _layer_spec(layer, (1, d)),
        ] + [e[0] for e in extra],
        out_specs=[pl.BlockSpec((tm, d), row)] + [e[1] for e in extra],
        out_shape=[jax.ShapeDtypeStruct((m, d), F32)] + [e[2] for e in extra],
        compiler_params=_params("parallel", "parallel"),
        name="xattn",
    )(x, k, v, pre_g, w_q, w_o, post_g, *cast[1:])
    return outs[0], tuple(outs[1:])


def _tile(n, want):
    t = min(n, want)
    while n % t:
        t -= 1
    return t


def _block_diag(t):
    depth, g, r, c = t.shape
    eye = jnp.eye(g, dtype=t.dtype)
    return jnp.einsum("lgrc,gk->lgrkc", t, eye).reshape(depth, g * r, g * c)


def kernel(x, mem, ffn1_pre_g, ffn1_w_gu, ffn1_w_down, ffn1_post_g, mix_pre_g, w_in, gla_gate_w, gla_gate_b, gla_norm_g, w_gla_up, s5_a_re, s5_a_im, s5_log_dt, s5_b_re, s5_b_im, s5_c_re, s5_c_im, s5_d, s5_glu_w, s5_glu_b, w_s5_up, fox_f_b, w_fox_up, w_mix_out, mix_post_g, xa_pre_g, xa_mem_g, xa_w_q, xa_w_kv, xa_w_o, xa_post_g, ffn2_pre_g, ffn2_w_gu, ffn2_w_down, ffn2_post_g):
    bsz, seq, d = x.shape
    n_mem = mem.shape[1]
    depth = w_in.shape[0]
    d_ff = ffn1_w_down.shape[1]
    m = bsz * seq
    assert seq % CHUNK == 0 and d % LANES == 0

    tm_ffn = _tile(m, 1024)
    tf = _tile(d_ff // LANES, 2) * LANES
    tm = _tile(seq, 1024)
    tk = _tile(seq, 512)
    ts = _tile(seq, 128)
    tb = 2 * ts if seq % (2 * ts) == 0 else ts

    bf = lambda a: a.astype(BF16)
    vec = lambda a: a.reshape(depth, 1, a.shape[-1])

    sizes = (GLA_QK, GLA_QK, GLA_V, GLA_V, GLA_RANK, S5_WIDTH, FOX_W, FOX_W, FOX_W, FOX_HEADS, N_BRANCH * d)
    offs = [0]
    for s in sizes:
        offs.append(offs[-1] + s)
    w_in_b = bf(w_in)
    seg = lambda i: w_in_b[:, :, offs[i]:offs[i + 1]]
    pad = jnp.zeros((depth, d, IN_SMALL - GLA_RANK - FOX_HEADS), BF16)
    w_proj = jnp.concatenate([seg(0), seg(1), seg(2), seg(3), seg(5), seg(6), seg(7), seg(8), seg(4), seg(9), pad],
                             axis=2)
    w_gate = seg(10)
    gla_wg = bf(jnp.concatenate([gla_gate_w, jnp.zeros((depth, IN_SMALL - GLA_RANK, GLA_QK), gla_gate_w.dtype)],
                                axis=1))

    ab_re, ab_im, bb_re, bb_im = _s5prep(s5_a_re, s5_a_im, s5_log_dt, s5_b_re, s5_b_im)
    grp = lambda t: t.reshape(depth, S5_GROUPS, S5_STATE, S5_CH)
    s5_bre = bf(_block_diag(grp(bb_re).transpose(0, 1, 3, 2)))
    s5_bim = bf(_block_diag(grp(bb_im).transpose(0, 1, 3, 2)))
    s5_cre = bf(_block_diag(s5_c_re.transpose(0, 1, 3, 2)))
    s5_cim = bf(_block_diag(s5_c_im.transpose(0, 1, 3, 2)))
    ab_re = ab_re.reshape(depth, 1, S5_NSTATE)
    ab_im = ab_im.reshape(depth, 1, S5_NSTATE)

    gla_up, s5_up, fox_up, mix_out = bf(w_gla_up), bf(w_s5_up), bf(w_fox_up), bf(w_mix_out)
    glu_w, xa_q, xa_kv, xa_o = bf(s5_glu_w), bf(xa_w_q), bf(xa_w_kv), bf(xa_w_o)
    fox_b = fox_f_b.reshape(depth, FOX_HEADS, 1)

    xs = x.reshape(m, d)
    mk, mv, ffn1_w = _memkv(mem.reshape(bsz * n_mem, d), vec(xa_mem_g), xa_kv, tm=_tile(bsz * n_mem, 512),
                            cast=(0, ffn1_w_gu, ffn1_w_down))
    for l in range(depth):
        xs, _ = _ffn(l, xs, vec(ffn1_pre_g), *ffn1_w, vec(ffn1_post_g), tm=tm_ffn, tf=tf)

        qk, vr, su, fox, small = _inproj(l, xs, vec(mix_pre_g), w_proj, bsz=bsz, seq=seq, tm=tm)
        gla_o = _gla(l, qk, vr, small, gla_wg, vec(gla_gate_b), vec(gla_norm_g), bsz=bsz, seq=seq)
        f_cols = _foxprep(l, small, fox_b, bsz=bsz, seq=seq)
        fox_o = _fox(fox, f_cols, bsz=bsz, seq=seq, tk=tk)
        s5_o = _s5(l, su.reshape(seq * bsz, S5_WIDTH), ab_re, ab_im, s5_bre, s5_bim, s5_cre, s5_cim,
                   vec(s5_d), glu_w, vec(s5_glu_b), bsz=bsz, seq=seq, tb=tb, ts=ts)
        xs = _merge(l, xs, gla_o, s5_o.reshape(seq, bsz * S5_WIDTH), fox_o, vec(mix_pre_g), w_gate, gla_up,
                    s5_up, fox_up, mix_out, vec(mix_post_g), bsz=bsz, seq=seq, tm=tm)

        xs, ffn2_w = _xattn(l, xs, mk, mv, vec(xa_pre_g), xa_q, xa_o, vec(xa_post_g), bsz=bsz, seq=seq, n_mem=n_mem,
                            tm=tm, cast=(l, ffn2_w_gu, ffn2_w_down))

        ahead = (l + 1, ffn1_w_gu, ffn1_w_down) if l + 1 < depth else None
        xs, ffn1_w = _ffn(l, xs, vec(ffn2_pre_g), *ffn2_w, vec(ffn2_post_g), tm=tm_ffn, tf=tf, cast=ahead)
    return xs.reshape(bsz, seq, d)
```

```python
import functools
import math

import jax
import jax.numpy as jnp
from jax import lax
from jax.experimental import pallas as pl
from jax.experimental.pallas import tpu as pltpu

F32 = jnp.float32
BF16 = jnp.bfloat16
EPS = 1e-6
LOG2E = math.log2(math.e)

LANES = 128
SUBLANES = 8
VMEM_LIMIT_BYTES = 56 * 1024 * 1024

EDGE_BLOCKS = 4
CHUNK = 64
GLA_GROUP = 8
GLA_HEADS, GLA_DK, GLA_DV, GLA_RANK, GLA_TAU = 4, 32, 64, 16, 16.0
GLA_QK = GLA_HEADS * GLA_DK
GLA_V = GLA_HEADS * GLA_DV
DK_SHIFT = GLA_DK.bit_length() - 1
DV_SHIFT = GLA_DV.bit_length() - 1
S5_GROUPS, S5_CH, S5_STATE = 16, 16, 64
S5_WIDTH = S5_GROUPS * S5_CH
S5_NSTATE = S5_GROUPS * S5_STATE
FOX_HEADS, FOX_DH = 8, 64
FOX_W = FOX_HEADS * FOX_DH
FOX_PAIRS = FOX_W // LANES
FOX_TERMS = 4
HEADS_SHIFT = FOX_HEADS.bit_length() - 1
XA_HEADS = 4
N_BRANCH = 3

NT_DIMS = (((1,), (1,)), ((), ()))
RESIDENT = pl.Buffered(1)


def _params(*semantics):
    return pltpu.CompilerParams(dimension_semantics=semantics, vmem_limit_bytes=VMEM_LIMIT_BYTES)


def _layer_spec(layer, block, index_map=None, pipeline_mode=None):
    block = tuple(block)
    if index_map is None:
        index_map = lambda *_: (0,) * len(block)
    return pl.BlockSpec((None,) + block, lambda *ids: (layer,) + tuple(index_map(*ids)),
                        pipeline_mode=pipeline_mode)


def _rms(x, g):
    return x * lax.rsqrt(jnp.mean(x * x, axis=-1, keepdims=True) + EPS) * g


def _dot(a, b):
    return jnp.dot(a, b, preferred_element_type=F32)


def _dot_nt(a, b):
    return lax.dot_general(a, b, NT_DIMS, preferred_element_type=F32)


def _log_sigmoid(x):
    return jnp.minimum(x, 0.0) - jnp.log1p(jnp.exp(-jnp.abs(x)))


def _split3(x):
    hi = x.astype(BF16)
    r1 = x - hi.astype(F32)
    mid = r1.astype(BF16)
    lo = (r1 - mid.astype(F32)).astype(BF16)
    return hi, mid, lo


def _cast_specs(layer, stacked, n_steps, step):
    _, rows, cols = stacked.shape
    slab = rows // n_steps
    assert slab * n_steps == rows and slab % (2 * SUBLANES) == 0
    return (pl.BlockSpec((None, slab, cols), lambda *ids: (layer, step(*ids), 0)),
            pl.BlockSpec((slab, cols), lambda *ids: (step(*ids), 0)),
            jax.ShapeDtypeStruct((rows, cols), BF16))


def _cast_slabs(refs):
    half = len(refs) // 2
    for src, dst in zip(refs[:half], refs[half:]):
        dst[...] = src[...].astype(BF16)


def _ffn_kernel(x_ref, pre_g_ref, wgu_ref, wd_ref, post_g_ref, *rest, tf):
    o_ref, cast_refs = rest[len(rest) // 2], rest[:len(rest) // 2] + rest[len(rest) // 2 + 1:]
    _cast_slabs(cast_refs)
    d_ff = wd_ref.shape[0]
    tm = x_ref.shape[0]
    blocks = [slice(r0, r0 + tm // EDGE_BLOCKS) for r0 in range(0, tm, tm // EDGE_BLOCKS)]

    def gate_up(h, c0):
        return _dot(h, wgu_ref[:, c0:c0 + tf]), _dot(h, wgu_ref[:, d_ff + c0:d_ff + c0 + tf])

    hs, first = [], []
    for rows in blocks:
        hs.append(_rms(x_ref[rows, :], pre_g_ref[...]).astype(BF16))
        first.append(gate_up(hs[-1], 0))
    h = jnp.concatenate(hs, axis=0)
    nxt = tuple(jnp.concatenate(t, axis=0) for t in zip(*first))

    acc = None
    for c0 in range(0, d_ff - tf, tf):
        gate, up = nxt
        nxt = gate_up(h, c0 + tf)
        a = (gate * jax.nn.sigmoid(gate) * up).astype(BF16)
        part = _dot(a, wd_ref[c0:c0 + tf, :])
        acc = part if acc is None else acc + part

    gate, up = nxt
    a = (gate * jax.nn.sigmoid(gate) * up).astype(BF16)
    for rows in blocks:
        total = _dot(a[rows, :], wd_ref[d_ff - tf:d_ff, :])
        if acc is not None:
            total = acc[rows, :] + total
        o_ref[rows, :] = x_ref[rows, :] + 0.5 * _rms(total, post_g_ref[...])


def _ffn(layer, x, pre_g, w_gu, w_down, post_g, *, tm, tf, cast=None):
    m, d = x.shape
    d_ff = w_down.shape[0]
    n_steps = m // tm
    extra = [_cast_specs(cast[0], w, n_steps, lambda i: i) for w in cast[1:]] if cast else []
    outs = pl.pallas_call(
        functools.partial(_ffn_kernel, tf=tf),
        grid=(n_steps,),
        in_specs=[
            pl.BlockSpec((tm, d), lambda i: (i, 0)),
            _layer_spec(layer, (1, d)),
            pl.BlockSpec((d, 2 * d_ff), lambda i: (0, 0), pipeline_mode=RESIDENT),
            pl.BlockSpec((d_ff, d), lambda i: (0, 0), pipeline_mode=RESIDENT),
            _layer_spec(layer, (1, d)),
        ] + [e[0] for e in extra],
        out_specs=[pl.BlockSpec((tm, d), lambda i: (i, 0))] + [e[1] for e in extra],
        out_shape=[jax.ShapeDtypeStruct((m, d), F32)] + [e[2] for e in extra],
        compiler_params=_params("parallel"),
        name="ffn",
    )(x, pre_g, w_gu, w_down, post_g, *(cast[1:] if cast else ()))
    return outs[0], tuple(outs[1:])


IN_QK = 2 * GLA_QK
IN_VR = 2 * GLA_V
IN_SU = S5_WIDTH
IN_FOX = 3 * FOX_W
IN_SMALL = LANES
OFF_VR = IN_QK
OFF_SU = OFF_VR + IN_VR
OFF_FOX = OFF_SU + IN_SU
OFF_SMALL = OFF_FOX + IN_FOX
IN_TOTAL = OFF_SMALL + IN_SMALL


def _inproj_kernel(x_ref, g_ref, w_ref, qk_ref, vr_ref, su_ref, fox_ref, small_ref):
    tm = x_ref.shape[0]
    hs = []
    for r0 in range(0, tm, tm // EDGE_BLOCKS):
        rows = slice(r0, r0 + tm // EDGE_BLOCKS)
        hs.append(_rms(x_ref[rows, :], g_ref[...]).astype(BF16))
        qk_ref[rows, :] = _dot(hs[-1], w_ref[:, 0:OFF_VR])
    h = jnp.concatenate(hs, axis=0)
    vr_ref[...] = _dot(h, w_ref[:, OFF_VR:OFF_SU]).astype(BF16)
    su_ref[...] = _dot(h, w_ref[:, OFF_SU:OFF_FOX])
    fox_ref[:, 0:FOX_W] = (_dot(h, w_ref[:, OFF_FOX:OFF_FOX + FOX_W]) * (FOX_DH ** -0.5 * LOG2E)).astype(BF16)
    fox_ref[:, FOX_W:IN_FOX] = _dot(h, w_ref[:, OFF_FOX + FOX_W:OFF_SMALL]).astype(BF16)
    small_ref[...] = _dot(h, w_ref[:, OFF_SMALL:IN_TOTAL])


def _inproj(layer, x, g, w, *, bsz, seq, tm):
    m, d = x.shape
    nl = seq // tm
    row = lambda b, l: (b * nl + l, 0)
    return pl.pallas_call(
        _inproj_kernel,
        grid=(bsz, nl),
        in_specs=[
            pl.BlockSpec((tm, d), row),
            _layer_spec(layer, (1, d)),
            _layer_spec(layer, (d, IN_TOTAL), pipeline_mode=RESIDENT),
        ],
        out_specs=[
            pl.BlockSpec((tm, IN_QK), row),
            pl.BlockSpec((tm, IN_VR), row),
            pl.BlockSpec((tm, IN_SU), lambda b, l: (l, b)),
            pl.BlockSpec((tm, IN_FOX), row),
            pl.BlockSpec((tm, IN_SMALL), row),
        ],
        out_shape=[
            jax.ShapeDtypeStruct((m, IN_QK), F32),
            jax.ShapeDtypeStruct((m, IN_VR), BF16),
            jax.ShapeDtypeStruct((seq, bsz * IN_SU), F32),
            jax.ShapeDtypeStruct((m, IN_FOX), BF16),
            jax.ShapeDtypeStruct((m, IN_SMALL), F32),
        ],
        compiler_params=_params("parallel", "parallel"),
        name="inproj",
    )(x, g, w)


def _gla_kernel(qk_ref, vr_ref, small_ref, wg_ref, bg_ref, ng_ref, o_ref, la_sc):
    seq = qk_ref.shape[0]
    n_chunks = seq // CHUNK
    c, hq, hv = CHUNK, GLA_QK, GLA_V

    z = _dot(small_ref[...].astype(BF16), wg_ref[...]) + bg_ref[...]
    la_sc[...] = _log_sigmoid(z) * (1.0 / GLA_TAU)

    row = lax.broadcasted_iota(jnp.int32, (c, c), 0)
    col = lax.broadcasted_iota(jnp.int32, (c, c), 1)
    tri = (col <= row).astype(BF16)
    row_x = lax.broadcasted_iota(jnp.int32, (GLA_HEADS * c, c), 0)
    col_x = lax.broadcasted_iota(jnp.int32, (GLA_HEADS * c, c), 1)
    lower = col_x <= (row_x & (c - 1))
    lane_q = lax.broadcasted_iota(jnp.int32, (c, hq), 1)
    lane_v = lax.broadcasted_iota(jnp.int32, (c, hv), 1)
    q_head = [(lane_q >> DK_SHIFT) == h for h in range(GLA_HEADS)]
    v_head = [(lane_v >> DV_SHIFT) == h for h in range(GLA_HEADS)]
    st_row = lax.broadcasted_iota(jnp.int32, (hv, hq), 0)
    st_col = lax.broadcasted_iota(jnp.int32, (hv, hq), 1)
    same_head = (st_row >> DV_SHIFT) == (st_col >> DK_SHIFT)

    def expand(t):
        return jnp.concatenate([jnp.where(q_head[h], t, 0.0) for h in range(GLA_HEADS)], axis=0).astype(BF16)

    gr = lax.broadcasted_iota(jnp.int32, (hv, hv), 0) >> DV_SHIFT
    gc = lax.broadcasted_iota(jnp.int32, (hv, hv), 1) >> DV_SHIFT
    avg = jnp.where(gr == gc, 1.0 / GLA_DV, 0.0).astype(BF16)

    def within_chunks(group):
        stage1 = []
        for i in group:
            hi, mid, lo = _split3(la_sc[i * c:(i + 1) * c, :])
            stage1.append(_dot(tri, hi) + _dot(tri, mid) + _dot(tri, lo))
        stage2 = []
        for i, g in zip(group, stage1):
            rows = slice(i * c, (i + 1) * c)
            g_last = g[c - 1:c, :]
            eg, ieg = jnp.exp(g), jnp.exp(-g)
            qc = qk_ref[rows, 0:hq] * (GLA_DK ** -0.5)
            kc = qk_ref[rows, hq:2 * hq]
            v = vr_ref[rows, 0:hv]
            q_fwd = qc * eg
            a_fwd = _dot_nt(expand(q_fwd), (kc * ieg).astype(BF16))
            a_bwd = _dot_nt(expand(qc * ieg), (kc * eg).astype(BF16))
            kw = (kc * jnp.exp(g_last - g)).astype(BF16)
            ds_t = _dot(v.astype(F32).T.astype(BF16), kw)
            stage2.append((a_fwd, a_bwd, v, q_fwd.astype(BF16), jnp.where(same_head, ds_t, 0.0), jnp.exp(g_last)))
        results = []
        for a_fwd, a_bwd, v, q_fwd, ds_t, decay in stage2:
            attn = jnp.where(lower, a_fwd, a_bwd).astype(BF16)
            p = _dot(attn, v)
            o = jnp.where(v_head[0], p[0:c, :], 0.0)
            for h in range(1, GLA_HEADS):
                o = o + jnp.where(v_head[h], p[h * c:(h + 1) * c, :], 0.0)
            results.append((o, q_fwd, ds_t, decay))
        return results

    def finish(group, results, s_t):
        outs = []
        for o, q_fwd, ds_t, decay in results:
            outs.append(o + _dot_nt(q_fwd, s_t.astype(BF16)))
            s_t = s_t * decay + ds_t
        o = jnp.concatenate(outs, axis=0)
        rows = slice(group[0] * c, (group[-1] + 1) * c)
        hi, mid, lo = _split3(o * o)
        ms = _dot(hi, avg) + _dot(mid, avg) + _dot(lo, avg)
        r = vr_ref[rows, hv:2 * hv].astype(F32)
        o_ref[rows, :] = (o * lax.rsqrt(ms + EPS) * ng_ref[...] * (r * jax.nn.sigmoid(r))).astype(BF16)
        return s_t

    groups = [range(first, min(first + GLA_GROUP, n_chunks)) for first in range(0, n_chunks, GLA_GROUP)]
    s_t = jnp.zeros((hv, hq), F32)
    pending = None
    for group in groups:
        results = within_chunks(group)
        if pending is not None:
            s_t = finish(*pending, s_t)
        pending = (group, results)
    finish(*pending, s_t)


def _gla(layer, qk, vr, small, wg, bg, ng, *, bsz, seq):
    m = qk.shape[0]
    row = lambda b: (b, 0)
    return pl.pallas_call(
        _gla_kernel,
        grid=(bsz,),
        in_specs=[
            pl.BlockSpec((seq, IN_QK), row),
            pl.BlockSpec((seq, IN_VR), row),
            pl.BlockSpec((seq, IN_SMALL), row),
            _layer_spec(layer, (IN_SMALL, GLA_QK)),
            _layer_spec(layer, (1, GLA_QK)),
            _layer_spec(layer, (1, GLA_V)),
        ],
        out_specs=pl.BlockSpec((seq, GLA_V), row),
        out_shape=jax.ShapeDtypeStruct((m, GLA_V), BF16),
        scratch_shapes=[pltpu.VMEM((seq, GLA_QK), F32)],
        compiler_params=_params("parallel"),
        name="gla",
    )(qk, vr, small, wg, bg, ng)


def _foxprep_kernel(small_ref, b_ref, o_ref, *, bsz):
    seq = small_ref.shape[0] // bsz
    ff = jnp.concatenate([small_ref[b * seq:(b + 1) * seq, :].T[GLA_RANK:GLA_RANK + FOX_HEADS, :]
                          for b in range(bsz)], axis=0)
    x = _log_sigmoid(ff + jnp.concatenate([b_ref[...]] * bsz, axis=0))
    lane = lax.broadcasted_iota(jnp.int32, x.shape, 1)
    shift = 1
    while shift < seq:
        x = x + jnp.where(lane >= shift, pltpu.roll(x, shift, axis=1), 0.0)
        shift *= 2
    hi, mid, lo = (t.astype(F32) for t in _split3(x * LOG2E))
    ones = jnp.ones((FOX_HEADS, seq), F32)
    pad = jnp.zeros((LANES - FOX_TERMS * FOX_HEADS, seq), F32)
    for b in range(bsz):
        mine = slice(b * FOX_HEADS, (b + 1) * FOX_HEADS)
        terms = jnp.concatenate([hi[mine], mid[mine], lo[mine], ones, pad], axis=0)
        o_ref[b * seq:(b + 1) * seq, :] = terms.T.astype(BF16)


def _foxprep(layer, small, bias, *, bsz, seq):
    whole = lambda i: (0, 0)
    return pl.pallas_call(
        functools.partial(_foxprep_kernel, bsz=bsz),
        grid=(1,),
        in_specs=[
            pl.BlockSpec((bsz * seq, IN_SMALL), whole),
            _layer_spec(layer, (FOX_HEADS, 1)),
        ],
        out_specs=pl.BlockSpec((bsz * seq, LANES), whole),
        out_shape=jax.ShapeDtypeStruct((bsz * seq, LANES), BF16),
        compiler_params=_params("arbitrary"),
        name="foxprep",
    )(small, bias)


def _fox_placement(head, base, *, key):
    r = lax.broadcasted_iota(jnp.int32, (LANES, LANES), 0)
    rel = lax.broadcasted_iota(jnp.int32, (LANES, LANES), 1) - base
    term = r >> HEADS_SHIFT
    mine = ((r & (FOX_HEADS - 1)) == head) & (term < FOX_TERMS)
    is_part = mine & (term < FOX_TERMS - 1)
    is_one = mine & (term == FOX_TERMS - 1)
    n = FOX_TERMS - 1
    if key:
        m = jnp.where(is_part & (rel == term + n), -1.0, jnp.where(is_one & (rel >= 0) & (rel < n), 1.0, 0.0))
    else:
        m = jnp.where(is_part & (rel == term), 1.0, jnp.where(is_one & (rel >= n) & (rel < 2 * n), 1.0, 0.0))
    return m.astype(BF16)


def _fox_kernel(q_ref, k_ref, v_ref, f_ref, o_ref, qx_sc, kx_sc, vx_sc, *, tk):
    seq = k_ref.shape[0]
    spare = (FOX_DH, 0)
    lane = lax.broadcasted_iota(jnp.int32, (seq, LANES), 1)
    q, k, v, f = q_ref[...], k_ref[...], v_ref[...], f_ref[...]
    for h in range(2):
        own = (lane < FOX_DH) if h == 0 else (lane >= FOX_DH)
        head = 2 * pl.program_id(1) + h
        qx_sc[h] = jnp.where(own, q, _dot(f, _fox_placement(head, spare[h], key=False)).astype(BF16))
        kx_sc[h] = jnp.where(own, k, _dot(f, _fox_placement(head, spare[h], key=True)).astype(BF16))
        vx_sc[h] = jnp.where(own, v, jnp.ones_like(v))

    row = lax.broadcasted_iota(jnp.int32, (tk, tk), 0)
    col = lax.broadcasted_iota(jnp.int32, (tk, tk), 1)
    causal = col <= row

    items = [(h, r0, c0) for c0 in range(0, seq, tk) for h in range(2) for r0 in range(c0, seq, tk)]

    def logits(item):
        h, r0, c0 = item
        s = _dot_nt(qx_sc[h, r0:r0 + tk, :], kx_sc[h, c0:c0 + tk, :])
        return jnp.where(causal, s, -jnp.inf) if r0 == c0 else s

    state = {}
    s_next = logits(items[0])
    for n, (h, r0, c0) in enumerate(items):
        s = s_next
        if n + 1 < len(items):
            s_next = logits(items[n + 1])
        vx = vx_sc[h, c0:c0 + tk, :]
        m_new = jnp.max(s, axis=1, keepdims=True)
        if c0 == 0:
            acc = _dot(jnp.exp2(s - m_new).astype(BF16), vx)
        else:
            m_old, acc_old = state[h, r0]
            m_new = jnp.maximum(m_old, m_new)
            acc = jnp.exp2(m_old - m_new) * acc_old + _dot(jnp.exp2(s - m_new).astype(BF16), vx)
        state[h, r0] = (m_new, acc)

    first = lax.broadcasted_iota(jnp.int32, (tk, LANES), 1) < FOX_DH
    for r0 in range(0, seq, tk):
        a0, a1 = state[0, r0][1], state[1, r0][1]
        o_ref[r0:r0 + tk, :] = jnp.where(first, a0 / pltpu.roll(a0, FOX_DH, axis=1),
                                         a1 / pltpu.roll(a1, FOX_DH, axis=1)).astype(BF16)


def _fox(fox, f, *, bsz, seq, tk):
    m = fox.shape[0]
    return pl.pallas_call(
        functools.partial(_fox_kernel, tk=tk),
        grid=(bsz, FOX_PAIRS),
        in_specs=[
            pl.BlockSpec((seq, LANES), lambda b, p: (b, p)),
            pl.BlockSpec((seq, LANES), lambda b, p: (b, FOX_PAIRS + p)),
            pl.BlockSpec((seq, LANES), lambda b, p: (b, 2 * FOX_PAIRS + p)),
            pl.BlockSpec((seq, LANES), lambda b, p: (b, 0)),
        ],
        out_specs=pl.BlockSpec((seq, LANES), lambda b, p: (b, p)),
        out_shape=jax.ShapeDtypeStruct((m, FOX_W), BF16),
        scratch_shapes=[pltpu.VMEM((2, seq, LANES), BF16)] * 3,
        compiler_params=_params("parallel", "parallel"),
        name="fox",
    )(fox, fox, fox, f)


def _s5prep_kernel(are_ref, aim_ref, ldt_ref, bre_ref, bim_ref, abre_ref, abim_ref, bbre_ref, bbim_ref):
    lam_re = jnp.minimum(are_ref[...], -1e-4)
    lam_im = aim_ref[...]
    dt = jnp.exp(ldt_ref[...])
    mag = jnp.exp(lam_re * dt)
    ab_re = mag * jnp.cos(lam_im * dt)
    ab_im = mag * jnp.sin(lam_im * dt)
    den = lam_re * lam_re + lam_im * lam_im
    z_re = ((ab_re - 1.0) * lam_re + ab_im * lam_im) / den
    z_im = (ab_im * lam_re - (ab_re - 1.0) * lam_im) / den
    br, bi = bre_ref[...], bim_ref[...]
    abre_ref[...] = ab_re
    abim_ref[...] = ab_im
    bbre_ref[...] = z_re * br - z_im * bi
    bbim_ref[...] = z_re * bi + z_im * br


def _s5prep(a_re, a_im, log_dt, b_re, b_im):
    depth = a_re.shape[0]
    col = lambda a: a.reshape(depth, S5_NSTATE, 1)
    ldt = jnp.broadcast_to(log_dt[:, :, None], (depth, S5_GROUPS, S5_STATE))
    vec = pl.BlockSpec((None, S5_NSTATE, 1), lambda l: (l, 0, 0))
    mat = pl.BlockSpec((None, S5_NSTATE, S5_CH), lambda l: (l, 0, 0))
    return pl.pallas_call(
        _s5prep_kernel,
        grid=(depth,),
        in_specs=[vec, vec, vec, mat, mat],
        out_specs=[vec, vec, mat, mat],
        out_shape=[jax.ShapeDtypeStruct((depth, S5_NSTATE, 1), F32)] * 2
        + [jax.ShapeDtypeStruct((depth, S5_NSTATE, S5_CH), F32)] * 2,
        compiler_params=_params("parallel"),
        name="s5prep",
    )(col(a_re), col(a_im), col(ldt), b_re.reshape(depth, S5_NSTATE, S5_CH), b_im.reshape(depth, S5_NSTATE, S5_CH))


def _s5_kernel(u_ref, abre_ref, abim_ref, bre_ref, bim_ref, cre_ref, cim_ref, d_ref, gw_ref, gb_ref, o_ref,
               sre_sc, sim_sc, *, bsz, ts):
    @pl.when(pl.program_id(0) == 0)
    def _():
        sre_sc[...] = jnp.zeros_like(sre_sc)
        sim_sc[...] = jnp.zeros_like(sim_sc)

    rows = ts * bsz
    n_sub = u_ref.shape[0] // rows
    a_re = jnp.broadcast_to(abre_ref[...], (bsz, S5_NSTATE))
    a_im = jnp.broadcast_to(abim_ref[...], (bsz, S5_NSTATE))

    bu = []
    for k in range(n_sub):
        ub = u_ref[k * rows:(k + 1) * rows, :].astype(BF16)
        bu.append((_dot(ub, bre_ref[...]), _dot(ub, bim_ref[...])))

    s_re, s_im = sre_sc[...], sim_sc[...]
    for k in range(n_sub):
        bu_re, bu_im = bu[k]
        x_re, x_im = [], []
        for t in range(ts):
            r = slice(t * bsz, (t + 1) * bsz)
            s_re, s_im = (a_re * s_re - a_im * s_im + bu_re[r, :], a_re * s_im + a_im * s_re + bu_im[r, :])
            x_re.append(s_re)
            x_im.append(s_im)
        x_re = jnp.concatenate(x_re, axis=0).astype(BF16)
        x_im = jnp.concatenate(x_im, axis=0).astype(BF16)
        u = u_ref[k * rows:(k + 1) * rows, :]
        y = _dot(x_re, cre_ref[...]) - _dot(x_im, cim_ref[...])
        y = jax.nn.gelu(y + d_ref[...] * u, approximate=True)
        o_ref[k * rows:(k + 1) * rows, :] = (
            y * jax.nn.sigmoid(_dot(y.astype(BF16), gw_ref[...]) + gb_ref[...])).astype(BF16)
    sre_sc[...] = s_re
    sim_sc[...] = s_im


def _s5(layer, u, ab_re, ab_im, b_re, b_im, c_re, c_im, d, glu_w, glu_b, *, bsz, seq, tb, ts):
    rows = tb * bsz
    return pl.pallas_call(
        functools.partial(_s5_kernel, bsz=bsz, ts=ts),
        grid=(seq // tb,),
        in_specs=[
            pl.BlockSpec((rows, S5_WIDTH), lambda i: (i, 0)),
            _layer_spec(layer, (1, S5_NSTATE)),
            _layer_spec(layer, (1, S5_NSTATE)),
            _layer_spec(layer, (S5_WIDTH, S5_NSTATE)),
            _layer_spec(layer, (S5_WIDTH, S5_NSTATE)),
            _layer_spec(layer, (S5_NSTATE, S5_WIDTH)),
            _layer_spec(layer, (S5_NSTATE, S5_WIDTH)),
            _layer_spec(layer, (1, S5_WIDTH)),
            _layer_spec(layer, (S5_WIDTH, S5_WIDTH)),
            _layer_spec(layer, (1, S5_WIDTH)),
        ],
        out_specs=pl.BlockSpec((rows, S5_WIDTH), lambda i: (i, 0)),
        out_shape=jax.ShapeDtypeStruct((seq * bsz, S5_WIDTH), BF16),
        scratch_shapes=[pltpu.VMEM((bsz, S5_NSTATE), F32), pltpu.VMEM((bsz, S5_NSTATE), F32)],
        compiler_params=_params("arbitrary"),
        name="s5",
    )(u, ab_re, ab_im, b_re, b_im, c_re, c_im, d, glu_w, glu_b)


def _merge_kernel(x_ref, gla_ref, s5_ref, fox_ref, pre_g_ref, wgate_ref, wgla_ref, ws5_ref, wfox_ref, wo_ref,
                  post_g_ref, o_ref):
    tm, d = x_ref.shape
    blocks = [slice(r0, r0 + tm // EDGE_BLOCKS) for r0 in range(0, tm, tm // EDGE_BLOCKS)]
    branches = ((gla_ref, wgla_ref), (s5_ref, ws5_ref), (fox_ref, wfox_ref))

    hs, first = [], []
    for rows in blocks:
        hs.append(_rms(x_ref[rows, :], pre_g_ref[...]).astype(BF16))
        first.append(_dot(hs[-1], wgate_ref[:, 0:d]))
    h = jnp.concatenate(hs, axis=0)
    logits = jnp.concatenate(first, axis=0)
    mix = None
    for n, (b_ref, w_ref) in enumerate(branches[:-1]):
        term = jax.nn.sigmoid(logits) * _dot(b_ref[...], w_ref[...])
        mix = term if mix is None else mix + term
        logits = _dot(h, wgate_ref[:, (n + 1) * d:(n + 2) * d])
    up = _dot(branches[-1][0][...], branches[-1][1][...])

    for rows in blocks:
        mixed = mix[rows, :] + jax.nn.sigmoid(logits[rows, :]) * up[rows, :]
        y = _dot(mixed.astype(BF16), wo_ref[...])
        o_ref[rows, :] = x_ref[rows, :] + _rms(y, post_g_ref[...])


def _merge(layer, x, gla_o, s5_o, fox_o, pre_g, w_gate, w_gla, w_s5, w_fox, w_o, post_g, *, bsz, seq, tm):
    m, d = x.shape
    nl = seq // tm
    row = lambda b, l: (b * nl + l, 0)
    whole = lambda shape: pl.BlockSpec(shape, lambda b, l: (0, 0), pipeline_mode=RESIDENT)
    return pl.pallas_call(
        _merge_kernel,
        grid=(bsz, nl),
        in_specs=[
            pl.BlockSpec((tm, d), row),
            pl.BlockSpec((tm, GLA_V), row),
            pl.BlockSpec((tm, S5_WIDTH), lambda b, l: (l, b)),
            pl.BlockSpec((tm, FOX_W), row),
            _layer_spec(layer, (1, d)),
            _layer_spec(layer, (d, N_BRANCH * d), pipeline_mode=RESIDENT),
            whole((GLA_V, d)),
            whole((S5_WIDTH, d)),
            whole((FOX_W, d)),
            whole((d, d)),
            _layer_spec(layer, (1, d)),
        ],
        out_specs=pl.BlockSpec((tm, d), row),
        out_shape=jax.ShapeDtypeStruct((m, d), F32),
        compiler_params=_params("parallel", "parallel"),
        name="merge",
    )(x, gla_o, s5_o, fox_o, pre_g, w_gate, w_gla, w_s5, w_fox, w_o, post_g)


def _memkv_kernel(mem_ref, g_ref, w_ref, *rest):
    n_cast = (len(rest) - 2) // 2
    k_ref, v_ref = rest[n_cast], rest[n_cast + 1]
    _cast_slabs(rest[:n_cast] + rest[n_cast + 2:])
    d = mem_ref.shape[1]
    h = _rms(mem_ref[...], g_ref[...]).astype(BF16)
    k_ref[...] = _dot(h, w_ref[:, 0:d]).astype(BF16)
    v_ref[...] = _dot(h, w_ref[:, d:2 * d]).astype(BF16)


def _memkv(mem, g, w_kv, *, tm, cast):
    m, d = mem.shape
    depth = w_kv.shape[0]
    tiles = m // tm
    per_layer = lambda block: pl.BlockSpec((None,) + block, lambda l, i: (l, 0, 0))
    out = pl.BlockSpec((None, tm, d), lambda l, i: (l, i, 0))
    extra = [_cast_specs(cast[0], w, depth * tiles, lambda l, i: l * tiles + i) for w in cast[1:]]
    outs = pl.pallas_call(
        _memkv_kernel,
        grid=(depth, tiles),
        in_specs=[pl.BlockSpec((tm, d), lambda l, i: (i, 0)), per_layer((1, d)), per_layer((d, 2 * d))]
        + [e[0] for e in extra],
        out_specs=[out, out] + [e[1] for e in extra],
        out_shape=[jax.ShapeDtypeStruct((depth, m, d), BF16)] * 2 + [e[2] for e in extra],
        compiler_params=_params("parallel", "parallel"),
        name="memkv",
    )(mem, g, w_kv, *cast[1:])
    return outs[0], outs[1], tuple(outs[2:])


def _xattn_kernel(x_ref, k_ref, v_ref, pre_g_ref, wq_ref, wo_ref, post_g_ref, *rest):
    o_ref, cast_refs = rest[len(rest) // 2], rest[:len(rest) // 2] + rest[len(rest) // 2 + 1:]
    _cast_slabs(cast_refs)
    tm, d = x_ref.shape
    dh = d // XA_HEADS
    blocks = [slice(r0, r0 + tm // EDGE_BLOCKS) for r0 in range(0, tm, tm // EDGE_BLOCKS)]
    cols = [slice(n * dh, (n + 1) * dh) for n in range(XA_HEADS)]

    def softmax(s):
        e = jnp.exp(s - jnp.max(s, axis=1, keepdims=True))
        return (e / jnp.sum(e, axis=1, keepdims=True)).astype(BF16)

    qs = []
    for rows in blocks:
        h = _rms(x_ref[rows, :], pre_g_ref[...]).astype(BF16)
        qs.append((_dot(h, wq_ref[...]) * (dh ** -0.5)).astype(BF16))
    logits = [[_dot_nt(q[:, c], k_ref[:, c]) for c in cols] for q in qs]
    probs = [[softmax(s) for s in per_head] for per_head in logits]
    heads = [jnp.concatenate([_dot(p, v_ref[:, c]).astype(BF16) for p, c in zip(per_head, cols)], axis=1)
             for per_head in probs]
    for rows, attended in zip(blocks, heads):
        y = _dot(attended, wo_ref[...])
        o_ref[rows, :] = x_ref[rows, :] + _rms(y, post_g_ref[...])


def _xattn(layer, x, k, v, pre_g, w_q, w_o, post_g, *, bsz, seq, n_mem, tm, cast):
    m, d = x.shape
    nl = seq // tm
    row = lambda b, l: (b * nl + l, 0)
    extra = [_cast_specs(cast[0], w, bsz * nl, lambda b, l: b * nl + l) for w in cast[1:]]
    outs = pl.pallas_call(
        _xattn_kernel,
        grid=(bsz, nl),
        in_specs=[
            pl.BlockSpec((tm, d), row),
            _layer_spec(layer, (n_mem, d), lambda b, l: (b, 0)),
            _layer_spec(layer, (n_mem, d), lambda b, l: (b, 0)),
            _layer_spec(layer, (1, d)),
            pl.BlockSpec((d, d), lambda b, l: (0, 0), pipeline_mode=RESIDENT),
            pl.BlockSpec((d, d), lambda b, l: (0, 0), pipeline_mode=RESIDENT),
            _layer_spec(layer, (1, d)),
        ] + [e[0] for e in extra],
        out_specs=[pl.BlockSpec((tm, d), row)] + [e[1] for e in extra],
        out_shape=[jax.ShapeDtypeStruct((m, d), F32)] + [e[2] for e in extra],
        compiler_params=_params("parallel", "parallel"),
        name="xattn",
    )(x, k, v, pre_g, w_q, w_o, post_g, *cast[1:])
    return outs[0], tuple(outs[1:])


def _tile(n, want):
    t = min(n, want)
    while n % t:
        t -= 1
    return t


def _block_diag(t):
    depth, g, r, c = t.shape
    eye = jnp.eye(g, dtype=t.dtype)
    return jnp.einsum("lgrc,gk->lgrkc", t, eye).reshape(depth, g * r, g * c)


def kernel(x, mem, ffn1_pre_g, ffn1_w_gu, ffn1_w_down, ffn1_post_g, mix_pre_g, w_in, gla_gate_w, gla_gate_b, gla_norm_g, w_gla_up, s5_a_re, s5_a_im, s5_log_dt, s5_b_re, s5_b_im, s5_c_re, s5_c_im, s5_d, s5_glu_w, s5_glu_b, w_s5_up, fox_f_b, w_fox_up, w_mix_out, mix_post_g, xa_pre_g, xa_mem_g, xa_w_q, xa_w_kv, xa_w_o, xa_post_g, ffn2_pre_g, ffn2_w_gu, ffn2_w_down, ffn2_post_g):
    bsz, seq, d = x.shape
    n_mem = mem.shape[1]
    depth = w_in.shape[0]
    d_ff = ffn1_w_down.shape[1]
    m = bsz * seq
    assert seq % CHUNK == 0 and d % LANES == 0

    tm_ffn = _tile(m, 1024)
    tf = _tile(d_ff // LANES, 2) * LANES
    tm = _tile(seq, 1024)
    tk = _tile(seq, 512)
    ts = _tile(seq, 128)
    tb = 2 * ts if seq % (2 * ts) == 0 else ts

    bf = lambda a: a.astype(BF16)
    vec = lambda a: a.reshape(depth, 1, a.shape[-1])

    sizes = (GLA_QK, GLA_QK, GLA_V, GLA_V, GLA_RANK, S5_WIDTH, FOX_W, FOX_W, FOX_W, FOX_HEADS, N_BRANCH * d)
    offs = [0]
    for s in sizes:
        offs.append(offs[-1] + s)
    w_in_b = bf(w_in)
    seg = lambda i: w_in_b[:, :, offs[i]:offs[i + 1]]
    pad = jnp.zeros((depth, d, IN_SMALL - GLA_RANK - FOX_HEADS), BF16)
    w_proj = jnp.concatenate([seg(0), seg(1), seg(2), seg(3), seg(5), seg(6), seg(7), seg(8), seg(4), seg(9), pad],
                             axis=2)
    w_gate = seg(10)
    gla_wg = bf(jnp.concatenate([gla_gate_w, jnp.zeros((depth, IN_SMALL - GLA_RANK, GLA_QK), gla_gate_w.dtype)],
                                axis=1))

    ab_re, ab_im, bb_re, bb_im = _s5prep(s5_a_re, s5_a_im, s5_log_dt, s5_b_re, s5_b_im)
    grp = lambda t: t.reshape(depth, S5_GROUPS, S5_STATE, S5_CH)
    s5_bre = bf(_block_diag(grp(bb_re).transpose(0, 1, 3, 2)))
    s5_bim = bf(_block_diag(grp(bb_im).transpose(0, 1, 3, 2)))
    s5_cre = bf(_block_diag(s5_c_re.transpose(0, 1, 3, 2)))
    s5_cim = bf(_block_diag(s5_c_im.transpose(0, 1, 3, 2)))
    ab_re = ab_re.reshape(depth, 1, S5_NSTATE)
    ab_im = ab_im.reshape(depth, 1, S5_NSTATE)

    glu_w, xa_kv = bf(s5_glu_w), bf(xa_w_kv)
    fox_b = fox_f_b.reshape(depth, FOX_HEADS, 1)

    xs = x.reshape(m, d)
    mk, mv, ffn1_w = _memkv(mem.reshape(bsz * n_mem, d), vec(xa_mem_g), xa_kv, tm=_tile(bsz * n_mem, 512),
                            cast=(0, ffn1_w_gu, ffn1_w_down))
    for l in range(depth):
        xs, (gla_up, s5_up, fox_up, mix_out, xa_q, xa_o) = _ffn(
            l, xs, vec(ffn1_pre_g), *ffn1_w, vec(ffn1_post_g), tm=tm_ffn, tf=tf,
            cast=(l, w_gla_up, w_s5_up, w_fox_up, w_mix_out, xa_w_q, xa_w_o))

        qk, vr, su, fox, small = _inproj(l, xs, vec(mix_pre_g), w_proj, bsz=bsz, seq=seq, tm=tm)
        gla_o = _gla(l, qk, vr, small, gla_wg, vec(gla_gate_b), vec(gla_norm_g), bsz=bsz, seq=seq)
        f_cols = _foxprep(l, small, fox_b, bsz=bsz, seq=seq)
        fox_o = _fox(fox, f_cols, bsz=bsz, seq=seq, tk=tk)
        s5_o = _s5(l, su.reshape(seq * bsz, S5_WIDTH), ab_re, ab_im, s5_bre, s5_bim, s5_cre, s5_cim,
                   vec(s5_d), glu_w, vec(s5_glu_b), bsz=bsz, seq=seq, tb=tb, ts=ts)
        xs = _merge(l, xs, gla_o, s5_o.reshape(seq, bsz * S5_WIDTH), fox_o, vec(mix_pre_g), w_gate, gla_up,
                    s5_up, fox_up, mix_out, vec(mix_post_g), bsz=bsz, seq=seq, tm=tm)

        xs, ffn2_w = _xattn(l, xs, mk, mv, vec(xa_pre_g), xa_q, xa_o, vec(xa_post_g), bsz=bsz, seq=seq, n_mem=n_mem,
                            tm=tm, cast=(l, ffn2_w_gu, ffn2_w_down))

        ahead = (l + 1, ffn1_w_gu, ffn1_w_down) if l + 1 < depth else None
        xs, ffn1_w = _ffn(l, xs, vec(ffn2_pre_g), *ffn2_w, vec(ffn2_post_g), tm=tm_ffn, tf=tf, cast=ahead)
    return xs.reshape(bsz, seq, d)
```

```python
import functools
import math

import jax
import jax.numpy as jnp
from jax import lax
from jax.experimental import pallas as pl
from jax.experimental.pallas import tpu as pltpu

F32 = jnp.float32
BF16 = jnp.bfloat16
EPS = 1e-6
LOG2E = math.log2(math.e)

LANES = 128
SUBLANES = 8
VMEM_LIMIT_BYTES = 56 * 1024 * 1024

EDGE_BLOCKS = 4
CHUNK = 64
GLA_GROUP = 8
GLA_HEADS, GLA_DK, GLA_DV, GLA_RANK, GLA_TAU = 4, 32, 64, 16, 16.0
GLA_QK = GLA_HEADS * GLA_DK
GLA_V = GLA_HEADS * GLA_DV
DK_SHIFT = GLA_DK.bit_length() - 1
DV_SHIFT = GLA_DV.bit_length() - 1
S5_GROUPS, S5_CH, S5_STATE = 16, 16, 64
S5_WIDTH = S5_GROUPS * S5_CH
S5_NSTATE = S5_GROUPS * S5_STATE
FOX_HEADS, FOX_DH = 8, 64
FOX_W = FOX_HEADS * FOX_DH
FOX_PAIRS = FOX_W // LANES
FOX_TERMS = 4
HEADS_SHIFT = FOX_HEADS.bit_length() - 1
XA_HEADS = 4
N_BRANCH = 3

NT_DIMS = (((1,), (1,)), ((), ()))
RESIDENT = pl.Buffered(1)


def _params(*semantics):
    return pltpu.CompilerParams(dimension_semantics=semantics, vmem_limit_bytes=VMEM_LIMIT_BYTES)


def _layer_spec(layer, block, index_map=None, pipeline_mode=None):
    block = tuple(block)
    if index_map is None:
        index_map = lambda *_: (0,) * len(block)
    return pl.BlockSpec((None,) + block, lambda *ids: (layer,) + tuple(index_map(*ids)),
                        pipeline_mode=pipeline_mode)


def _rms(x, g):
    return x * lax.rsqrt(jnp.mean(x * x, axis=-1, keepdims=True) + EPS) * g


def _dot(a, b):
    return jnp.dot(a, b, preferred_element_type=F32)


def _dot_nt(a, b):
    return lax.dot_general(a, b, NT_DIMS, preferred_element_type=F32)


def _log_sigmoid(x):
    return jnp.minimum(x, 0.0) - jnp.log1p(jnp.exp(-jnp.abs(x)))


def _split3(x):
    hi = x.astype(BF16)
    r1 = x - hi.astype(F32)
    mid = r1.astype(BF16)
    lo = (r1 - mid.astype(F32)).astype(BF16)
    return hi, mid, lo


def _cast_specs(layer, stacked, n_steps, step):
    _, rows, cols = stacked.shape
    slab = rows // n_steps
    assert slab * n_steps == rows and slab % (2 * SUBLANES) == 0
    return (pl.BlockSpec((None, slab, cols), lambda *ids: (layer, step(*ids), 0)),
            pl.BlockSpec((slab, cols), lambda *ids: (step(*ids), 0)),
            jax.ShapeDtypeStruct((rows, cols), BF16))


def _cast_slabs(refs):
    half = len(refs) // 2
    for src, dst in zip(refs[:half], refs[half:]):
        dst[...] = src[...].astype(BF16)


def _ffn_kernel(x_ref, pre_g_ref, wgu_ref, wd_ref, post_g_ref, *rest, tf):
    o_ref, cast_refs = rest[len(rest) // 2], rest[:len(rest) // 2] + rest[len(rest) // 2 + 1:]
    _cast_slabs(cast_refs)
    d_ff = wd_ref.shape[0]
    tm = x_ref.shape[0]
    blocks = [slice(r0, r0 + tm // EDGE_BLOCKS) for r0 in range(0, tm, tm // EDGE_BLOCKS)]

    def gate_up(h, c0):
        return _dot(h, wgu_ref[:, c0:c0 + tf]), _dot(h, wgu_ref[:, d_ff + c0:d_ff + c0 + tf])

    hs, first = [], []
    for rows in blocks:
        hs.append(_rms(x_ref[rows, :], pre_g_ref[...]).astype(BF16))
        first.append(gate_up(hs[-1], 0))
    h = jnp.concatenate(hs, axis=0)
    nxt = tuple(jnp.concatenate(t, axis=0) for t in zip(*first))

    acc = None
    for c0 in range(0, d_ff - tf, tf):
        gate, up = nxt
        nxt = gate_up(h, c0 + tf)
        a = (gate * jax.nn.sigmoid(gate) * up).astype(BF16)
        part = _dot(a, wd_ref[c0:c0 + tf, :])
        acc = part if acc is None else acc + part

    gate, up = nxt
    a = (gate * jax.nn.sigmoid(gate) * up).astype(BF16)
    for rows in blocks:
        total = _dot(a[rows, :], wd_ref[d_ff - tf:d_ff, :])
        if acc is not None:
            total = acc[rows, :] + total
        o_ref[rows, :] = x_ref[rows, :] + 0.5 * _rms(total, post_g_ref[...])


def _ffn(layer, x, pre_g, w_gu, w_down, post_g, *, tm, tf, cast=None):
    m, d = x.shape
    d_ff = w_down.shape[0]
    n_steps = m // tm
    extra = [_cast_specs(cast[0], w, n_steps, lambda i: i) for w in cast[1:]] if cast else []
    outs = pl.pallas_call(
        functools.partial(_ffn_kernel, tf=tf),
        grid=(n_steps,),
        in_specs=[
            pl.BlockSpec((tm, d), lambda i: (i, 0)),
            _layer_spec(layer, (1, d)),
            pl.BlockSpec((d, 2 * d_ff), lambda i: (0, 0), pipeline_mode=RESIDENT),
            pl.BlockSpec((d_ff, d), lambda i: (0, 0), pipeline_mode=RESIDENT),
            _layer_spec(layer, (1, d)),
        ] + [e[0] for e in extra],
        out_specs=[pl.BlockSpec((tm, d), lambda i: (i, 0))] + [e[1] for e in extra],
        out_shape=[jax.ShapeDtypeStruct((m, d), F32)] + [e[2] for e in extra],
        compiler_params=_params("parallel"),
        name="ffn",
    )(x, pre_g, w_gu, w_down, post_g, *(cast[1:] if cast else ()))
    return outs[0], tuple(outs[1:])


IN_QK = 2 * GLA_QK
IN_VR = 2 * GLA_V
IN_SU = S5_WIDTH
IN_FOX = 3 * FOX_W
IN_SMALL = LANES
OFF_VR = IN_QK
OFF_SU = OFF_VR + IN_VR
OFF_FOX = OFF_SU + IN_SU
OFF_SMALL = OFF_FOX + IN_FOX
IN_TOTAL = OFF_SMALL + IN_SMALL


def _inproj_kernel(x_ref, g_ref, w_ref, qk_ref, vr_ref, su_ref, fox_ref, small_ref):
    tm = x_ref.shape[0]
    hs = []
    for r0 in range(0, tm, tm // EDGE_BLOCKS):
        rows = slice(r0, r0 + tm // EDGE_BLOCKS)
        hs.append(_rms(x_ref[rows, :], g_ref[...]).astype(BF16))
        qk_ref[rows, :] = _dot(hs[-1], w_ref[:, 0:OFF_VR])
    h = jnp.concatenate(hs, axis=0)
    vr_ref[...] = _dot(h, w_ref[:, OFF_VR:OFF_SU]).astype(BF16)
    su_ref[...] = _dot(h, w_ref[:, OFF_SU:OFF_FOX])
    fox_ref[:, 0:FOX_W] = (_dot(h, w_ref[:, OFF_FOX:OFF_FOX + FOX_W]) * (FOX_DH ** -0.5 * LOG2E)).astype(BF16)
    fox_ref[:, FOX_W:IN_FOX] = _dot(h, w_ref[:, OFF_FOX + FOX_W:OFF_SMALL]).astype(BF16)
    small_ref[...] = _dot(h, w_ref[:, OFF_SMALL:IN_TOTAL])


def _inproj(layer, x, g, w, *, bsz, seq, tm):
    m, d = x.shape
    nl = seq // tm
    row = lambda b, l: (b * nl + l, 0)
    return pl.pallas_call(
        _inproj_kernel,
        grid=(bsz, nl),
        in_specs=[
            pl.BlockSpec((tm, d), row),
            _layer_spec(layer, (1, d)),
            _layer_spec(layer, (d, IN_TOTAL), pipeline_mode=RESIDENT),
        ],
        out_specs=[
            pl.BlockSpec((tm, IN_QK), row),
            pl.BlockSpec((tm, IN_VR), row),
            pl.BlockSpec((tm, IN_SU), lambda b, l: (l, b)),
            pl.BlockSpec((tm, IN_FOX), row),
            pl.BlockSpec((tm, IN_SMALL), row),
        ],
        out_shape=[
            jax.ShapeDtypeStruct((m, IN_QK), F32),
            jax.ShapeDtypeStruct((m, IN_VR), BF16),
            jax.ShapeDtypeStruct((seq, bsz * IN_SU), F32),
            jax.ShapeDtypeStruct((m, IN_FOX), BF16),
            jax.ShapeDtypeStruct((m, IN_SMALL), F32),
        ],
        compiler_params=_params("parallel", "parallel"),
        name="inproj",
    )(x, g, w)


def _gla_kernel(qk_ref, vr_ref, small_ref, wg_ref, bg_ref, ng_ref, o_ref, la_sc):
    seq = qk_ref.shape[0]
    n_chunks = seq // CHUNK
    c, hq, hv = CHUNK, GLA_QK, GLA_V

    z = _dot(small_ref[...].astype(BF16), wg_ref[...]) + bg_ref[...]
    la_sc[...] = _log_sigmoid(z) * (1.0 / GLA_TAU)

    row = lax.broadcasted_iota(jnp.int32, (c, c), 0)
    col = lax.broadcasted_iota(jnp.int32, (c, c), 1)
    tri = (col <= row).astype(BF16)
    row_x = lax.broadcasted_iota(jnp.int32, (GLA_HEADS * c, c), 0)
    col_x = lax.broadcasted_iota(jnp.int32, (GLA_HEADS * c, c), 1)
    lower = col_x <= (row_x & (c - 1))
    lane_q = lax.broadcasted_iota(jnp.int32, (c, hq), 1)
    lane_v = lax.broadcasted_iota(jnp.int32, (c, hv), 1)
    q_head = [(lane_q >> DK_SHIFT) == h for h in range(GLA_HEADS)]
    v_head = [(lane_v >> DV_SHIFT) == h for h in range(GLA_HEADS)]
    st_row = lax.broadcasted_iota(jnp.int32, (hv, hq), 0)
    st_col = lax.broadcasted_iota(jnp.int32, (hv, hq), 1)
    same_head = (st_row >> DV_SHIFT) == (st_col >> DK_SHIFT)

    def expand(t):
        return jnp.concatenate([jnp.where(q_head[h], t, 0.0) for h in range(GLA_HEADS)], axis=0).astype(BF16)

    gr = lax.broadcasted_iota(jnp.int32, (hv, hv), 0) >> DV_SHIFT
    gc = lax.broadcasted_iota(jnp.int32, (hv, hv), 1) >> DV_SHIFT
    avg = jnp.where(gr == gc, 1.0 / GLA_DV, 0.0).astype(BF16)

    def within_chunks(group):
        stage1 = []
        for i in group:
            hi, mid, lo = _split3(la_sc[i * c:(i + 1) * c, :])
            stage1.append(_dot(tri, hi) + _dot(tri, mid) + _dot(tri, lo))
        stage2 = []
        for i, g in zip(group, stage1):
            rows = slice(i * c, (i + 1) * c)
            g_last = g[c - 1:c, :]
            eg, ieg = jnp.exp(g), jnp.exp(-g)
            qc = qk_ref[rows, 0:hq] * (GLA_DK ** -0.5)
            kc = qk_ref[rows, hq:2 * hq]
            v = vr_ref[rows, 0:hv]
            q_fwd = qc * eg
            a_fwd = _dot_nt(expand(q_fwd), (kc * ieg).astype(BF16))
            a_bwd = _dot_nt(expand(qc * ieg), (kc * eg).astype(BF16))
            kw = (kc * jnp.exp(g_last - g)).astype(BF16)
            ds_t = _dot(v.astype(F32).T.astype(BF16), kw)
            stage2.append((a_fwd, a_bwd, v, q_fwd.astype(BF16), jnp.where(same_head, ds_t, 0.0), jnp.exp(g_last)))
        results = []
        for a_fwd, a_bwd, v, q_fwd, ds_t, decay in stage2:
            attn = jnp.where(lower, a_fwd, a_bwd).astype(BF16)
            p = _dot(attn, v)
            o = jnp.where(v_head[0], p[0:c, :], 0.0)
            for h in range(1, GLA_HEADS):
                o = o + jnp.where(v_head[h], p[h * c:(h + 1) * c, :], 0.0)
            results.append((o, q_fwd, ds_t, decay))
        return results

    def finish(group, results, s_t):
        outs = []
        for o, q_fwd, ds_t, decay in results:
            outs.append(o + _dot_nt(q_fwd, s_t.astype(BF16)))
            s_t = s_t * decay + ds_t
        o = jnp.concatenate(outs, axis=0)
        rows = slice(group[0] * c, (group[-1] + 1) * c)
        hi, mid, lo = _split3(o * o)
        ms = _dot(hi, avg) + _dot(mid, avg) + _dot(lo, avg)
        r = vr_ref[rows, hv:2 * hv].astype(F32)
        o_ref[rows, :] = (o * lax.rsqrt(ms + EPS) * ng_ref[...] * (r * jax.nn.sigmoid(r))).astype(BF16)
        return s_t

    groups = [range(first, min(first + GLA_GROUP, n_chunks)) for first in range(0, n_chunks, GLA_GROUP)]
    s_t = jnp.zeros((hv, hq), F32)
    pending = None
    for group in groups:
        results = within_chunks(group)
        if pending is not None:
            s_t = finish(*pending, s_t)
        pending = (group, results)
    finish(*pending, s_t)


def _gla(layer, qk, vr, small, wg, bg, ng, *, bsz, seq):
    m = qk.shape[0]
    row = lambda b: (b, 0)
    return pl.pallas_call(
        _gla_kernel,
        grid=(bsz,),
        in_specs=[
            pl.BlockSpec((seq, IN_QK), row),
            pl.BlockSpec((seq, IN_VR), row),
            pl.BlockSpec((seq, IN_SMALL), row),
            _layer_spec(layer, (IN_SMALL, GLA_QK)),
            _layer_spec(layer, (1, GLA_QK)),
            _layer_spec(layer, (1, GLA_V)),
        ],
        out_specs=pl.BlockSpec((seq, GLA_V), row),
        out_shape=jax.ShapeDtypeStruct((m, GLA_V), BF16),
        scratch_shapes=[pltpu.VMEM((seq, GLA_QK), F32)],
        compiler_params=_params("parallel"),
        name="gla",
    )(qk, vr, small, wg, bg, ng)


def _foxprep_kernel(small_ref, b_ref, o_ref, *, bsz):
    seq = small_ref.shape[0] // bsz
    ff = jnp.concatenate([small_ref[b * seq:(b + 1) * seq, :].T[GLA_RANK:GLA_RANK + FOX_HEADS, :]
                          for b in range(bsz)], axis=0)
    x = _log_sigmoid(ff + jnp.concatenate([b_ref[...]] * bsz, axis=0))
    lane = lax.broadcasted_iota(jnp.int32, x.shape, 1)
    shift = 1
    while shift < seq:
        x = x + jnp.where(lane >= shift, pltpu.roll(x, shift, axis=1), 0.0)
        shift *= 2
    hi, mid, lo = (t.astype(F32) for t in _split3(x * LOG2E))
    ones = jnp.ones((FOX_HEADS, seq), F32)
    pad = jnp.zeros((LANES - FOX_TERMS * FOX_HEADS, seq), F32)
    for b in range(bsz):
        mine = slice(b * FOX_HEADS, (b + 1) * FOX_HEADS)
        terms = jnp.concatenate([hi[mine], mid[mine], lo[mine], ones, pad], axis=0)
        o_ref[b * seq:(b + 1) * seq, :] = terms.T.astype(BF16)


def _foxprep(layer, small, bias, *, bsz, seq):
    whole = lambda i: (0, 0)
    return pl.pallas_call(
        functools.partial(_foxprep_kernel, bsz=bsz),
        grid=(1,),
        in_specs=[
            pl.BlockSpec((bsz * seq, IN_SMALL), whole),
            _layer_spec(layer, (FOX_HEADS, 1)),
        ],
        out_specs=pl.BlockSpec((bsz * seq, LANES), whole),
        out_shape=jax.ShapeDtypeStruct((bsz * seq, LANES), BF16),
        compiler_params=_params("arbitrary"),
        name="foxprep",
    )(small, bias)


def _fox_placement(head, base, *, key):
    r = lax.broadcasted_iota(jnp.int32, (LANES, LANES), 0)
    rel = lax.broadcasted_iota(jnp.int32, (LANES, LANES), 1) - base
    term = r >> HEADS_SHIFT
    mine = ((r & (FOX_HEADS - 1)) == head) & (term < FOX_TERMS)
    is_part = mine & (term < FOX_TERMS - 1)
    is_one = mine & (term == FOX_TERMS - 1)
    n = FOX_TERMS - 1
    if key:
        m = jnp.where(is_part & (rel == term + n), -1.0, jnp.where(is_one & (rel >= 0) & (rel < n), 1.0, 0.0))
    else:
        m = jnp.where(is_part & (rel == term), 1.0, jnp.where(is_one & (rel >= n) & (rel < 2 * n), 1.0, 0.0))
    return m.astype(BF16)


def _fox_kernel(q_ref, k_ref, v_ref, f_ref, o_ref, qx_sc, kx_sc, vx_sc, *, tk):
    seq = k_ref.shape[0]
    spare = (FOX_DH, 0)
    lane = lax.broadcasted_iota(jnp.int32, (seq, LANES), 1)
    q, k, v, f = q_ref[...], k_ref[...], v_ref[...], f_ref[...]
    for h in range(2):
        own = (lane < FOX_DH) if h == 0 else (lane >= FOX_DH)
        head = 2 * pl.program_id(1) + h
        qx_sc[h] = jnp.where(own, q, _dot(f, _fox_placement(head, spare[h], key=False)).astype(BF16))
        kx_sc[h] = jnp.where(own, k, _dot(f, _fox_placement(head, spare[h], key=True)).astype(BF16))
        vx_sc[h] = jnp.where(own, v, jnp.ones_like(v))

    row = lax.broadcasted_iota(jnp.int32, (tk, tk), 0)
    col = lax.broadcasted_iota(jnp.int32, (tk, tk), 1)
    causal = col <= row

    items = [(h, r0, c0) for c0 in range(0, seq, tk) for h in range(2) for r0 in range(c0, seq, tk)]

    def logits(item):
        h, r0, c0 = item
        s = _dot_nt(qx_sc[h, r0:r0 + tk, :], kx_sc[h, c0:c0 + tk, :])
        return jnp.where(causal, s, -jnp.inf) if r0 == c0 else s

    state = {}
    s_next = logits(items[0])
    for n, (h, r0, c0) in enumerate(items):
        s = s_next
        if n + 1 < len(items):
            s_next = logits(items[n + 1])
        vx = vx_sc[h, c0:c0 + tk, :]
        m_new = jnp.max(s, axis=1, keepdims=True)
        if c0 == 0:
            acc = _dot(jnp.exp2(s - m_new).astype(BF16), vx)
        else:
            m_old, acc_old = state[h, r0]
            m_new = jnp.maximum(m_old, m_new)
            acc = jnp.exp2(m_old - m_new) * acc_old + _dot(jnp.exp2(s - m_new).astype(BF16), vx)
        state[h, r0] = (m_new, acc)

    first = lax.broadcasted_iota(jnp.int32, (tk, LANES), 1) < FOX_DH
    for r0 in range(0, seq, tk):
        a0, a1 = state[0, r0][1], state[1, r0][1]
        o_ref[r0:r0 + tk, :] = jnp.where(first, a0 / pltpu.roll(a0, FOX_DH, axis=1),
                                         a1 / pltpu.roll(a1, FOX_DH, axis=1)).astype(BF16)


def _fox(fox, f, *, bsz, seq, tk):
    m = fox.shape[0]
    return pl.pallas_call(
        functools.partial(_fox_kernel, tk=tk),
        grid=(bsz, FOX_PAIRS),
        in_specs=[
            pl.BlockSpec((seq, LANES), lambda b, p: (b, p)),
            pl.BlockSpec((seq, LANES), lambda b, p: (b, FOX_PAIRS + p)),
            pl.BlockSpec((seq, LANES), lambda b, p: (b, 2 * FOX_PAIRS + p)),
            pl.BlockSpec((seq, LANES), lambda b, p: (b, 0)),
        ],
        out_specs=pl.BlockSpec((seq, LANES), lambda b, p: (b, p)),
        out_shape=jax.ShapeDtypeStruct((m, FOX_W), BF16),
        scratch_shapes=[pltpu.VMEM((2, seq, LANES), BF16)] * 3,
        compiler_params=_params("parallel", "parallel"),
        name="fox",
    )(fox, fox, fox, f)


def _s5prep_kernel(are_ref, aim_ref, ldt_ref, bre_ref, bim_ref, abre_ref, abim_ref, bbre_ref, bbim_ref):
    lam_re = jnp.minimum(are_ref[...], -1e-4)
    lam_im = aim_ref[...]
    dt = jnp.exp(ldt_ref[...])
    mag = jnp.exp(lam_re * dt)
    ab_re = mag * jnp.cos(lam_im * dt)
    ab_im = mag * jnp.sin(lam_im * dt)
    den = lam_re * lam_re + lam_im * lam_im
    z_re = ((ab_re - 1.0) * lam_re + ab_im * lam_im) / den
    z_im = (ab_im * lam_re - (ab_re - 1.0) * lam_im) / den
    br, bi = bre_ref[...], bim_ref[...]
    abre_ref[...] = ab_re
    abim_ref[...] = ab_im
    bbre_ref[...] = z_re * br - z_im * bi
    bbim_ref[...] = z_re * bi + z_im * br


def _s5prep(a_re, a_im, log_dt, b_re, b_im):
    depth = a_re.shape[0]
    col = lambda a: a.reshape(depth, S5_NSTATE, 1)
    ldt = jnp.broadcast_to(log_dt[:, :, None], (depth, S5_GROUPS, S5_STATE))
    vec = pl.BlockSpec((None, S5_NSTATE, 1), lambda l: (l, 0, 0))
    mat = pl.BlockSpec((None, S5_NSTATE, S5_CH), lambda l: (l, 0, 0))
    return pl.pallas_call(
        _s5prep_kernel,
        grid=(depth,),
        in_specs=[vec, vec, vec, mat, mat],
        out_specs=[vec, vec, mat, mat],
        out_shape=[jax.ShapeDtypeStruct((depth, S5_NSTATE, 1), F32)] * 2
        + [jax.ShapeDtypeStruct((depth, S5_NSTATE, S5_CH), F32)] * 2,
        compiler_params=_params("parallel"),
        name="s5prep",
    )(col(a_re), col(a_im), col(ldt), b_re.reshape(depth, S5_NSTATE, S5_CH), b_im.reshape(depth, S5_NSTATE, S5_CH))


def _s5_kernel(u_ref, abre_ref, abim_ref, bre_ref, bim_ref, cre_ref, cim_ref, d_ref, gw_ref, gb_ref, o_ref,
               sre_sc, sim_sc, u_sc, y_sc, *, bsz, ts):
    @pl.when(pl.program_id(0) == 0)
    def _():
        sre_sc[...] = jnp.zeros_like(sre_sc)
        sim_sc[...] = jnp.zeros_like(sim_sc)

    tb = u_ref.shape[0]
    slabs = S5_WIDTH // LANES
    for b in range(bsz):
        for j in range(slabs):
            lanes = slice(b * S5_WIDTH + j * LANES, b * S5_WIDTH + (j + 1) * LANES)
            u_sc[j, pl.ds(b, tb, stride=bsz), :] = u_ref[:, lanes]

    rows = ts * bsz
    n_sub = tb // ts
    a_re = jnp.broadcast_to(abre_ref[...], (bsz, S5_NSTATE))
    a_im = jnp.broadcast_to(abim_ref[...], (bsz, S5_NSTATE))
    us = [jnp.concatenate([u_sc[j, k * rows:(k + 1) * rows, :] for j in range(slabs)], axis=1) for k in range(n_sub)]

    bu = []
    for u in us:
        ub = u.astype(BF16)
        bu.append((_dot(ub, bre_ref[...]), _dot(ub, bim_ref[...])))

    s_re, s_im = sre_sc[...], sim_sc[...]
    for k in range(n_sub):
        bu_re, bu_im = bu[k]
        x_re, x_im = [], []
        for t in range(ts):
            r = slice(t * bsz, (t + 1) * bsz)
            s_re, s_im = (a_re * s_re - a_im * s_im + bu_re[r, :], a_re * s_im + a_im * s_re + bu_im[r, :])
            x_re.append(s_re)
            x_im.append(s_im)
        x_re = jnp.concatenate(x_re, axis=0).astype(BF16)
        x_im = jnp.concatenate(x_im, axis=0).astype(BF16)
        y = _dot(x_re, cre_ref[...]) - _dot(x_im, cim_ref[...])
        y = jax.nn.gelu(y + d_ref[...] * us[k], approximate=True)
        y = y * jax.nn.sigmoid(_dot(y.astype(BF16), gw_ref[...]) + gb_ref[...])
        for j in range(slabs):
            y_sc[j, k * rows:(k + 1) * rows, :] = y[:, j * LANES:(j + 1) * LANES]
    sre_sc[...] = s_re
    sim_sc[...] = s_im

    for b in range(bsz):
        for j in range(slabs):
            lanes = slice(b * S5_WIDTH + j * LANES, b * S5_WIDTH + (j + 1) * LANES)
            o_ref[:, lanes] = y_sc[j, pl.ds(b, tb, stride=bsz), :].astype(BF16)


def _s5(layer, u, ab_re, ab_im, b_re, b_im, c_re, c_im, d, glu_w, glu_b, *, bsz, seq, tb, ts):
    rows = tb * bsz
    slabs = S5_WIDTH // LANES
    return pl.pallas_call(
        functools.partial(_s5_kernel, bsz=bsz, ts=ts),
        grid=(seq // tb,),
        in_specs=[
            pl.BlockSpec((tb, bsz * S5_WIDTH), lambda i: (i, 0)),
            _layer_spec(layer, (1, S5_NSTATE)),
            _layer_spec(layer, (1, S5_NSTATE)),
            _layer_spec(layer, (S5_WIDTH, S5_NSTATE)),
            _layer_spec(layer, (S5_WIDTH, S5_NSTATE)),
            _layer_spec(layer, (S5_NSTATE, S5_WIDTH)),
            _layer_spec(layer, (S5_NSTATE, S5_WIDTH)),
            _layer_spec(layer, (1, S5_WIDTH)),
            _layer_spec(layer, (S5_WIDTH, S5_WIDTH)),
            _layer_spec(layer, (1, S5_WIDTH)),
        ],
        out_specs=pl.BlockSpec((tb, bsz * S5_WIDTH), lambda i: (i, 0)),
        out_shape=jax.ShapeDtypeStruct((seq, bsz * S5_WIDTH), BF16),
        scratch_shapes=[pltpu.VMEM((bsz, S5_NSTATE), F32), pltpu.VMEM((bsz, S5_NSTATE), F32),
                        pltpu.VMEM((slabs, rows, LANES), F32), pltpu.VMEM((slabs, rows, LANES), F32)],
        compiler_params=_params("arbitrary"),
        name="s5",
    )(u, ab_re, ab_im, b_re, b_im, c_re, c_im, d, glu_w, glu_b)


def _merge_kernel(x_ref, gla_ref, s5_ref, fox_ref, pre_g_ref, wgate_ref, wgla_ref, ws5_ref, wfox_ref, wo_ref,
                  post_g_ref, o_ref):
    tm, d = x_ref.shape
    blocks = [slice(r0, r0 + tm // EDGE_BLOCKS) for r0 in range(0, tm, tm // EDGE_BLOCKS)]
    branches = ((gla_ref, wgla_ref), (s5_ref, ws5_ref), (fox_ref, wfox_ref))

    hs, first = [], []
    for rows in blocks:
        hs.append(_rms(x_ref[rows, :], pre_g_ref[...]).astype(BF16))
        first.append(_dot(hs[-1], wgate_ref[:, 0:d]))
    h = jnp.concatenate(hs, axis=0)
    logits = jnp.concatenate(first, axis=0)
    mix = None
    for n, (b_ref, w_ref) in enumerate(branches[:-1]):
        term = jax.nn.sigmoid(logits) * _dot(b_ref[...], w_ref[...])
        mix = term if mix is None else mix + term
        logits = _dot(h, wgate_ref[:, (n + 1) * d:(n + 2) * d])
    up = _dot(branches[-1][0][...], branches[-1][1][...])

    for rows in blocks:
        mixed = mix[rows, :] + jax.nn.sigmoid(logits[rows, :]) * up[rows, :]
        y = _dot(mixed.astype(BF16), wo_ref[...])
        o_ref[rows, :] = x_ref[rows, :] + _rms(y, post_g_ref[...])


def _merge(layer, x, gla_o, s5_o, fox_o, pre_g, w_gate, w_gla, w_s5, w_fox, w_o, post_g, *, bsz, seq, tm):
    m, d = x.shape
    nl = seq // tm
    row = lambda b, l: (b * nl + l, 0)
    whole = lambda shape: pl.BlockSpec(shape, lambda b, l: (0, 0), pipeline_mode=RESIDENT)
    return pl.pallas_call(
        _merge_kernel,
        grid=(bsz, nl),
        in_specs=[
            pl.BlockSpec((tm, d), row),
            pl.BlockSpec((tm, GLA_V), row),
            pl.BlockSpec((tm, S5_WIDTH), lambda b, l: (l, b)),
            pl.BlockSpec((tm, FOX_W), row),
            _layer_spec(layer, (1, d)),
            _layer_spec(layer, (d, N_BRANCH * d), pipeline_mode=RESIDENT),
            whole((GLA_V, d)),
            whole((S5_WIDTH, d)),
            whole((FOX_W, d)),
            whole((d, d)),
            _layer_spec(layer, (1, d)),
        ],
        out_specs=pl.BlockSpec((tm, d), row),
        out_shape=jax.ShapeDtypeStruct((m, d), F32),
        compiler_params=_params("parallel", "parallel"),
        name="merge",
    )(x, gla_o, s5_o, fox_o, pre_g, w_gate, w_gla, w_s5, w_fox, w_o, post_g)


def _memkv_kernel(mem_ref, g_ref, w_ref, *rest):
    n_cast = (len(rest) - 2) // 2
    k_ref, v_ref = rest[n_cast], rest[n_cast + 1]
    _cast_slabs(rest[:n_cast] + rest[n_cast + 2:])
    d = mem_ref.shape[1]
    h = _rms(mem_ref[...], g_ref[...]).astype(BF16)
    k_ref[...] = _dot(h, w_ref[:, 0:d]).astype(BF16)
    v_ref[...] = _dot(h, w_ref[:, d:2 * d]).astype(BF16)


def _memkv(mem, g, w_kv, *, tm, cast):
    m, d = mem.shape
    depth = w_kv.shape[0]
    tiles = m // tm
    per_layer = lambda block: pl.BlockSpec((None,) + block, lambda l, i: (l, 0, 0))
    out = pl.BlockSpec((None, tm, d), lambda l, i: (l, i, 0))
    extra = [_cast_specs(cast[0], w, depth * tiles, lambda l, i: l * tiles + i) for w in cast[1:]]
    outs = pl.pallas_call(
        _memkv_kernel,
        grid=(depth, tiles),
        in_specs=[pl.BlockSpec((tm, d), lambda l, i: (i, 0)), per_layer((1, d)), per_layer((d, 2 * d))]
        + [e[0] for e in extra],
        out_specs=[out, out] + [e[1] for e in extra],
        out_shape=[jax.ShapeDtypeStruct((depth, m, d), BF16)] * 2 + [e[2] for e in extra],
        compiler_params=_params("parallel", "parallel"),
        name="memkv",
    )(mem, g, w_kv, *cast[1:])
    return outs[0], outs[1], tuple(outs[2:])


def _xattn_kernel(x_ref, k_ref, v_ref, pre_g_ref, wq_ref, wo_ref, post_g_ref, *rest):
    o_ref, cast_refs = rest[len(rest) // 2], rest[:len(rest) // 2] + rest[len(rest) // 2 + 1:]
    _cast_slabs(cast_refs)
    tm, d = x_ref.shape
    dh = d // XA_HEADS
    blocks = [slice(r0, r0 + tm // EDGE_BLOCKS) for r0 in range(0, tm, tm // EDGE_BLOCKS)]
    cols = [slice(n * dh, (n + 1) * dh) for n in range(XA_HEADS)]

    def softmax(s):
        e = jnp.exp(s - jnp.max(s, axis=1, keepdims=True))
        return (e / jnp.sum(e, axis=1, keepdims=True)).astype(BF16)

    qs = []
    for rows in blocks:
        h = _rms(x_ref[rows, :], pre_g_ref[...]).astype(BF16)
        qs.append((_dot(h, wq_ref[...]) * (dh ** -0.5)).astype(BF16))
    logits = [[_dot_nt(q[:, c], k_ref[:, c]) for c in cols] for q in qs]
    probs = [[softmax(s) for s in per_head] for per_head in logits]
    heads = [jnp.concatenate([_dot(p, v_ref[:, c]).astype(BF16) for p, c in zip(per_head, cols)], axis=1)
             for per_head in probs]
    for rows, attended in zip(blocks, heads):
        y = _dot(attended, wo_ref[...])
        o_ref[rows, :] = x_ref[rows, :] + _rms(y, post_g_ref[...])


def _xattn(layer, x, k, v, pre_g, w_q, w_o, post_g, *, bsz, seq, n_mem, tm, cast):
    m, d = x.shape
    nl = seq // tm
    row = lambda b, l: (b * nl + l, 0)
    extra = [_cast_specs(cast[0], w, bsz * nl, lambda b, l: b * nl + l) for w in cast[1:]]
    outs = pl.pallas_call(
        _xattn_kernel,
        grid=(bsz, nl),
        in_specs=[
            pl.BlockSpec((tm, d), row),
            _layer_spec(layer, (n_mem, d), lambda b, l: (b, 0)),
            _layer_spec(layer, (n_mem, d), lambda b, l: (b, 0)),
            _layer_spec(layer, (1, d)),
            pl.BlockSpec((d, d), lambda b, l: (0, 0), pipeline_mode=RESIDENT),
            pl.BlockSpec((d, d), lambda b, l: (0, 0), pipeline_mode=RESIDENT),
            _layer_spec(layer, (1, d)),
        ] + [e[0] for e in extra],
        out_specs=[pl.BlockSpec((tm, d), row)] + [e[1] for e in extra],
        out_shape=[jax.ShapeDtypeStruct((m, d), F32)] + [e[2] for e in extra],
        compiler_params=_params("parallel", "parallel"),
        name="xattn",
    )(x, k, v, pre_g, w_q, w_o, post_g, *cast[1:])
    return outs[0], tuple(outs[1:])


def _tile(n, want):
    t = min(n, want)
    while n % t:
        t -= 1
    return t


def _block_diag(t):
    depth, g, r, c = t.shape
    eye = jnp.eye(g, dtype=t.dtype)
    return jnp.einsum("lgrc,gk->lgrkc", t, eye).reshape(depth, g * r, g * c)


def kernel(x, mem, ffn1_pre_g, ffn1_w_gu, ffn1_w_down, ffn1_post_g, mix_pre_g, w_in, gla_gate_w, gla_gate_b, gla_norm_g, w_gla_up, s5_a_re, s5_a_im, s5_log_dt, s5_b_re, s5_b_im, s5_c_re, s5_c_im, s5_d, s5_glu_w, s5_glu_b, w_s5_up, fox_f_b, w_fox_up, w_mix_out, mix_post_g, xa_pre_g, xa_mem_g, xa_w_q, xa_w_kv, xa_w_o, xa_post_g, ffn2_pre_g, ffn2_w_gu, ffn2_w_down, ffn2_post_g):
    bsz, seq, d = x.shape
    n_mem = mem.shape[1]
    depth = w_in.shape[0]
    d_ff = ffn1_w_down.shape[1]
    m = bsz * seq
    assert seq % CHUNK == 0 and d % LANES == 0

    tm_ffn = _tile(m, 1024)
    tf = _tile(d_ff // LANES, 2) * LANES
    tm = _tile(seq, 1024)
    tk = _tile(seq, 512)
    ts = _tile(seq, 128)
    tb = 2 * ts if seq % (2 * ts) == 0 else ts

    bf = lambda a: a.astype(BF16)
    vec = lambda a: a.reshape(depth, 1, a.shape[-1])

    sizes = (GLA_QK, GLA_QK, GLA_V, GLA_V, GLA_RANK, S5_WIDTH, FOX_W, FOX_W, FOX_W, FOX_HEADS, N_BRANCH * d)
    offs = [0]
    for s in sizes:
        offs.append(offs[-1] + s)
    w_in_b = bf(w_in)
    seg = lambda i: w_in_b[:, :, offs[i]:offs[i + 1]]
    pad = jnp.zeros((depth, d, IN_SMALL - GLA_RANK - FOX_HEADS), BF16)
    w_proj = jnp.concatenate([seg(0), seg(1), seg(2), seg(3), seg(5), seg(6), seg(7), seg(8), seg(4), seg(9), pad],
                             axis=2)
    w_gate = seg(10)
    gla_wg = bf(jnp.concatenate([gla_gate_w, jnp.zeros((depth, IN_SMALL - GLA_RANK, GLA_QK), gla_gate_w.dtype)],
                                axis=1))

    ab_re, ab_im, bb_re, bb_im = _s5prep(s5_a_re, s5_a_im, s5_log_dt, s5_b_re, s5_b_im)
    grp = lambda t: t.reshape(depth, S5_GROUPS, S5_STATE, S5_CH)
    s5_bre = bf(_block_diag(grp(bb_re).transpose(0, 1, 3, 2)))
    s5_bim = bf(_block_diag(grp(bb_im).transpose(0, 1, 3, 2)))
    s5_cre = bf(_block_diag(s5_c_re.transpose(0, 1, 3, 2)))
    s5_cim = bf(_block_diag(s5_c_im.transpose(0, 1, 3, 2)))
    ab_re = ab_re.reshape(depth, 1, S5_NSTATE)
    ab_im = ab_im.reshape(depth, 1, S5_NSTATE)

    glu_w, xa_kv = bf(s5_glu_w), bf(xa_w_kv)
    fox_b = fox_f_b.reshape(depth, FOX_HEADS, 1)

    xs = x.reshape(m, d)
    mk, mv, ffn1_w = _memkv(mem.reshape(bsz * n_mem, d), vec(xa_mem_g), xa_kv, tm=_tile(bsz * n_mem, 512),
                            cast=(0, ffn1_w_gu, ffn1_w_down))
    for l in range(depth):
        xs, (gla_up, s5_up, fox_up, mix_out, xa_q, xa_o) = _ffn(
            l, xs, vec(ffn1_pre_g), *ffn1_w, vec(ffn1_post_g), tm=tm_ffn, tf=tf,
            cast=(l, w_gla_up, w_s5_up, w_fox_up, w_mix_out, xa_w_q, xa_w_o))

        qk, vr, su, fox, small = _inproj(l, xs, vec(mix_pre_g), w_proj, bsz=bsz, seq=seq, tm=tm)
        gla_o = _gla(l, qk, vr, small, gla_wg, vec(gla_gate_b), vec(gla_norm_g), bsz=bsz, seq=seq)
        f_cols = _foxprep(l, small, fox_b, bsz=bsz, seq=seq)
        fox_o = _fox(fox, f_cols, bsz=bsz, seq=seq, tk=tk)
        s5_o = _s5(l, su, ab_re, ab_im, s5_bre, s5_bim, s5_cre, s5_cim,
                   vec(s5_d), glu_w, vec(s5_glu_b), bsz=bsz, seq=seq, tb=tb, ts=ts)
        xs = _merge(l, xs, gla_o, s5_o, fox_o, vec(mix_pre_g), w_gate, gla_up,
                    s5_up, fox_up, mix_out, vec(mix_post_g), bsz=bsz, seq=seq, tm=tm)

        xs, ffn2_w = _xattn(l, xs, mk, mv, vec(xa_pre_g), xa_q, xa_o, vec(xa_post_g), bsz=bsz, seq=seq, n_mem=n_mem,
                            tm=tm, cast=(l, ffn2_w_gu, ffn2_w_down))

        ahead = (l + 1, ffn1_w_gu, ffn1_w_down) if l + 1 < depth else None
        xs, ffn1_w = _ffn(l, xs, vec(ffn2_pre_g), *ffn2_w, vec(ffn2_post_g), tm=tm_ffn, tf=tf, cast=ahead)
    return xs.reshape(bsz, seq, d)
```

```python
import functools
import math

import jax
import jax.numpy as jnp
from jax import lax
from jax.experimental import pallas as pl
from jax.experimental.pallas import tpu as pltpu

F32 = jnp.float32
BF16 = jnp.bfloat16
EPS = 1e-6
LOG2E = math.log2(math.e)

LANES = 128
SUBLANES = 8
VMEM_LIMIT_BYTES = 56 * 1024 * 1024

EDGE_BLOCKS = 4
CHUNK = 64
GLA_GROUP = 8
GLA_HEADS, GLA_DK, GLA_DV, GLA_RANK, GLA_TAU = 4, 32, 64, 16, 16.0
GLA_QK = GLA_HEADS * GLA_DK
GLA_V = GLA_HEADS * GLA_DV
DK_SHIFT = GLA_DK.bit_length() - 1
DV_SHIFT = GLA_DV.bit_length() - 1
S5_GROUPS, S5_CH, S5_STATE = 16, 16, 64
S5_WIDTH = S5_GROUPS * S5_CH
S5_NSTATE = S5_GROUPS * S5_STATE
FOX_HEADS, FOX_DH = 8, 64
FOX_W = FOX_HEADS * FOX_DH
FOX_PAIRS = FOX_W // LANES
FOX_TERMS = 4
HEADS_SHIFT = FOX_HEADS.bit_length() - 1
XA_HEADS = 4
N_BRANCH = 3

NT_DIMS = (((1,), (1,)), ((), ()))
RESIDENT = pl.Buffered(1)


def _params(*semantics):
    return pltpu.CompilerParams(dimension_semantics=semantics, vmem_limit_bytes=VMEM_LIMIT_BYTES)


def _layer_spec(layer, block, index_map=None, pipeline_mode=None):
    block = tuple(block)
    if index_map is None:
        index_map = lambda *_: (0,) * len(block)
    return pl.BlockSpec((None,) + block, lambda *ids: (layer,) + tuple(index_map(*ids)),
                        pipeline_mode=pipeline_mode)


def _rms(x, g):
    return x * lax.rsqrt(jnp.mean(x * x, axis=-1, keepdims=True) + EPS) * g


def _dot(a, b):
    return jnp.dot(a, b, preferred_element_type=F32)


def _dot_nt(a, b):
    return lax.dot_general(a, b, NT_DIMS, preferred_element_type=F32)


def _log_sigmoid(x):
    return jnp.minimum(x, 0.0) - jnp.log1p(jnp.exp(-jnp.abs(x)))


def _split3(x):
    hi = x.astype(BF16)
    r1 = x - hi.astype(F32)
    mid = r1.astype(BF16)
    lo = (r1 - mid.astype(F32)).astype(BF16)
    return hi, mid, lo


def _cast_specs(layer, stacked, n_steps, step):
    _, rows, cols = stacked.shape
    slab = rows // n_steps
    assert slab * n_steps == rows and slab % (2 * SUBLANES) == 0
    return (pl.BlockSpec((None, slab, cols), lambda *ids: (layer, step(*ids), 0)),
            pl.BlockSpec((slab, cols), lambda *ids: (step(*ids), 0)),
            jax.ShapeDtypeStruct((rows, cols), BF16))


def _cast_slabs(refs):
    half = len(refs) // 2
    for src, dst in zip(refs[:half], refs[half:]):
        dst[...] = src[...].astype(BF16)


def _ffn_kernel(x_ref, pre_g_ref, wgu_ref, wd_ref, post_g_ref, *rest, tf):
    o_ref, cast_refs = rest[len(rest) // 2], rest[:len(rest) // 2] + rest[len(rest) // 2 + 1:]
    _cast_slabs(cast_refs)
    d_ff = wd_ref.shape[0]
    tm = x_ref.shape[0]
    blocks = [slice(r0, r0 + tm // EDGE_BLOCKS) for r0 in range(0, tm, tm // EDGE_BLOCKS)]

    def gate_up(h, c0):
        return _dot(h, wgu_ref[:, c0:c0 + tf]), _dot(h, wgu_ref[:, d_ff + c0:d_ff + c0 + tf])

    hs, first = [], []
    for rows in blocks:
        hs.append(_rms(x_ref[rows, :], pre_g_ref[...]).astype(BF16))
        first.append(gate_up(hs[-1], 0))
    h = jnp.concatenate(hs, axis=0)
    nxt = tuple(jnp.concatenate(t, axis=0) for t in zip(*first))

    acc = None
    for c0 in range(0, d_ff - tf, tf):
        gate, up = nxt
        nxt = gate_up(h, c0 + tf)
        a = (gate * jax.nn.sigmoid(gate) * up).astype(BF16)
        part = _dot(a, wd_ref[c0:c0 + tf, :])
        acc = part if acc is None else acc + part

    gate, up = nxt
    a = (gate * jax.nn.sigmoid(gate) * up).astype(BF16)
    for rows in blocks:
        total = _dot(a[rows, :], wd_ref[d_ff - tf:d_ff, :])
        if acc is not None:
            total = acc[rows, :] + total
        o_ref[rows, :] = x_ref[rows, :] + 0.5 * _rms(total, post_g_ref[...])


def _ffn(layer, x, pre_g, w_gu, w_down, post_g, *, tm, tf, cast=None):
    m, d = x.shape
    d_ff = w_down.shape[0]
    n_steps = m // tm
    extra = [_cast_specs(cast[0], w, n_steps, lambda i: i) for w in cast[1:]] if cast else []
    outs = pl.pallas_call(
        functools.partial(_ffn_kernel, tf=tf),
        grid=(n_steps,),
        in_specs=[
            pl.BlockSpec((tm, d), lambda i: (i, 0)),
            _layer_spec(layer, (1, d)),
            pl.BlockSpec((d, 2 * d_ff), lambda i: (0, 0), pipeline_mode=RESIDENT),
            pl.BlockSpec((d_ff, d), lambda i: (0, 0), pipeline_mode=RESIDENT),
            _layer_spec(layer, (1, d)),
        ] + [e[0] for e in extra],
        out_specs=[pl.BlockSpec((tm, d), lambda i: (i, 0))] + [e[1] for e in extra],
        out_shape=[jax.ShapeDtypeStruct((m, d), F32)] + [e[2] for e in extra],
        compiler_params=_params("parallel"),
        name="ffn",
    )(x, pre_g, w_gu, w_down, post_g, *(cast[1:] if cast else ()))
    return outs[0], tuple(outs[1:])


def _repack_kernel(w_ref, main_ref, gate_ref, small_ref, *, aligned, mid, gate_base, gates, shift_mid, shift_gate):
    rows = w_ref.shape[0]
    block = lambda n: w_ref[:, n * LANES:(n + 1) * LANES]

    def shifted(first, count, shift):
        windows = jnp.concatenate([w_ref[:, (first + i) * LANES:(first + i + 2) * LANES] for i in range(count)], axis=0)
        src = lax.broadcasted_iota(jnp.int32, (2 * LANES, LANES), 0)
        dst = lax.broadcasted_iota(jnp.int32, (2 * LANES, LANES), 1)
        out = _dot(windows, (src == dst + shift).astype(BF16)).astype(BF16)
        return [out[i * rows:(i + 1) * rows, :] for i in range(count)]

    main_ref[:, 0:aligned * LANES] = w_ref[:, 0:aligned * LANES]
    for i, piece in enumerate(shifted(aligned, mid, shift_mid)):
        main_ref[:, (aligned + i) * LANES:(aligned + i + 1) * LANES] = piece
    for i, piece in enumerate(shifted(gate_base, gates, shift_gate)):
        gate_ref[:, i * LANES:(i + 1) * LANES] = piece
    lane = lax.broadcasted_iota(jnp.int32, (rows, LANES), 1)
    small_ref[...] = jnp.where(lane < shift_mid, block(aligned),
                               jnp.where(lane < shift_gate, block(gate_base), jnp.zeros((rows, LANES), BF16)))


def _repack(w, *, rows, aligned, mid, gate_base, gates, shift_mid, shift_gate):
    depth, d, cols = w.shape
    spec = lambda n: pl.BlockSpec((None, rows, n), lambda l, i: (l, i, 0))
    return pl.pallas_call(
        functools.partial(_repack_kernel, aligned=aligned, mid=mid, gate_base=gate_base, gates=gates,
                          shift_mid=shift_mid, shift_gate=shift_gate),
        grid=(depth, d // rows),
        in_specs=[spec(cols)],
        out_specs=[spec((aligned + mid) * LANES), spec(gates * LANES), spec(LANES)],
        out_shape=[jax.ShapeDtypeStruct((depth, d, (aligned + mid) * LANES), BF16),
                   jax.ShapeDtypeStruct((depth, d, gates * LANES), BF16),
                   jax.ShapeDtypeStruct((depth, d, LANES), BF16)],
        compiler_params=_params("parallel", "parallel"),
        name="repack",
    )(w)


IN_QK = 2 * GLA_QK
IN_VR = 2 * GLA_V
IN_SU = S5_WIDTH
IN_FOX = 3 * FOX_W
IN_SMALL = LANES
OFF_VR = IN_QK
OFF_SU = OFF_VR + IN_VR
OFF_FOX = OFF_SU + IN_SU
OFF_SMALL = OFF_FOX + IN_FOX


def _inproj_kernel(x_ref, g_ref, w_ref, ws_ref, qk_ref, vr_ref, su_ref, fox_ref, small_ref):
    tm = x_ref.shape[0]
    hs = []
    for r0 in range(0, tm, tm // EDGE_BLOCKS):
        rows = slice(r0, r0 + tm // EDGE_BLOCKS)
        hs.append(_rms(x_ref[rows, :], g_ref[...]).astype(BF16))
        qk_ref[rows, :] = _dot(hs[-1], w_ref[:, 0:OFF_VR])
    h = jnp.concatenate(hs, axis=0)
    vr_ref[...] = _dot(h, w_ref[:, OFF_VR:OFF_SU]).astype(BF16)
    su_ref[...] = _dot(h, w_ref[:, OFF_SU:OFF_FOX])
    fox_ref[:, 0:FOX_W] = (_dot(h, w_ref[:, OFF_FOX:OFF_FOX + FOX_W]) * (FOX_DH ** -0.5 * LOG2E)).astype(BF16)
    fox_ref[:, FOX_W:IN_FOX] = _dot(h, w_ref[:, OFF_FOX + FOX_W:OFF_SMALL]).astype(BF16)
    small_ref[...] = _dot(h, ws_ref[...])


def _inproj(layer, x, g, w, w_small, *, bsz, seq, tm):
    m, d = x.shape
    nl = seq // tm
    row = lambda b, l: (b * nl + l, 0)
    return pl.pallas_call(
        _inproj_kernel,
        grid=(bsz, nl),
        in_specs=[
            pl.BlockSpec((tm, d), row),
            _layer_spec(layer, (1, d)),
            _layer_spec(layer, (d, OFF_SMALL), pipeline_mode=RESIDENT),
            _layer_spec(layer, (d, IN_SMALL), pipeline_mode=RESIDENT),
        ],
        out_specs=[
            pl.BlockSpec((tm, IN_QK), row),
            pl.BlockSpec((tm, IN_VR), row),
            pl.BlockSpec((tm, IN_SU), lambda b, l: (l, b)),
            pl.BlockSpec((tm, IN_FOX), row),
            pl.BlockSpec((tm, IN_SMALL), row),
        ],
        out_shape=[
            jax.ShapeDtypeStruct((m, IN_QK), F32),
            jax.ShapeDtypeStruct((m, IN_VR), BF16),
            jax.ShapeDtypeStruct((seq, bsz * IN_SU), F32),
            jax.ShapeDtypeStruct((m, IN_FOX), BF16),
            jax.ShapeDtypeStruct((m, IN_SMALL), F32),
        ],
        compiler_params=_params("parallel", "parallel"),
        name="inproj",
    )(x, g, w, w_small)


def _gla_kernel(qk_ref, vr_ref, small_ref, wg_ref, bg_ref, ng_ref, o_ref, la_sc):
    seq = qk_ref.shape[0]
    n_chunks = seq // CHUNK
    c, hq, hv = CHUNK, GLA_QK, GLA_V

    z = _dot(small_ref[...].astype(BF16), wg_ref[...]) + bg_ref[...]
    la_sc[...] = _log_sigmoid(z) * (1.0 / GLA_TAU)

    row = lax.broadcasted_iota(jnp.int32, (c, c), 0)
    col = lax.broadcasted_iota(jnp.int32, (c, c), 1)
    tri = (col <= row).astype(BF16)
    row_x = lax.broadcasted_iota(jnp.int32, (GLA_HEADS * c, c), 0)
    col_x = lax.broadcasted_iota(jnp.int32, (GLA_HEADS * c, c), 1)
    lower = col_x <= (row_x & (c - 1))
    lane_q = lax.broadcasted_iota(jnp.int32, (c, hq), 1)
    lane_v = lax.broadcasted_iota(jnp.int32, (c, hv), 1)
    q_head = [(lane_q >> DK_SHIFT) == h for h in range(GLA_HEADS)]
    v_head = [(lane_v >> DV_SHIFT) == h for h in range(GLA_HEADS)]
    st_row = lax.broadcasted_iota(jnp.int32, (hv, hq), 0)
    st_col = lax.broadcasted_iota(jnp.int32, (hv, hq), 1)
    same_head = (st_row >> DV_SHIFT) == (st_col >> DK_SHIFT)

    def expand(t):
        return jnp.concatenate([jnp.where(q_head[h], t, 0.0) for h in range(GLA_HEADS)], axis=0).astype(BF16)

    gr = lax.broadcasted_iota(jnp.int32, (hv, hv), 0) >> DV_SHIFT
    gc = lax.broadcasted_iota(jnp.int32, (hv, hv), 1) >> DV_SHIFT
    avg = jnp.where(gr == gc, 1.0 / GLA_DV, 0.0).astype(BF16)

    def within_chunks(group):
        stage1 = []
        for i in group:
            hi, mid, lo = _split3(la_sc[i * c:(i + 1) * c, :])
            stage1.append(_dot(tri, hi) + _dot(tri, mid) + _dot(tri, lo))
        stage2 = []
        for i, g in zip(group, stage1):
            rows = slice(i * c, (i + 1) * c)
            g_last = g[c - 1:c, :]
            eg, ieg = jnp.exp(g), jnp.exp(-g)
            qc = qk_ref[rows, 0:hq] * (GLA_DK ** -0.5)
            kc = qk_ref[rows, hq:2 * hq]
            v = vr_ref[rows, 0:hv]
            q_fwd = qc * eg
            a_fwd = _dot_nt(expand(q_fwd), (kc * ieg).astype(BF16))
            a_bwd = _dot_nt(expand(qc * ieg), (kc * eg).astype(BF16))
            kw = (kc * jnp.exp(g_last - g)).astype(BF16)
            ds_t = _dot(v.astype(F32).T.astype(BF16), kw)
            stage2.append((a_fwd, a_bwd, v, q_fwd.astype(BF16), jnp.where(same_head, ds_t, 0.0), jnp.exp(g_last)))
        results = []
        for a_fwd, a_bwd, v, q_fwd, ds_t, decay in stage2:
            attn = jnp.where(lower, a_fwd, a_bwd).astype(BF16)
            p = _dot(attn, v)
            o = jnp.where(v_head[0], p[0:c, :], 0.0)
            for h in range(1, GLA_HEADS):
                o = o + jnp.where(v_head[h], p[h * c:(h + 1) * c, :], 0.0)
            results.append((o, q_fwd, ds_t, decay))
        return results

    def finish(group, results, s_t):
        outs = []
        for o, q_fwd, ds_t, decay in results:
            outs.append(o + _dot_nt(q_fwd, s_t.astype(BF16)))
            s_t = s_t * decay + ds_t
        o = jnp.concatenate(outs, axis=0)
        rows = slice(group[0] * c, (group[-1] + 1) * c)
        hi, mid, lo = _split3(o * o)
        ms = _dot(hi, avg) + _dot(mid, avg) + _dot(lo, avg)
        r = vr_ref[rows, hv:2 * hv].astype(F32)
        o_ref[rows, :] = (o * lax.rsqrt(ms + EPS) * ng_ref[...] * (r * jax.nn.sigmoid(r))).astype(BF16)
        return s_t

    groups = [range(first, min(first + GLA_GROUP, n_chunks)) for first in range(0, n_chunks, GLA_GROUP)]
    s_t = jnp.zeros((hv, hq), F32)
    pending = None
    for group in groups:
        results = within_chunks(group)
        if pending is not None:
            s_t = finish(*pending, s_t)
        pending = (group, results)
    finish(*pending, s_t)


def _gla(layer, qk, vr, small, wg, bg, ng, *, bsz, seq):
    m = qk.shape[0]
    row = lambda b: (b, 0)
    return pl.pallas_call(
        _gla_kernel,
        grid=(bsz,),
        in_specs=[
            pl.BlockSpec((seq, IN_QK), row),
            pl.BlockSpec((seq, IN_VR), row),
            pl.BlockSpec((seq, IN_SMALL), row),
            _layer_spec(layer, (IN_SMALL, GLA_QK)),
            _layer_spec(layer, (1, GLA_QK)),
            _layer_spec(layer, (1, GLA_V)),
        ],
        out_specs=pl.BlockSpec((seq, GLA_V), row),
        out_shape=jax.ShapeDtypeStruct((m, GLA_V), BF16),
        scratch_shapes=[pltpu.VMEM((seq, GLA_QK), F32)],
        compiler_params=_params("parallel"),
        name="gla",
    )(qk, vr, small, wg, bg, ng)


def _foxprep_kernel(small_ref, b_ref, o_ref, *, bsz):
    seq = small_ref.shape[0] // bsz
    ff = jnp.concatenate([small_ref[b * seq:(b + 1) * seq, :].T[GLA_RANK:GLA_RANK + FOX_HEADS, :]
                          for b in range(bsz)], axis=0)
    x = _log_sigmoid(ff + jnp.concatenate([b_ref[...]] * bsz, axis=0))
    lane = lax.broadcasted_iota(jnp.int32, x.shape, 1)
    shift = 1
    while shift < seq:
        x = x + jnp.where(lane >= shift, pltpu.roll(x, shift, axis=1), 0.0)
        shift *= 2
    hi, mid, lo = (t.astype(F32) for t in _split3(x * LOG2E))
    ones = jnp.ones((FOX_HEADS, seq), F32)
    pad = jnp.zeros((LANES - FOX_TERMS * FOX_HEADS, seq), F32)
    for b in range(bsz):
        mine = slice(b * FOX_HEADS, (b + 1) * FOX_HEADS)
        terms = jnp.concatenate([hi[mine], mid[mine], lo[mine], ones, pad], axis=0)
        o_ref[b * seq:(b + 1) * seq, :] = terms.T.astype(BF16)


def _foxprep(layer, small, bias, *, bsz, seq):
    whole = lambda i: (0, 0)
    return pl.pallas_call(
        functools.partial(_foxprep_kernel, bsz=bsz),
        grid=(1,),
        in_specs=[
            pl.BlockSpec((bsz * seq, IN_SMALL), whole),
            _layer_spec(layer, (FOX_HEADS, 1)),
        ],
        out_specs=pl.BlockSpec((bsz * seq, LANES), whole),
        out_shape=jax.ShapeDtypeStruct((bsz * seq, LANES), BF16),
        compiler_params=_params("arbitrary"),
        name="foxprep",
    )(small, bias)


def _fox_placement(head, base, *, key):
    r = lax.broadcasted_iota(jnp.int32, (LANES, LANES), 0)
    rel = lax.broadcasted_iota(jnp.int32, (LANES, LANES), 1) - base
    term = r >> HEADS_SHIFT
    mine = ((r & (FOX_HEADS - 1)) == head) & (term < FOX_TERMS)
    is_part = mine & (term < FOX_TERMS - 1)
    is_one = mine & (term == FOX_TERMS - 1)
    n = FOX_TERMS - 1
    if key:
        m = jnp.where(is_part & (rel == term + n), -1.0, jnp.where(is_one & (rel >= 0) & (rel < n), 1.0, 0.0))
    else:
        m = jnp.where(is_part & (rel == term), 1.0, jnp.where(is_one & (rel >= n) & (rel < 2 * n), 1.0, 0.0))
    return m.astype(BF16)


def _fox_kernel(q_ref, k_ref, v_ref, f_ref, o_ref, qx_sc, kx_sc, vx_sc, *, tk):
    seq = k_ref.shape[0]
    spare = (FOX_DH, 0)
    lane = lax.broadcasted_iota(jnp.int32, (seq, LANES), 1)
    q, k, v, f = q_ref[...], k_ref[...], v_ref[...], f_ref[...]
    for h in range(2):
        own = (lane < FOX_DH) if h == 0 else (lane >= FOX_DH)
        head = 2 * pl.program_id(1) + h
        qx_sc[h] = jnp.where(own, q, _dot(f, _fox_placement(head, spare[h], key=False)).astype(BF16))
        kx_sc[h] = jnp.where(own, k, _dot(f, _fox_placement(head, spare[h], key=True)).astype(BF16))
        vx_sc[h] = jnp.where(own, v, jnp.ones_like(v))

    row = lax.broadcasted_iota(jnp.int32, (tk, tk), 0)
    col = lax.broadcasted_iota(jnp.int32, (tk, tk), 1)
    causal = col <= row

    items = [(h, r0, c0) for c0 in range(0, seq, tk) for h in range(2) for r0 in range(c0, seq, tk)]

    def logits(item):
        h, r0, c0 = item
        s = _dot_nt(qx_sc[h, r0:r0 + tk, :], kx_sc[h, c0:c0 + tk, :])
        return jnp.where(causal, s, -jnp.inf) if r0 == c0 else s

    state = {}
    s_next = logits(items[0])
    for n, (h, r0, c0) in enumerate(items):
        s = s_next
        if n + 1 < len(items):
            s_next = logits(items[n + 1])
        vx = vx_sc[h, c0:c0 + tk, :]
        m_new = jnp.max(s, axis=1, keepdims=True)
        if c0 == 0:
            acc = _dot(jnp.exp2(s - m_new).astype(BF16), vx)
        else:
            m_old, acc_old = state[h, r0]
            m_new = jnp.maximum(m_old, m_new)
            acc = jnp.exp2(m_old - m_new) * acc_old + _dot(jnp.exp2(s - m_new).astype(BF16), vx)
        state[h, r0] = (m_new, acc)

    first = lax.broadcasted_iota(jnp.int32, (tk, LANES), 1) < FOX_DH
    for r0 in range(0, seq, tk):
        a0, a1 = state[0, r0][1], state[1, r0][1]
        o_ref[r0:r0 + tk, :] = jnp.where(first, a0 / pltpu.roll(a0, FOX_DH, axis=1),
                                         a1 / pltpu.roll(a1, FOX_DH, axis=1)).astype(BF16)


def _fox(fox, f, *, bsz, seq, tk):
    m = fox.shape[0]
    return pl.pallas_call(
        functools.partial(_fox_kernel, tk=tk),
        grid=(bsz, FOX_PAIRS),
        in_specs=[
            pl.BlockSpec((seq, LANES), lambda b, p: (b, p)),
            pl.BlockSpec((seq, LANES), lambda b, p: (b, FOX_PAIRS + p)),
            pl.BlockSpec((seq, LANES), lambda b, p: (b, 2 * FOX_PAIRS + p)),
            pl.BlockSpec((seq, LANES), lambda b, p: (b, 0)),
        ],
        out_specs=pl.BlockSpec((seq, LANES), lambda b, p: (b, p)),
        out_shape=jax.ShapeDtypeStruct((m, FOX_W), BF16),
        scratch_shapes=[pltpu.VMEM((2, seq, LANES), BF16)] * 3,
        compiler_params=_params("parallel", "parallel"),
        name="fox",
    )(fox, fox, fox, f)


def _s5prep_kernel(are_ref, aim_ref, ldt_ref, bre_ref, bim_ref, abre_ref, abim_ref, bbre_ref, bbim_ref):
    lam_re = jnp.minimum(are_ref[...], -1e-4)
    lam_im = aim_ref[...]
    dt = jnp.exp(ldt_ref[...])
    mag = jnp.exp(lam_re * dt)
    ab_re = mag * jnp.cos(lam_im * dt)
    ab_im = mag * jnp.sin(lam_im * dt)
    den = lam_re * lam_re + lam_im * lam_im
    z_re = ((ab_re - 1.0) * lam_re + ab_im * lam_im) / den
    z_im = (ab_im * lam_re - (ab_re - 1.0) * lam_im) / den
    br, bi = bre_ref[...], bim_ref[...]
    abre_ref[...] = ab_re
    abim_ref[...] = ab_im
    bbre_ref[...] = z_re * br - z_im * bi
    bbim_ref[...] = z_re * bi + z_im * br


def _s5prep(a_re, a_im, log_dt, b_re, b_im):
    depth = a_re.shape[0]
    col = lambda a: a.reshape(depth, S5_NSTATE, 1)
    ldt = jnp.broadcast_to(log_dt[:, :, None], (depth, S5_GROUPS, S5_STATE))
    vec = pl.BlockSpec((None, S5_NSTATE, 1), lambda l: (l, 0, 0))
    mat = pl.BlockSpec((None, S5_NSTATE, S5_CH), lambda l: (l, 0, 0))
    return pl.pallas_call(
        _s5prep_kernel,
        grid=(depth,),
        in_specs=[vec, vec, vec, mat, mat],
        out_specs=[vec, vec, mat, mat],
        out_shape=[jax.ShapeDtypeStruct((depth, S5_NSTATE, 1), F32)] * 2
        + [jax.ShapeDtypeStruct((depth, S5_NSTATE, S5_CH), F32)] * 2,
        compiler_params=_params("parallel"),
        name="s5prep",
    )(col(a_re), col(a_im), col(ldt), b_re.reshape(depth, S5_NSTATE, S5_CH), b_im.reshape(depth, S5_NSTATE, S5_CH))


def _s5_kernel(u_ref, abre_ref, abim_ref, bre_ref, bim_ref, cre_ref, cim_ref, d_ref, gw_ref, gb_ref, o_ref,
               sre_sc, sim_sc, u_sc, y_sc, *, bsz, ts):
    @pl.when(pl.program_id(0) == 0)
    def _():
        sre_sc[...] = jnp.zeros_like(sre_sc)
        sim_sc[...] = jnp.zeros_like(sim_sc)

    tb = u_ref.shape[0]
    slabs = S5_WIDTH // LANES
    for b in range(bsz):
        for j in range(slabs):
            lanes = slice(b * S5_WIDTH + j * LANES, b * S5_WIDTH + (j + 1) * LANES)
            u_sc[j, pl.ds(b, tb, stride=bsz), :] = u_ref[:, lanes]

    rows = ts * bsz
    n_sub = tb // ts
    a_re = jnp.broadcast_to(abre_ref[...], (bsz, S5_NSTATE))
    a_im = jnp.broadcast_to(abim_ref[...], (bsz, S5_NSTATE))
    us = [jnp.concatenate([u_sc[j, k * rows:(k + 1) * rows, :] for j in range(slabs)], axis=1) for k in range(n_sub)]

    bu = []
    for u in us:
        ub = u.astype(BF16)
        bu.append((_dot(ub, bre_ref[...]), _dot(ub, bim_ref[...])))

    s_re, s_im = sre_sc[...], sim_sc[...]
    for k in range(n_sub):
        bu_re, bu_im = bu[k]
        x_re, x_im = [], []
        for t in range(ts):
            r = slice(t * bsz, (t + 1) * bsz)
            s_re, s_im = (a_re * s_re - a_im * s_im + bu_re[r, :], a_re * s_im + a_im * s_re + bu_im[r, :])
            x_re.append(s_re)
            x_im.append(s_im)
        x_re = jnp.concatenate(x_re, axis=0).astype(BF16)
        x_im = jnp.concatenate(x_im, axis=0).astype(BF16)
        y = _dot(x_re, cre_ref[...]) - _dot(x_im, cim_ref[...])
        y = jax.nn.gelu(y + d_ref[...] * us[k], approximate=True)
        y = y * jax.nn.sigmoid(_dot(y.astype(BF16), gw_ref[...]) + gb_ref[...])
        for j in range(slabs):
            y_sc[j, k * rows:(k + 1) * rows, :] = y[:, j * LANES:(j + 1) * LANES]
    sre_sc[...] = s_re
    sim_sc[...] = s_im

    for b in range(bsz):
        for j in range(slabs):
            lanes = slice(b * S5_WIDTH + j * LANES, b * S5_WIDTH + (j + 1) * LANES)
            o_ref[:, lanes] = y_sc[j, pl.ds(b, tb, stride=bsz), :].astype(BF16)


def _s5(layer, u, ab_re, ab_im, b_re, b_im, c_re, c_im, d, glu_w, glu_b, *, bsz, seq, tb, ts):
    rows = tb * bsz
    slabs = S5_WIDTH // LANES
    return pl.pallas_call(
        functools.partial(_s5_kernel, bsz=bsz, ts=ts),
        grid=(seq // tb,),
        in_specs=[
            pl.BlockSpec((tb, bsz * S5_WIDTH), lambda i: (i, 0)),
            _layer_spec(layer, (1, S5_NSTATE)),
            _layer_spec(layer, (1, S5_NSTATE)),
            _layer_spec(layer, (S5_WIDTH, S5_NSTATE)),
            _layer_spec(layer, (S5_WIDTH, S5_NSTATE)),
            _layer_spec(layer, (S5_NSTATE, S5_WIDTH)),
            _layer_spec(layer, (S5_NSTATE, S5_WIDTH)),
            _layer_spec(layer, (1, S5_WIDTH)),
            _layer_spec(layer, (S5_WIDTH, S5_WIDTH)),
            _layer_spec(layer, (1, S5_WIDTH)),
        ],
        out_specs=pl.BlockSpec((tb, bsz * S5_WIDTH), lambda i: (i, 0)),
        out_shape=jax.ShapeDtypeStruct((seq, bsz * S5_WIDTH), BF16),
        scratch_shapes=[pltpu.VMEM((bsz, S5_NSTATE), F32), pltpu.VMEM((bsz, S5_NSTATE), F32),
                        pltpu.VMEM((slabs, rows, LANES), F32), pltpu.VMEM((slabs, rows, LANES), F32)],
        compiler_params=_params("arbitrary"),
        name="s5",
    )(u, ab_re, ab_im, b_re, b_im, c_re, c_im, d, glu_w, glu_b)


def _merge_kernel(x_ref, gla_ref, s5_ref, fox_ref, pre_g_ref, wgate_ref, wgla_ref, ws5_ref, wfox_ref, wo_ref,
                  post_g_ref, o_ref):
    tm, d = x_ref.shape
    blocks = [slice(r0, r0 + tm // EDGE_BLOCKS) for r0 in range(0, tm, tm // EDGE_BLOCKS)]
    branches = ((gla_ref, wgla_ref), (s5_ref, ws5_ref), (fox_ref, wfox_ref))

    hs, first = [], []
    for rows in blocks:
        hs.append(_rms(x_ref[rows, :], pre_g_ref[...]).astype(BF16))
        first.append(_dot(hs[-1], wgate_ref[:, 0:d]))
    h = jnp.concatenate(hs, axis=0)
    logits = jnp.concatenate(first, axis=0)
    mix = None
    for n, (b_ref, w_ref) in enumerate(branches[:-1]):
        term = jax.nn.sigmoid(logits) * _dot(b_ref[...], w_ref[...])
        mix = term if mix is None else mix + term
        logits = _dot(h, wgate_ref[:, (n + 1) * d:(n + 2) * d])
    up = _dot(branches[-1][0][...], branches[-1][1][...])

    for rows in blocks:
        mixed = mix[rows, :] + jax.nn.sigmoid(logits[rows, :]) * up[rows, :]
        y = _dot(mixed.astype(BF16), wo_ref[...])
        o_ref[rows, :] = x_ref[rows, :] + _rms(y, post_g_ref[...])


def _merge(layer, x, gla_o, s5_o, fox_o, pre_g, w_gate, w_gla, w_s5, w_fox, w_o, post_g, *, bsz, seq, tm):
    m, d = x.shape
    nl = seq // tm
    row = lambda b, l: (b * nl + l, 0)
    whole = lambda shape: pl.BlockSpec(shape, lambda b, l: (0, 0), pipeline_mode=RESIDENT)
    return pl.pallas_call(
        _merge_kernel,
        grid=(bsz, nl),
        in_specs=[
            pl.BlockSpec((tm, d), row),
            pl.BlockSpec((tm, GLA_V), row),
            pl.BlockSpec((tm, S5_WIDTH), lambda b, l: (l, b)),
            pl.BlockSpec((tm, FOX_W), row),
            _layer_spec(layer, (1, d)),
            _layer_spec(layer, (d, N_BRANCH * d), pipeline_mode=RESIDENT),
            whole((GLA_V, d)),
            whole((S5_WIDTH, d)),
            whole((FOX_W, d)),
            whole((d, d)),
            _layer_spec(layer, (1, d)),
        ],
        out_specs=pl.BlockSpec((tm, d), row),
        out_shape=jax.ShapeDtypeStruct((m, d), F32),
        compiler_params=_params("parallel", "parallel"),
        name="merge",
    )(x, gla_o, s5_o, fox_o, pre_g, w_gate, w_gla, w_s5, w_fox, w_o, post_g)


def _memkv_kernel(mem_ref, g_ref, w_ref, *rest):
    n_cast = (len(rest) - 2) // 2
    k_ref, v_ref = rest[n_cast], rest[n_cast + 1]
    _cast_slabs(rest[:n_cast] + rest[n_cast + 2:])
    d = mem_ref.shape[1]
    h = _rms(mem_ref[...], g_ref[...]).astype(BF16)
    k_ref[...] = _dot(h, w_ref[:, 0:d]).astype(BF16)
    v_ref[...] = _dot(h, w_ref[:, d:2 * d]).astype(BF16)


def _memkv(mem, g, w_kv, *, tm, cast):
    m, d = mem.shape
    depth = w_kv.shape[0]
    tiles = m // tm
    per_layer = lambda block: pl.BlockSpec((None,) + block, lambda l, i: (l, 0, 0))
    out = pl.BlockSpec((None, tm, d), lambda l, i: (l, i, 0))
    extra = [_cast_specs(cast[0], w, depth * tiles, lambda l, i: l * tiles + i) for w in cast[1:]]
    outs = pl.pallas_call(
        _memkv_kernel,
        grid=(depth, tiles),
        in_specs=[pl.BlockSpec((tm, d), lambda l, i: (i, 0)), per_layer((1, d)), per_layer((d, 2 * d))]
        + [e[0] for e in extra],
        out_specs=[out, out] + [e[1] for e in extra],
        out_shape=[jax.ShapeDtypeStruct((depth, m, d), BF16)] * 2 + [e[2] for e in extra],
        compiler_params=_params("parallel", "parallel"),
        name="memkv",
    )(mem, g, w_kv, *cast[1:])
    return outs[0], outs[1], tuple(outs[2:])


def _xattn_kernel(x_ref, k_ref, v_ref, pre_g_ref, wq_ref, wo_ref, post_g_ref, *rest):
    o_ref, cast_refs = rest[len(rest) // 2], rest[:len(rest) // 2] + rest[len(rest) // 2 + 1:]
    _cast_slabs(cast_refs)
    tm, d = x_ref.shape
    dh = d // XA_HEADS
    blocks = [slice(r0, r0 + tm // EDGE_BLOCKS) for r0 in range(0, tm, tm // EDGE_BLOCKS)]
    cols = [slice(n * dh, (n + 1) * dh) for n in range(XA_HEADS)]

    def softmax(s):
        e = jnp.exp(s - jnp.max(s, axis=1, keepdims=True))
        return (e / jnp.sum(e, axis=1, keepdims=True)).astype(BF16)

    qs = []
    for rows in blocks:
        h = _rms(x_ref[rows, :], pre_g_ref[...]).astype(BF16)
        qs.append((_dot(h, wq_ref[...]) * (dh ** -0.5)).astype(BF16))
    logits = [[_dot_nt(q[:, c], k_ref[:, c]) for c in cols] for q in qs]
    probs = [[softmax(s) for s in per_head] for per_head in logits]
    heads = [jnp.concatenate([_dot(p, v_ref[:, c]).astype(BF16) for p, c in zip(per_head, cols)], axis=1)
             for per_head in probs]
    for rows, attended in zip(blocks, heads):
        y = _dot(attended, wo_ref[...])
        o_ref[rows, :] = x_ref[rows, :] + _rms(y, post_g_ref[...])


def _xattn(layer, x, k, v, pre_g, w_q, w_o, post_g, *, bsz, seq, n_mem, tm, cast):
    m, d = x.shape
    nl = seq // tm
    row = lambda b, l: (b * nl + l, 0)
    extra = [_cast_specs(cast[0], w, bsz * nl, lambda b, l: b * nl + l) for w in cast[1:]]
    outs = pl.pallas_call(
        _xattn_kernel,
        grid=(bsz, nl),
        in_specs=[
            pl.BlockSpec((tm, d), row),
            _layer_spec(layer, (n_mem, d), lambda b, l: (b, 0)),
            _layer_spec(layer, (n_mem, d), lambda b, l: (b, 0)),
            _layer_spec(layer, (1, d)),
            pl.BlockSpec((d, d), lambda b, l: (0, 0), pipeline_mode=RESIDENT),
            pl.BlockSpec((d, d), lambda b, l: (0, 0), pipeline_mode=RESIDENT),
            _layer_spec(layer, (1, d)),
        ] + [e[0] for e in extra],
        out_specs=[pl.BlockSpec((tm, d), row)] + [e[1] for e in extra],
        out_shape=[jax.ShapeDtypeStruct((m, d), F32)] + [e[2] for e in extra],
        compiler_params=_params("parallel", "parallel"),
        name="xattn",
    )(x, k, v, pre_g, w_q, w_o, post_g, *cast[1:])
    return outs[0], tuple(outs[1:])


def _tile(n, want):
    t = min(n, want)
    while n % t:
        t -= 1
    return t


def _block_diag(t):
    depth, g, r, c = t.shape
    eye = jnp.eye(g, dtype=t.dtype)
    return jnp.einsum("lgrc,gk->lgrkc", t, eye).reshape(depth, g * r, g * c)


def kernel(x, mem, ffn1_pre_g, ffn1_w_gu, ffn1_w_down, ffn1_post_g, mix_pre_g, w_in, gla_gate_w, gla_gate_b, gla_norm_g, w_gla_up, s5_a_re, s5_a_im, s5_log_dt, s5_b_re, s5_b_im, s5_c_re, s5_c_im, s5_d, s5_glu_w, s5_glu_b, w_s5_up, fox_f_b, w_fox_up, w_mix_out, mix_post_g, xa_pre_g, xa_mem_g, xa_w_q, xa_w_kv, xa_w_o, xa_post_g, ffn2_pre_g, ffn2_w_gu, ffn2_w_down, ffn2_post_g):
    bsz, seq, d = x.shape
    n_mem = mem.shape[1]
    depth = w_in.shape[0]
    d_ff = ffn1_w_down.shape[1]
    m = bsz * seq
    assert seq % CHUNK == 0 and d % LANES == 0

    tm_ffn = _tile(m, 1024)
    tf = _tile(d_ff // LANES, 2) * LANES
    tm = _tile(seq, 1024)
    tk = _tile(seq, 512)
    ts = _tile(seq, 128)
    tb = 2 * ts if seq % (2 * ts) == 0 else ts

    bf = lambda a: a.astype(BF16)
    vec = lambda a: a.reshape(depth, 1, a.shape[-1])

    aligned = 2 * GLA_QK + 2 * GLA_V
    mid = S5_WIDTH + 3 * FOX_W
    gates_at = aligned + GLA_RANK + mid + FOX_HEADS
    assert aligned % LANES == 0 and mid % LANES == 0 and aligned + mid == OFF_SMALL and IN_SMALL == LANES
    assert gates_at % LANES == GLA_RANK + FOX_HEADS
    padded = pl.cdiv(w_in.shape[2], LANES) * LANES
    assert gates_at // LANES * LANES + (N_BRANCH * d + LANES) <= padded
    w_in_b = jnp.pad(bf(w_in), ((0, 0), (0, 0), (0, padded - w_in.shape[2])))
    w_main, w_gate, w_small = _repack(
        w_in_b, rows=_tile(d, 256), aligned=aligned // LANES, mid=mid // LANES, gate_base=gates_at // LANES,
        gates=N_BRANCH * d // LANES, shift_mid=GLA_RANK, shift_gate=GLA_RANK + FOX_HEADS)
    gla_wg = bf(jnp.concatenate([gla_gate_w, jnp.zeros((depth, IN_SMALL - GLA_RANK, GLA_QK), gla_gate_w.dtype)],
                                axis=1))

    ab_re, ab_im, bb_re, bb_im = _s5prep(s5_a_re, s5_a_im, s5_log_dt, s5_b_re, s5_b_im)
    grp = lambda t: t.reshape(depth, S5_GROUPS, S5_STATE, S5_CH)
    s5_bre = bf(_block_diag(grp(bb_re).transpose(0, 1, 3, 2)))
    s5_bim = bf(_block_diag(grp(bb_im).transpose(0, 1, 3, 2)))
    s5_cre = bf(_block_diag(s5_c_re.transpose(0, 1, 3, 2)))
    s5_cim = bf(_block_diag(s5_c_im.transpose(0, 1, 3, 2)))
    ab_re = ab_re.reshape(depth, 1, S5_NSTATE)
    ab_im = ab_im.reshape(depth, 1, S5_NSTATE)

    glu_w, xa_kv = bf(s5_glu_w), bf(xa_w_kv)
    fox_b = fox_f_b.reshape(depth, FOX_HEADS, 1)

    xs = x.reshape(m, d)
    mk, mv, ffn1_w = _memkv(mem.reshape(bsz * n_mem, d), vec(xa_mem_g), xa_kv, tm=_tile(bsz * n_mem, 512),
                            cast=(0, ffn1_w_gu, ffn1_w_down))
    for l in range(depth):
        xs, (gla_up, s5_up, fox_up, mix_out, xa_q, xa_o) = _ffn(
            l, xs, vec(ffn1_pre_g), *ffn1_w, vec(ffn1_post_g), tm=tm_ffn, tf=tf,
            cast=(l, w_gla_up, w_s5_up, w_fox_up, w_mix_out, xa_w_q, xa_w_o))

        qk, vr, su, fox, small = _inproj(l, xs, vec(mix_pre_g), w_main, w_small, bsz=bsz, seq=seq, tm=tm)
        gla_o = _gla(l, qk, vr, small, gla_wg, vec(gla_gate_b), vec(gla_norm_g), bsz=bsz, seq=seq)
        f_cols = _foxprep(l, small, fox_b, bsz=bsz, seq=seq)
        fox_o = _fox(fox, f_cols, bsz=bsz, seq=seq, tk=tk)
        s5_o = _s5(l, su, ab_re, ab_im, s5_bre, s5_bim, s5_cre, s5_cim,
                   vec(s5_d), glu_w, vec(s5_glu_b), bsz=bsz, seq=seq, tb=tb, ts=ts)
        xs = _merge(l, xs, gla_o, s5_o, fox_o, vec(mix_pre_g), w_gate, gla_up,
                    s5_up, fox_up, mix_out, vec(mix_post_g), bsz=bsz, seq=seq, tm=tm)

        xs, ffn2_w = _xattn(l, xs, mk, mv, vec(xa_pre_g), xa_q, xa_o, vec(xa_post_g), bsz=bsz, seq=seq, n_mem=n_mem,
                            tm=tm, cast=(l, ffn2_w_gu, ffn2_w_down))

        ahead = (l + 1, ffn1_w_gu, ffn1_w_down) if l + 1 < depth else None
        xs, ffn1_w = _ffn(l, xs, vec(ffn2_pre_g), *ffn2_w, vec(ffn2_post_g), tm=tm_ffn, tf=tf, cast=ahead)
    return xs.reshape(bsz, seq, d)
```

```python
import functools
import math

import jax
import jax.numpy as jnp
from jax import lax
from jax.experimental import pallas as pl
from jax.experimental.pallas import tpu as pltpu

F32 = jnp.float32
BF16 = jnp.bfloat16
EPS = 1e-6
LOG2E = math.log2(math.e)

LANES = 128
SUBLANES = 8
VMEM_LIMIT_BYTES = 56 * 1024 * 1024

EDGE_BLOCKS = 4
CHUNK = 64
GLA_GROUP = 8
GLA_HEADS, GLA_DK, GLA_DV, GLA_RANK, GLA_TAU = 4, 32, 64, 16, 16.0
GLA_QK = GLA_HEADS * GLA_DK
GLA_V = GLA_HEADS * GLA_DV
DK_SHIFT = GLA_DK.bit_length() - 1
DV_SHIFT = GLA_DV.bit_length() - 1
S5_GROUPS, S5_CH, S5_STATE = 16, 16, 64
S5_WIDTH = S5_GROUPS * S5_CH
S5_NSTATE = S5_GROUPS * S5_STATE
FOX_HEADS, FOX_DH = 8, 64
FOX_W = FOX_HEADS * FOX_DH
FOX_PAIRS = FOX_W // LANES
FOX_TERMS = 4
HEADS_SHIFT = FOX_HEADS.bit_length() - 1
XA_HEADS = 4
N_BRANCH = 3

NT_DIMS = (((1,), (1,)), ((), ()))
RESIDENT = pl.Buffered(1)


def _params(*semantics):
    return pltpu.CompilerParams(dimension_semantics=semantics, vmem_limit_bytes=VMEM_LIMIT_BYTES)


def _layer_spec(layer, block, index_map=None, pipeline_mode=None):
    block = tuple(block)
    if index_map is None:
        index_map = lambda *_: (0,) * len(block)
    return pl.BlockSpec((None,) + block, lambda *ids: (layer,) + tuple(index_map(*ids)),
                        pipeline_mode=pipeline_mode)


def _rms(x, g):
    return x * lax.rsqrt(jnp.mean(x * x, axis=-1, keepdims=True) + EPS) * g


def _dot(a, b):
    return jnp.dot(a, b, preferred_element_type=F32)


def _dot_nt(a, b):
    return lax.dot_general(a, b, NT_DIMS, preferred_element_type=F32)


def _log_sigmoid(x):
    return jnp.minimum(x, 0.0) - jnp.log1p(jnp.exp(-jnp.abs(x)))


def _split3(x):
    hi = x.astype(BF16)
    r1 = x - hi.astype(F32)
    mid = r1.astype(BF16)
    lo = (r1 - mid.astype(F32)).astype(BF16)
    return hi, mid, lo


def _cast_specs(layer, stacked, n_steps, step):
    _, rows, cols = stacked.shape
    slab = rows // n_steps
    assert slab * n_steps == rows and slab % (2 * SUBLANES) == 0
    return (pl.BlockSpec((None, slab, cols), lambda *ids: (layer, step(*ids), 0)),
            pl.BlockSpec((slab, cols), lambda *ids: (step(*ids), 0)),
            jax.ShapeDtypeStruct((rows, cols), BF16))


def _cast_slabs(refs):
    half = len(refs) // 2
    for src, dst in zip(refs[:half], refs[half:]):
        dst[...] = src[...].astype(BF16)


def _ffn_kernel(x_ref, pre_g_ref, wgu_ref, wd_ref, post_g_ref, *rest, tf):
    o_ref, cast_refs = rest[len(rest) // 2], rest[:len(rest) // 2] + rest[len(rest) // 2 + 1:]
    _cast_slabs(cast_refs)
    d_ff = wd_ref.shape[0]
    tm = x_ref.shape[0]
    blocks = [slice(r0, r0 + tm // EDGE_BLOCKS) for r0 in range(0, tm, tm // EDGE_BLOCKS)]

    def gate_up(h, c0):
        return _dot(h, wgu_ref[:, c0:c0 + tf]), _dot(h, wgu_ref[:, d_ff + c0:d_ff + c0 + tf])

    hs, first = [], []
    for rows in blocks:
        hs.append(_rms(x_ref[rows, :], pre_g_ref[...]).astype(BF16))
        first.append(gate_up(hs[-1], 0))
    h = jnp.concatenate(hs, axis=0)
    nxt = tuple(jnp.concatenate(t, axis=0) for t in zip(*first))

    acc = None
    for c0 in range(0, d_ff - tf, tf):
        gate, up = nxt
        nxt = gate_up(h, c0 + tf)
        a = (gate * jax.nn.sigmoid(gate) * up).astype(BF16)
        part = _dot(a, wd_ref[c0:c0 + tf, :])
        acc = part if acc is None else acc + part

    gate, up = nxt
    a = (gate * jax.nn.sigmoid(gate) * up).astype(BF16)
    for rows in blocks:
        total = _dot(a[rows, :], wd_ref[d_ff - tf:d_ff, :])
        if acc is not None:
            total = acc[rows, :] + total
        o_ref[rows, :] = x_ref[rows, :] + 0.5 * _rms(total, post_g_ref[...])


def _ffn(layer, x, pre_g, w_gu, w_down, post_g, *, tm, tf, cast=None):
    m, d = x.shape
    d_ff = w_down.shape[0]
    n_steps = m // tm
    extra = [_cast_specs(cast[0], w, n_steps, lambda i: i) for w in cast[1:]] if cast else []
    outs = pl.pallas_call(
        functools.partial(_ffn_kernel, tf=tf),
        grid=(n_steps,),
        in_specs=[
            pl.BlockSpec((tm, d), lambda i: (i, 0)),
            _layer_spec(layer, (1, d)),
            pl.BlockSpec((d, 2 * d_ff), lambda i: (0, 0), pipeline_mode=RESIDENT),
            pl.BlockSpec((d_ff, d), lambda i: (0, 0), pipeline_mode=RESIDENT),
            _layer_spec(layer, (1, d)),
        ] + [e[0] for e in extra],
        out_specs=[pl.BlockSpec((tm, d), lambda i: (i, 0))] + [e[1] for e in extra],
        out_shape=[jax.ShapeDtypeStruct((m, d), F32)] + [e[2] for e in extra],
        compiler_params=_params("parallel"),
        name="ffn",
    )(x, pre_g, w_gu, w_down, post_g, *(cast[1:] if cast else ()))
    return outs[0], tuple(outs[1:])


def _repack_kernel(w_ref, tail_ref, main_ref, gate_ref, small_ref, *, aligned, mid, gate_base, gates, shift_mid,
                   shift_gate):
    rows = w_ref.shape[0]
    block = lambda n: w_ref[:, n * LANES:(n + 1) * LANES]
    last = w_ref.shape[1] // LANES - 1

    def window(n):
        return tail_ref[...] if n == last else w_ref[:, n * LANES:(n + 2) * LANES]

    def shifted(first, count, shift):
        windows = jnp.concatenate([window(first + i) for i in range(count)], axis=0)
        src = lax.broadcasted_iota(jnp.int32, (2 * LANES, LANES), 0)
        dst = lax.broadcasted_iota(jnp.int32, (2 * LANES, LANES), 1)
        out = _dot(windows, (src == dst + shift).astype(BF16)).astype(BF16)
        return [out[i * rows:(i + 1) * rows, :] for i in range(count)]

    main_ref[:, 0:aligned * LANES] = w_ref[:, 0:aligned * LANES]
    for i, piece in enumerate(shifted(aligned, mid, shift_mid)):
        main_ref[:, (aligned + i) * LANES:(aligned + i + 1) * LANES] = piece
    for i, piece in enumerate(shifted(gate_base, gates, shift_gate)):
        gate_ref[:, i * LANES:(i + 1) * LANES] = piece
    lane = lax.broadcasted_iota(jnp.int32, (rows, LANES), 1)
    small_ref[...] = jnp.where(lane < shift_mid, block(aligned),
                               jnp.where(lane < shift_gate, block(gate_base), jnp.zeros((rows, LANES), BF16)))


def _repack(w, *, rows, aligned, mid, gate_base, gates, shift_mid, shift_gate):
    depth, d, cols = w.shape
    spec = lambda n: pl.BlockSpec((None, rows, n), lambda l, i: (l, i, 0))
    last = cols // LANES - 1
    assert gate_base + gates - 1 == last and cols > (last + 1) * LANES
    tail = jnp.pad(w[:, :, last * LANES:], ((0, 0), (0, 0), (0, (last + 2) * LANES - cols)))
    return pl.pallas_call(
        functools.partial(_repack_kernel, aligned=aligned, mid=mid, gate_base=gate_base, gates=gates,
                          shift_mid=shift_mid, shift_gate=shift_gate),
        grid=(depth, d // rows),
        in_specs=[spec(cols), spec(2 * LANES)],
        out_specs=[spec((aligned + mid) * LANES), spec(gates * LANES), spec(LANES)],
        out_shape=[jax.ShapeDtypeStruct((depth, d, (aligned + mid) * LANES), BF16),
                   jax.ShapeDtypeStruct((depth, d, gates * LANES), BF16),
                   jax.ShapeDtypeStruct((depth, d, LANES), BF16)],
        compiler_params=_params("parallel", "parallel"),
        name="repack",
    )(w, tail)


IN_QK = 2 * GLA_QK
IN_VR = 2 * GLA_V
IN_SU = S5_WIDTH
IN_FOX = 3 * FOX_W
IN_SMALL = LANES
OFF_VR = IN_QK
OFF_SU = OFF_VR + IN_VR
OFF_FOX = OFF_SU + IN_SU
OFF_SMALL = OFF_FOX + IN_FOX


def _inproj_kernel(x_ref, g_ref, w_ref, ws_ref, qk_ref, vr_ref, su_ref, fox_ref, small_ref):
    tm = x_ref.shape[0]
    hs = []
    for r0 in range(0, tm, tm // EDGE_BLOCKS):
        rows = slice(r0, r0 + tm // EDGE_BLOCKS)
        hs.append(_rms(x_ref[rows, :], g_ref[...]).astype(BF16))
        qk_ref[rows, :] = _dot(hs[-1], w_ref[:, 0:OFF_VR])
    h = jnp.concatenate(hs, axis=0)
    vr_ref[...] = _dot(h, w_ref[:, OFF_VR:OFF_SU]).astype(BF16)
    su_ref[...] = _dot(h, w_ref[:, OFF_SU:OFF_FOX])
    fox_ref[:, 0:FOX_W] = (_dot(h, w_ref[:, OFF_FOX:OFF_FOX + FOX_W]) * (FOX_DH ** -0.5 * LOG2E)).astype(BF16)
    fox_ref[:, FOX_W:IN_FOX] = _dot(h, w_ref[:, OFF_FOX + FOX_W:OFF_SMALL]).astype(BF16)
    small_ref[...] = _dot(h, ws_ref[...])


def _inproj(layer, x, g, w, w_small, *, bsz, seq, tm):
    m, d = x.shape
    nl = seq // tm
    row = lambda b, l: (b * nl + l, 0)
    return pl.pallas_call(
        _inproj_kernel,
        grid=(bsz, nl),
        in_specs=[
            pl.BlockSpec((tm, d), row),
            _layer_spec(layer, (1, d)),
            _layer_spec(layer, (d, OFF_SMALL), pipeline_mode=RESIDENT),
            _layer_spec(layer, (d, IN_SMALL), pipeline_mode=RESIDENT),
        ],
        out_specs=[
            pl.BlockSpec((tm, IN_QK), row),
            pl.BlockSpec((tm, IN_VR), row),
            pl.BlockSpec((tm, IN_SU), lambda b, l: (l, b)),
            pl.BlockSpec((tm, IN_FOX), row),
            pl.BlockSpec((tm, IN_SMALL), row),
        ],
        out_shape=[
            jax.ShapeDtypeStruct((m, IN_QK), F32),
            jax.ShapeDtypeStruct((m, IN_VR), BF16),
            jax.ShapeDtypeStruct((seq, bsz * IN_SU), F32),
            jax.ShapeDtypeStruct((m, IN_FOX), BF16),
            jax.ShapeDtypeStruct((m, IN_SMALL), F32),
        ],
        compiler_params=_params("parallel", "parallel"),
        name="inproj",
    )(x, g, w, w_small)


def _gla_kernel(qk_ref, vr_ref, small_ref, wg_ref, bg_ref, ng_ref, o_ref, la_sc):
    seq = qk_ref.shape[0]
    n_chunks = seq // CHUNK
    c, hq, hv = CHUNK, GLA_QK, GLA_V

    z = _dot(small_ref[...].astype(BF16), wg_ref[...]) + bg_ref[...]
    la_sc[...] = _log_sigmoid(z) * (1.0 / GLA_TAU)

    row = lax.broadcasted_iota(jnp.int32, (c, c), 0)
    col = lax.broadcasted_iota(jnp.int32, (c, c), 1)
    tri = (col <= row).astype(BF16)
    row_x = lax.broadcasted_iota(jnp.int32, (GLA_HEADS * c, c), 0)
    col_x = lax.broadcasted_iota(jnp.int32, (GLA_HEADS * c, c), 1)
    lower = col_x <= (row_x & (c - 1))
    lane_q = lax.broadcasted_iota(jnp.int32, (c, hq), 1)
    lane_v = lax.broadcasted_iota(jnp.int32, (c, hv), 1)
    q_head = [(lane_q >> DK_SHIFT) == h for h in range(GLA_HEADS)]
    v_head = [(lane_v >> DV_SHIFT) == h for h in range(GLA_HEADS)]
    st_row = lax.broadcasted_iota(jnp.int32, (hv, hq), 0)
    st_col = lax.broadcasted_iota(jnp.int32, (hv, hq), 1)
    same_head = (st_row >> DV_SHIFT) == (st_col >> DK_SHIFT)

    def expand(t):
        return jnp.concatenate([jnp.where(q_head[h], t, 0.0) for h in range(GLA_HEADS)], axis=0).astype(BF16)

    gr = lax.broadcasted_iota(jnp.int32, (hv, hv), 0) >> DV_SHIFT
    gc = lax.broadcasted_iota(jnp.int32, (hv, hv), 1) >> DV_SHIFT
    avg = jnp.where(gr == gc, 1.0 / GLA_DV, 0.0).astype(BF16)

    def within_chunks(group):
        stage1 = []
        for i in group:
            hi, mid, lo = _split3(la_sc[i * c:(i + 1) * c, :])
            stage1.append(_dot(tri, hi) + _dot(tri, mid) + _dot(tri, lo))
        stage2 = []
        for i, g in zip(group, stage1):
            rows = slice(i * c, (i + 1) * c)
            g_last = g[c - 1:c, :]
            eg, ieg = jnp.exp(g), jnp.exp(-g)
            qc = qk_ref[rows, 0:hq] * (GLA_DK ** -0.5)
            kc = qk_ref[rows, hq:2 * hq]
            v = vr_ref[rows, 0:hv]
            q_fwd = qc * eg
            a_fwd = _dot_nt(expand(q_fwd), (kc * ieg).astype(BF16))
            a_bwd = _dot_nt(expand(qc * ieg), (kc * eg).astype(BF16))
            kw = (kc * jnp.exp(g_last - g)).astype(BF16)
            ds_t = _dot(v.astype(F32).T.astype(BF16), kw)
            stage2.append((a_fwd, a_bwd, v, q_fwd.astype(BF16), jnp.where(same_head, ds_t, 0.0), jnp.exp(g_last)))
        results = []
        for a_fwd, a_bwd, v, q_fwd, ds_t, decay in stage2:
            attn = jnp.where(lower, a_fwd, a_bwd).astype(BF16)
            p = _dot(attn, v)
            o = jnp.where(v_head[0], p[0:c, :], 0.0)
            for h in range(1, GLA_HEADS):
                o = o + jnp.where(v_head[h], p[h * c:(h + 1) * c, :], 0.0)
            results.append((o, q_fwd, ds_t, decay))
        return results

    def finish(group, results, s_t):
        outs = []
        for o, q_fwd, ds_t, decay in results:
            outs.append(o + _dot_nt(q_fwd, s_t.astype(BF16)))
            s_t = s_t * decay + ds_t
        o = jnp.concatenate(outs, axis=0)
        rows = slice(group[0] * c, (group[-1] + 1) * c)
        hi, mid, lo = _split3(o * o)
        ms = _dot(hi, avg) + _dot(mid, avg) + _dot(lo, avg)
        r = vr_ref[rows, hv:2 * hv].astype(F32)
        o_ref[rows, :] = (o * lax.rsqrt(ms + EPS) * ng_ref[...] * (r * jax.nn.sigmoid(r))).astype(BF16)
        return s_t

    groups = [range(first, min(first + GLA_GROUP, n_chunks)) for first in range(0, n_chunks, GLA_GROUP)]
    s_t = jnp.zeros((hv, hq), F32)
    pending = None
    for group in groups:
        results = within_chunks(group)
        if pending is not None:
            s_t = finish(*pending, s_t)
        pending = (group, results)
    finish(*pending, s_t)


def _gla(layer, qk, vr, small, wg, bg, ng, *, bsz, seq):
    m = qk.shape[0]
    row = lambda b: (b, 0)
    return pl.pallas_call(
        _gla_kernel,
        grid=(bsz,),
        in_specs=[
            pl.BlockSpec((seq, IN_QK), row),
            pl.BlockSpec((seq, IN_VR), row),
            pl.BlockSpec((seq, IN_SMALL), row),
            _layer_spec(layer, (IN_SMALL, GLA_QK)),
            _layer_spec(layer, (1, GLA_QK)),
            _layer_spec(layer, (1, GLA_V)),
        ],
        out_specs=pl.BlockSpec((seq, GLA_V), row),
        out_shape=jax.ShapeDtypeStruct((m, GLA_V), BF16),
        scratch_shapes=[pltpu.VMEM((seq, GLA_QK), F32)],
        compiler_params=_params("parallel"),
        name="gla",
    )(qk, vr, small, wg, bg, ng)


def _foxprep_kernel(small_ref, b_ref, o_ref, *, bsz):
    seq = small_ref.shape[0] // bsz
    ff = jnp.concatenate([small_ref[b * seq:(b + 1) * seq, :].T[GLA_RANK:GLA_RANK + FOX_HEADS, :]
                          for b in range(bsz)], axis=0)
    x = _log_sigmoid(ff + jnp.concatenate([b_ref[...]] * bsz, axis=0))
    lane = lax.broadcasted_iota(jnp.int32, x.shape, 1)
    shift = 1
    while shift < seq:
        x = x + jnp.where(lane >= shift, pltpu.roll(x, shift, axis=1), 0.0)
        shift *= 2
    hi, mid, lo = (t.astype(F32) for t in _split3(x * LOG2E))
    ones = jnp.ones((FOX_HEADS, seq), F32)
    pad = jnp.zeros((LANES - FOX_TERMS * FOX_HEADS, seq), F32)
    for b in range(bsz):
        mine = slice(b * FOX_HEADS, (b + 1) * FOX_HEADS)
        terms = jnp.concatenate([hi[mine], mid[mine], lo[mine], ones, pad], axis=0)
        o_ref[b * seq:(b + 1) * seq, :] = terms.T.astype(BF16)


def _foxprep(layer, small, bias, *, bsz, seq):
    whole = lambda i: (0, 0)
    return pl.pallas_call(
        functools.partial(_foxprep_kernel, bsz=bsz),
        grid=(1,),
        in_specs=[
            pl.BlockSpec((bsz * seq, IN_SMALL), whole),
            _layer_spec(layer, (FOX_HEADS, 1)),
        ],
        out_specs=pl.BlockSpec((bsz * seq, LANES), whole),
        out_shape=jax.ShapeDtypeStruct((bsz * seq, LANES), BF16),
        compiler_params=_params("arbitrary"),
        name="foxprep",
    )(small, bias)


def _fox_placement(head, base, *, key):
    r = lax.broadcasted_iota(jnp.int32, (LANES, LANES), 0)
    rel = lax.broadcasted_iota(jnp.int32, (LANES, LANES), 1) - base
    term = r >> HEADS_SHIFT
    mine = ((r & (FOX_HEADS - 1)) == head) & (term < FOX_TERMS)
    is_part = mine & (term < FOX_TERMS - 1)
    is_one = mine & (term == FOX_TERMS - 1)
    n = FOX_TERMS - 1
    if key:
        m = jnp.where(is_part & (rel == term + n), -1.0, jnp.where(is_one & (rel >= 0) & (rel < n), 1.0, 0.0))
    else:
        m = jnp.where(is_part & (rel == term), 1.0, jnp.where(is_one & (rel >= n) & (rel < 2 * n), 1.0, 0.0))
    return m.astype(BF16)


def _fox_kernel(q_ref, k_ref, v_ref, f_ref, o_ref, qx_sc, kx_sc, vx_sc, *, tk):
    seq = k_ref.shape[0]
    spare = (FOX_DH, 0)
    lane = lax.broadcasted_iota(jnp.int32, (seq, LANES), 1)
    q, k, v, f = q_ref[...], k_ref[...], v_ref[...], f_ref[...]
    for h in range(2):
        own = (lane < FOX_DH) if h == 0 else (lane >= FOX_DH)
        head = 2 * pl.program_id(1) + h
        qx_sc[h] = jnp.where(own, q, _dot(f, _fox_placement(head, spare[h], key=False)).astype(BF16))
        kx_sc[h] = jnp.where(own, k, _dot(f, _fox_placement(head, spare[h], key=True)).astype(BF16))
        vx_sc[h] = jnp.where(own, v, jnp.ones_like(v))

    row = lax.broadcasted_iota(jnp.int32, (tk, tk), 0)
    col = lax.broadcasted_iota(jnp.int32, (tk, tk), 1)
    causal = col <= row

    items = [(h, r0, c0) for c0 in range(0, seq, tk) for h in range(2) for r0 in range(c0, seq, tk)]

    def logits(item):
        h, r0, c0 = item
        s = _dot_nt(qx_sc[h, r0:r0 + tk, :], kx_sc[h, c0:c0 + tk, :])
        return jnp.where(causal, s, -jnp.inf) if r0 == c0 else s

    state = {}
    s_next = logits(items[0])
    for n, (h, r0, c0) in enumerate(items):
        s = s_next
        if n + 1 < len(items):
            s_next = logits(items[n + 1])
        vx = vx_sc[h, c0:c0 + tk, :]
        m_new = jnp.max(s, axis=1, keepdims=True)
        if c0 == 0:
            acc = _dot(jnp.exp2(s - m_new).astype(BF16), vx)
        else:
            m_old, acc_old = state[h, r0]
            m_new = jnp.maximum(m_old, m_new)
            acc = jnp.exp2(m_old - m_new) * acc_old + _dot(jnp.exp2(s - m_new).astype(BF16), vx)
        state[h, r0] = (m_new, acc)

    first = lax.broadcasted_iota(jnp.int32, (tk, LANES), 1) < FOX_DH
    for r0 in range(0, seq, tk):
        a0, a1 = state[0, r0][1], state[1, r0][1]
        o_ref[r0:r0 + tk, :] = jnp.where(first, a0 / pltpu.roll(a0, FOX_DH, axis=1),
                                         a1 / pltpu.roll(a1, FOX_DH, axis=1)).astype(BF16)


def _fox(fox, f, *, bsz, seq, tk):
    m = fox.shape[0]
    return pl.pallas_call(
        functools.partial(_fox_kernel, tk=tk),
        grid=(bsz, FOX_PAIRS),
        in_specs=[
            pl.BlockSpec((seq, LANES), lambda b, p: (b, p)),
            pl.BlockSpec((seq, LANES), lambda b, p: (b, FOX_PAIRS + p)),
            pl.BlockSpec((seq, LANES), lambda b, p: (b, 2 * FOX_PAIRS + p)),
            pl.BlockSpec((seq, LANES), lambda b, p: (b, 0)),
        ],
        out_specs=pl.BlockSpec((seq, LANES), lambda b, p: (b, p)),
        out_shape=jax.ShapeDtypeStruct((m, FOX_W), BF16),
        scratch_shapes=[pltpu.VMEM((2, seq, LANES), BF16)] * 3,
        compiler_params=_params("parallel", "parallel"),
        name="fox",
    )(fox, fox, fox, f)


def _s5prep_kernel(are_ref, aim_ref, ldt_ref, bre_ref, bim_ref, abre_ref, abim_ref, bbre_ref, bbim_ref):
    lam_re = jnp.minimum(are_ref[...], -1e-4)
    lam_im = aim_ref[...]
    dt = jnp.exp(ldt_ref[...])
    mag = jnp.exp(lam_re * dt)
    ab_re = mag * jnp.cos(lam_im * dt)
    ab_im = mag * jnp.sin(lam_im * dt)
    den = lam_re * lam_re + lam_im * lam_im
    z_re = ((ab_re - 1.0) * lam_re + ab_im * lam_im) / den
    z_im = (ab_im * lam_re - (ab_re - 1.0) * lam_im) / den
    br, bi = bre_ref[...], bim_ref[...]
    abre_ref[...] = ab_re
    abim_ref[...] = ab_im
    bbre_ref[...] = z_re * br - z_im * bi
    bbim_ref[...] = z_re * bi + z_im * br


def _s5prep(a_re, a_im, log_dt, b_re, b_im):
    depth = a_re.shape[0]
    col = lambda a: a.reshape(depth, S5_NSTATE, 1)
    ldt = jnp.broadcast_to(log_dt[:, :, None], (depth, S5_GROUPS, S5_STATE))
    vec = pl.BlockSpec((None, S5_NSTATE, 1), lambda l: (l, 0, 0))
    mat = pl.BlockSpec((None, S5_NSTATE, S5_CH), lambda l: (l, 0, 0))
    return pl.pallas_call(
        _s5prep_kernel,
        grid=(depth,),
        in_specs=[vec, vec, vec, mat, mat],
        out_specs=[vec, vec, mat, mat],
        out_shape=[jax.ShapeDtypeStruct((depth, S5_NSTATE, 1), F32)] * 2
        + [jax.ShapeDtypeStruct((depth, S5_NSTATE, S5_CH), F32)] * 2,
        compiler_params=_params("parallel"),
        name="s5prep",
    )(col(a_re), col(a_im), col(ldt), b_re.reshape(depth, S5_NSTATE, S5_CH), b_im.reshape(depth, S5_NSTATE, S5_CH))


def _s5_kernel(u_ref, abre_ref, abim_ref, bre_ref, bim_ref, cre_ref, cim_ref, d_ref, gw_ref, gb_ref, o_ref,
               sre_sc, sim_sc, u_sc, y_sc, *, bsz, ts):
    @pl.when(pl.program_id(0) == 0)
    def _():
        sre_sc[...] = jnp.zeros_like(sre_sc)
        sim_sc[...] = jnp.zeros_like(sim_sc)

    tb = u_ref.shape[0]
    slabs = S5_WIDTH // LANES
    for b in range(bsz):
        for j in range(slabs):
            lanes = slice(b * S5_WIDTH + j * LANES, b * S5_WIDTH + (j + 1) * LANES)
            u_sc[j, pl.ds(b, tb, stride=bsz), :] = u_ref[:, lanes]

    rows = ts * bsz
    n_sub = tb // ts
    a_re = jnp.broadcast_to(abre_ref[...], (bsz, S5_NSTATE))
    a_im = jnp.broadcast_to(abim_ref[...], (bsz, S5_NSTATE))
    us = [jnp.concatenate([u_sc[j, k * rows:(k + 1) * rows, :] for j in range(slabs)], axis=1) for k in range(n_sub)]

    bu = []
    for u in us:
        ub = u.astype(BF16)
        bu.append((_dot(ub, bre_ref[...]), _dot(ub, bim_ref[...])))

    s_re, s_im = sre_sc[...], sim_sc[...]
    for k in range(n_sub):
        bu_re, bu_im = bu[k]
        x_re, x_im = [], []
        for t in range(ts):
            r = slice(t * bsz, (t + 1) * bsz)
            s_re, s_im = (a_re * s_re - a_im * s_im + bu_re[r, :], a_re * s_im + a_im * s_re + bu_im[r, :])
            x_re.append(s_re)
            x_im.append(s_im)
        x_re = jnp.concatenate(x_re, axis=0).astype(BF16)
        x_im = jnp.concatenate(x_im, axis=0).astype(BF16)
        y = _dot(x_re, cre_ref[...]) - _dot(x_im, cim_ref[...])
        y = jax.nn.gelu(y + d_ref[...] * us[k], approximate=True)
        y = y * jax.nn.sigmoid(_dot(y.astype(BF16), gw_ref[...]) + gb_ref[...])
        for j in range(slabs):
            y_sc[j, k * rows:(k + 1) * rows, :] = y[:, j * LANES:(j + 1) * LANES]
    sre_sc[...] = s_re
    sim_sc[...] = s_im

    for b in range(bsz):
        for j in range(slabs):
            lanes = slice(b * S5_WIDTH + j * LANES, b * S5_WIDTH + (j + 1) * LANES)
            o_ref[:, lanes] = y_sc[j, pl.ds(b, tb, stride=bsz), :].astype(BF16)


def _s5(layer, u, ab_re, ab_im, b_re, b_im, c_re, c_im, d, glu_w, glu_b, *, bsz, seq, tb, ts):
    rows = tb * bsz
    slabs = S5_WIDTH // LANES
    return pl.pallas_call(
        functools.partial(_s5_kernel, bsz=bsz, ts=ts),
        grid=(seq // tb,),
        in_specs=[
            pl.BlockSpec((tb, bsz * S5_WIDTH), lambda i: (i, 0)),
            _layer_spec(layer, (1, S5_NSTATE)),
            _layer_spec(layer, (1, S5_NSTATE)),
            _layer_spec(layer, (S5_WIDTH, S5_NSTATE)),
            _layer_spec(layer, (S5_WIDTH, S5_NSTATE)),
            _layer_spec(layer, (S5_NSTATE, S5_WIDTH)),
            _layer_spec(layer, (S5_NSTATE, S5_WIDTH)),
            _layer_spec(layer, (1, S5_WIDTH)),
            _layer_spec(layer, (S5_WIDTH, S5_WIDTH)),
            _layer_spec(layer, (1, S5_WIDTH)),
        ],
        out_specs=pl.BlockSpec((tb, bsz * S5_WIDTH), lambda i: (i, 0)),
        out_shape=jax.ShapeDtypeStruct((seq, bsz * S5_WIDTH), BF16),
        scratch_shapes=[pltpu.VMEM((bsz, S5_NSTATE), F32), pltpu.VMEM((bsz, S5_NSTATE), F32),
                        pltpu.VMEM((slabs, rows, LANES), F32), pltpu.VMEM((slabs, rows, LANES), F32)],
        compiler_params=_params("arbitrary"),
        name="s5",
    )(u, ab_re, ab_im, b_re, b_im, c_re, c_im, d, glu_w, glu_b)


def _merge_kernel(x_ref, gla_ref, s5_ref, fox_ref, pre_g_ref, wgate_ref, wgla_ref, ws5_ref, wfox_ref, wo_ref,
                  post_g_ref, o_ref):
    tm, d = x_ref.shape
    blocks = [slice(r0, r0 + tm // EDGE_BLOCKS) for r0 in range(0, tm, tm // EDGE_BLOCKS)]
    branches = ((gla_ref, wgla_ref), (s5_ref, ws5_ref), (fox_ref, wfox_ref))

    hs, first = [], []
    for rows in blocks:
        hs.append(_rms(x_ref[rows, :], pre_g_ref[...]).astype(BF16))
        first.append(_dot(hs[-1], wgate_ref[:, 0:d]))
    h = jnp.concatenate(hs, axis=0)
    logits = jnp.concatenate(first, axis=0)
    mix = None
    for n, (b_ref, w_ref) in enumerate(branches[:-1]):
        term = jax.nn.sigmoid(logits) * _dot(b_ref[...], w_ref[...])
        mix = term if mix is None else mix + term
        logits = _dot(h, wgate_ref[:, (n + 1) * d:(n + 2) * d])
    up = _dot(branches[-1][0][...], branches[-1][1][...])

    for rows in blocks:
        mixed = mix[rows, :] + jax.nn.sigmoid(logits[rows, :]) * up[rows, :]
        y = _dot(mixed.astype(BF16), wo_ref[...])
        o_ref[rows, :] = x_ref[rows, :] + _rms(y, post_g_ref[...])


def _merge(layer, x, gla_o, s5_o, fox_o, pre_g, w_gate, w_gla, w_s5, w_fox, w_o, post_g, *, bsz, seq, tm):
    m, d = x.shape
    nl = seq // tm
    row = lambda b, l: (b * nl + l, 0)
    whole = lambda shape: pl.BlockSpec(shape, lambda b, l: (0, 0), pipeline_mode=RESIDENT)
    return pl.pallas_call(
        _merge_kernel,
        grid=(bsz, nl),
        in_specs=[
            pl.BlockSpec((tm, d), row),
            pl.BlockSpec((tm, GLA_V), row),
            pl.BlockSpec((tm, S5_WIDTH), lambda b, l: (l, b)),
            pl.BlockSpec((tm, FOX_W), row),
            _layer_spec(layer, (1, d)),
            _layer_spec(layer, (d, N_BRANCH * d), pipeline_mode=RESIDENT),
            whole((GLA_V, d)),
            whole((S5_WIDTH, d)),
            whole((FOX_W, d)),
            whole((d, d)),
            _layer_spec(layer, (1, d)),
        ],
        out_specs=pl.BlockSpec((tm, d), row),
        out_shape=jax.ShapeDtypeStruct((m, d), F32),
        compiler_params=_params("parallel", "parallel"),
        name="merge",
    )(x, gla_o, s5_o, fox_o, pre_g, w_gate, w_gla, w_s5, w_fox, w_o, post_g)


def _memkv_kernel(mem_ref, g_ref, w_ref, *rest):
    n_cast = (len(rest) - 2) // 2
    k_ref, v_ref = rest[n_cast], rest[n_cast + 1]
    _cast_slabs(rest[:n_cast] + rest[n_cast + 2:])
    d = mem_ref.shape[1]
    h = _rms(mem_ref[...], g_ref[...]).astype(BF16)
    k_ref[...] = _dot(h, w_ref[:, 0:d]).astype(BF16)
    v_ref[...] = _dot(h, w_ref[:, d:2 * d]).astype(BF16)


def _memkv(mem, g, w_kv, *, tm, cast):
    m, d = mem.shape
    depth = w_kv.shape[0]
    tiles = m // tm
    per_layer = lambda block: pl.BlockSpec((None,) + block, lambda l, i: (l, 0, 0))
    out = pl.BlockSpec((None, tm, d), lambda l, i: (l, i, 0))
    extra = [_cast_specs(cast[0], w, depth * tiles, lambda l, i: l * tiles + i) for w in cast[1:]]
    outs = pl.pallas_call(
        _memkv_kernel,
        grid=(depth, tiles),
        in_specs=[pl.BlockSpec((tm, d), lambda l, i: (i, 0)), per_layer((1, d)), per_layer((d, 2 * d))]
        + [e[0] for e in extra],
        out_specs=[out, out] + [e[1] for e in extra],
        out_shape=[jax.ShapeDtypeStruct((depth, m, d), BF16)] * 2 + [e[2] for e in extra],
        compiler_params=_params("parallel", "parallel"),
        name="memkv",
    )(mem, g, w_kv, *cast[1:])
    return outs[0], outs[1], tuple(outs[2:])


def _xattn_kernel(x_ref, k_ref, v_ref, pre_g_ref, wq_ref, wo_ref, post_g_ref, *rest):
    o_ref, cast_refs = rest[len(rest) // 2], rest[:len(rest) // 2] + rest[len(rest) // 2 + 1:]
    _cast_slabs(cast_refs)
    tm, d = x_ref.shape
    dh = d // XA_HEADS
    blocks = [slice(r0, r0 + tm // EDGE_BLOCKS) for r0 in range(0, tm, tm // EDGE_BLOCKS)]
    cols = [slice(n * dh, (n + 1) * dh) for n in range(XA_HEADS)]

    def softmax(s):
        e = jnp.exp(s - jnp.max(s, axis=1, keepdims=True))
        return (e / jnp.sum(e, axis=1, keepdims=True)).astype(BF16)

    qs = []
    for rows in blocks:
        h = _rms(x_ref[rows, :], pre_g_ref[...]).astype(BF16)
        qs.append((_dot(h, wq_ref[...]) * (dh ** -0.5)).astype(BF16))
    logits = [[_dot_nt(q[:, c], k_ref[:, c]) for c in cols] for q in qs]
    probs = [[softmax(s) for s in per_head] for per_head in logits]
    heads = [jnp.concatenate([_dot(p, v_ref[:, c]).astype(BF16) for p, c in zip(per_head, cols)], axis=1)
             for per_head in probs]
    for rows, attended in zip(blocks, heads):
        y = _dot(attended, wo_ref[...])
        o_ref[rows, :] = x_ref[rows, :] + _rms(y, post_g_ref[...])


def _xattn(layer, x, k, v, pre_g, w_q, w_o, post_g, *, bsz, seq, n_mem, tm, cast):
    m, d = x.shape
    nl = seq // tm
    row = lambda b, l: (b * nl + l, 0)
    extra = [_cast_specs(cast[0], w, bsz * nl, lambda b, l: b * nl + l) for w in cast[1:]]
    outs = pl.pallas_call(
        _xattn_kernel,
        grid=(bsz, nl),
        in_specs=[
            pl.BlockSpec((tm, d), row),
            _layer_spec(layer, (n_mem, d), lambda b, l: (b, 0)),
            _layer_spec(layer, (n_mem, d), lambda b, l: (b, 0)),
            _layer_spec(layer, (1, d)),
            pl.BlockSpec((d, d), lambda b, l: (0, 0), pipeline_mode=RESIDENT),
            pl.BlockSpec((d, d), lambda b, l: (0, 0), pipeline_mode=RESIDENT),
            _layer_spec(layer, (1, d)),
        ] + [e[0] for e in extra],
        out_specs=[pl.BlockSpec((tm, d), row)] + [e[1] for e in extra],
        out_shape=[jax.ShapeDtypeStruct((m, d), F32)] + [e[2] for e in extra],
        compiler_params=_params("parallel", "parallel"),
        name="xattn",
    )(x, k, v, pre_g, w_q, w_o, post_g, *cast[1:])
    return outs[0], tuple(outs[1:])


def _tile(n, want):
    t = min(n, want)
    while n % t:
        t -= 1
    return t


def _block_diag(t):
    depth, g, r, c = t.shape
    eye = jnp.eye(g, dtype=t.dtype)
    return jnp.einsum("lgrc,gk->lgrkc", t, eye).reshape(depth, g * r, g * c)


def kernel(x, mem, ffn1_pre_g, ffn1_w_gu, ffn1_w_down, ffn1_post_g, mix_pre_g, w_in, gla_gate_w, gla_gate_b, gla_norm_g, w_gla_up, s5_a_re, s5_a_im, s5_log_dt, s5_b_re, s5_b_im, s5_c_re, s5_c_im, s5_d, s5_glu_w, s5_glu_b, w_s5_up, fox_f_b, w_fox_up, w_mix_out, mix_post_g, xa_pre_g, xa_mem_g, xa_w_q, xa_w_kv, xa_w_o, xa_post_g, ffn2_pre_g, ffn2_w_gu, ffn2_w_down, ffn2_post_g):
    bsz, seq, d = x.shape
    n_mem = mem.shape[1]
    depth = w_in.shape[0]
    d_ff = ffn1_w_down.shape[1]
    m = bsz * seq
    assert seq % CHUNK == 0 and d % LANES == 0

    tm_ffn = _tile(m, 1024)
    tf = _tile(d_ff // LANES, 2) * LANES
    tm = _tile(seq, 1024)
    tk = _tile(seq, 512)
    ts = _tile(seq, 128)
    tb = 2 * ts if seq % (2 * ts) == 0 else ts

    bf = lambda a: a.astype(BF16)
    vec = lambda a: a.reshape(depth, 1, a.shape[-1])

    aligned = 2 * GLA_QK + 2 * GLA_V
    mid = S5_WIDTH + 3 * FOX_W
    gates_at = aligned + GLA_RANK + mid + FOX_HEADS
    assert aligned % LANES == 0 and mid % LANES == 0 and aligned + mid == OFF_SMALL and IN_SMALL == LANES
    assert gates_at % LANES == GLA_RANK + FOX_HEADS
    w_main, w_gate, w_small = _repack(
        bf(w_in), rows=_tile(d, 256), aligned=aligned // LANES, mid=mid // LANES, gate_base=gates_at // LANES,
        gates=N_BRANCH * d // LANES, shift_mid=GLA_RANK, shift_gate=GLA_RANK + FOX_HEADS)
    gla_wg = bf(jnp.concatenate([gla_gate_w, jnp.zeros((depth, IN_SMALL - GLA_RANK, GLA_QK), gla_gate_w.dtype)],
                                axis=1))

    ab_re, ab_im, bb_re, bb_im = _s5prep(s5_a_re, s5_a_im, s5_log_dt, s5_b_re, s5_b_im)
    grp = lambda t: t.reshape(depth, S5_GROUPS, S5_STATE, S5_CH)
    s5_bre = bf(_block_diag(grp(bb_re).transpose(0, 1, 3, 2)))
    s5_bim = bf(_block_diag(grp(bb_im).transpose(0, 1, 3, 2)))
    s5_cre = bf(_block_diag(s5_c_re.transpose(0, 1, 3, 2)))
    s5_cim = bf(_block_diag(s5_c_im.transpose(0, 1, 3, 2)))
    ab_re = ab_re.reshape(depth, 1, S5_NSTATE)
    ab_im = ab_im.reshape(depth, 1, S5_NSTATE)

    glu_w, xa_kv = bf(s5_glu_w), bf(xa_w_kv)
    fox_b = fox_f_b.reshape(depth, FOX_HEADS, 1)

    xs = x.reshape(m, d)
    mk, mv, ffn1_w = _memkv(mem.reshape(bsz * n_mem, d), vec(xa_mem_g), xa_kv, tm=_tile(bsz * n_mem, 512),
                            cast=(0, ffn1_w_gu, ffn1_w_down))
    for l in range(depth):
        xs, (gla_up, s5_up, fox_up, mix_out, xa_q, xa_o) = _ffn(
            l, xs, vec(ffn1_pre_g), *ffn1_w, vec(ffn1_post_g), tm=tm_ffn, tf=tf,
            cast=(l, w_gla_up, w_s5_up, w_fox_up, w_mix_out, xa_w_q, xa_w_o))

        qk, vr, su, fox, small = _inproj(l, xs, vec(mix_pre_g), w_main, w_small, bsz=bsz, seq=seq, tm=tm)
        gla_o = _gla(l, qk, vr, small, gla_wg, vec(gla_gate_b), vec(gla_norm_g), bsz=bsz, seq=seq)
        f_cols = _foxprep(l, small, fox_b, bsz=bsz, seq=seq)
        fox_o = _fox(fox, f_cols, bsz=bsz, seq=seq, tk=tk)
        s5_o = _s5(l, su, ab_re, ab_im, s5_bre, s5_bim, s5_cre, s5_cim,
                   vec(s5_d), glu_w, vec(s5_glu_b), bsz=bsz, seq=seq, tb=tb, ts=ts)
        xs = _merge(l, xs, gla_o, s5_o, fox_o, vec(mix_pre_g), w_gate, gla_up,
                    s5_up, fox_up, mix_out, vec(mix_post_g), bsz=bsz, seq=seq, tm=tm)

        xs, ffn2_w = _xattn(l, xs, mk, mv, vec(xa_pre_g), xa_q, xa_o, vec(xa_post_g), bsz=bsz, seq=seq, n_mem=n_mem,
                            tm=tm, cast=(l, ffn2_w_gu, ffn2_w_down))

        ahead = (l + 1, ffn1_w_gu, ffn1_w_down) if l + 1 < depth else None
        xs, ffn1_w = _ffn(l, xs, vec(ffn2_pre_g), *ffn2_w, vec(ffn2_post_g), tm=tm_ffn, tf=tf, cast=ahead)
    return xs.reshape(bsz, seq, d)
```

```python
import functools
import math

import jax
import jax.numpy as jnp
from jax import lax
from jax.experimental import pallas as pl
from jax.experimental.pallas import tpu as pltpu

F32 = jnp.float32
BF16 = jnp.bfloat16
EPS = 1e-6
LOG2E = math.log2(math.e)

LANES = 128
SUBLANES = 8
VMEM_LIMIT_BYTES = 56 * 1024 * 1024

EDGE_BLOCKS = 4
CHUNK = 64
GLA_GROUP = 8
GLA_HEADS, GLA_DK, GLA_DV, GLA_RANK, GLA_TAU = 4, 32, 64, 16, 16.0
GLA_QK = GLA_HEADS * GLA_DK
GLA_V = GLA_HEADS * GLA_DV
DK_SHIFT = GLA_DK.bit_length() - 1
DV_SHIFT = GLA_DV.bit_length() - 1
S5_GROUPS, S5_CH, S5_STATE = 16, 16, 64
S5_WIDTH = S5_GROUPS * S5_CH
S5_NSTATE = S5_GROUPS * S5_STATE
FOX_HEADS, FOX_DH = 8, 64
FOX_W = FOX_HEADS * FOX_DH
FOX_PAIRS = FOX_W // LANES
FOX_TERMS = 4
HEADS_SHIFT = FOX_HEADS.bit_length() - 1
XA_HEADS = 4
N_BRANCH = 3

NT_DIMS = (((1,), (1,)), ((), ()))
RESIDENT = pl.Buffered(1)


def _params(*semantics):
    return pltpu.CompilerParams(dimension_semantics=semantics, vmem_limit_bytes=VMEM_LIMIT_BYTES)


def _layer_spec(layer, block, index_map=None, pipeline_mode=None):
    block = tuple(block)
    if index_map is None:
        index_map = lambda *_: (0,) * len(block)
    return pl.BlockSpec((None,) + block, lambda *ids: (layer,) + tuple(index_map(*ids)),
                        pipeline_mode=pipeline_mode)


def _rms(x, g):
    return x * lax.rsqrt(jnp.mean(x * x, axis=-1, keepdims=True) + EPS) * g


def _dot(a, b):
    return jnp.dot(a, b, preferred_element_type=F32)


def _dot_nt(a, b):
    return lax.dot_general(a, b, NT_DIMS, preferred_element_type=F32)


def _sigmoid(x):
    return 0.5 * jnp.tanh(0.5 * x) + 0.5


def _log_sigmoid(x):
    return jnp.minimum(x, 0.0) - jnp.log1p(jnp.exp(-jnp.abs(x)))


def _split3(x):
    hi = x.astype(BF16)
    r1 = x - hi.astype(F32)
    mid = r1.astype(BF16)
    lo = (r1 - mid.astype(F32)).astype(BF16)
    return hi, mid, lo


def _cast_specs(layer, stacked, n_steps, step):
    _, rows, cols = stacked.shape
    slab = rows // n_steps
    assert slab * n_steps == rows and slab % (2 * SUBLANES) == 0
    return (pl.BlockSpec((None, slab, cols), lambda *ids: (layer, step(*ids), 0)),
            pl.BlockSpec((slab, cols), lambda *ids: (step(*ids), 0)),
            jax.ShapeDtypeStruct((rows, cols), BF16))


def _cast_slabs(refs):
    half = len(refs) // 2
    for src, dst in zip(refs[:half], refs[half:]):
        dst[...] = src[...].astype(BF16)


def _ffn_kernel(x_ref, pre_g_ref, wgu_ref, wd_ref, post_g_ref, *rest, tf):
    o_ref, cast_refs = rest[len(rest) // 2], rest[:len(rest) // 2] + rest[len(rest) // 2 + 1:]
    _cast_slabs(cast_refs)
    d_ff = wd_ref.shape[0]
    tm = x_ref.shape[0]
    blocks = [slice(r0, r0 + tm // EDGE_BLOCKS) for r0 in range(0, tm, tm // EDGE_BLOCKS)]

    def gate_up(h, c0):
        return _dot(h, wgu_ref[:, c0:c0 + tf]), _dot(h, wgu_ref[:, d_ff + c0:d_ff + c0 + tf])

    hs, first = [], []
    for rows in blocks:
        hs.append(_rms(x_ref[rows, :], pre_g_ref[...]).astype(BF16))
        first.append(gate_up(hs[-1], 0))
    h = jnp.concatenate(hs, axis=0)
    nxt = tuple(jnp.concatenate(t, axis=0) for t in zip(*first))

    acc = None
    for c0 in range(0, d_ff - tf, tf):
        gate, up = nxt
        nxt = gate_up(h, c0 + tf)
        a = (gate * _sigmoid(gate) * up).astype(BF16)
        part = _dot(a, wd_ref[c0:c0 + tf, :])
        acc = part if acc is None else acc + part

    gate, up = nxt
    a = (gate * _sigmoid(gate) * up).astype(BF16)
    for rows in blocks:
        total = _dot(a[rows, :], wd_ref[d_ff - tf:d_ff, :])
        if acc is not None:
            total = acc[rows, :] + total
        o_ref[rows, :] = x_ref[rows, :] + 0.5 * _rms(total, post_g_ref[...])


def _ffn(layer, x, pre_g, w_gu, w_down, post_g, *, tm, tf, cast=None):
    m, d = x.shape
    d_ff = w_down.shape[0]
    n_steps = m // tm
    extra = [_cast_specs(cast[0], w, n_steps, lambda i: i) for w in cast[1:]] if cast else []
    outs = pl.pallas_call(
        functools.partial(_ffn_kernel, tf=tf),
        grid=(n_steps,),
        in_specs=[
            pl.BlockSpec((tm, d), lambda i: (i, 0)),
            _layer_spec(layer, (1, d)),
            pl.BlockSpec((d, 2 * d_ff), lambda i: (0, 0), pipeline_mode=RESIDENT),
            pl.BlockSpec((d_ff, d), lambda i: (0, 0), pipeline_mode=RESIDENT),
            _layer_spec(layer, (1, d)),
        ] + [e[0] for e in extra],
        out_specs=[pl.BlockSpec((tm, d), lambda i: (i, 0))] + [e[1] for e in extra],
        out_shape=[jax.ShapeDtypeStruct((m, d), F32)] + [e[2] for e in extra],
        compiler_params=_params("parallel"),
        name="ffn",
    )(x, pre_g, w_gu, w_down, post_g, *(cast[1:] if cast else ()))
    return outs[0], tuple(outs[1:])


def _repack_kernel(w_ref, tail_ref, main_ref, gate_ref, small_ref, *, aligned, mid, gate_base, gates, shift_mid,
                   shift_gate):
    rows = w_ref.shape[0]
    block = lambda n: w_ref[:, n * LANES:(n + 1) * LANES]
    last = w_ref.shape[1] // LANES - 1

    def window(n):
        return tail_ref[...] if n == last else w_ref[:, n * LANES:(n + 2) * LANES]

    def shifted(first, count, shift):
        windows = jnp.concatenate([window(first + i) for i in range(count)], axis=0)
        src = lax.broadcasted_iota(jnp.int32, (2 * LANES, LANES), 0)
        dst = lax.broadcasted_iota(jnp.int32, (2 * LANES, LANES), 1)
        out = _dot(windows, (src == dst + shift).astype(BF16)).astype(BF16)
        return [out[i * rows:(i + 1) * rows, :] for i in range(count)]

    main_ref[:, 0:aligned * LANES] = w_ref[:, 0:aligned * LANES]
    for i, piece in enumerate(shifted(aligned, mid, shift_mid)):
        main_ref[:, (aligned + i) * LANES:(aligned + i + 1) * LANES] = piece
    for i, piece in enumerate(shifted(gate_base, gates, shift_gate)):
        gate_ref[:, i * LANES:(i + 1) * LANES] = piece
    lane = lax.broadcasted_iota(jnp.int32, (rows, LANES), 1)
    small_ref[...] = jnp.where(lane < shift_mid, block(aligned),
                               jnp.where(lane < shift_gate, block(gate_base), jnp.zeros((rows, LANES), BF16)))


def _repack(w, *, rows, aligned, mid, gate_base, gates, shift_mid, shift_gate):
    depth, d, cols = w.shape
    spec = lambda n: pl.BlockSpec((None, rows, n), lambda l, i: (l, i, 0))
    last = cols // LANES - 1
    assert gate_base + gates - 1 == last and cols > (last + 1) * LANES
    tail = jnp.pad(w[:, :, last * LANES:], ((0, 0), (0, 0), (0, (last + 2) * LANES - cols)))
    return pl.pallas_call(
        functools.partial(_repack_kernel, aligned=aligned, mid=mid, gate_base=gate_base, gates=gates,
                          shift_mid=shift_mid, shift_gate=shift_gate),
        grid=(depth, d // rows),
        in_specs=[spec(cols), spec(2 * LANES)],
        out_specs=[spec((aligned + mid) * LANES), spec(gates * LANES), spec(LANES)],
        out_shape=[jax.ShapeDtypeStruct((depth, d, (aligned + mid) * LANES), BF16),
                   jax.ShapeDtypeStruct((depth, d, gates * LANES), BF16),
                   jax.ShapeDtypeStruct((depth, d, LANES), BF16)],
        compiler_params=_params("parallel", "parallel"),
        name="repack",
    )(w, tail)


IN_QK = 2 * GLA_QK
IN_VR = 2 * GLA_V
IN_SU = S5_WIDTH
IN_FOX = 3 * FOX_W
IN_SMALL = LANES
OFF_VR = IN_QK
OFF_SU = OFF_VR + IN_VR
OFF_FOX = OFF_SU + IN_SU
OFF_SMALL = OFF_FOX + IN_FOX


def _inproj_kernel(x_ref, g_ref, w_ref, ws_ref, qk_ref, vr_ref, su_ref, fox_ref, small_ref):
    tm = x_ref.shape[0]
    hs = []
    for r0 in range(0, tm, tm // EDGE_BLOCKS):
        rows = slice(r0, r0 + tm // EDGE_BLOCKS)
        hs.append(_rms(x_ref[rows, :], g_ref[...]).astype(BF16))
        qk_ref[rows, :] = _dot(hs[-1], w_ref[:, 0:OFF_VR])
    h = jnp.concatenate(hs, axis=0)
    vr_ref[...] = _dot(h, w_ref[:, OFF_VR:OFF_SU]).astype(BF16)
    su_ref[...] = _dot(h, w_ref[:, OFF_SU:OFF_FOX])
    fox_ref[:, 0:FOX_W] = (_dot(h, w_ref[:, OFF_FOX:OFF_FOX + FOX_W]) * (FOX_DH ** -0.5 * LOG2E)).astype(BF16)
    fox_ref[:, FOX_W:IN_FOX] = _dot(h, w_ref[:, OFF_FOX + FOX_W:OFF_SMALL]).astype(BF16)
    small_ref[...] = _dot(h, ws_ref[...])


def _inproj(layer, x, g, w, w_small, *, bsz, seq, tm):
    m, d = x.shape
    nl = seq // tm
    row = lambda b, l: (b * nl + l, 0)
    return pl.pallas_call(
        _inproj_kernel,
        grid=(bsz, nl),
        in_specs=[
            pl.BlockSpec((tm, d), row),
            _layer_spec(layer, (1, d)),
            _layer_spec(layer, (d, OFF_SMALL), pipeline_mode=RESIDENT),
            _layer_spec(layer, (d, IN_SMALL), pipeline_mode=RESIDENT),
        ],
        out_specs=[
            pl.BlockSpec((tm, IN_QK), row),
            pl.BlockSpec((tm, IN_VR), row),
            pl.BlockSpec((tm, IN_SU), lambda b, l: (l, b)),
            pl.BlockSpec((tm, IN_FOX), row),
            pl.BlockSpec((tm, IN_SMALL), row),
        ],
        out_shape=[
            jax.ShapeDtypeStruct((m, IN_QK), F32),
            jax.ShapeDtypeStruct((m, IN_VR), BF16),
            jax.ShapeDtypeStruct((seq, bsz * IN_SU), F32),
            jax.ShapeDtypeStruct((m, IN_FOX), BF16),
            jax.ShapeDtypeStruct((m, IN_SMALL), F32),
        ],
        compiler_params=_params("parallel", "parallel"),
        name="inproj",
    )(x, g, w, w_small)


def _gla_kernel(qk_ref, vr_ref, small_ref, wg_ref, bg_ref, ng_ref, o_ref, la_sc):
    seq = qk_ref.shape[0]
    n_chunks = seq // CHUNK
    c, hq, hv = CHUNK, GLA_QK, GLA_V

    z = _dot(small_ref[...].astype(BF16), wg_ref[...]) + bg_ref[...]
    la_sc[...] = _log_sigmoid(z) * (1.0 / GLA_TAU)

    row = lax.broadcasted_iota(jnp.int32, (c, c), 0)
    col = lax.broadcasted_iota(jnp.int32, (c, c), 1)
    tri = (col <= row).astype(BF16)
    row_x = lax.broadcasted_iota(jnp.int32, (c, GLA_HEADS * c), 0)
    col_x = lax.broadcasted_iota(jnp.int32, (c, GLA_HEADS * c), 1)
    lower = (col_x & (c - 1)) <= row_x
    lane_q = lax.broadcasted_iota(jnp.int32, (c, hq), 1)
    lane_v = lax.broadcasted_iota(jnp.int32, (c, hv), 1)
    q_head = [(lane_q >> DK_SHIFT) == h for h in range(GLA_HEADS)]
    v_head = [(lane_v >> DV_SHIFT) == h for h in range(GLA_HEADS)]
    st_row = lax.broadcasted_iota(jnp.int32, (hv, hq), 0)
    st_col = lax.broadcasted_iota(jnp.int32, (hv, hq), 1)
    same_head = (st_row >> DV_SHIFT) == (st_col >> DK_SHIFT)

    def expand(t, heads):
        return jnp.concatenate([jnp.where(heads[h], t, 0).astype(BF16) for h in range(GLA_HEADS)], axis=0)

    gr = lax.broadcasted_iota(jnp.int32, (hv, hv), 0) >> DV_SHIFT
    gc = lax.broadcasted_iota(jnp.int32, (hv, hv), 1) >> DV_SHIFT
    avg = jnp.where(gr == gc, 1.0 / GLA_DV, 0.0).astype(BF16)

    def within_chunks(group):
        stage1 = []
        for i in group:
            hi, mid, lo = _split3(la_sc[i * c:(i + 1) * c, :])
            stage1.append(_dot(tri, hi) + _dot(tri, mid) + _dot(tri, lo))
        stage2 = []
        for i, g in zip(group, stage1):
            rows = slice(i * c, (i + 1) * c)
            g_last = g[c - 1:c, :]
            eg, ieg = jnp.exp(g), jnp.exp(-g)
            qc = qk_ref[rows, 0:hq] * (GLA_DK ** -0.5)
            kc = qk_ref[rows, hq:2 * hq]
            v = vr_ref[rows, 0:hv]
            q_fwd = qc * eg
            a_fwd = _dot_nt(q_fwd.astype(BF16), expand(kc * ieg, q_head))
            a_bwd = _dot_nt((qc * ieg).astype(BF16), expand(kc * eg, q_head))
            kw = (kc * jnp.exp(g_last - g)).astype(BF16)
            ds_t = _dot(v.astype(F32).T.astype(BF16), kw)
            stage2.append((a_fwd, a_bwd, v, q_fwd.astype(BF16), jnp.where(same_head, ds_t, 0.0), jnp.exp(g_last)))
        results = []
        for a_fwd, a_bwd, v, q_fwd, ds_t, decay in stage2:
            attn = jnp.where(lower, a_fwd, a_bwd).astype(BF16)
            o = _dot(attn, expand(v, v_head))
            results.append((o, q_fwd, ds_t, decay))
        return results

    def finish(group, results, s_t):
        outs = []
        for o, q_fwd, ds_t, decay in results:
            outs.append(o + _dot_nt(q_fwd, s_t.astype(BF16)))
            s_t = s_t * decay + ds_t
        o = jnp.concatenate(outs, axis=0)
        rows = slice(group[0] * c, (group[-1] + 1) * c)
        hi, mid, lo = _split3(o * o)
        ms = _dot(hi, avg) + _dot(mid, avg) + _dot(lo, avg)
        r = vr_ref[rows, hv:2 * hv].astype(F32)
        o_ref[rows, :] = (o * lax.rsqrt(ms + EPS) * ng_ref[...] * (r * _sigmoid(r))).astype(BF16)
        return s_t

    groups = [range(first, min(first + GLA_GROUP, n_chunks)) for first in range(0, n_chunks, GLA_GROUP)]
    s_t = jnp.zeros((hv, hq), F32)
    pending = None
    for group in groups:
        results = within_chunks(group)
        if pending is not None:
            s_t = finish(*pending, s_t)
        pending = (group, results)
    finish(*pending, s_t)


def _gla(layer, qk, vr, small, wg, bg, ng, *, bsz, seq):
    m = qk.shape[0]
    row = lambda b: (b, 0)
    return pl.pallas_call(
        _gla_kernel,
        grid=(bsz,),
        in_specs=[
            pl.BlockSpec((seq, IN_QK), row),
            pl.BlockSpec((seq, IN_VR), row),
            pl.BlockSpec((seq, IN_SMALL), row),
            _layer_spec(layer, (IN_SMALL, GLA_QK)),
            _layer_spec(layer, (1, GLA_QK)),
            _layer_spec(layer, (1, GLA_V)),
        ],
        out_specs=pl.BlockSpec((seq, GLA_V), row),
        out_shape=jax.ShapeDtypeStruct((m, GLA_V), BF16),
        scratch_shapes=[pltpu.VMEM((seq, GLA_QK), F32)],
        compiler_params=_params("parallel"),
        name="gla",
    )(qk, vr, small, wg, bg, ng)


def _foxprep_kernel(small_ref, b_ref, o_ref, *, bsz):
    seq = small_ref.shape[0] // bsz
    ff = jnp.concatenate([small_ref[b * seq:(b + 1) * seq, :].T[GLA_RANK:GLA_RANK + FOX_HEADS, :]
                          for b in range(bsz)], axis=0)
    x = _log_sigmoid(ff + jnp.concatenate([b_ref[...]] * bsz, axis=0))
    lane = lax.broadcasted_iota(jnp.int32, x.shape, 1)
    shift = 1
    while shift < seq:
        x = x + jnp.where(lane >= shift, pltpu.roll(x, shift, axis=1), 0.0)
        shift *= 2
    hi, mid, lo = (t.astype(F32) for t in _split3(x * LOG2E))
    ones = jnp.ones((FOX_HEADS, seq), F32)
    pad = jnp.zeros((LANES - FOX_TERMS * FOX_HEADS, seq), F32)
    for b in range(bsz):
        mine = slice(b * FOX_HEADS, (b + 1) * FOX_HEADS)
        terms = jnp.concatenate([hi[mine], mid[mine], lo[mine], ones, pad], axis=0)
        o_ref[b * seq:(b + 1) * seq, :] = terms.T.astype(BF16)


def _foxprep(layer, small, bias, *, bsz, seq):
    whole = lambda i: (0, 0)
    return pl.pallas_call(
        functools.partial(_foxprep_kernel, bsz=bsz),
        grid=(1,),
        in_specs=[
            pl.BlockSpec((bsz * seq, IN_SMALL), whole),
            _layer_spec(layer, (FOX_HEADS, 1)),
        ],
        out_specs=pl.BlockSpec((bsz * seq, LANES), whole),
        out_shape=jax.ShapeDtypeStruct((bsz * seq, LANES), BF16),
        compiler_params=_params("arbitrary"),
        name="foxprep",
    )(small, bias)


def _fox_placement(head, base, *, key):
    r = lax.broadcasted_iota(jnp.int32, (LANES, LANES), 0)
    rel = lax.broadcasted_iota(jnp.int32, (LANES, LANES), 1) - base
    term = r >> HEADS_SHIFT
    mine = ((r & (FOX_HEADS - 1)) == head) & (term < FOX_TERMS)
    is_part = mine & (term < FOX_TERMS - 1)
    is_one = mine & (term == FOX_TERMS - 1)
    n = FOX_TERMS - 1
    if key:
        m = jnp.where(is_part & (rel == term + n), -1.0, jnp.where(is_one & (rel >= 0) & (rel < n), 1.0, 0.0))
    else:
        m = jnp.where(is_part & (rel == term), 1.0, jnp.where(is_one & (rel >= n) & (rel < 2 * n), 1.0, 0.0))
    return m.astype(BF16)


def _fox_kernel(q_ref, k_ref, v_ref, f_ref, o_ref, qx_sc, kx_sc, vx_sc, *, tk):
    seq = k_ref.shape[0]
    spare = (FOX_DH, 0)
    lane = lax.broadcasted_iota(jnp.int32, (seq, LANES), 1)
    q, k, v, f = q_ref[...], k_ref[...], v_ref[...], f_ref[...]
    for h in range(2):
        own = (lane < FOX_DH) if h == 0 else (lane >= FOX_DH)
        head = 2 * pl.program_id(1) + h
        qx_sc[h] = jnp.where(own, q, _dot(f, _fox_placement(head, spare[h], key=False)).astype(BF16))
        kx_sc[h] = jnp.where(own, k, _dot(f, _fox_placement(head, spare[h], key=True)).astype(BF16))
        vx_sc[h] = jnp.where(own, v, jnp.ones_like(v))

    row = lax.broadcasted_iota(jnp.int32, (tk, tk), 0)
    col = lax.broadcasted_iota(jnp.int32, (tk, tk), 1)
    causal = col <= row

    items = [(h, r0, c0) for c0 in range(0, seq, tk) for h in range(2) for r0 in range(c0, seq, tk)]

    def logits(item):
        h, r0, c0 = item
        s = _dot_nt(qx_sc[h, r0:r0 + tk, :], kx_sc[h, c0:c0 + tk, :])
        return jnp.where(causal, s, -jnp.inf) if r0 == c0 else s

    state = {}
    s_next = logits(items[0])
    for n, (h, r0, c0) in enumerate(items):
        s = s_next
        if n + 1 < len(items):
            s_next = logits(items[n + 1])
        vx = vx_sc[h, c0:c0 + tk, :]
        m_new = jnp.max(s, axis=1, keepdims=True)
        if c0 == 0:
            acc = _dot(jnp.exp2(s - m_new).astype(BF16), vx)
        else:
            m_old, acc_old = state[h, r0]
            m_new = jnp.maximum(m_old, m_new)
            acc = jnp.exp2(m_old - m_new) * acc_old + _dot(jnp.exp2(s - m_new).astype(BF16), vx)
        state[h, r0] = (m_new, acc)

    first = lax.broadcasted_iota(jnp.int32, (tk, LANES), 1) < FOX_DH
    for r0 in range(0, seq, tk):
        a0, a1 = state[0, r0][1], state[1, r0][1]
        o_ref[r0:r0 + tk, :] = jnp.where(first, a0 / pltpu.roll(a0, FOX_DH, axis=1),
                                         a1 / pltpu.roll(a1, FOX_DH, axis=1)).astype(BF16)


def _fox(fox, f, *, bsz, seq, tk):
    m = fox.shape[0]
    return pl.pallas_call(
        functools.partial(_fox_kernel, tk=tk),
        grid=(bsz, FOX_PAIRS),
        in_specs=[
            pl.BlockSpec((seq, LANES), lambda b, p: (b, p)),
            pl.BlockSpec((seq, LANES), lambda b, p: (b, FOX_PAIRS + p)),
            pl.BlockSpec((seq, LANES), lambda b, p: (b, 2 * FOX_PAIRS + p)),
            pl.BlockSpec((seq, LANES), lambda b, p: (b, 0)),
        ],
        out_specs=pl.BlockSpec((seq, LANES), lambda b, p: (b, p)),
        out_shape=jax.ShapeDtypeStruct((m, FOX_W), BF16),
        scratch_shapes=[pltpu.VMEM((2, seq, LANES), BF16)] * 3,
        compiler_params=_params("parallel", "parallel"),
        name="fox",
    )(fox, fox, fox, f)


def _s5prep_kernel(are_ref, aim_ref, ldt_ref, bre_ref, bim_ref, abre_ref, abim_ref, bbre_ref, bbim_ref):
    lam_re = jnp.minimum(are_ref[...], -1e-4)
    lam_im = aim_ref[...]
    dt = jnp.exp(ldt_ref[...])
    mag = jnp.exp(lam_re * dt)
    ab_re = mag * jnp.cos(lam_im * dt)
    ab_im = mag * jnp.sin(lam_im * dt)
    den = lam_re * lam_re + lam_im * lam_im
    z_re = ((ab_re - 1.0) * lam_re + ab_im * lam_im) / den
    z_im = (ab_im * lam_re - (ab_re - 1.0) * lam_im) / den
    br, bi = bre_ref[...], bim_ref[...]
    abre_ref[...] = ab_re
    abim_ref[...] = ab_im
    bbre_ref[...] = z_re * br - z_im * bi
    bbim_ref[...] = z_re * bi + z_im * br


def _s5prep(a_re, a_im, log_dt, b_re, b_im):
    depth = a_re.shape[0]
    col = lambda a: a.reshape(depth, S5_NSTATE, 1)
    ldt = jnp.broadcast_to(log_dt[:, :, None], (depth, S5_GROUPS, S5_STATE))
    vec = pl.BlockSpec((None, S5_NSTATE, 1), lambda l: (l, 0, 0))
    mat = pl.BlockSpec((None, S5_NSTATE, S5_CH), lambda l: (l, 0, 0))
    return pl.pallas_call(
        _s5prep_kernel,
        grid=(depth,),
        in_specs=[vec, vec, vec, mat, mat],
        out_specs=[vec, vec, mat, mat],
        out_shape=[jax.ShapeDtypeStruct((depth, S5_NSTATE, 1), F32)] * 2
        + [jax.ShapeDtypeStruct((depth, S5_NSTATE, S5_CH), F32)] * 2,
        compiler_params=_params("parallel"),
        name="s5prep",
    )(col(a_re), col(a_im), col(ldt), b_re.reshape(depth, S5_NSTATE, S5_CH), b_im.reshape(depth, S5_NSTATE, S5_CH))


def _s5_kernel(u_ref, abre_ref, abim_ref, bre_ref, bim_ref, cre_ref, cim_ref, d_ref, gw_ref, gb_ref, o_ref,
               sre_sc, sim_sc, u_sc, y_sc, *, bsz, ts):
    @pl.when(pl.program_id(0) == 0)
    def _():
        sre_sc[...] = jnp.zeros_like(sre_sc)
        sim_sc[...] = jnp.zeros_like(sim_sc)

    tb = u_ref.shape[0]
    slabs = S5_WIDTH // LANES
    for b in range(bsz):
        for j in range(slabs):
            lanes = slice(b * S5_WIDTH + j * LANES, b * S5_WIDTH + (j + 1) * LANES)
            u_sc[j, pl.ds(b, tb, stride=bsz), :] = u_ref[:, lanes]

    rows = ts * bsz
    n_sub = tb // ts
    a_re = jnp.broadcast_to(abre_ref[...], (bsz, S5_NSTATE))
    a_im = jnp.broadcast_to(abim_ref[...], (bsz, S5_NSTATE))
    us = [jnp.concatenate([u_sc[j, k * rows:(k + 1) * rows, :] for j in range(slabs)], axis=1) for k in range(n_sub)]

    bu = []
    for u in us:
        ub = u.astype(BF16)
        bu.append((_dot(ub, bre_ref[...]), _dot(ub, bim_ref[...])))

    s_re, s_im = sre_sc[...], sim_sc[...]
    for k in range(n_sub):
        bu_re, bu_im = bu[k]
        x_re, x_im = [], []
        for t in range(ts):
            r = slice(t * bsz, (t + 1) * bsz)
            s_re, s_im = (a_re * s_re - a_im * s_im + bu_re[r, :], a_re * s_im + a_im * s_re + bu_im[r, :])
            x_re.append(s_re)
            x_im.append(s_im)
        x_re = jnp.concatenate(x_re, axis=0).astype(BF16)
        x_im = jnp.concatenate(x_im, axis=0).astype(BF16)
        y = _dot(x_re, cre_ref[...]) - _dot(x_im, cim_ref[...])
        y = jax.nn.gelu(y + d_ref[...] * us[k], approximate=True)
        y = y * _sigmoid(_dot(y.astype(BF16), gw_ref[...]) + gb_ref[...])
        for j in range(slabs):
            y_sc[j, k * rows:(k + 1) * rows, :] = y[:, j * LANES:(j + 1) * LANES]
    sre_sc[...] = s_re
    sim_sc[...] = s_im

    for b in range(bsz):
        for j in range(slabs):
            lanes = slice(b * S5_WIDTH + j * LANES, b * S5_WIDTH + (j + 1) * LANES)
            o_ref[:, lanes] = y_sc[j, pl.ds(b, tb, stride=bsz), :].astype(BF16)


def _s5(layer, u, ab_re, ab_im, b_re, b_im, c_re, c_im, d, glu_w, glu_b, *, bsz, seq, tb, ts):
    rows = tb * bsz
    slabs = S5_WIDTH // LANES
    return pl.pallas_call(
        functools.partial(_s5_kernel, bsz=bsz, ts=ts),
        grid=(seq // tb,),
        in_specs=[
            pl.BlockSpec((tb, bsz * S5_WIDTH), lambda i: (i, 0)),
            _layer_spec(layer, (1, S5_NSTATE)),
            _layer_spec(layer, (1, S5_NSTATE)),
            _layer_spec(layer, (S5_WIDTH, S5_NSTATE)),
            _layer_spec(layer, (S5_WIDTH, S5_NSTATE)),
            _layer_spec(layer, (S5_NSTATE, S5_WIDTH)),
            _layer_spec(layer, (S5_NSTATE, S5_WIDTH)),
            _layer_spec(layer, (1, S5_WIDTH)),
            _layer_spec(layer, (S5_WIDTH, S5_WIDTH)),
            _layer_spec(layer, (1, S5_WIDTH)),
        ],
        out_specs=pl.BlockSpec((tb, bsz * S5_WIDTH), lambda i: (i, 0)),
        out_shape=jax.ShapeDtypeStruct((seq, bsz * S5_WIDTH), BF16),
        scratch_shapes=[pltpu.VMEM((bsz, S5_NSTATE), F32), pltpu.VMEM((bsz, S5_NSTATE), F32),
                        pltpu.VMEM((slabs, rows, LANES), F32), pltpu.VMEM((slabs, rows, LANES), F32)],
        compiler_params=_params("arbitrary"),
        name="s5",
    )(u, ab_re, ab_im, b_re, b_im, c_re, c_im, d, glu_w, glu_b)


def _merge_kernel(x_ref, gla_ref, s5_ref, fox_ref, pre_g_ref, wgate_ref, wgla_ref, ws5_ref, wfox_ref, wo_ref,
                  post_g_ref, o_ref):
    tm, d = x_ref.shape
    blocks = [slice(r0, r0 + tm // EDGE_BLOCKS) for r0 in range(0, tm, tm // EDGE_BLOCKS)]
    branches = ((gla_ref, wgla_ref), (s5_ref, ws5_ref), (fox_ref, wfox_ref))

    hs, first = [], []
    for rows in blocks:
        hs.append(_rms(x_ref[rows, :], pre_g_ref[...]).astype(BF16))
        first.append(_dot(hs[-1], wgate_ref[:, 0:d]))
    h = jnp.concatenate(hs, axis=0)
    logits = jnp.concatenate(first, axis=0)
    mix = None
    for n, (b_ref, w_ref) in enumerate(branches[:-1]):
        term = _sigmoid(logits) * _dot(b_ref[...], w_ref[...])
        mix = term if mix is None else mix + term
        logits = _dot(h, wgate_ref[:, (n + 1) * d:(n + 2) * d])
    up = _dot(branches[-1][0][...], branches[-1][1][...])

    for rows in blocks:
        mixed = mix[rows, :] + _sigmoid(logits[rows, :]) * up[rows, :]
        y = _dot(mixed.astype(BF16), wo_ref[...])
        o_ref[rows, :] = x_ref[rows, :] + _rms(y, post_g_ref[...])


def _merge(layer, x, gla_o, s5_o, fox_o, pre_g, w_gate, w_gla, w_s5, w_fox, w_o, post_g, *, bsz, seq, tm):
    m, d = x.shape
    nl = seq // tm
    row = lambda b, l: (b * nl + l, 0)
    whole = lambda shape: pl.BlockSpec(shape, lambda b, l: (0, 0), pipeline_mode=RESIDENT)
    return pl.pallas_call(
        _merge_kernel,
        grid=(bsz, nl),
        in_specs=[
            pl.BlockSpec((tm, d), row),
            pl.BlockSpec((tm, GLA_V), row),
            pl.BlockSpec((tm, S5_WIDTH), lambda b, l: (l, b)),
            pl.BlockSpec((tm, FOX_W), row),
            _layer_spec(layer, (1, d)),
            _layer_spec(layer, (d, N_BRANCH * d), pipeline_mode=RESIDENT),
            whole((GLA_V, d)),
            whole((S5_WIDTH, d)),
            whole((FOX_W, d)),
            whole((d, d)),
            _layer_spec(layer, (1, d)),
        ],
        out_specs=pl.BlockSpec((tm, d), row),
        out_shape=jax.ShapeDtypeStruct((m, d), F32),
        compiler_params=_params("parallel", "parallel"),
        name="merge",
    )(x, gla_o, s5_o, fox_o, pre_g, w_gate, w_gla, w_s5, w_fox, w_o, post_g)


def _memkv_kernel(mem_ref, g_ref, w_ref, *rest):
    n_cast = (len(rest) - 2) // 2
    k_ref, v_ref = rest[n_cast], rest[n_cast + 1]
    _cast_slabs(rest[:n_cast] + rest[n_cast + 2:])
    d = mem_ref.shape[1]
    h = _rms(mem_ref[...], g_ref[...]).astype(BF16)
    k_ref[...] = _dot(h, w_ref[:, 0:d]).astype(BF16)
    v_ref[...] = _dot(h, w_ref[:, d:2 * d]).astype(BF16)


def _memkv(mem, g, w_kv, *, tm, cast):
    m, d = mem.shape
    depth = w_kv.shape[0]
    tiles = m // tm
    per_layer = lambda block: pl.BlockSpec((None,) + block, lambda l, i: (l, 0, 0))
    out = pl.BlockSpec((None, tm, d), lambda l, i: (l, i, 0))
    extra = [_cast_specs(cast[0], w, depth * tiles, lambda l, i: l * tiles + i) for w in cast[1:]]
    outs = pl.pallas_call(
        _memkv_kernel,
        grid=(depth, tiles),
        in_specs=[pl.BlockSpec((tm, d), lambda l, i: (i, 0)), per_layer((1, d)), per_layer((d, 2 * d))]
        + [e[0] for e in extra],
        out_specs=[out, out] + [e[1] for e in extra],
        out_shape=[jax.ShapeDtypeStruct((depth, m, d), BF16)] * 2 + [e[2] for e in extra],
        compiler_params=_params("parallel", "parallel"),
        name="memkv",
    )(mem, g, w_kv, *cast[1:])
    return outs[0], outs[1], tuple(outs[2:])


def _xattn_kernel(x_ref, k_ref, v_ref, pre_g_ref, wq_ref, wo_ref, post_g_ref, *rest):
    o_ref, cast_refs = rest[len(rest) // 2], rest[:len(rest) // 2] + rest[len(rest) // 2 + 1:]
    _cast_slabs(cast_refs)
    tm, d = x_ref.shape
    dh = d // XA_HEADS
    blocks = [slice(r0, r0 + tm // EDGE_BLOCKS) for r0 in range(0, tm, tm // EDGE_BLOCKS)]
    cols = [slice(n * dh, (n + 1) * dh) for n in range(XA_HEADS)]

    def softmax(s):
        e = jnp.exp(s - jnp.max(s, axis=1, keepdims=True))
        return (e / jnp.sum(e, axis=1, keepdims=True)).astype(BF16)

    qs = []
    for rows in blocks:
        h = _rms(x_ref[rows, :], pre_g_ref[...]).astype(BF16)
        qs.append((_dot(h, wq_ref[...]) * (dh ** -0.5)).astype(BF16))
    logits = [[_dot_nt(q[:, c], k_ref[:, c]) for c in cols] for q in qs]
    probs = [[softmax(s) for s in per_head] for per_head in logits]
    heads = [jnp.concatenate([_dot(p, v_ref[:, c]).astype(BF16) for p, c in zip(per_head, cols)], axis=1)
             for per_head in probs]
    for rows, attended in zip(blocks, heads):
        y = _dot(attended, wo_ref[...])
        o_ref[rows, :] = x_ref[rows, :] + _rms(y, post_g_ref[...])


def _xattn(layer, x, k, v, pre_g, w_q, w_o, post_g, *, bsz, seq, n_mem, tm, cast):
    m, d = x.shape
    nl = seq // tm
    row = lambda b, l: (b * nl + l, 0)
    extra = [_cast_specs(cast[0], w, bsz * nl, lambda b, l: b * nl + l) for w in cast[1:]]
    outs = pl.pallas_call(
        _xattn_kernel,
        grid=(bsz, nl),
        in_specs=[
            pl.BlockSpec((tm, d), row),
            _layer_spec(layer, (n_mem, d), lambda b, l: (b, 0)),
            _layer_spec(layer, (n_mem, d), lambda b, l: (b, 0)),
            _layer_spec(layer, (1, d)),
            pl.BlockSpec((d, d), lambda b, l: (0, 0), pipeline_mode=RESIDENT),
            pl.BlockSpec((d, d), lambda b, l: (0, 0), pipeline_mode=RESIDENT),
            _layer_spec(layer, (1, d)),
        ] + [e[0] for e in extra],
        out_specs=[pl.BlockSpec((tm, d), row)] + [e[1] for e in extra],
        out_shape=[jax.ShapeDtypeStruct((m, d), F32)] + [e[2] for e in extra],
        compiler_params=_params("parallel", "parallel"),
        name="xattn",
    )(x, k, v, pre_g, w_q, w_o, post_g, *cast[1:])
    return outs[0], tuple(outs[1:])


def _tile(n, want):
    t = min(n, want)
    while n % t:
        t -= 1
    return t


def _block_diag(t):
    depth, g, r, c = t.shape
    eye = jnp.eye(g, dtype=t.dtype)
    return jnp.einsum("lgrc,gk->lgrkc", t, eye).reshape(depth, g * r, g * c)


def kernel(x, mem, ffn1_pre_g, ffn1_w_gu, ffn1_w_down, ffn1_post_g, mix_pre_g, w_in, gla_gate_w, gla_gate_b, gla_norm_g, w_gla_up, s5_a_re, s5_a_im, s5_log_dt, s5_b_re, s5_b_im, s5_c_re, s5_c_im, s5_d, s5_glu_w, s5_glu_b, w_s5_up, fox_f_b, w_fox_up, w_mix_out, mix_post_g, xa_pre_g, xa_mem_g, xa_w_q, xa_w_kv, xa_w_o, xa_post_g, ffn2_pre_g, ffn2_w_gu, ffn2_w_down, ffn2_post_g):
    bsz, seq, d = x.shape
    n_mem = mem.shape[1]
    depth = w_in.shape[0]
    d_ff = ffn1_w_down.shape[1]
    m = bsz * seq
    assert seq % CHUNK == 0 and d % LANES == 0

    tm_ffn = _tile(m, 1024)
    tf = _tile(d_ff // LANES, 2) * LANES
    tm = _tile(seq, 1024)
    tk = _tile(seq, 512)
    ts = _tile(seq, 128)
    tb = 2 * ts if seq % (2 * ts) == 0 else ts

    bf = lambda a: a.astype(BF16)
    vec = lambda a: a.reshape(depth, 1, a.shape[-1])

    aligned = 2 * GLA_QK + 2 * GLA_V
    mid = S5_WIDTH + 3 * FOX_W
    gates_at = aligned + GLA_RANK + mid + FOX_HEADS
    assert aligned % LANES == 0 and mid % LANES == 0 and aligned + mid == OFF_SMALL and IN_SMALL == LANES
    assert gates_at % LANES == GLA_RANK + FOX_HEADS
    w_main, w_gate, w_small = _repack(
        bf(w_in), rows=_tile(d, 256), aligned=aligned // LANES, mid=mid // LANES, gate_base=gates_at // LANES,
        gates=N_BRANCH * d // LANES, shift_mid=GLA_RANK, shift_gate=GLA_RANK + FOX_HEADS)
    gla_wg = bf(jnp.concatenate([gla_gate_w, jnp.zeros((depth, IN_SMALL - GLA_RANK, GLA_QK), gla_gate_w.dtype)],
                                axis=1))

    ab_re, ab_im, bb_re, bb_im = _s5prep(s5_a_re, s5_a_im, s5_log_dt, s5_b_re, s5_b_im)
    grp = lambda t: t.reshape(depth, S5_GROUPS, S5_STATE, S5_CH)
    s5_bre = bf(_block_diag(grp(bb_re).transpose(0, 1, 3, 2)))
    s5_bim = bf(_block_diag(grp(bb_im).transpose(0, 1, 3, 2)))
    s5_cre = bf(_block_diag(s5_c_re.transpose(0, 1, 3, 2)))
    s5_cim = bf(_block_diag(s5_c_im.transpose(0, 1, 3, 2)))
    ab_re = ab_re.reshape(depth, 1, S5_NSTATE)
    ab_im = ab_im.reshape(depth, 1, S5_NSTATE)

    glu_w, xa_kv = bf(s5_glu_w), bf(xa_w_kv)
    fox_b = fox_f_b.reshape(depth, FOX_HEADS, 1)

    xs = x.reshape(m, d)
    mk, mv, ffn1_w = _memkv(mem.reshape(bsz * n_mem, d), vec(xa_mem_g), xa_kv, tm=_tile(bsz * n_mem, 512),
                            cast=(0, ffn1_w_gu, ffn1_w_down))
    for l in range(depth):
        xs, (gla_up, s5_up, fox_up, mix_out, xa_q, xa_o) = _ffn(
            l, xs, vec(ffn1_pre_g), *ffn1_w, vec(ffn1_post_g), tm=tm_ffn, tf=tf,
            cast=(l, w_gla_up, w_s5_up, w_fox_up, w_mix_out, xa_w_q, xa_w_o))

        qk, vr, su, fox, small = _inproj(l, xs, vec(mix_pre_g), w_main, w_small, bsz=bsz, seq=seq, tm=tm)
        gla_o = _gla(l, qk, vr, small, gla_wg, vec(gla_gate_b), vec(gla_norm_g), bsz=bsz, seq=seq)
        f_cols = _foxprep(l, small, fox_b, bsz=bsz, seq=seq)
        fox_o = _fox(fox, f_cols, bsz=bsz, seq=seq, tk=tk)
        s5_o = _s5(l, su, ab_re, ab_im, s5_bre, s5_bim, s5_cre, s5_cim,
                   vec(s5_d), glu_w, vec(s5_glu_b), bsz=bsz, seq=seq, tb=tb, ts=ts)
        xs = _merge(l, xs, gla_o, s5_o, fox_o, vec(mix_pre_g), w_gate, gla_up,
                    s5_up, fox_up, mix_out, vec(mix_post_g), bsz=bsz, seq=seq, tm=tm)

        xs, ffn2_w = _xattn(l, xs, mk, mv, vec(xa_pre_g), xa_q, xa_o, vec(xa_post_g), bsz=bsz, seq=seq, n_mem=n_mem,
                            tm=tm, cast=(l, ffn2_w_gu, ffn2_w_down))

        ahead = (l + 1, ffn1_w_gu, ffn1_w_down) if l + 1 < depth else None
        xs, ffn1_w = _ffn(l, xs, vec(ffn2_pre_g), *ffn2_w, vec(ffn2_post_g), tm=tm_ffn, tf=tf, cast=ahead)
    return xs.reshape(bsz, seq, d)
```

```python
import functools
import math

import jax
import jax.numpy as jnp
from jax import lax
from jax.experimental import pallas as pl
from jax.experimental.pallas import tpu as pltpu

F32 = jnp.float32
BF16 = jnp.bfloat16
EPS = 1e-6
LOG2E = math.log2(math.e)

LANES = 128
SUBLANES = 8
VMEM_LIMIT_BYTES = 56 * 1024 * 1024

EDGE_BLOCKS = 4
CHUNK = 64
GLA_GROUP = 8
GLA_HEADS, GLA_DK, GLA_DV, GLA_RANK, GLA_TAU = 4, 32, 64, 16, 16.0
GLA_QK = GLA_HEADS * GLA_DK
GLA_V = GLA_HEADS * GLA_DV
DK_SHIFT = GLA_DK.bit_length() - 1
DV_SHIFT = GLA_DV.bit_length() - 1
S5_GROUPS, S5_CH, S5_STATE = 16, 16, 64
S5_WIDTH = S5_GROUPS * S5_CH
S5_NSTATE = S5_GROUPS * S5_STATE
FOX_HEADS, FOX_DH = 8, 64
FOX_W = FOX_HEADS * FOX_DH
FOX_PAIRS = FOX_W // LANES
FOX_TERMS = 4
HEADS_SHIFT = FOX_HEADS.bit_length() - 1
XA_HEADS = 4
N_BRANCH = 3

NT_DIMS = (((1,), (1,)), ((), ()))
RESIDENT = pl.Buffered(1)


def _params(*semantics):
    return pltpu.CompilerParams(dimension_semantics=semantics, vmem_limit_bytes=VMEM_LIMIT_BYTES)


def _layer_spec(layer, block, index_map=None, pipeline_mode=None):
    block = tuple(block)
    if index_map is None:
        index_map = lambda *_: (0,) * len(block)
    return pl.BlockSpec((None,) + block, lambda *ids: (layer,) + tuple(index_map(*ids)),
                        pipeline_mode=pipeline_mode)


def _rms(x, g):
    return x * lax.rsqrt(jnp.mean(x * x, axis=-1, keepdims=True) + EPS) * g


def _dot(a, b):
    return jnp.dot(a, b, preferred_element_type=F32)


def _dot_nt(a, b):
    return lax.dot_general(a, b, NT_DIMS, preferred_element_type=F32)


def _sigmoid(x):
    return 0.5 * jnp.tanh(0.5 * x) + 0.5


def _log_sigmoid(x):
    return jnp.minimum(x, 0.0) - jnp.log1p(jnp.exp(-jnp.abs(x)))


def _split3(x):
    hi = x.astype(BF16)
    r1 = x - hi.astype(F32)
    mid = r1.astype(BF16)
    lo = (r1 - mid.astype(F32)).astype(BF16)
    return hi, mid, lo


def _cast_specs(layer, stacked, n_steps, step):
    _, rows, cols = stacked.shape
    slab = rows // n_steps
    assert slab * n_steps == rows and slab % (2 * SUBLANES) == 0
    return (pl.BlockSpec((None, slab, cols), lambda *ids: (layer, step(*ids), 0)),
            pl.BlockSpec((slab, cols), lambda *ids: (step(*ids), 0)),
            jax.ShapeDtypeStruct((rows, cols), BF16))


def _cast_slabs(refs):
    half = len(refs) // 2
    for src, dst in zip(refs[:half], refs[half:]):
        dst[...] = src[...].astype(BF16)


def _ffn_kernel(x_ref, pre_g_ref, wgu_ref, wd_ref, post_g_ref, *rest, tf):
    o_ref, cast_refs = rest[len(rest) // 2], rest[:len(rest) // 2] + rest[len(rest) // 2 + 1:]
    _cast_slabs(cast_refs)
    d_ff = wd_ref.shape[0]
    tm = x_ref.shape[0]
    blocks = [slice(r0, r0 + tm // EDGE_BLOCKS) for r0 in range(0, tm, tm // EDGE_BLOCKS)]

    def gate_up(h, c0):
        return _dot(h, wgu_ref[:, c0:c0 + tf]), _dot(h, wgu_ref[:, d_ff + c0:d_ff + c0 + tf])

    hs, first = [], []
    for rows in blocks:
        hs.append(_rms(x_ref[rows, :], pre_g_ref[...]).astype(BF16))
        first.append(gate_up(hs[-1], 0))
    h = jnp.concatenate(hs, axis=0)
    nxt = tuple(jnp.concatenate(t, axis=0) for t in zip(*first))

    acc = None
    for c0 in range(0, d_ff - tf, tf):
        gate, up = nxt
        nxt = gate_up(h, c0 + tf)
        a = (gate * _sigmoid(gate) * up).astype(BF16)
        part = _dot(a, wd_ref[c0:c0 + tf, :])
        acc = part if acc is None else acc + part

    gate, up = nxt
    a = (gate * _sigmoid(gate) * up).astype(BF16)
    for rows in blocks:
        total = _dot(a[rows, :], wd_ref[d_ff - tf:d_ff, :])
        if acc is not None:
            total = acc[rows, :] + total
        o_ref[rows, :] = x_ref[rows, :] + 0.5 * _rms(total, post_g_ref[...])


def _ffn(layer, x, pre_g, w_gu, w_down, post_g, *, tm, tf, cast=None):
    m, d = x.shape
    d_ff = w_down.shape[0]
    n_steps = m // tm
    extra = [_cast_specs(cast[0], w, n_steps, lambda i: i) for w in cast[1:]] if cast else []
    outs = pl.pallas_call(
        functools.partial(_ffn_kernel, tf=tf),
        grid=(n_steps,),
        in_specs=[
            pl.BlockSpec((tm, d), lambda i: (i, 0)),
            _layer_spec(layer, (1, d)),
            pl.BlockSpec((d, 2 * d_ff), lambda i: (0, 0), pipeline_mode=RESIDENT),
            pl.BlockSpec((d_ff, d), lambda i: (0, 0), pipeline_mode=RESIDENT),
            _layer_spec(layer, (1, d)),
        ] + [e[0] for e in extra],
        out_specs=[pl.BlockSpec((tm, d), lambda i: (i, 0))] + [e[1] for e in extra],
        out_shape=[jax.ShapeDtypeStruct((m, d), F32)] + [e[2] for e in extra],
        compiler_params=_params("parallel"),
        name="ffn",
    )(x, pre_g, w_gu, w_down, post_g, *(cast[1:] if cast else ()))
    return outs[0], tuple(outs[1:])


def _repack_kernel(w_ref, tail_ref, main_ref, gate_ref, small_ref, *, aligned, mid, gate_base, gates, shift_mid,
                   shift_gate):
    rows = w_ref.shape[0]
    block = lambda n: w_ref[:, n * LANES:(n + 1) * LANES]
    last = w_ref.shape[1] // LANES - 1

    def window(n):
        return tail_ref[...] if n == last else w_ref[:, n * LANES:(n + 2) * LANES]

    def shifted(first, count, shift):
        windows = jnp.concatenate([window(first + i) for i in range(count)], axis=0)
        src = lax.broadcasted_iota(jnp.int32, (2 * LANES, LANES), 0)
        dst = lax.broadcasted_iota(jnp.int32, (2 * LANES, LANES), 1)
        out = _dot(windows, (src == dst + shift).astype(BF16)).astype(BF16)
        return [out[i * rows:(i + 1) * rows, :] for i in range(count)]

    main_ref[:, 0:aligned * LANES] = w_ref[:, 0:aligned * LANES]
    for i, piece in enumerate(shifted(aligned, mid, shift_mid)):
        main_ref[:, (aligned + i) * LANES:(aligned + i + 1) * LANES] = piece
    for i, piece in enumerate(shifted(gate_base, gates, shift_gate)):
        gate_ref[:, i * LANES:(i + 1) * LANES] = piece
    lane = lax.broadcasted_iota(jnp.int32, (rows, LANES), 1)
    small_ref[...] = jnp.where(lane < shift_mid, block(aligned),
                               jnp.where(lane < shift_gate, block(gate_base), jnp.zeros((rows, LANES), BF16)))


def _repack(w, *, rows, aligned, mid, gate_base, gates, shift_mid, shift_gate):
    depth, d, cols = w.shape
    spec = lambda n: pl.BlockSpec((None, rows, n), lambda l, i: (l, i, 0))
    last = cols // LANES - 1
    assert gate_base + gates - 1 == last and cols > (last + 1) * LANES
    tail = jnp.pad(w[:, :, last * LANES:], ((0, 0), (0, 0), (0, (last + 2) * LANES - cols)))
    return pl.pallas_call(
        functools.partial(_repack_kernel, aligned=aligned, mid=mid, gate_base=gate_base, gates=gates,
                          shift_mid=shift_mid, shift_gate=shift_gate),
        grid=(depth, d // rows),
        in_specs=[spec(cols), spec(2 * LANES)],
        out_specs=[spec((aligned + mid) * LANES), spec(gates * LANES), spec(LANES)],
        out_shape=[jax.ShapeDtypeStruct((depth, d, (aligned + mid) * LANES), BF16),
                   jax.ShapeDtypeStruct((depth, d, gates * LANES), BF16),
                   jax.ShapeDtypeStruct((depth, d, LANES), BF16)],
        compiler_params=_params("parallel", "parallel"),
        name="repack",
    )(w, tail)


IN_QK = 2 * GLA_QK
IN_VR = 2 * GLA_V
IN_SU = S5_WIDTH
IN_FOX = 3 * FOX_W
IN_SMALL = LANES
OFF_VR = IN_QK
OFF_SU = OFF_VR + IN_VR
OFF_FOX = OFF_SU + IN_SU
OFF_SMALL = OFF_FOX + IN_FOX


def _inproj_kernel(x_ref, g_ref, w_ref, ws_ref, qk_ref, vr_ref, su_ref, fox_ref, small_ref):
    tm = x_ref.shape[0]
    hs = []
    for r0 in range(0, tm, tm // EDGE_BLOCKS):
        rows = slice(r0, r0 + tm // EDGE_BLOCKS)
        hs.append(_rms(x_ref[rows, :], g_ref[...]).astype(BF16))
        qk_ref[rows, :] = _dot(hs[-1], w_ref[:, 0:OFF_VR])
    h = jnp.concatenate(hs, axis=0)
    vr_ref[...] = _dot(h, w_ref[:, OFF_VR:OFF_SU]).astype(BF16)
    su_ref[...] = _dot(h, w_ref[:, OFF_SU:OFF_FOX])
    fox_ref[:, 0:FOX_W] = (_dot(h, w_ref[:, OFF_FOX:OFF_FOX + FOX_W]) * (FOX_DH ** -0.5 * LOG2E)).astype(BF16)
    fox_ref[:, FOX_W:IN_FOX] = _dot(h, w_ref[:, OFF_FOX + FOX_W:OFF_SMALL]).astype(BF16)
    small_ref[...] = _dot(h, ws_ref[...])


def _inproj(layer, x, g, w, w_small, *, bsz, seq, tm):
    m, d = x.shape
    nl = seq // tm
    row = lambda b, l: (b * nl + l, 0)
    return pl.pallas_call(
        _inproj_kernel,
        grid=(bsz, nl),
        in_specs=[
            pl.BlockSpec((tm, d), row),
            _layer_spec(layer, (1, d)),
            _layer_spec(layer, (d, OFF_SMALL), pipeline_mode=RESIDENT),
            _layer_spec(layer, (d, IN_SMALL), pipeline_mode=RESIDENT),
        ],
        out_specs=[
            pl.BlockSpec((tm, IN_QK), row),
            pl.BlockSpec((tm, IN_VR), row),
            pl.BlockSpec((tm, IN_SU), lambda b, l: (l, b)),
            pl.BlockSpec((tm, IN_FOX), row),
            pl.BlockSpec((tm, IN_SMALL), row),
        ],
        out_shape=[
            jax.ShapeDtypeStruct((m, IN_QK), F32),
            jax.ShapeDtypeStruct((m, IN_VR), BF16),
            jax.ShapeDtypeStruct((seq, bsz * IN_SU), F32),
            jax.ShapeDtypeStruct((m, IN_FOX), BF16),
            jax.ShapeDtypeStruct((m, IN_SMALL), F32),
        ],
        compiler_params=_params("parallel", "parallel"),
        name="inproj",
    )(x, g, w, w_small)


def _gla_kernel(qk_ref, vr_ref, small_ref, wg_ref, bg_ref, ng_ref, fb_ref, o_ref, f_ref, la_sc):
    seq = qk_ref.shape[0]
    f_ref[...] = _forget_terms(small_ref[...], fb_ref[...])
    n_chunks = seq // CHUNK
    c, hq, hv = CHUNK, GLA_QK, GLA_V

    z = _dot(small_ref[...].astype(BF16), wg_ref[...]) + bg_ref[...]
    la_sc[...] = _log_sigmoid(z) * (1.0 / GLA_TAU)

    row = lax.broadcasted_iota(jnp.int32, (c, c), 0)
    col = lax.broadcasted_iota(jnp.int32, (c, c), 1)
    tri = (col <= row).astype(BF16)
    row_x = lax.broadcasted_iota(jnp.int32, (c, GLA_HEADS * c), 0)
    col_x = lax.broadcasted_iota(jnp.int32, (c, GLA_HEADS * c), 1)
    lower = (col_x & (c - 1)) <= row_x
    lane_q = lax.broadcasted_iota(jnp.int32, (c, hq), 1)
    lane_v = lax.broadcasted_iota(jnp.int32, (c, hv), 1)
    q_head = [(lane_q >> DK_SHIFT) == h for h in range(GLA_HEADS)]
    v_head = [(lane_v >> DV_SHIFT) == h for h in range(GLA_HEADS)]
    st_row = lax.broadcasted_iota(jnp.int32, (hv, hq), 0)
    st_col = lax.broadcasted_iota(jnp.int32, (hv, hq), 1)
    same_head = (st_row >> DV_SHIFT) == (st_col >> DK_SHIFT)

    def expand(t, heads):
        return jnp.concatenate([jnp.where(heads[h], t, 0).astype(BF16) for h in range(GLA_HEADS)], axis=0)

    gr = lax.broadcasted_iota(jnp.int32, (hv, hv), 0) >> DV_SHIFT
    gc = lax.broadcasted_iota(jnp.int32, (hv, hv), 1) >> DV_SHIFT
    avg = jnp.where(gr == gc, 1.0 / GLA_DV, 0.0).astype(BF16)

    def within_chunks(group):
        stage1 = []
        for i in group:
            hi, mid, lo = _split3(la_sc[i * c:(i + 1) * c, :])
            stage1.append(_dot(tri, hi) + _dot(tri, mid) + _dot(tri, lo))
        stage2 = []
        for i, g in zip(group, stage1):
            rows = slice(i * c, (i + 1) * c)
            g_last = g[c - 1:c, :]
            eg, ieg = jnp.exp(g), jnp.exp(-g)
            qc = qk_ref[rows, 0:hq] * (GLA_DK ** -0.5)
            kc = qk_ref[rows, hq:2 * hq]
            v = vr_ref[rows, 0:hv]
            q_fwd = qc * eg
            a_fwd = _dot_nt(q_fwd.astype(BF16), expand(kc * ieg, q_head))
            a_bwd = _dot_nt((qc * ieg).astype(BF16), expand(kc * eg, q_head))
            kw = (kc * jnp.exp(g_last - g)).astype(BF16)
            ds_t = _dot(v.astype(F32).T.astype(BF16), kw)
            stage2.append((a_fwd, a_bwd, v, q_fwd.astype(BF16), jnp.where(same_head, ds_t, 0.0), jnp.exp(g_last)))
        results = []
        for a_fwd, a_bwd, v, q_fwd, ds_t, decay in stage2:
            attn = jnp.where(lower, a_fwd, a_bwd).astype(BF16)
            o = _dot(attn, expand(v, v_head))
            results.append((o, q_fwd, ds_t, decay))
        return results

    def finish(group, results, s_t):
        outs = []
        for o, q_fwd, ds_t, decay in results:
            outs.append(o + _dot_nt(q_fwd, s_t.astype(BF16)))
            s_t = s_t * decay + ds_t
        o = jnp.concatenate(outs, axis=0)
        rows = slice(group[0] * c, (group[-1] + 1) * c)
        hi, mid, lo = _split3(o * o)
        ms = _dot(hi, avg) + _dot(mid, avg) + _dot(lo, avg)
        r = vr_ref[rows, hv:2 * hv].astype(F32)
        o_ref[rows, :] = (o * lax.rsqrt(ms + EPS) * ng_ref[...] * (r * _sigmoid(r))).astype(BF16)
        return s_t

    groups = [range(first, min(first + GLA_GROUP, n_chunks)) for first in range(0, n_chunks, GLA_GROUP)]
    s_t = jnp.zeros((hv, hq), F32)
    pending = None
    for group in groups:
        results = within_chunks(group)
        if pending is not None:
            s_t = finish(*pending, s_t)
        pending = (group, results)
    finish(*pending, s_t)


def _gla(layer, qk, vr, small, wg, bg, ng, fox_b, *, bsz, seq):
    m = qk.shape[0]
    row = lambda b: (b, 0)
    return pl.pallas_call(
        _gla_kernel,
        grid=(bsz,),
        in_specs=[
            pl.BlockSpec((seq, IN_QK), row),
            pl.BlockSpec((seq, IN_VR), row),
            pl.BlockSpec((seq, IN_SMALL), row),
            _layer_spec(layer, (IN_SMALL, GLA_QK)),
            _layer_spec(layer, (1, GLA_QK)),
            _layer_spec(layer, (1, GLA_V)),
            _layer_spec(layer, (FOX_HEADS, 1)),
        ],
        out_specs=[pl.BlockSpec((seq, GLA_V), row), pl.BlockSpec((seq, LANES), row)],
        out_shape=[jax.ShapeDtypeStruct((m, GLA_V), BF16), jax.ShapeDtypeStruct((m, LANES), BF16)],
        scratch_shapes=[pltpu.VMEM((seq, GLA_QK), F32)],
        compiler_params=_params("parallel"),
        name="gla",
    )(qk, vr, small, wg, bg, ng, fox_b)


def _forget_terms(small, bias):
    seq = small.shape[0]
    x = _log_sigmoid(small.T[GLA_RANK:GLA_RANK + FOX_HEADS, :] + bias)
    lane = lax.broadcasted_iota(jnp.int32, x.shape, 1)
    shift = 1
    while shift < seq:
        x = x + jnp.where(lane >= shift, pltpu.roll(x, shift, axis=1), 0.0)
        shift *= 2
    hi, mid, lo = (t.astype(F32) for t in _split3(x * LOG2E))
    pad = jnp.zeros((LANES - FOX_TERMS * FOX_HEADS, seq), F32)
    return jnp.concatenate([hi, mid, lo, jnp.ones_like(x), pad], axis=0).T.astype(BF16)


def _fox_placement(head, base, *, key):
    r = lax.broadcasted_iota(jnp.int32, (LANES, LANES), 0)
    rel = lax.broadcasted_iota(jnp.int32, (LANES, LANES), 1) - base
    term = r >> HEADS_SHIFT
    mine = ((r & (FOX_HEADS - 1)) == head) & (term < FOX_TERMS)
    is_part = mine & (term < FOX_TERMS - 1)
    is_one = mine & (term == FOX_TERMS - 1)
    n = FOX_TERMS - 1
    if key:
        m = jnp.where(is_part & (rel == term + n), -1.0, jnp.where(is_one & (rel >= 0) & (rel < n), 1.0, 0.0))
    else:
        m = jnp.where(is_part & (rel == term), 1.0, jnp.where(is_one & (rel >= n) & (rel < 2 * n), 1.0, 0.0))
    return m.astype(BF16)


def _fox_kernel(q_ref, k_ref, v_ref, f_ref, o_ref, qx_sc, kx_sc, vx_sc, *, tk):
    seq = k_ref.shape[0]
    spare = (FOX_DH, 0)
    lane = lax.broadcasted_iota(jnp.int32, (seq, LANES), 1)
    q, k, v, f = q_ref[...], k_ref[...], v_ref[...], f_ref[...]
    for h in range(2):
        own = (lane < FOX_DH) if h == 0 else (lane >= FOX_DH)
        head = 2 * pl.program_id(1) + h
        qx_sc[h] = jnp.where(own, q, _dot(f, _fox_placement(head, spare[h], key=False)).astype(BF16))
        kx_sc[h] = jnp.where(own, k, _dot(f, _fox_placement(head, spare[h], key=True)).astype(BF16))
        vx_sc[h] = jnp.where(own, v, jnp.ones_like(v))

    row = lax.broadcasted_iota(jnp.int32, (tk, tk), 0)
    col = lax.broadcasted_iota(jnp.int32, (tk, tk), 1)
    causal = col <= row

    items = [(h, r0, c0) for c0 in range(0, seq, tk) for h in range(2) for r0 in range(c0, seq, tk)]

    def logits(item):
        h, r0, c0 = item
        s = _dot_nt(qx_sc[h, r0:r0 + tk, :], kx_sc[h, c0:c0 + tk, :])
        return jnp.where(causal, s, -jnp.inf) if r0 == c0 else s

    state = {}
    s_next = logits(items[0])
    for n, (h, r0, c0) in enumerate(items):
        s = s_next
        if n + 1 < len(items):
            s_next = logits(items[n + 1])
        vx = vx_sc[h, c0:c0 + tk, :]
        m_new = jnp.max(s, axis=1, keepdims=True)
        if c0 == 0:
            acc = _dot(jnp.exp2(s - m_new).astype(BF16), vx)
        else:
            m_old, acc_old = state[h, r0]
            m_new = jnp.maximum(m_old, m_new)
            acc = jnp.exp2(m_old - m_new) * acc_old + _dot(jnp.exp2(s - m_new).astype(BF16), vx)
        state[h, r0] = (m_new, acc)

    first = lax.broadcasted_iota(jnp.int32, (tk, LANES), 1) < FOX_DH
    for r0 in range(0, seq, tk):
        a0, a1 = state[0, r0][1], state[1, r0][1]
        o_ref[r0:r0 + tk, :] = jnp.where(first, a0 / pltpu.roll(a0, FOX_DH, axis=1),
                                         a1 / pltpu.roll(a1, FOX_DH, axis=1)).astype(BF16)


def _fox(fox, f, *, bsz, seq, tk):
    m = fox.shape[0]
    return pl.pallas_call(
        functools.partial(_fox_kernel, tk=tk),
        grid=(bsz, FOX_PAIRS),
        in_specs=[
            pl.BlockSpec((seq, LANES), lambda b, p: (b, p)),
            pl.BlockSpec((seq, LANES), lambda b, p: (b, FOX_PAIRS + p)),
            pl.BlockSpec((seq, LANES), lambda b, p: (b, 2 * FOX_PAIRS + p)),
            pl.BlockSpec((seq, LANES), lambda b, p: (b, 0)),
        ],
        out_specs=pl.BlockSpec((seq, LANES), lambda b, p: (b, p)),
        out_shape=jax.ShapeDtypeStruct((m, FOX_W), BF16),
        scratch_shapes=[pltpu.VMEM((2, seq, LANES), BF16)] * 3,
        compiler_params=_params("parallel", "parallel"),
        name="fox",
    )(fox, fox, fox, f)


def _s5prep_kernel(are_ref, aim_ref, ldt_ref, bre_ref, bim_ref, abre_ref, abim_ref, bbre_ref, bbim_ref):
    lam_re = jnp.minimum(are_ref[...], -1e-4)
    lam_im = aim_ref[...]
    dt = jnp.exp(ldt_ref[...])
    mag = jnp.exp(lam_re * dt)
    ab_re = mag * jnp.cos(lam_im * dt)
    ab_im = mag * jnp.sin(lam_im * dt)
    den = lam_re * lam_re + lam_im * lam_im
    z_re = ((ab_re - 1.0) * lam_re + ab_im * lam_im) / den
    z_im = (ab_im * lam_re - (ab_re - 1.0) * lam_im) / den
    br, bi = bre_ref[...], bim_ref[...]
    abre_ref[...] = ab_re
    abim_ref[...] = ab_im
    bbre_ref[...] = z_re * br - z_im * bi
    bbim_ref[...] = z_re * bi + z_im * br


def _s5prep(a_re, a_im, log_dt, b_re, b_im):
    depth = a_re.shape[0]
    col = lambda a: a.reshape(depth, S5_NSTATE, 1)
    ldt = jnp.broadcast_to(log_dt[:, :, None], (depth, S5_GROUPS, S5_STATE))
    vec = pl.BlockSpec((None, S5_NSTATE, 1), lambda l: (l, 0, 0))
    mat = pl.BlockSpec((None, S5_NSTATE, S5_CH), lambda l: (l, 0, 0))
    return pl.pallas_call(
        _s5prep_kernel,
        grid=(depth,),
        in_specs=[vec, vec, vec, mat, mat],
        out_specs=[vec, vec, mat, mat],
        out_shape=[jax.ShapeDtypeStruct((depth, S5_NSTATE, 1), F32)] * 2
        + [jax.ShapeDtypeStruct((depth, S5_NSTATE, S5_CH), F32)] * 2,
        compiler_params=_params("parallel"),
        name="s5prep",
    )(col(a_re), col(a_im), col(ldt), b_re.reshape(depth, S5_NSTATE, S5_CH), b_im.reshape(depth, S5_NSTATE, S5_CH))


def _s5_kernel(u_ref, abre_ref, abim_ref, bre_ref, bim_ref, cre_ref, cim_ref, d_ref, gw_ref, gb_ref, o_ref,
               sre_sc, sim_sc, u_sc, y_sc, *, bsz, ts):
    @pl.when(pl.program_id(0) == 0)
    def _():
        sre_sc[...] = jnp.zeros_like(sre_sc)
        sim_sc[...] = jnp.zeros_like(sim_sc)

    tb = u_ref.shape[0]
    slabs = S5_WIDTH // LANES
    for b in range(bsz):
        for j in range(slabs):
            lanes = slice(b * S5_WIDTH + j * LANES, b * S5_WIDTH + (j + 1) * LANES)
            u_sc[j, pl.ds(b, tb, stride=bsz), :] = u_ref[:, lanes]

    rows = ts * bsz
    n_sub = tb // ts
    a_re = jnp.broadcast_to(abre_ref[...], (bsz, S5_NSTATE))
    a_im = jnp.broadcast_to(abim_ref[...], (bsz, S5_NSTATE))
    us = [jnp.concatenate([u_sc[j, k * rows:(k + 1) * rows, :] for j in range(slabs)], axis=1) for k in range(n_sub)]

    bu = []
    for u in us:
        ub = u.astype(BF16)
        bu.append((_dot(ub, bre_ref[...]), _dot(ub, bim_ref[...])))

    s_re, s_im = sre_sc[...], sim_sc[...]
    for k in range(n_sub):
        bu_re, bu_im = bu[k]
        x_re, x_im = [], []
        for t in range(ts):
            r = slice(t * bsz, (t + 1) * bsz)
            s_re, s_im = (a_re * s_re - a_im * s_im + bu_re[r, :], a_re * s_im + a_im * s_re + bu_im[r, :])
            x_re.append(s_re)
            x_im.append(s_im)
        x_re = jnp.concatenate(x_re, axis=0).astype(BF16)
        x_im = jnp.concatenate(x_im, axis=0).astype(BF16)
        y = _dot(x_re, cre_ref[...]) - _dot(x_im, cim_ref[...])
        y = jax.nn.gelu(y + d_ref[...] * us[k], approximate=True)
        y = y * _sigmoid(_dot(y.astype(BF16), gw_ref[...]) + gb_ref[...])
        for j in range(slabs):
            y_sc[j, k * rows:(k + 1) * rows, :] = y[:, j * LANES:(j + 1) * LANES]
    sre_sc[...] = s_re
    sim_sc[...] = s_im

    for b in range(bsz):
        for j in range(slabs):
            lanes = slice(b * S5_WIDTH + j * LANES, b * S5_WIDTH + (j + 1) * LANES)
            o_ref[:, lanes] = y_sc[j, pl.ds(b, tb, stride=bsz), :].astype(BF16)


def _s5(layer, u, ab_re, ab_im, b_re, b_im, c_re, c_im, d, glu_w, glu_b, *, bsz, seq, tb, ts):
    rows = tb * bsz
    slabs = S5_WIDTH // LANES
    return pl.pallas_call(
        functools.partial(_s5_kernel, bsz=bsz, ts=ts),
        grid=(seq // tb,),
        in_specs=[
            pl.BlockSpec((tb, bsz * S5_WIDTH), lambda i: (i, 0)),
            _layer_spec(layer, (1, S5_NSTATE)),
            _layer_spec(layer, (1, S5_NSTATE)),
            _layer_spec(layer, (S5_WIDTH, S5_NSTATE)),
            _layer_spec(layer, (S5_WIDTH, S5_NSTATE)),
            _layer_spec(layer, (S5_NSTATE, S5_WIDTH)),
            _layer_spec(layer, (S5_NSTATE, S5_WIDTH)),
            _layer_spec(layer, (1, S5_WIDTH)),
            _layer_spec(layer, (S5_WIDTH, S5_WIDTH)),
            _layer_spec(layer, (1, S5_WIDTH)),
        ],
        out_specs=pl.BlockSpec((tb, bsz * S5_WIDTH), lambda i: (i, 0)),
        out_shape=jax.ShapeDtypeStruct((seq, bsz * S5_WIDTH), BF16),
        scratch_shapes=[pltpu.VMEM((bsz, S5_NSTATE), F32), pltpu.VMEM((bsz, S5_NSTATE), F32),
                        pltpu.VMEM((slabs, rows, LANES), F32), pltpu.VMEM((slabs, rows, LANES), F32)],
        compiler_params=_params("arbitrary"),
        name="s5",
    )(u, ab_re, ab_im, b_re, b_im, c_re, c_im, d, glu_w, glu_b)


def _merge_kernel(x_ref, gla_ref, s5_ref, fox_ref, pre_g_ref, wgate_ref, wgla_ref, ws5_ref, wfox_ref, wo_ref,
                  post_g_ref, o_ref):
    tm, d = x_ref.shape
    blocks = [slice(r0, r0 + tm // EDGE_BLOCKS) for r0 in range(0, tm, tm // EDGE_BLOCKS)]
    branches = ((gla_ref, wgla_ref), (s5_ref, ws5_ref), (fox_ref, wfox_ref))

    hs, first = [], []
    for rows in blocks:
        hs.append(_rms(x_ref[rows, :], pre_g_ref[...]).astype(BF16))
        first.append(_dot(hs[-1], wgate_ref[:, 0:d]))
    h = jnp.concatenate(hs, axis=0)
    logits = jnp.concatenate(first, axis=0)
    mix = None
    for n, (b_ref, w_ref) in enumerate(branches[:-1]):
        term = _sigmoid(logits) * _dot(b_ref[...], w_ref[...])
        mix = term if mix is None else mix + term
        logits = _dot(h, wgate_ref[:, (n + 1) * d:(n + 2) * d])
    up = _dot(branches[-1][0][...], branches[-1][1][...])

    for rows in blocks:
        mixed = mix[rows, :] + _sigmoid(logits[rows, :]) * up[rows, :]
        y = _dot(mixed.astype(BF16), wo_ref[...])
        o_ref[rows, :] = x_ref[rows, :] + _rms(y, post_g_ref[...])


def _merge(layer, x, gla_o, s5_o, fox_o, pre_g, w_gate, w_gla, w_s5, w_fox, w_o, post_g, *, bsz, seq, tm):
    m, d = x.shape
    nl = seq // tm
    row = lambda b, l: (b * nl + l, 0)
    whole = lambda shape: pl.BlockSpec(shape, lambda b, l: (0, 0), pipeline_mode=RESIDENT)
    return pl.pallas_call(
        _merge_kernel,
        grid=(bsz, nl),
        in_specs=[
            pl.BlockSpec((tm, d), row),
            pl.BlockSpec((tm, GLA_V), row),
            pl.BlockSpec((tm, S5_WIDTH), lambda b, l: (l, b)),
            pl.BlockSpec((tm, FOX_W), row),
            _layer_spec(layer, (1, d)),
            _layer_spec(layer, (d, N_BRANCH * d), pipeline_mode=RESIDENT),
            whole((GLA_V, d)),
            whole((S5_WIDTH, d)),
            whole((FOX_W, d)),
            whole((d, d)),
            _layer_spec(layer, (1, d)),
        ],
        out_specs=pl.BlockSpec((tm, d), row),
        out_shape=jax.ShapeDtypeStruct((m, d), F32),
        compiler_params=_params("parallel", "parallel"),
        name="merge",
    )(x, gla_o, s5_o, fox_o, pre_g, w_gate, w_gla, w_s5, w_fox, w_o, post_g)


def _memkv_kernel(mem_ref, g_ref, w_ref, *rest):
    n_cast = (len(rest) - 2) // 2
    k_ref, v_ref = rest[n_cast], rest[n_cast + 1]
    _cast_slabs(rest[:n_cast] + rest[n_cast + 2:])
    d = mem_ref.shape[1]
    h = _rms(mem_ref[...], g_ref[...]).astype(BF16)
    k_ref[...] = _dot(h, w_ref[:, 0:d]).astype(BF16)
    v_ref[...] = _dot(h, w_ref[:, d:2 * d]).astype(BF16)


def _memkv(mem, g, w_kv, *, tm, cast):
    m, d = mem.shape
    depth = w_kv.shape[0]
    tiles = m // tm
    per_layer = lambda block: pl.BlockSpec((None,) + block, lambda l, i: (l, 0, 0))
    out = pl.BlockSpec((None, tm, d), lambda l, i: (l, i, 0))
    extra = [_cast_specs(cast[0], w, depth * tiles, lambda l, i: l * tiles + i) for w in cast[1:]]
    outs = pl.pallas_call(
        _memkv_kernel,
        grid=(depth, tiles),
        in_specs=[pl.BlockSpec((tm, d), lambda l, i: (i, 0)), per_layer((1, d)), per_layer((d, 2 * d))]
        + [e[0] for e in extra],
        out_specs=[out, out] + [e[1] for e in extra],
        out_shape=[jax.ShapeDtypeStruct((depth, m, d), BF16)] * 2 + [e[2] for e in extra],
        compiler_params=_params("parallel", "parallel"),
        name="memkv",
    )(mem, g, w_kv, *cast[1:])
    return outs[0], outs[1], tuple(outs[2:])


def _xattn_kernel(x_ref, k_ref, v_ref, pre_g_ref, wq_ref, wo_ref, post_g_ref, *rest):
    o_ref, cast_refs = rest[len(rest) // 2], rest[:len(rest) // 2] + rest[len(rest) // 2 + 1:]
    _cast_slabs(cast_refs)
    tm, d = x_ref.shape
    dh = d // XA_HEADS
    blocks = [slice(r0, r0 + tm // EDGE_BLOCKS) for r0 in range(0, tm, tm // EDGE_BLOCKS)]
    cols = [slice(n * dh, (n + 1) * dh) for n in range(XA_HEADS)]

    def softmax(s):
        e = jnp.exp(s - jnp.max(s, axis=1, keepdims=True))
        return (e / jnp.sum(e, axis=1, keepdims=True)).astype(BF16)

    qs = []
    for rows in blocks:
        h = _rms(x_ref[rows, :], pre_g_ref[...]).astype(BF16)
        qs.append((_dot(h, wq_ref[...]) * (dh ** -0.5)).astype(BF16))
    logits = [[_dot_nt(q[:, c], k_ref[:, c]) for c in cols] for q in qs]
    probs = [[softmax(s) for s in per_head] for per_head in logits]
    heads = [jnp.concatenate([_dot(p, v_ref[:, c]).astype(BF16) for p, c in zip(per_head, cols)], axis=1)
             for per_head in probs]
    for rows, attended in zip(blocks, heads):
        y = _dot(attended, wo_ref[...])
        o_ref[rows, :] = x_ref[rows, :] + _rms(y, post_g_ref[...])


def _xattn(layer, x, k, v, pre_g, w_q, w_o, post_g, *, bsz, seq, n_mem, tm, cast):
    m, d = x.shape
    nl = seq // tm
    row = lambda b, l: (b * nl + l, 0)
    extra = [_cast_specs(cast[0], w, bsz * nl, lambda b, l: b * nl + l) for w in cast[1:]]
    outs = pl.pallas_call(
        _xattn_kernel,
        grid=(bsz, nl),
        in_specs=[
            pl.BlockSpec((tm, d), row),
            _layer_spec(layer, (n_mem, d), lambda b, l: (b, 0)),
            _layer_spec(layer, (n_mem, d), lambda b, l: (b, 0)),
            _layer_spec(layer, (1, d)),
            pl.BlockSpec((d, d), lambda b, l: (0, 0), pipeline_mode=RESIDENT),
            pl.BlockSpec((d, d), lambda b, l: (0, 0), pipeline_mode=RESIDENT),
            _layer_spec(layer, (1, d)),
        ] + [e[0] for e in extra],
        out_specs=[pl.BlockSpec((tm, d), row)] + [e[1] for e in extra],
        out_shape=[jax.ShapeDtypeStruct((m, d), F32)] + [e[2] for e in extra],
        compiler_params=_params("parallel", "parallel"),
        name="xattn",
    )(x, k, v, pre_g, w_q, w_o, post_g, *cast[1:])
    return outs[0], tuple(outs[1:])


def _tile(n, want):
    t = min(n, want)
    while n % t:
        t -= 1
    return t


def _block_diag(t):
    depth, g, r, c = t.shape
    eye = jnp.eye(g, dtype=t.dtype)
    return jnp.einsum("lgrc,gk->lgrkc", t, eye).reshape(depth, g * r, g * c)


def kernel(x, mem, ffn1_pre_g, ffn1_w_gu, ffn1_w_down, ffn1_post_g, mix_pre_g, w_in, gla_gate_w, gla_gate_b, gla_norm_g, w_gla_up, s5_a_re, s5_a_im, s5_log_dt, s5_b_re, s5_b_im, s5_c_re, s5_c_im, s5_d, s5_glu_w, s5_glu_b, w_s5_up, fox_f_b, w_fox_up, w_mix_out, mix_post_g, xa_pre_g, xa_mem_g, xa_w_q, xa_w_kv, xa_w_o, xa_post_g, ffn2_pre_g, ffn2_w_gu, ffn2_w_down, ffn2_post_g):
    bsz, seq, d = x.shape
    n_mem = mem.shape[1]
    depth = w_in.shape[0]
    d_ff = ffn1_w_down.shape[1]
    m = bsz * seq
    assert seq % CHUNK == 0 and d % LANES == 0

    tm_ffn = _tile(m, 1024)
    tf = _tile(d_ff // LANES, 2) * LANES
    tm = _tile(seq, 1024)
    tk = _tile(seq, 512)
    ts = _tile(seq, 128)
    tb = 2 * ts if seq % (2 * ts) == 0 else ts

    bf = lambda a: a.astype(BF16)
    vec = lambda a: a.reshape(depth, 1, a.shape[-1])

    aligned = 2 * GLA_QK + 2 * GLA_V
    mid = S5_WIDTH + 3 * FOX_W
    gates_at = aligned + GLA_RANK + mid + FOX_HEADS
    assert aligned % LANES == 0 and mid % LANES == 0 and aligned + mid == OFF_SMALL and IN_SMALL == LANES
    assert gates_at % LANES == GLA_RANK + FOX_HEADS
    w_main, w_gate, w_small = _repack(
        bf(w_in), rows=_tile(d, 256), aligned=aligned // LANES, mid=mid // LANES, gate_base=gates_at // LANES,
        gates=N_BRANCH * d // LANES, shift_mid=GLA_RANK, shift_gate=GLA_RANK + FOX_HEADS)
    gla_wg = bf(jnp.concatenate([gla_gate_w, jnp.zeros((depth, IN_SMALL - GLA_RANK, GLA_QK), gla_gate_w.dtype)],
                                axis=1))

    ab_re, ab_im, bb_re, bb_im = _s5prep(s5_a_re, s5_a_im, s5_log_dt, s5_b_re, s5_b_im)
    grp = lambda t: t.reshape(depth, S5_GROUPS, S5_STATE, S5_CH)
    s5_bre = bf(_block_diag(grp(bb_re).transpose(0, 1, 3, 2)))
    s5_bim = bf(_block_diag(grp(bb_im).transpose(0, 1, 3, 2)))
    s5_cre = bf(_block_diag(s5_c_re.transpose(0, 1, 3, 2)))
    s5_cim = bf(_block_diag(s5_c_im.transpose(0, 1, 3, 2)))
    ab_re = ab_re.reshape(depth, 1, S5_NSTATE)
    ab_im = ab_im.reshape(depth, 1, S5_NSTATE)

    glu_w, xa_kv = bf(s5_glu_w), bf(xa_w_kv)
    fox_b = fox_f_b.reshape(depth, FOX_HEADS, 1)

    xs = x.reshape(m, d)
    mk, mv, ffn1_w = _memkv(mem.reshape(bsz * n_mem, d), vec(xa_mem_g), xa_kv, tm=_tile(bsz * n_mem, 512),
                            cast=(0, ffn1_w_gu, ffn1_w_down))
    for l in range(depth):
        xs, (gla_up, s5_up, fox_up, mix_out, xa_q, xa_o) = _ffn(
            l, xs, vec(ffn1_pre_g), *ffn1_w, vec(ffn1_post_g), tm=tm_ffn, tf=tf,
            cast=(l, w_gla_up, w_s5_up, w_fox_up, w_mix_out, xa_w_q, xa_w_o))

        qk, vr, su, fox, small = _inproj(l, xs, vec(mix_pre_g), w_main, w_small, bsz=bsz, seq=seq, tm=tm)
        gla_o, f_cols = _gla(l, qk, vr, small, gla_wg, vec(gla_gate_b), vec(gla_norm_g), fox_b, bsz=bsz, seq=seq)
        fox_o = _fox(fox, f_cols, bsz=bsz, seq=seq, tk=tk)
        s5_o = _s5(l, su, ab_re, ab_im, s5_bre, s5_bim, s5_cre, s5_cim,
                   vec(s5_d), glu_w, vec(s5_glu_b), bsz=bsz, seq=seq, tb=tb, ts=ts)
        xs = _merge(l, xs, gla_o, s5_o, fox_o, vec(mix_pre_g), w_gate, gla_up,
                    s5_up, fox_up, mix_out, vec(mix_post_g), bsz=bsz, seq=seq, tm=tm)

        xs, ffn2_w = _xattn(l, xs, mk, mv, vec(xa_pre_g), xa_q, xa_o, vec(xa_post_g), bsz=bsz, seq=seq, n_mem=n_mem,
                            tm=tm, cast=(l, ffn2_w_gu, ffn2_w_down))

        ahead = (l + 1, ffn1_w_gu, ffn1_w_down) if l + 1 < depth else None
        xs, ffn1_w = _ffn(l, xs, vec(ffn2_pre_g), *ffn2_w, vec(ffn2_post_g), tm=tm_ffn, tf=tf, cast=ahead)
    return xs.reshape(bsz, seq, d)
```

```python
import functools
import math

import jax
import jax.numpy as jnp
from jax import lax
from jax.experimental import pallas as pl
from jax.experimental.pallas import tpu as pltpu

F32 = jnp.float32
BF16 = jnp.bfloat16
EPS = 1e-6
LOG2E = math.log2(math.e)

LANES = 128
SUBLANES = 8
VMEM_LIMIT_BYTES = 56 * 1024 * 1024

EDGE_BLOCKS = 4
CHUNK = 64
GLA_GROUP = 8
GLA_HEADS, GLA_DK, GLA_DV, GLA_RANK, GLA_TAU = 4, 32, 64, 16, 16.0
GLA_QK = GLA_HEADS * GLA_DK
GLA_V = GLA_HEADS * GLA_DV
DK_SHIFT = GLA_DK.bit_length() - 1
DV_SHIFT = GLA_DV.bit_length() - 1
S5_GROUPS, S5_CH, S5_STATE = 16, 16, 64
S5_WIDTH = S5_GROUPS * S5_CH
S5_NSTATE = S5_GROUPS * S5_STATE
FOX_HEADS, FOX_DH = 8, 64
FOX_W = FOX_HEADS * FOX_DH
FOX_PAIRS = FOX_W // LANES
FOX_TERMS = 4
HEADS_SHIFT = FOX_HEADS.bit_length() - 1
XA_HEADS = 4
N_BRANCH = 3

NT_DIMS = (((1,), (1,)), ((), ()))
RESIDENT = pl.Buffered(1)


def _params(*semantics):
    return pltpu.CompilerParams(dimension_semantics=semantics, vmem_limit_bytes=VMEM_LIMIT_BYTES)


def _layer_spec(layer, block, index_map=None, pipeline_mode=None):
    block = tuple(block)
    if index_map is None:
        index_map = lambda *_: (0,) * len(block)
    return pl.BlockSpec((None,) + block, lambda *ids: (layer,) + tuple(index_map(*ids)),
                        pipeline_mode=pipeline_mode)


def _rms(x, g):
    return x * lax.rsqrt(jnp.mean(x * x, axis=-1, keepdims=True) + EPS) * g


def _dot(a, b):
    return jnp.dot(a, b, preferred_element_type=F32)


def _dot_nt(a, b):
    return lax.dot_general(a, b, NT_DIMS, preferred_element_type=F32)


def _sigmoid(x):
    return 0.5 * jnp.tanh(0.5 * x) + 0.5


def _log_sigmoid(x):
    return jnp.minimum(x, 0.0) - jnp.log1p(jnp.exp(-jnp.abs(x)))


def _split3(x):
    hi = x.astype(BF16)
    r1 = x - hi.astype(F32)
    mid = r1.astype(BF16)
    lo = (r1 - mid.astype(F32)).astype(BF16)
    return hi, mid, lo


def _cast_specs(layer, stacked, n_steps, step):
    _, rows, cols = stacked.shape
    slab = rows // n_steps
    assert slab * n_steps == rows and slab % (2 * SUBLANES) == 0
    return (pl.BlockSpec((None, slab, cols), lambda *ids: (layer, step(*ids), 0)),
            pl.BlockSpec((slab, cols), lambda *ids: (step(*ids), 0)),
            jax.ShapeDtypeStruct((rows, cols), BF16))


def _cast_slabs(refs):
    half = len(refs) // 2
    for src, dst in zip(refs[:half], refs[half:]):
        dst[...] = src[...].astype(BF16)


def _ffn_kernel(x_ref, pre_g_ref, wgu_ref, wd_ref, post_g_ref, *rest, tf):
    o_ref, cast_refs = rest[len(rest) // 2], rest[:len(rest) // 2] + rest[len(rest) // 2 + 1:]
    _cast_slabs(cast_refs)
    d_ff = wd_ref.shape[0]
    tm = x_ref.shape[0]
    blocks = [slice(r0, r0 + tm // EDGE_BLOCKS) for r0 in range(0, tm, tm // EDGE_BLOCKS)]

    def gate_up(h, c0):
        return _dot(h, wgu_ref[:, c0:c0 + tf]), _dot(h, wgu_ref[:, d_ff + c0:d_ff + c0 + tf])

    hs, first = [], []
    for rows in blocks:
        hs.append(_rms(x_ref[rows, :], pre_g_ref[...]).astype(BF16))
        first.append(gate_up(hs[-1], 0))
    h = jnp.concatenate(hs, axis=0)
    nxt = tuple(jnp.concatenate(t, axis=0) for t in zip(*first))

    acc = None
    for c0 in range(0, d_ff - tf, tf):
        gate, up = nxt
        nxt = gate_up(h, c0 + tf)
        a = (gate * _sigmoid(gate) * up).astype(BF16)
        part = _dot(a, wd_ref[c0:c0 + tf, :])
        acc = part if acc is None else acc + part

    gate, up = nxt
    a = (gate * _sigmoid(gate) * up).astype(BF16)
    for rows in blocks:
        total = _dot(a[rows, :], wd_ref[d_ff - tf:d_ff, :])
        if acc is not None:
            total = acc[rows, :] + total
        o_ref[rows, :] = x_ref[rows, :] + 0.5 * _rms(total, post_g_ref[...])


def _ffn(layer, x, pre_g, w_gu, w_down, post_g, *, tm, tf, cast=None):
    m, d = x.shape
    d_ff = w_down.shape[0]
    n_steps = m // tm
    extra = [_cast_specs(cast[0], w, n_steps, lambda i: i) for w in cast[1:]] if cast else []
    outs = pl.pallas_call(
        functools.partial(_ffn_kernel, tf=tf),
        grid=(n_steps,),
        in_specs=[
            pl.BlockSpec((tm, d), lambda i: (i, 0)),
            _layer_spec(layer, (1, d)),
            pl.BlockSpec((d, 2 * d_ff), lambda i: (0, 0), pipeline_mode=RESIDENT),
            pl.BlockSpec((d_ff, d), lambda i: (0, 0), pipeline_mode=RESIDENT),
            _layer_spec(layer, (1, d)),
        ] + [e[0] for e in extra],
        out_specs=[pl.BlockSpec((tm, d), lambda i: (i, 0))] + [e[1] for e in extra],
        out_shape=[jax.ShapeDtypeStruct((m, d), F32)] + [e[2] for e in extra],
        compiler_params=_params("parallel"),
        name="ffn",
    )(x, pre_g, w_gu, w_down, post_g, *(cast[1:] if cast else ()))
    return outs[0], tuple(outs[1:])


def _repack_kernel(w_ref, tail_ref, kv_ref, main_ref, gate_ref, small_ref, kvb_ref, *, aligned, mid, gate_base, gates,
                   shift_mid, shift_gate):
    rows = w_ref.shape[0]
    kvb_ref[...] = kv_ref[...].astype(BF16)
    block = lambda n: w_ref[:, n * LANES:(n + 1) * LANES]
    last = w_ref.shape[1] // LANES - 1

    def window(n):
        return tail_ref[...] if n == last else w_ref[:, n * LANES:(n + 2) * LANES]

    def shifted(first, count, shift):
        windows = jnp.concatenate([window(first + i) for i in range(count)], axis=0)
        src = lax.broadcasted_iota(jnp.int32, (2 * LANES, LANES), 0)
        dst = lax.broadcasted_iota(jnp.int32, (2 * LANES, LANES), 1)
        out = _dot(windows, (src == dst + shift).astype(BF16)).astype(BF16)
        return [out[i * rows:(i + 1) * rows, :] for i in range(count)]

    main_ref[:, 0:aligned * LANES] = w_ref[:, 0:aligned * LANES]
    for i, piece in enumerate(shifted(aligned, mid, shift_mid)):
        main_ref[:, (aligned + i) * LANES:(aligned + i + 1) * LANES] = piece
    for i, piece in enumerate(shifted(gate_base, gates, shift_gate)):
        gate_ref[:, i * LANES:(i + 1) * LANES] = piece
    lane = lax.broadcasted_iota(jnp.int32, (rows, LANES), 1)
    small_ref[...] = jnp.where(lane < shift_mid, block(aligned),
                               jnp.where(lane < shift_gate, block(gate_base), jnp.zeros((rows, LANES), BF16)))


def _repack(w, w_kv, *, rows, aligned, mid, gate_base, gates, shift_mid, shift_gate):
    depth, d, cols = w.shape
    spec = lambda n: pl.BlockSpec((None, rows, n), lambda l, i: (l, i, 0))
    last = cols // LANES - 1
    assert gate_base + gates - 1 == last and cols > (last + 1) * LANES
    tail = jnp.pad(w[:, :, last * LANES:], ((0, 0), (0, 0), (0, (last + 2) * LANES - cols)))
    return pl.pallas_call(
        functools.partial(_repack_kernel, aligned=aligned, mid=mid, gate_base=gate_base, gates=gates,
                          shift_mid=shift_mid, shift_gate=shift_gate),
        grid=(depth, d // rows),
        in_specs=[spec(cols), spec(2 * LANES), spec(w_kv.shape[2])],
        out_specs=[spec((aligned + mid) * LANES), spec(gates * LANES), spec(LANES), spec(w_kv.shape[2])],
        out_shape=[jax.ShapeDtypeStruct((depth, d, (aligned + mid) * LANES), BF16),
                   jax.ShapeDtypeStruct((depth, d, gates * LANES), BF16),
                   jax.ShapeDtypeStruct((depth, d, LANES), BF16),
                   jax.ShapeDtypeStruct(w_kv.shape, BF16)],
        compiler_params=_params("parallel", "parallel"),
        name="repack",
    )(w, tail, w_kv)


IN_QK = 2 * GLA_QK
IN_VR = 2 * GLA_V
IN_SU = S5_WIDTH
IN_FOX = 3 * FOX_W
IN_SMALL = LANES
OFF_VR = IN_QK
OFF_SU = OFF_VR + IN_VR
OFF_FOX = OFF_SU + IN_SU
OFF_SMALL = OFF_FOX + IN_FOX


def _inproj_kernel(x_ref, g_ref, w_ref, ws_ref, qk_ref, vr_ref, su_ref, fox_ref, small_ref):
    tm = x_ref.shape[0]
    hs = []
    for r0 in range(0, tm, tm // EDGE_BLOCKS):
        rows = slice(r0, r0 + tm // EDGE_BLOCKS)
        hs.append(_rms(x_ref[rows, :], g_ref[...]).astype(BF16))
        qk_ref[rows, :] = _dot(hs[-1], w_ref[:, 0:OFF_VR])
    h = jnp.concatenate(hs, axis=0)
    vr_ref[...] = _dot(h, w_ref[:, OFF_VR:OFF_SU]).astype(BF16)
    su_ref[...] = _dot(h, w_ref[:, OFF_SU:OFF_FOX])
    fox_ref[:, 0:FOX_W] = (_dot(h, w_ref[:, OFF_FOX:OFF_FOX + FOX_W]) * (FOX_DH ** -0.5 * LOG2E)).astype(BF16)
    fox_ref[:, FOX_W:IN_FOX] = _dot(h, w_ref[:, OFF_FOX + FOX_W:OFF_SMALL]).astype(BF16)
    small_ref[...] = _dot(h, ws_ref[...])


def _inproj(layer, x, g, w, w_small, *, bsz, seq, tm):
    m, d = x.shape
    nl = seq // tm
    row = lambda b, l: (b * nl + l, 0)
    return pl.pallas_call(
        _inproj_kernel,
        grid=(bsz, nl),
        in_specs=[
            pl.BlockSpec((tm, d), row),
            _layer_spec(layer, (1, d)),
            _layer_spec(layer, (d, OFF_SMALL), pipeline_mode=RESIDENT),
            _layer_spec(layer, (d, IN_SMALL), pipeline_mode=RESIDENT),
        ],
        out_specs=[
            pl.BlockSpec((tm, IN_QK), row),
            pl.BlockSpec((tm, IN_VR), row),
            pl.BlockSpec((tm, IN_SU), lambda b, l: (l, b)),
            pl.BlockSpec((tm, IN_FOX), row),
            pl.BlockSpec((tm, IN_SMALL), row),
        ],
        out_shape=[
            jax.ShapeDtypeStruct((m, IN_QK), F32),
            jax.ShapeDtypeStruct((m, IN_VR), BF16),
            jax.ShapeDtypeStruct((seq, bsz * IN_SU), F32),
            jax.ShapeDtypeStruct((m, IN_FOX), BF16),
            jax.ShapeDtypeStruct((m, IN_SMALL), F32),
        ],
        compiler_params=_params("parallel", "parallel"),
        name="inproj",
    )(x, g, w, w_small)


def _gla_kernel(qk_ref, vr_ref, small_ref, wg_ref, bg_ref, ng_ref, fb_ref, o_ref, f_ref, la_sc):
    seq = qk_ref.shape[0]
    f_ref[...] = _forget_terms(small_ref[...], fb_ref[...])
    n_chunks = seq // CHUNK
    c, hq, hv = CHUNK, GLA_QK, GLA_V

    z = _dot(small_ref[...].astype(BF16), wg_ref[...]) + bg_ref[...]
    la_sc[...] = _log_sigmoid(z) * (1.0 / GLA_TAU)

    row = lax.broadcasted_iota(jnp.int32, (c, c), 0)
    col = lax.broadcasted_iota(jnp.int32, (c, c), 1)
    tri = (col <= row).astype(BF16)
    row_x = lax.broadcasted_iota(jnp.int32, (c, GLA_HEADS * c), 0)
    col_x = lax.broadcasted_iota(jnp.int32, (c, GLA_HEADS * c), 1)
    lower = (col_x & (c - 1)) <= row_x
    lane_q = lax.broadcasted_iota(jnp.int32, (c, hq), 1)
    lane_v = lax.broadcasted_iota(jnp.int32, (c, hv), 1)
    q_head = [(lane_q >> DK_SHIFT) == h for h in range(GLA_HEADS)]
    v_head = [(lane_v >> DV_SHIFT) == h for h in range(GLA_HEADS)]
    st_row = lax.broadcasted_iota(jnp.int32, (hv, hq), 0)
    st_col = lax.broadcasted_iota(jnp.int32, (hv, hq), 1)
    same_head = (st_row >> DV_SHIFT) == (st_col >> DK_SHIFT)

    def expand(t, heads):
        return jnp.concatenate([jnp.where(heads[h], t, 0).astype(BF16) for h in range(GLA_HEADS)], axis=0)

    gr = lax.broadcasted_iota(jnp.int32, (hv, hv), 0) >> DV_SHIFT
    gc = lax.broadcasted_iota(jnp.int32, (hv, hv), 1) >> DV_SHIFT
    avg = jnp.where(gr == gc, 1.0 / GLA_DV, 0.0).astype(BF16)

    def within_chunks(group):
        stage1 = []
        for i in group:
            hi, mid, lo = _split3(la_sc[i * c:(i + 1) * c, :])
            stage1.append(_dot(tri, hi) + _dot(tri, mid) + _dot(tri, lo))
        stage2 = []
        for i, g in zip(group, stage1):
            rows = slice(i * c, (i + 1) * c)
            g_last = g[c - 1:c, :]
            eg, ieg = jnp.exp(g), jnp.exp(-g)
            qc = qk_ref[rows, 0:hq] * (GLA_DK ** -0.5)
            kc = qk_ref[rows, hq:2 * hq]
            v = vr_ref[rows, 0:hv]
            q_fwd = qc * eg
            a_fwd = _dot_nt(q_fwd.astype(BF16), expand(kc * ieg, q_head))
            a_bwd = _dot_nt((qc * ieg).astype(BF16), expand(kc * eg, q_head))
            kw = (kc * jnp.exp(g_last - g)).astype(BF16)
            ds_t = _dot(v.astype(F32).T.astype(BF16), kw)
            stage2.append((a_fwd, a_bwd, v, q_fwd.astype(BF16), jnp.where(same_head, ds_t, 0.0), jnp.exp(g_last)))
        results = []
        for a_fwd, a_bwd, v, q_fwd, ds_t, decay in stage2:
            attn = jnp.where(lower, a_fwd, a_bwd).astype(BF16)
            o = _dot(attn, expand(v, v_head))
            results.append((o, q_fwd, ds_t, decay))
        return results

    def finish(group, results, s_t):
        outs = []
        for o, q_fwd, ds_t, decay in results:
            outs.append(o + _dot_nt(q_fwd, s_t.astype(BF16)))
            s_t = s_t * decay + ds_t
        o = jnp.concatenate(outs, axis=0)
        rows = slice(group[0] * c, (group[-1] + 1) * c)
        hi, mid, lo = _split3(o * o)
        ms = _dot(hi, avg) + _dot(mid, avg) + _dot(lo, avg)
        r = vr_ref[rows, hv:2 * hv].astype(F32)
        o_ref[rows, :] = (o * lax.rsqrt(ms + EPS) * ng_ref[...] * (r * _sigmoid(r))).astype(BF16)
        return s_t

    groups = [range(first, min(first + GLA_GROUP, n_chunks)) for first in range(0, n_chunks, GLA_GROUP)]
    s_t = jnp.zeros((hv, hq), F32)
    pending = None
    for group in groups:
        results = within_chunks(group)
        if pending is not None:
            s_t = finish(*pending, s_t)
        pending = (group, results)
    finish(*pending, s_t)


def _gla(layer, qk, vr, small, wg, bg, ng, fox_b, *, bsz, seq):
    m = qk.shape[0]
    row = lambda b: (b, 0)
    return pl.pallas_call(
        _gla_kernel,
        grid=(bsz,),
        in_specs=[
            pl.BlockSpec((seq, IN_QK), row),
            pl.BlockSpec((seq, IN_VR), row),
            pl.BlockSpec((seq, IN_SMALL), row),
            _layer_spec(layer, (IN_SMALL, GLA_QK)),
            _layer_spec(layer, (1, GLA_QK)),
            _layer_spec(layer, (1, GLA_V)),
            _layer_spec(layer, (FOX_HEADS, 1)),
        ],
        out_specs=[pl.BlockSpec((seq, GLA_V), row), pl.BlockSpec((seq, LANES), row)],
        out_shape=[jax.ShapeDtypeStruct((m, GLA_V), BF16), jax.ShapeDtypeStruct((m, LANES), BF16)],
        scratch_shapes=[pltpu.VMEM((seq, GLA_QK), F32)],
        compiler_params=_params("parallel"),
        name="gla",
    )(qk, vr, small, wg, bg, ng, fox_b)


def _forget_terms(small, bias):
    seq = small.shape[0]
    x = _log_sigmoid(small.T[GLA_RANK:GLA_RANK + FOX_HEADS, :] + bias)
    lane = lax.broadcasted_iota(jnp.int32, x.shape, 1)
    shift = 1
    while shift < seq:
        x = x + jnp.where(lane >= shift, pltpu.roll(x, shift, axis=1), 0.0)
        shift *= 2
    hi, mid, lo = (t.astype(F32) for t in _split3(x * LOG2E))
    pad = jnp.zeros((LANES - FOX_TERMS * FOX_HEADS, seq), F32)
    return jnp.concatenate([hi, mid, lo, jnp.ones_like(x), pad], axis=0).T.astype(BF16)


def _fox_placement(head, base, *, key):
    r = lax.broadcasted_iota(jnp.int32, (LANES, LANES), 0)
    rel = lax.broadcasted_iota(jnp.int32, (LANES, LANES), 1) - base
    term = r >> HEADS_SHIFT
    mine = ((r & (FOX_HEADS - 1)) == head) & (term < FOX_TERMS)
    is_part = mine & (term < FOX_TERMS - 1)
    is_one = mine & (term == FOX_TERMS - 1)
    n = FOX_TERMS - 1
    if key:
        m = jnp.where(is_part & (rel == term + n), -1.0, jnp.where(is_one & (rel >= 0) & (rel < n), 1.0, 0.0))
    else:
        m = jnp.where(is_part & (rel == term), 1.0, jnp.where(is_one & (rel >= n) & (rel < 2 * n), 1.0, 0.0))
    return m.astype(BF16)


def _fox_kernel(q_ref, k_ref, v_ref, f_ref, o_ref, qx_sc, kx_sc, vx_sc, *, tk):
    seq = k_ref.shape[0]
    spare = (FOX_DH, 0)
    lane = lax.broadcasted_iota(jnp.int32, (seq, LANES), 1)
    q, k, v, f = q_ref[...], k_ref[...], v_ref[...], f_ref[...]
    for h in range(2):
        own = (lane < FOX_DH) if h == 0 else (lane >= FOX_DH)
        head = 2 * pl.program_id(1) + h
        qx_sc[h] = jnp.where(own, q, _dot(f, _fox_placement(head, spare[h], key=False)).astype(BF16))
        kx_sc[h] = jnp.where(own, k, _dot(f, _fox_placement(head, spare[h], key=True)).astype(BF16))
        vx_sc[h] = jnp.where(own, v, jnp.ones_like(v))

    row = lax.broadcasted_iota(jnp.int32, (tk, tk), 0)
    col = lax.broadcasted_iota(jnp.int32, (tk, tk), 1)
    causal = col <= row

    items = [(h, r0, c0) for c0 in range(0, seq, tk) for h in range(2) for r0 in range(c0, seq, tk)]

    def logits(item):
        h, r0, c0 = item
        s = _dot_nt(qx_sc[h, r0:r0 + tk, :], kx_sc[h, c0:c0 + tk, :])
        return jnp.where(causal, s, -jnp.inf) if r0 == c0 else s

    state = {}
    s_next = logits(items[0])
    for n, (h, r0, c0) in enumerate(items):
        s = s_next
        if n + 1 < len(items):
            s_next = logits(items[n + 1])
        vx = vx_sc[h, c0:c0 + tk, :]
        m_new = jnp.max(s, axis=1, keepdims=True)
        if c0 == 0:
            acc = _dot(jnp.exp2(s - m_new).astype(BF16), vx)
        else:
            m_old, acc_old = state[h, r0]
            m_new = jnp.maximum(m_old, m_new)
            acc = jnp.exp2(m_old - m_new) * acc_old + _dot(jnp.exp2(s - m_new).astype(BF16), vx)
        state[h, r0] = (m_new, acc)

    first = lax.broadcasted_iota(jnp.int32, (tk, LANES), 1) < FOX_DH
    for r0 in range(0, seq, tk):
        a0, a1 = state[0, r0][1], state[1, r0][1]
        o_ref[r0:r0 + tk, :] = jnp.where(first, a0 / pltpu.roll(a0, FOX_DH, axis=1),
                                         a1 / pltpu.roll(a1, FOX_DH, axis=1)).astype(BF16)


def _fox(fox, f, *, bsz, seq, tk):
    m = fox.shape[0]
    return pl.pallas_call(
        functools.partial(_fox_kernel, tk=tk),
        grid=(bsz, FOX_PAIRS),
        in_specs=[
            pl.BlockSpec((seq, LANES), lambda b, p: (b, p)),
            pl.BlockSpec((seq, LANES), lambda b, p: (b, FOX_PAIRS + p)),
            pl.BlockSpec((seq, LANES), lambda b, p: (b, 2 * FOX_PAIRS + p)),
            pl.BlockSpec((seq, LANES), lambda b, p: (b, 0)),
        ],
        out_specs=pl.BlockSpec((seq, LANES), lambda b, p: (b, p)),
        out_shape=jax.ShapeDtypeStruct((m, FOX_W), BF16),
        scratch_shapes=[pltpu.VMEM((2, seq, LANES), BF16)] * 3,
        compiler_params=_params("parallel", "parallel"),
        name="fox",
    )(fox, fox, fox, f)


def _s5prep_kernel(are_ref, aim_ref, ldt_ref, bre_ref, bim_ref, abre_ref, abim_ref, bbre_ref, bbim_ref):
    lam_re = jnp.minimum(are_ref[...], -1e-4)
    lam_im = aim_ref[...]
    dt = jnp.exp(ldt_ref[...])
    mag = jnp.exp(lam_re * dt)
    ab_re = mag * jnp.cos(lam_im * dt)
    ab_im = mag * jnp.sin(lam_im * dt)
    den = lam_re * lam_re + lam_im * lam_im
    z_re = ((ab_re - 1.0) * lam_re + ab_im * lam_im) / den
    z_im = (ab_im * lam_re - (ab_re - 1.0) * lam_im) / den
    br, bi = bre_ref[...], bim_ref[...]
    abre_ref[...] = ab_re
    abim_ref[...] = ab_im
    bbre_ref[...] = z_re * br - z_im * bi
    bbim_ref[...] = z_re * bi + z_im * br


def _s5prep(a_re, a_im, log_dt, b_re, b_im):
    depth = a_re.shape[0]
    col = lambda a: a.reshape(depth, S5_NSTATE, 1)
    ldt = jnp.broadcast_to(log_dt[:, :, None], (depth, S5_GROUPS, S5_STATE))
    vec = pl.BlockSpec((None, S5_NSTATE, 1), lambda l: (l, 0, 0))
    mat = pl.BlockSpec((None, S5_NSTATE, S5_CH), lambda l: (l, 0, 0))
    return pl.pallas_call(
        _s5prep_kernel,
        grid=(depth,),
        in_specs=[vec, vec, vec, mat, mat],
        out_specs=[vec, vec, mat, mat],
        out_shape=[jax.ShapeDtypeStruct((depth, S5_NSTATE, 1), F32)] * 2
        + [jax.ShapeDtypeStruct((depth, S5_NSTATE, S5_CH), F32)] * 2,
        compiler_params=_params("parallel"),
        name="s5prep",
    )(col(a_re), col(a_im), col(ldt), b_re.reshape(depth, S5_NSTATE, S5_CH), b_im.reshape(depth, S5_NSTATE, S5_CH))


def _s5_kernel(u_ref, abre_ref, abim_ref, bre_ref, bim_ref, cre_ref, cim_ref, d_ref, gw_ref, gb_ref, o_ref,
               sre_sc, sim_sc, u_sc, y_sc, *, bsz, ts):
    @pl.when(pl.program_id(0) == 0)
    def _():
        sre_sc[...] = jnp.zeros_like(sre_sc)
        sim_sc[...] = jnp.zeros_like(sim_sc)

    tb = u_ref.shape[0]
    slabs = S5_WIDTH // LANES
    for b in range(bsz):
        for j in range(slabs):
            lanes = slice(b * S5_WIDTH + j * LANES, b * S5_WIDTH + (j + 1) * LANES)
            u_sc[j, pl.ds(b, tb, stride=bsz), :] = u_ref[:, lanes]

    rows = ts * bsz
    n_sub = tb // ts
    a_re = jnp.broadcast_to(abre_ref[...], (bsz, S5_NSTATE))
    a_im = jnp.broadcast_to(abim_ref[...], (bsz, S5_NSTATE))
    us = [jnp.concatenate([u_sc[j, k * rows:(k + 1) * rows, :] for j in range(slabs)], axis=1) for k in range(n_sub)]

    bu = []
    for u in us:
        ub = u.astype(BF16)
        bu.append((_dot(ub, bre_ref[...]), _dot(ub, bim_ref[...])))

    s_re, s_im = sre_sc[...], sim_sc[...]
    for k in range(n_sub):
        bu_re, bu_im = bu[k]
        x_re, x_im = [], []
        for t in range(ts):
            r = slice(t * bsz, (t + 1) * bsz)
            s_re, s_im = (a_re * s_re - a_im * s_im + bu_re[r, :], a_re * s_im + a_im * s_re + bu_im[r, :])
            x_re.append(s_re)
            x_im.append(s_im)
        x_re = jnp.concatenate(x_re, axis=0).astype(BF16)
        x_im = jnp.concatenate(x_im, axis=0).astype(BF16)
        y = _dot(x_re, cre_ref[...]) - _dot(x_im, cim_ref[...])
        y = jax.nn.gelu(y + d_ref[...] * us[k], approximate=True)
        y = y * _sigmoid(_dot(y.astype(BF16), gw_ref[...]) + gb_ref[...])
        for j in range(slabs):
            y_sc[j, k * rows:(k + 1) * rows, :] = y[:, j * LANES:(j + 1) * LANES]
    sre_sc[...] = s_re
    sim_sc[...] = s_im

    for b in range(bsz):
        for j in range(slabs):
            lanes = slice(b * S5_WIDTH + j * LANES, b * S5_WIDTH + (j + 1) * LANES)
            o_ref[:, lanes] = y_sc[j, pl.ds(b, tb, stride=bsz), :].astype(BF16)


def _s5(layer, u, ab_re, ab_im, b_re, b_im, c_re, c_im, d, glu_w, glu_b, *, bsz, seq, tb, ts):
    rows = tb * bsz
    slabs = S5_WIDTH // LANES
    return pl.pallas_call(
        functools.partial(_s5_kernel, bsz=bsz, ts=ts),
        grid=(seq // tb,),
        in_specs=[
            pl.BlockSpec((tb, bsz * S5_WIDTH), lambda i: (i, 0)),
            _layer_spec(layer, (1, S5_NSTATE)),
            _layer_spec(layer, (1, S5_NSTATE)),
            _layer_spec(layer, (S5_WIDTH, S5_NSTATE)),
            _layer_spec(layer, (S5_WIDTH, S5_NSTATE)),
            _layer_spec(layer, (S5_NSTATE, S5_WIDTH)),
            _layer_spec(layer, (S5_NSTATE, S5_WIDTH)),
            _layer_spec(layer, (1, S5_WIDTH)),
            _layer_spec(layer, (S5_WIDTH, S5_WIDTH)),
            _layer_spec(layer, (1, S5_WIDTH)),
        ],
        out_specs=pl.BlockSpec((tb, bsz * S5_WIDTH), lambda i: (i, 0)),
        out_shape=jax.ShapeDtypeStruct((seq, bsz * S5_WIDTH), BF16),
        scratch_shapes=[pltpu.VMEM((bsz, S5_NSTATE), F32), pltpu.VMEM((bsz, S5_NSTATE), F32),
                        pltpu.VMEM((slabs, rows, LANES), F32), pltpu.VMEM((slabs, rows, LANES), F32)],
        compiler_params=_params("arbitrary"),
        name="s5",
    )(u, ab_re, ab_im, b_re, b_im, c_re, c_im, d, glu_w, glu_b)


def _merge_kernel(x_ref, gla_ref, s5_ref, fox_ref, pre_g_ref, wgate_ref, wgla_ref, ws5_ref, wfox_ref, wo_ref,
                  post_g_ref, o_ref):
    tm, d = x_ref.shape
    blocks = [slice(r0, r0 + tm // EDGE_BLOCKS) for r0 in range(0, tm, tm // EDGE_BLOCKS)]
    branches = ((gla_ref, wgla_ref), (s5_ref, ws5_ref), (fox_ref, wfox_ref))

    hs, first = [], []
    for rows in blocks:
        hs.append(_rms(x_ref[rows, :], pre_g_ref[...]).astype(BF16))
        first.append(_dot(hs[-1], wgate_ref[:, 0:d]))
    h = jnp.concatenate(hs, axis=0)
    logits = jnp.concatenate(first, axis=0)
    mix = None
    for n, (b_ref, w_ref) in enumerate(branches[:-1]):
        term = _sigmoid(logits) * _dot(b_ref[...], w_ref[...])
        mix = term if mix is None else mix + term
        logits = _dot(h, wgate_ref[:, (n + 1) * d:(n + 2) * d])
    up = _dot(branches[-1][0][...], branches[-1][1][...])

    for rows in blocks:
        mixed = mix[rows, :] + _sigmoid(logits[rows, :]) * up[rows, :]
        y = _dot(mixed.astype(BF16), wo_ref[...])
        o_ref[rows, :] = x_ref[rows, :] + _rms(y, post_g_ref[...])


def _merge(layer, x, gla_o, s5_o, fox_o, pre_g, w_gate, w_gla, w_s5, w_fox, w_o, post_g, *, bsz, seq, tm):
    m, d = x.shape
    nl = seq // tm
    row = lambda b, l: (b * nl + l, 0)
    whole = lambda shape: pl.BlockSpec(shape, lambda b, l: (0, 0), pipeline_mode=RESIDENT)
    return pl.pallas_call(
        _merge_kernel,
        grid=(bsz, nl),
        in_specs=[
            pl.BlockSpec((tm, d), row),
            pl.BlockSpec((tm, GLA_V), row),
            pl.BlockSpec((tm, S5_WIDTH), lambda b, l: (l, b)),
            pl.BlockSpec((tm, FOX_W), row),
            _layer_spec(layer, (1, d)),
            _layer_spec(layer, (d, N_BRANCH * d), pipeline_mode=RESIDENT),
            whole((GLA_V, d)),
            whole((S5_WIDTH, d)),
            whole((FOX_W, d)),
            whole((d, d)),
            _layer_spec(layer, (1, d)),
        ],
        out_specs=pl.BlockSpec((tm, d), row),
        out_shape=jax.ShapeDtypeStruct((m, d), F32),
        compiler_params=_params("parallel", "parallel"),
        name="merge",
    )(x, gla_o, s5_o, fox_o, pre_g, w_gate, w_gla, w_s5, w_fox, w_o, post_g)


def _memkv_kernel(mem_ref, g_ref, w_ref, *rest):
    n_cast = (len(rest) - 2) // 2
    k_ref, v_ref = rest[n_cast], rest[n_cast + 1]
    _cast_slabs(rest[:n_cast] + rest[n_cast + 2:])
    d = mem_ref.shape[1]
    h = _rms(mem_ref[...], g_ref[...]).astype(BF16)
    k_ref[...] = _dot(h, w_ref[:, 0:d]).astype(BF16)
    v_ref[...] = _dot(h, w_ref[:, d:2 * d]).astype(BF16)


def _memkv(mem, g, w_kv, *, tm, cast):
    m, d = mem.shape
    depth = w_kv.shape[0]
    tiles = m // tm
    per_layer = lambda block: pl.BlockSpec((None,) + block, lambda l, i: (l, 0, 0))
    out = pl.BlockSpec((None, tm, d), lambda l, i: (l, i, 0))
    extra = [_cast_specs(cast[0], w, depth * tiles, lambda l, i: l * tiles + i) for w in cast[1:]]
    outs = pl.pallas_call(
        _memkv_kernel,
        grid=(depth, tiles),
        in_specs=[pl.BlockSpec((tm, d), lambda l, i: (i, 0)), per_layer((1, d)), per_layer((d, 2 * d))]
        + [e[0] for e in extra],
        out_specs=[out, out] + [e[1] for e in extra],
        out_shape=[jax.ShapeDtypeStruct((depth, m, d), BF16)] * 2 + [e[2] for e in extra],
        compiler_params=_params("parallel", "parallel"),
        name="memkv",
    )(mem, g, w_kv, *cast[1:])
    return outs[0], outs[1], tuple(outs[2:])


def _xattn_kernel(x_ref, k_ref, v_ref, pre_g_ref, wq_ref, wo_ref, post_g_ref, *rest):
    o_ref, cast_refs = rest[len(rest) // 2], rest[:len(rest) // 2] + rest[len(rest) // 2 + 1:]
    _cast_slabs(cast_refs)
    tm, d = x_ref.shape
    dh = d // XA_HEADS
    blocks = [slice(r0, r0 + tm // EDGE_BLOCKS) for r0 in range(0, tm, tm // EDGE_BLOCKS)]
    cols = [slice(n * dh, (n + 1) * dh) for n in range(XA_HEADS)]

    def softmax(s):
        e = jnp.exp(s - jnp.max(s, axis=1, keepdims=True))
        return (e / jnp.sum(e, axis=1, keepdims=True)).astype(BF16)

    qs = []
    for rows in blocks:
        h = _rms(x_ref[rows, :], pre_g_ref[...]).astype(BF16)
        qs.append((_dot(h, wq_ref[...]) * (dh ** -0.5)).astype(BF16))
    logits = [[_dot_nt(q[:, c], k_ref[:, c]) for c in cols] for q in qs]
    probs = [[softmax(s) for s in per_head] for per_head in logits]
    heads = [jnp.concatenate([_dot(p, v_ref[:, c]).astype(BF16) for p, c in zip(per_head, cols)], axis=1)
             for per_head in probs]
    for rows, attended in zip(blocks, heads):
        y = _dot(attended, wo_ref[...])
        o_ref[rows, :] = x_ref[rows, :] + _rms(y, post_g_ref[...])


def _xattn(layer, x, k, v, pre_g, w_q, w_o, post_g, *, bsz, seq, n_mem, tm, cast):
    m, d = x.shape
    nl = seq // tm
    row = lambda b, l: (b * nl + l, 0)
    extra = [_cast_specs(cast[0], w, bsz * nl, lambda b, l: b * nl + l) for w in cast[1:]]
    outs = pl.pallas_call(
        _xattn_kernel,
        grid=(bsz, nl),
        in_specs=[
            pl.BlockSpec((tm, d), row),
            _layer_spec(layer, (n_mem, d), lambda b, l: (b, 0)),
            _layer_spec(layer, (n_mem, d), lambda b, l: (b, 0)),
            _layer_spec(layer, (1, d)),
            pl.BlockSpec((d, d), lambda b, l: (0, 0), pipeline_mode=RESIDENT),
            pl.BlockSpec((d, d), lambda b, l: (0, 0), pipeline_mode=RESIDENT),
            _layer_spec(layer, (1, d)),
        ] + [e[0] for e in extra],
        out_specs=[pl.BlockSpec((tm, d), row)] + [e[1] for e in extra],
        out_shape=[jax.ShapeDtypeStruct((m, d), F32)] + [e[2] for e in extra],
        compiler_params=_params("parallel", "parallel"),
        name="xattn",
    )(x, k, v, pre_g, w_q, w_o, post_g, *cast[1:])
    return outs[0], tuple(outs[1:])


def _tile(n, want):
    t = min(n, want)
    while n % t:
        t -= 1
    return t


def _block_diag(t):
    depth, g, r, c = t.shape
    eye = jnp.eye(g, dtype=t.dtype)
    return jnp.einsum("lgrc,gk->lgrkc", t, eye).reshape(depth, g * r, g * c)


def kernel(x, mem, ffn1_pre_g, ffn1_w_gu, ffn1_w_down, ffn1_post_g, mix_pre_g, w_in, gla_gate_w, gla_gate_b, gla_norm_g, w_gla_up, s5_a_re, s5_a_im, s5_log_dt, s5_b_re, s5_b_im, s5_c_re, s5_c_im, s5_d, s5_glu_w, s5_glu_b, w_s5_up, fox_f_b, w_fox_up, w_mix_out, mix_post_g, xa_pre_g, xa_mem_g, xa_w_q, xa_w_kv, xa_w_o, xa_post_g, ffn2_pre_g, ffn2_w_gu, ffn2_w_down, ffn2_post_g):
    bsz, seq, d = x.shape
    n_mem = mem.shape[1]
    depth = w_in.shape[0]
    d_ff = ffn1_w_down.shape[1]
    m = bsz * seq
    assert seq % CHUNK == 0 and d % LANES == 0

    tm_ffn = _tile(m, 1024)
    tf = _tile(d_ff // LANES, 2) * LANES
    tm = _tile(seq, 1024)
    tk = _tile(seq, 512)
    ts = _tile(seq, 128)
    tb = 2 * ts if seq % (2 * ts) == 0 else ts

    bf = lambda a: a.astype(BF16)
    vec = lambda a: a.reshape(depth, 1, a.shape[-1])

    aligned = 2 * GLA_QK + 2 * GLA_V
    mid = S5_WIDTH + 3 * FOX_W
    gates_at = aligned + GLA_RANK + mid + FOX_HEADS
    assert aligned % LANES == 0 and mid % LANES == 0 and aligned + mid == OFF_SMALL and IN_SMALL == LANES
    assert gates_at % LANES == GLA_RANK + FOX_HEADS
    w_main, w_gate, w_small, xa_kv = _repack(
        bf(w_in), xa_w_kv, rows=_tile(d, 256), aligned=aligned // LANES, mid=mid // LANES, gate_base=gates_at // LANES,
        gates=N_BRANCH * d // LANES, shift_mid=GLA_RANK, shift_gate=GLA_RANK + FOX_HEADS)
    gla_wg = bf(jnp.concatenate([gla_gate_w, jnp.zeros((depth, IN_SMALL - GLA_RANK, GLA_QK), gla_gate_w.dtype)],
                                axis=1))

    ab_re, ab_im, bb_re, bb_im = _s5prep(s5_a_re, s5_a_im, s5_log_dt, s5_b_re, s5_b_im)
    grp = lambda t: t.reshape(depth, S5_GROUPS, S5_STATE, S5_CH)
    s5_bre = bf(_block_diag(grp(bb_re).transpose(0, 1, 3, 2)))
    s5_bim = bf(_block_diag(grp(bb_im).transpose(0, 1, 3, 2)))
    s5_cre = bf(_block_diag(s5_c_re.transpose(0, 1, 3, 2)))
    s5_cim = bf(_block_diag(s5_c_im.transpose(0, 1, 3, 2)))
    ab_re = ab_re.reshape(depth, 1, S5_NSTATE)
    ab_im = ab_im.reshape(depth, 1, S5_NSTATE)

    glu_w = bf(s5_glu_w)
    fox_b = fox_f_b.reshape(depth, FOX_HEADS, 1)

    xs = x.reshape(m, d)
    mk, mv, ffn1_w = _memkv(mem.reshape(bsz * n_mem, d), vec(xa_mem_g), xa_kv, tm=_tile(bsz * n_mem, 512),
                            cast=(0, ffn1_w_gu, ffn1_w_down))
    for l in range(depth):
        xs, (gla_up, s5_up, fox_up, mix_out, xa_q, xa_o) = _ffn(
            l, xs, vec(ffn1_pre_g), *ffn1_w, vec(ffn1_post_g), tm=tm_ffn, tf=tf,
            cast=(l, w_gla_up, w_s5_up, w_fox_up, w_mix_out, xa_w_q, xa_w_o))

        qk, vr, su, fox, small = _inproj(l, xs, vec(mix_pre_g), w_main, w_small, bsz=bsz, seq=seq, tm=tm)
        gla_o, f_cols = _gla(l, qk, vr, small, gla_wg, vec(gla_gate_b), vec(gla_norm_g), fox_b, bsz=bsz, seq=seq)
        fox_o = _fox(fox, f_cols, bsz=bsz, seq=seq, tk=tk)
        s5_o = _s5(l, su, ab_re, ab_im, s5_bre, s5_bim, s5_cre, s5_cim,
                   vec(s5_d), glu_w, vec(s5_glu_b), bsz=bsz, seq=seq, tb=tb, ts=ts)
        xs = _merge(l, xs, gla_o, s5_o, fox_o, vec(mix_pre_g), w_gate, gla_up,
                    s5_up, fox_up, mix_out, vec(mix_post_g), bsz=bsz, seq=seq, tm=tm)

        xs, ffn2_w = _xattn(l, xs, mk, mv, vec(xa_pre_g), xa_q, xa_o, vec(xa_post_g), bsz=bsz, seq=seq, n_mem=n_mem,
                            tm=tm, cast=(l, ffn2_w_gu, ffn2_w_down))

        ahead = (l + 1, ffn1_w_gu, ffn1_w_down) if l + 1 < depth else None
        xs, ffn1_w = _ffn(l, xs, vec(ffn2_pre_g), *ffn2_w, vec(ffn2_post_g), tm=tm_ffn, tf=tf, cast=ahead)
    return xs.reshape(bsz, seq, d)
```
